```python
import math
import jax, jax.numpy as jnp
from jax import lax
import numpy as np

D_MODEL = 1024
BATCH = 4
SEQ = 4096
DEPTH = 2
DEC_BATCH = 32
DEC_SEQ = 64
PAST_LEN = 2048

CHUNK = 64
D_HEAD = 64
H_FOX = (3 * D_MODEL // 8) // D_HEAD
H_SB = (3 * D_MODEL // 8) // D_HEAD
H_SGU = D_MODEL // D_HEAD - H_FOX - H_SB
W_FOX = H_FOX * D_HEAD
W_SB = H_SB * D_HEAD
W_SGU = H_SGU * D_HEAD
D_MIX = W_FOX + W_SGU + W_SB
D_IN = 3 * W_FOX + H_FOX + 2 * W_SGU + 3 * W_SB
SGU_CHUNK = 128
Q_BLOCK = 128
D_FF = 4 * D_MODEL
ALPHA = (2 * DEPTH) ** 0.25
BETA = (8 * DEPTH) ** -0.25
FORGET_BIAS = 2.0
LN_EPS = 1e-5
RMS_EPS = 1e-6
NEG_INF = -1e30

kernel_name = "hybrid_fox_sgu_stickbreak_stream_encoder"


def _split_points():
    sizes = [W_FOX, W_FOX, W_FOX, H_FOX, W_SGU, W_SGU, W_SB, W_SB, W_SB]
    return np.cumsum(sizes)[:-1].tolist()


def layer_norm(x, g, b):
    xf = x.astype(jnp.float32)
    mu = jnp.mean(xf, axis=-1, keepdims=True)
    var = jnp.mean(jnp.square(xf - mu), axis=-1, keepdims=True)
    return ((xf - mu) * lax.rsqrt(var + LN_EPS) * g + b).astype(x.dtype)


def in_proj(x, w_in, b_f):
    B, T, _ = x.shape
    z = jnp.einsum('btd,de->bte', x, w_in)
    q_f, k_f, v_f, f_lg, u_g, v_g, q_s, k_s, v_s = jnp.split(z, _split_points(), axis=-1)
    heads = lambda a, h: a.reshape(B, T, h, D_HEAD)
    log_f = jax.nn.log_sigmoid((f_lg + b_f).astype(jnp.float32))
    return (heads(q_f, H_FOX), heads(k_f, H_FOX), heads(v_f, H_FOX), log_f,
            u_g, v_g, heads(q_s, H_SB), heads(k_s, H_SB), heads(v_s, H_SB))


def fox_block(q, c_q, qpos, k, v, c_k):
    kpos = jnp.arange(k.shape[1])
    s = jnp.einsum('bqhd,bkhd->bhqk', q, k).astype(jnp.float32) / math.sqrt(D_HEAD)
    s = s + jnp.transpose(c_q, (0, 2, 1))[..., :, None] - jnp.transpose(c_k, (0, 2, 1))[..., None, :]
    mask = kpos[None, :] <= qpos[:, None]
    p = jax.nn.softmax(jnp.where(mask, s, NEG_INF), axis=-1)
    return jnp.einsum('bhqk,bkhd->bqhd', p.astype(v.dtype), v)


def sb_block(q, qpos, k, v):
    kpos = jnp.arange(k.shape[1])
    z = jnp.einsum('bqhd,bkhd->bhqk', q, k).astype(jnp.float32) / math.sqrt(D_HEAD)
    mask = kpos[None, :] < qpos[:, None]
    log_rem = jnp.where(mask, jax.nn.log_sigmoid(-z), 0.0)
    after = lax.cumsum(log_rem, axis=3, reverse=True) - log_rem
    a = jnp.where(mask, jnp.exp(jax.nn.log_sigmoid(z) + after), 0.0)
    return jnp.einsum('bhqk,bkhd->bqhd', a.astype(v.dtype), v)


def sgu_gate(u_g, v_g, g_v, b_v):
    u = jax.nn.gelu(u_g)
    v = layer_norm(jax.nn.gelu(v_g), g_v, b_v)
    return u, v


def sgu_mix(u, v, w_s, b_s):
    B, T, _ = v.shape
    L = min(T, SGU_CHUNK)
    n = T // L
    w = jnp.tril(w_s[:, :L, :L])
    vb = v.reshape(B, n, L, H_SGU, D_HEAD)
    s = jnp.einsum('gij,bnjgc->bnigc', w, vb) + b_s[:, :L].T[None, None, :, :, None]
    return u * s.reshape(B, T, W_SGU)


def merge_heads(o_fox, o_sgu, o_sb, g_mix, w_out):
    B, T = o_sgu.shape[:2]
    o = jnp.concatenate([o_fox.reshape(B, T, W_FOX), o_sgu, o_sb.reshape(B, T, W_SB)], axis=-1)
    oh = o.reshape(B, T, D_MIX // D_HEAD, D_HEAD).astype(jnp.float32)
    oh = oh * lax.rsqrt(jnp.mean(oh * oh, axis=-1, keepdims=True) + RMS_EPS)
    o = (oh.reshape(B, T, D_MIX) * g_mix).astype(o_sgu.dtype)
    return jnp.einsum('bte,ed->btd', o, w_out)


def sq_relu_mlp(x, w_up, w_down):
    h = jnp.square(jax.nn.relu(jnp.einsum('btd,df->btf', x, w_up)))
    return jnp.einsum('btf,fd->btd', h, w_down)


def prompt_attention(q_f, k_f, v_f, c, q_s, k_s, v_s):
    B, T = q_f.shape[:2]
    nb = T // Q_BLOCK

    def blk(i):
        start = i * Q_BLOCK
        qpos = start + jnp.arange(Q_BLOCK)
        qf = lax.dynamic_slice_in_dim(q_f, start, Q_BLOCK, axis=1)
        cq = lax.dynamic_slice_in_dim(c, start, Q_BLOCK, axis=1)
        qs = lax.dynamic_slice_in_dim(q_s, start, Q_BLOCK, axis=1)
        return fox_block(qf, cq, qpos, k_f, v_f, c), sb_block(qs, qpos, k_s, v_s)

    o_f, o_s = lax.map(blk, jnp.arange(nb))
    o_f = jnp.transpose(o_f, (1, 0, 2, 3, 4)).reshape(B, T, H_FOX, D_HEAD)
    o_s = jnp.transpose(o_s, (1, 0, 2, 3, 4)).reshape(B, T, H_SB, D_HEAD)
    return o_f, o_s


def setup_inputs(seed: int = 0) -> dict:
    key = jax.random.key(seed)
    ks = jax.random.split(key, 24)
    nrm = lambda k, shape: jax.random.normal(k, shape, jnp.float32)
    return {
        "x_prompt": nrm(ks[0], (BATCH, SEQ, D_MODEL)),
        "x_sample": nrm(ks[1], (DEC_BATCH, DEC_SEQ, D_MODEL)),
        "cache_fox_k": nrm(ks[2], (DEPTH, DEC_BATCH, PAST_LEN, H_FOX, D_HEAD)),
        "cache_fox_v": nrm(ks[3], (DEPTH, DEC_BATCH, PAST_LEN, H_FOX, D_HEAD)),
        "cache_fox_logf": jax.nn.log_sigmoid(FORGET_BIAS + nrm(ks[4], (DEPTH, DEC_BATCH, PAST_LEN, H_FOX))),
        "cache_sb_k": nrm(ks[5], (DEPTH, DEC_BATCH, PAST_LEN, H_SB, D_HEAD)),
        "cache_sb_v": nrm(ks[6], (DEPTH, DEC_BATCH, PAST_LEN, H_SB, D_HEAD)),
        "w_in": nrm(ks[7], (DEPTH, D_MODEL, D_IN)) * D_MODEL ** -0.5,
        "b_f": FORGET_BIAS + 0.1 * nrm(ks[8], (DEPTH, H_FOX)),
        "g_v": 1.0 + 0.1 * nrm(ks[9], (DEPTH, W_SGU)),
        "b_v": 0.02 * nrm(ks[10], (DEPTH, W_SGU)),
        "w_s": nrm(ks[11], (DEPTH, H_SGU, SGU_CHUNK, SGU_CHUNK)) * SGU_CHUNK ** -0.5,
        "b_s": 1.0 + 0.1 * nrm(ks[12], (DEPTH, H_SGU, SGU_CHUNK)),
        "g_mix": 1.0 + 0.1 * nrm(ks[13], (DEPTH, D_MIX)),
        "w_out": nrm(ks[14], (DEPTH, D_MIX, D_MODEL)) * (D_MIX ** -0.5 * BETA),
        "ln1_g": 1.0 + 0.1 * nrm(ks[15], (DEPTH, D_MODEL)),
        "ln1_b": 0.02 * nrm(ks[16], (DEPTH, D_MODEL)),
        "w_up": nrm(ks[17], (DEPTH, D_MODEL, D_FF)) * D_MODEL ** -0.5,
        "w_down": nrm(ks[18], (DEPTH, D_FF, D_MODEL)) * (D_FF ** -0.5 * BETA),
        "ln2_g": 1.0 + 0.1 * nrm(ks[19], (DEPTH, D_MODEL)),
        "ln2_b": 0.02 * nrm(ks[20], (DEPTH, D_MODEL)),
    }


def reference(x_prompt, x_sample, cache_fox_k, cache_fox_v, cache_fox_logf, cache_sb_k, cache_sb_v,
              w_in, b_f, g_v, b_v, w_s, b_s, g_mix, w_out, ln1_g, ln1_b, w_up, w_down, ln2_g, ln2_b):
    assert x_sample.shape[1] <= CHUNK
    xp, xs = x_prompt, x_sample
    p_fk, p_fv, p_fl, p_sk, p_sv = [], [], [], [], []
    s_fk, s_fv, s_fl, s_sk, s_sv, s_gv = [], [], [], [], [], []
    for l in range(DEPTH):
        q_f, k_f, v_f, log_f, u_g, v_g, q_s, k_s, v_s = in_proj(xp, w_in[l], b_f[l])
        c = lax.cumsum(log_f, axis=1)
        o_f, o_s = prompt_attention(q_f, k_f, v_f, c, q_s, k_s, v_s)
        u, v = sgu_gate(u_g, v_g, g_v[l], b_v[l])
        o_g = sgu_mix(u, v, w_s[l], b_s[l])
        mix = merge_heads(o_f, o_g, o_s, g_mix[l], w_out[l])
        xp = layer_norm(ALPHA * xp + mix, ln1_g[l], ln1_b[l])
        xp = layer_norm(ALPHA * xp + sq_relu_mlp(xp, w_up[l], w_down[l]), ln2_g[l], ln2_b[l])
        p_fk.append(k_f); p_fv.append(v_f); p_fl.append(log_f); p_sk.append(k_s); p_sv.append(v_s)

        q_f, k_f, v_f, log_f, u_g, v_g, q_s, k_s, v_s = in_proj(xs, w_in[l], b_f[l])
        past = cache_fox_k.shape[2]
        qpos = past + jnp.arange(xs.shape[1])
        kf_all = jnp.concatenate([cache_fox_k[l], k_f], axis=1)
        vf_all = jnp.concatenate([cache_fox_v[l], v_f], axis=1)
        c_all = lax.cumsum(jnp.concatenate([cache_fox_logf[l].astype(jnp.float32), log_f], axis=1), axis=1)
        o_f = fox_block(q_f, c_all[:, past:], qpos, kf_all, vf_all, c_all)
        ks_all = jnp.concatenate([cache_sb_k[l], k_s], axis=1)
        vs_all = jnp.concatenate([cache_sb_v[l], v_s], axis=1)
        o_s = sb_block(q_s, qpos, ks_all, vs_all)
        u, v = sgu_gate(u_g, v_g, g_v[l], b_v[l])
        o_g = sgu_mix(u, v, w_s[l], b_s[l])
        mix = merge_heads(o_f, o_g, o_s, g_mix[l], w_out[l])
        xs = layer_norm(ALPHA * xs + mix, ln1_g[l], ln1_b[l])
        xs = layer_norm(ALPHA * xs + sq_relu_mlp(xs, w_up[l], w_down[l]), ln2_g[l], ln2_b[l])
        s_fk.append(k_f); s_fv.append(v_f); s_fl.append(log_f); s_sk.append(k_s); s_sv.append(v_s)
        s_gv.append(v)

    return (xp, xs,
            jnp.stack(p_fk), jnp.stack(p_fv), jnp.stack(p_fl), jnp.stack(p_sk), jnp.stack(p_sv),
            jnp.stack(s_fk), jnp.stack(s_fv), jnp.stack(s_fl), jnp.stack(s_sk), jnp.stack(s_sv),
            jnp.stack(s_gv))
```

```python
import functools
import math

import jax
import jax.numpy as jnp
from jax import lax
from jax.experimental import pallas as pl
from jax.experimental.pallas import tpu as pltpu

F32 = jnp.float32
BF16 = jnp.bfloat16

LANES = 128
SUBLANES = 8
VMEM_BYTES_V7X = 64 * 1024 * 1024

D_HEAD = 64
H_FOX = 6
H_SB = 6
H_SGU = 4
W_ATT = H_FOX * D_HEAD
W_SGU = H_SGU * D_HEAD
N_PAIR = W_ATT // LANES
SGU_CHUNK = 128
LN_EPS = 1e-5
RMS_EPS = 1e-6
NEG_INF = -1e30

ROW_TILE = 512
ATT_TILE = 256

_C_Q = 0
_C_K = 2 * W_ATT
_C_V = 4 * W_ATT
_C_G = 6 * W_ATT
_C_F = _C_G + 2 * W_SGU
W_ROWS = _C_F + LANES
_R_K = 0
_R_V = 2 * W_ATT
_R_F = 4 * W_ATT
F_ROWS = 2 * SUBLANES
W_COLS = _R_F + F_ROWS


def _params(semantics, vmem_mib):
    assert vmem_mib * 1024 * 1024 < VMEM_BYTES_V7X
    return pltpu.CompilerParams(dimension_semantics=semantics,
                                vmem_limit_bytes=vmem_mib * 1024 * 1024)


def _resident(shape):
    nd = len(shape)
    return pl.BlockSpec(shape, lambda *_: (0,) * nd, pipeline_mode=pl.Buffered(1))


def _split3(x):
    hi = x.astype(BF16)
    r1 = x - hi.astype(F32)
    mid = r1.astype(BF16)
    lo = (r1 - mid.astype(F32)).astype(BF16)
    return hi, mid, lo


def _dot(a, b):
    return jnp.dot(a, b, preferred_element_type=F32)


def _dot_nt(a, b):
    return lax.dot_general(a, b, (((1,), (1,)), ((), ())), preferred_element_type=F32)


def _ones_dot3(x, ones_mat, left):
    if left:
        return sum(_dot(ones_mat, p) for p in _split3(x))
    return sum(_dot(p, ones_mat) for p in _split3(x))


def _gelu_tanh(x):
    return 0.5 * x * (1.0 + jnp.tanh(0.7978845608028654 * (x + 0.044715 * (x * x * x))))


def _log_sigmoid(x):
    return jnp.minimum(x, 0.0) - jnp.log1p(jnp.exp(-jnp.abs(x)))


def _softplus(z):
    return jnp.maximum(z, 0.0) + jnp.log(1.0 + jnp.exp(-jnp.abs(z)))


def _layer_norm(x, g, b):
    mu = jnp.mean(x, axis=-1, keepdims=True)
    xc = x - mu
    var = jnp.mean(xc * xc, axis=-1, keepdims=True)
    return xc * lax.rsqrt(var + LN_EPS) * g + b


def _tri_ones(n, kind, seg=None):
    r = lax.broadcasted_iota(jnp.int32, (n, n), 0)
    c = lax.broadcasted_iota(jnp.int32, (n, n), 1)
    keep = {"lower": c <= r, "upper": r <= c, "strict_lower": r > c}[kind]
    if seg is not None:
        keep = jnp.logical_and(keep, r // seg == c // seg)
    return jnp.where(keep, 1.0, 0.0).astype(BF16)


def _pick_lane(x, idx):
    lane = lax.broadcasted_iota(jnp.int32, x.shape, 1)
    return jnp.sum(jnp.where(lane == idx, x, 0.0), axis=1, keepdims=True)


def _pick_row(x, idx):
    sub = lax.broadcasted_iota(jnp.int32, x.shape, 0)
    return jnp.sum(jnp.where(sub == idx, x, 0.0), axis=0, keepdims=True)


def _gates(xb, wr_ref, gv_ref, bv_ref, u_ref, vn_ref):
    u_ref[...] = _gelu_tanh(_dot(xb, wr_ref[:, _C_G:_C_G + W_SGU]))
    vg = _dot(xb, wr_ref[:, _C_G + W_SGU:_C_G + 2 * W_SGU])
    vn_ref[...] = _layer_norm(_gelu_tanh(vg), gv_ref[...], bv_ref[...])


def _log_forget_cols(xb, wr_ref, bfc_ref):
    lane = lax.broadcasted_iota(jnp.int32, (xb.shape[0], LANES), 1)
    z = _dot(xb, wr_ref[:, _C_F:_C_F + LANES]) + bfc_ref[...]
    return jnp.where(lane < H_FOX, _log_sigmoid(z), 0.0)


def _log_forget_rows(xb, wc_ref, bfr_ref):
    sub = lax.broadcasted_iota(jnp.int32, (SUBLANES, xb.shape[0]), 0)
    z = _dot_nt(wc_ref[_R_F:_R_F + F_ROWS, :], xb)[:SUBLANES] + bfr_ref[:, 0:1]
    return jnp.where(sub < H_FOX, _log_sigmoid(z), 0.0)


def _in_proj_prompt_kernel(x_ref, wr_ref, wc_ref, bfc_ref, bfr_ref, gv_ref, bv_ref,
                           q_ref, ktf_ref, kts_ref, vtf_ref, vts_ref, ktbf_ref, ktbs_ref, vb_ref,
                           lfrow_ref, crow_ref, ccol_ref, u_ref, vn_ref,
                           carry_col_ref, carry_row_ref):
    i = pl.program_id(1)
    tm = x_ref.shape[0]
    tk = ktbf_ref.shape[3]
    xb = x_ref[...].astype(BF16)

    scale = 1.0 / math.sqrt(D_HEAD)
    q_ref[...] = (_dot(xb, wr_ref[:, _C_Q:_C_Q + 2 * W_ATT]) * scale).astype(BF16)
    vb_ref[...] = _dot(xb, wr_ref[:, _C_V:_C_V + 2 * W_ATT]).astype(BF16)

    for r0, full_ref, blk_ref in ((_R_K, ktf_ref, ktbf_ref), (_R_K + W_ATT, kts_ref, ktbs_ref),
                                  (_R_V, vtf_ref, None), (_R_V + W_ATT, vts_ref, None)):
        zt = _dot_nt(wc_ref[r0:r0 + W_ATT, :], xb)
        full_ref[0] = zt
        if blk_ref is not None:
            for kb in range(tm // tk):
                blk_ref[0, kb] = zt[:, kb * tk:(kb + 1) * tk].astype(BF16)

    _gates(xb, wr_ref, gv_ref, bv_ref, u_ref, vn_ref)

    @pl.when(i == 0)
    def _():
        carry_col_ref[...] = jnp.zeros_like(carry_col_ref)
        carry_row_ref[...] = jnp.zeros_like(carry_row_ref)

    lf_col = _log_forget_cols(xb, wr_ref, bfc_ref)
    c_col = _ones_dot3(lf_col, _tri_ones(tm, "lower"), left=True) + carry_col_ref[...]
    ccol_ref[...] = c_col
    carry_col_ref[...] = c_col[tm - 1:tm, :]

    lf_row = _log_forget_rows(xb, wc_ref, bfr_ref)
    lfrow_ref[...] = lf_row
    c_row = _ones_dot3(lf_row, _tri_ones(tm, "upper"), left=False) + carry_row_ref[:, 0:1]
    for kb in range(tm // tk):
        crow_ref[kb] = c_row[:, kb * tk:(kb + 1) * tk]
    carry_row_ref[...] = jnp.broadcast_to(c_row[:, tm - 1:tm], carry_row_ref.shape)


def _in_proj_prompt(x, w_rows, w_cols, bfc, bfr, gv, bv, *, batch, seq):
    n, d = x.shape
    tm, tk = ROW_TILE, ATT_TILE
    nt = seq // tm
    row = lambda w: pl.BlockSpec((tm, w), lambda b, i: (b * nt + i, 0))
    kt_spec = pl.BlockSpec((1, W_ATT, tm), lambda b, i: (b, 0, i))
    ktb_spec = pl.BlockSpec((1, tm // tk, W_ATT, tk), lambda b, i: (b, i, 0, 0))
    kt_shape = jax.ShapeDtypeStruct((batch, W_ATT, seq), F32)
    ktb_shape = jax.ShapeDtypeStruct((batch, seq // tk, W_ATT, tk), BF16)
    out_shape = [jax.ShapeDtypeStruct((n, 2 * W_ATT), BF16),
                 kt_shape, kt_shape, kt_shape, kt_shape, ktb_shape, ktb_shape,
                 jax.ShapeDtypeStruct((n, 2 * W_ATT), BF16),
                 jax.ShapeDtypeStruct((SUBLANES, n), F32),
                 jax.ShapeDtypeStruct((n // tk, SUBLANES, tk), F32),
                 jax.ShapeDtypeStruct((n, LANES), F32),
                 jax.ShapeDtypeStruct((n, W_SGU), F32), jax.ShapeDtypeStruct((n, W_SGU), F32)]
    out_specs = [row(2 * W_ATT), kt_spec, kt_spec, kt_spec, kt_spec, ktb_spec, ktb_spec,
                 row(2 * W_ATT),
                 pl.BlockSpec((SUBLANES, tm), lambda b, i: (0, b * nt + i)),
                 pl.BlockSpec((tm // tk, SUBLANES, tk), lambda b, i: (b * nt + i, 0, 0)),
                 row(LANES), row(W_SGU), row(W_SGU)]
    return pl.pallas_call(
        _in_proj_prompt_kernel, grid=(batch, nt), name="in_proj_prompt",
        in_specs=[row(d)] + [_resident(a.shape) for a in (w_rows, w_cols, bfc, bfr, gv, bv)],
        out_specs=out_specs, out_shape=out_shape,
        scratch_shapes=[pltpu.VMEM((1, LANES), F32), pltpu.VMEM((SUBLANES, LANES), F32)],
        compiler_params=_params(("parallel", "arbitrary"), 48),
    )(x, w_rows, w_cols, bfc, bfr, gv, bv)


def _in_proj_sample_kernel(x_ref, wr_ref, wc_ref, bfc_ref, bfr_ref, gv_ref, bv_ref,
                           q_ref, kb_ref, vb_ref, k_ref, v_ref, lfcol_ref, ccol_ref, crow_ref,
                           u_ref, vn_ref, *, dec_seq):
    tm = x_ref.shape[0]
    xb = x_ref[...].astype(BF16)
    scale = 1.0 / math.sqrt(D_HEAD)
    q_ref[...] = (_dot(xb, wr_ref[:, _C_Q:_C_Q + 2 * W_ATT]) * scale).astype(BF16)
    for c0, full_ref, half_ref in ((_C_K, k_ref, kb_ref), (_C_V, v_ref, vb_ref)):
        z = _dot(xb, wr_ref[:, c0:c0 + 2 * W_ATT])
        full_ref[...] = z
        half_ref[...] = z.astype(BF16)
    _gates(xb, wr_ref, gv_ref, bv_ref, u_ref, vn_ref)

    lf_col = _log_forget_cols(xb, wr_ref, bfc_ref)
    lfcol_ref[...] = lf_col
    ccol_ref[...] = _ones_dot3(lf_col, _tri_ones(tm, "lower", dec_seq), left=True)
    lf_row = _log_forget_rows(xb, wc_ref, bfr_ref)
    c_row = _ones_dot3(lf_row, _tri_ones(tm, "upper", dec_seq), left=False)
    for s in range(tm // dec_seq):
        crow_ref[s] = c_row[:, s * dec_seq:(s + 1) * dec_seq]


def _in_proj_sample(x, w_rows, w_cols, bfc, bfr, gv, bv, *, dec_seq):
    n, d = x.shape
    tm = ROW_TILE
    row = lambda w: pl.BlockSpec((tm, w), lambda i: (i, 0))
    sds = lambda w, dt: jax.ShapeDtypeStruct((n, w), dt)
    out_shape = [sds(2 * W_ATT, BF16)] * 3 + [sds(2 * W_ATT, F32)] * 2 + [sds(LANES, F32)] * 2 + [
        jax.ShapeDtypeStruct((n // dec_seq, SUBLANES, dec_seq), F32), sds(W_SGU, F32), sds(W_SGU, F32)]
    out_specs = [row(2 * W_ATT)] * 5 + [row(LANES)] * 2 + [
        pl.BlockSpec((tm // dec_seq, SUBLANES, dec_seq), lambda i: (i, 0, 0)), row(W_SGU), row(W_SGU)]
    return pl.pallas_call(
        functools.partial(_in_proj_sample_kernel, dec_seq=dec_seq), grid=(n // tm,),
        name="in_proj_sample",
        in_specs=[row(d)] + [_resident(a.shape) for a in (w_rows, w_cols, bfc, bfr, gv, bv)],
        out_specs=out_specs, out_shape=out_shape,
        compiler_params=_params(("parallel",), 48),
    )(x, w_rows, w_cols, bfc, bfr, gv, bv)


def _split_heads(q2):
    lane = lax.broadcasted_iota(jnp.int32, q2.shape, 1)
    low = lane < D_HEAD
    zero = jnp.zeros_like(q2)
    return jnp.where(low, q2, zero), jnp.where(low, zero, q2)


def _merge_heads(o_a, o_b):
    lane = lax.broadcasted_iota(jnp.int32, o_a.shape, 1)
    return jnp.where(lane < D_HEAD, o_a, o_b)


def _fox_update(s, cq, pv, state, valid):
    m, l, acc = state
    if valid is not None:
        s = jnp.where(valid, s, NEG_INF)
    m_new = jnp.maximum(m, jnp.max(s, axis=1, keepdims=True) + cq)
    p = jnp.exp(s + (cq - m_new))
    alpha = jnp.exp(m - m_new)
    l = alpha * l + jnp.sum(p, axis=1, keepdims=True)
    acc = alpha * acc + pv(p.astype(BF16))
    return m_new, l, acc


def _sb_update(z, pv, suffix_ones, state, valid):
    carry, acc = state
    r = _softplus(z)
    if valid is not None:
        r = jnp.where(valid, r, 0.0)
    r_hi = r.astype(BF16)
    r_lo = (r - r_hi.astype(F32)).astype(BF16)
    after = _dot(r_hi, suffix_ones) + _dot(r_lo, suffix_ones) + carry
    a = jnp.exp((z - r) - after)
    if valid is not None:
        a = jnp.where(valid, a, 0.0)
    acc = acc + pv(a.astype(BF16))
    carry = carry + jnp.sum(r, axis=1, keepdims=True)
    return carry, acc


def _fox_prompt_kernel(q_ref, kt_ref, v_ref, crow_ref, ccol_ref, o_ref):
    tq = q_ref.shape[0]
    tk = kt_ref.shape[3]
    hp = pl.program_id(1)
    i = pl.program_id(2)
    q_heads = _split_heads(q_ref[...])
    cc = ccol_ref[...]
    cqs = (_pick_lane(cc, 2 * hp), _pick_lane(cc, 2 * hp + 1))

    def block(j, states, valid):
        kt = kt_ref[0, j]
        v2 = v_ref[pl.ds(pl.multiple_of(j * tk, tk), tk), :]
        c_rows = crow_ref[j]
        pv = lambda p: _dot(p, v2)
        return tuple(
            _fox_update(_dot(q_heads[h], kt) - _pick_row(c_rows, 2 * hp + h), cqs[h], pv,
                        states[h], valid)
            for h in range(2))

    init = tuple((jnp.full((tq, 1), NEG_INF, F32), jnp.zeros((tq, 1), F32),
                  jnp.zeros((tq, LANES), F32)) for _ in range(2))
    states = lax.fori_loop(0, i, lambda j, st: block(j, st, None), init)
    row = lax.broadcasted_iota(jnp.int32, (tq, tk), 0)
    col = lax.broadcasted_iota(jnp.int32, (tq, tk), 1)
    (_, l_a, acc_a), (_, l_b, acc_b) = block(i, states, col <= row)
    o_ref[...] = _merge_heads(acc_a / l_a, acc_b / l_b)


def _sb_prompt_kernel(q_ref, kt_ref, v_ref, o_ref):
    tq = q_ref.shape[0]
    tk = kt_ref.shape[3]
    i = pl.program_id(2)
    q_heads = _split_heads(q_ref[...])
    suffix_ones = _tri_ones(tk, "strict_lower")

    def block(j, states, valid):
        kt = kt_ref[0, j]
        v2 = v_ref[pl.ds(pl.multiple_of(j * tk, tk), tk), :]
        pv = lambda a: _dot(a, v2)
        return tuple(_sb_update(_dot(q_heads[h], kt), pv, suffix_ones, states[h], valid)
                     for h in range(2))

    init = tuple((jnp.zeros((tq, 1), F32), jnp.zeros((tq, LANES), F32)) for _ in range(2))
    row = lax.broadcasted_iota(jnp.int32, (tq, tk), 0)
    col = lax.broadcasted_iota(jnp.int32, (tq, tk), 1)
    states = block(i, init, col < row)
    states = lax.fori_loop(0, i, lambda t, st: block(i - 1 - t, st, None), states)
    o_ref[...] = _merge_heads(states[0][1], states[1][1])


def _prompt_attention(q, ktb, vb, group, *, batch, seq, crow=None, ccol=None):
    tq = tk = ATT_TILE
    nq = seq // tq
    col0 = group * N_PAIR
    q_spec = pl.BlockSpec((tq, LANES), lambda b, hp, i: (b * nq + i, col0 + hp))
    kt_spec = pl.BlockSpec((1, seq // tk, LANES, tk), lambda b, hp, i: (b, 0, hp, 0))
    v_spec = pl.BlockSpec((seq, LANES), lambda b, hp, i: (b, col0 + hp))
    out_spec = pl.BlockSpec((tq, LANES), lambda b, hp, i: (b * nq + i, hp))
    out_shape = jax.ShapeDtypeStruct((batch * seq, W_ATT), F32)
    params = _params(("parallel", "parallel", "arbitrary"), 40)
    grid = (batch, N_PAIR, nq)
    if crow is None:
        return pl.pallas_call(
            _sb_prompt_kernel, grid=grid, name="sb_prompt",
            in_specs=[q_spec, kt_spec, v_spec], out_specs=out_spec, out_shape=out_shape,
            compiler_params=params)(q, ktb, vb)
    crow_spec = pl.BlockSpec((seq // tk, SUBLANES, tk), lambda b, hp, i: (b, 0, 0))
    ccol_spec = pl.BlockSpec((tq, LANES), lambda b, hp, i: (b * nq + i, 0))
    return pl.pallas_call(
        _fox_prompt_kernel, grid=grid, name="fox_prompt",
        in_specs=[q_spec, kt_spec, v_spec, crow_spec, ccol_spec],
        out_specs=out_spec, out_shape=out_shape, compiler_params=params)(q, ktb, vb, crow, ccol)


def _suffix_sum_kernel(x_ref, o_ref):
    rows, width = x_ref.shape
    ones = _tri_ones(LANES, "strict_lower")
    carry = jnp.zeros((rows, 1), F32)
    for kb in reversed(range(width // LANES)):
        xb = x_ref[:, kb * LANES:(kb + 1) * LANES]
        o_ref[:, kb * LANES:(kb + 1) * LANES] = _ones_dot3(xb, ones, left=False) + carry
        carry = carry + jnp.sum(xb, axis=1, keepdims=True)


def _suffix_sum(x):
    return pl.pallas_call(
        _suffix_sum_kernel, name="cache_suffix_sum",
        out_shape=jax.ShapeDtypeStruct(x.shape, F32),
        compiler_params=_params(None, 32))(x)


def _stack_heads(q2):
    return jnp.concatenate(_split_heads(q2), axis=0)


def _unstack_heads(o):
    half = o.shape[0] // 2
    return _merge_heads(o[:half], o[half:])


def _sample_attn_kernel(qf_ref, kfn_ref, vfn_ref, kfc_ref, vfc_ref, drow_ref, cnrow_ref, ccol_ref,
                        qs_ref, ksn_ref, vsn_ref, ksc_ref, vsc_ref, of_ref, os_ref, *, blk):
    b = pl.program_id(0)
    hp = pl.program_id(1)
    tq = qf_ref.shape[0]
    past = kfc_ref.shape[2]
    row = lax.broadcasted_iota(jnp.int32, (2 * tq, tq), 0) % tq
    col = lax.broadcasted_iota(jnp.int32, (2 * tq, tq), 1)

    def per_head(fn):
        return jnp.concatenate([fn(0), fn(1)], axis=0)

    q2 = _stack_heads(qf_ref[...])
    kct = kfc_ref[0].astype(BF16)
    vct = vfc_ref[0].astype(BF16)
    cc = ccol_ref[...]
    cn = cnrow_ref[0]
    cq = per_head(lambda h: _pick_lane(cc, 2 * hp + h))
    d_cache = per_head(lambda h: jnp.broadcast_to(
        _pick_row(drow_ref[2 * hp + h], b % SUBLANES), (tq, past)))
    c_new = per_head(lambda h: jnp.broadcast_to(_pick_row(cn, 2 * hp + h), (tq, tq)))
    s_c = _dot(q2, kct) + d_cache
    s_n = jnp.where(col <= row, _dot_nt(q2, kfn_ref[...]) - c_new, NEG_INF)
    m = jnp.maximum(jnp.max(s_c, axis=1, keepdims=True), jnp.max(s_n, axis=1, keepdims=True)) + cq
    shift = cq - m
    p_c = jnp.exp(s_c + shift)
    p_n = jnp.exp(s_n + shift)
    l = jnp.sum(p_c, axis=1, keepdims=True) + jnp.sum(p_n, axis=1, keepdims=True)
    o = _dot_nt(p_c.astype(BF16), vct) + _dot(p_n.astype(BF16), vfn_ref[...])
    of_ref[...] = _unstack_heads(o / l)

    q2 = _stack_heads(qs_ref[...])
    vn = vsn_ref[...]
    state = (jnp.zeros((2 * tq, 1), F32), jnp.zeros((2 * tq, LANES), F32))
    state = _sb_update(_dot_nt(q2, ksn_ref[...]), lambda a: _dot(a, vn),
                       _tri_ones(tq, "strict_lower"), state, col < row)
    suffix_ones = _tri_ones(blk, "strict_lower")
    for jb in reversed(range(past // blk)):
        kt = ksc_ref[0, :, jb * blk:(jb + 1) * blk].astype(BF16)
        vt = vsc_ref[0, :, jb * blk:(jb + 1) * blk].astype(BF16)
        state = _sb_update(_dot(q2, kt), lambda a, vt=vt: _dot_nt(a, vt), suffix_ones, state, None)
    os_ref[...] = _unstack_heads(state[1])


def _sample_attention(q, kb, vb, cache_fk, cache_fv, drow, cnrow, ccol, cache_sk, cache_sv,
                      *, streams, dec_seq):
    past = cache_fk.shape[2]
    new_spec = lambda g: pl.BlockSpec((dec_seq, LANES), lambda b, hp: (b, g * N_PAIR + hp))
    cache_spec = pl.BlockSpec((1, LANES, past), lambda b, hp: (b, hp, 0))
    drow_spec = pl.BlockSpec((H_FOX, SUBLANES, past), lambda b, hp: (0, b // SUBLANES, 0))
    cnrow_spec = pl.BlockSpec((1, SUBLANES, dec_seq), lambda b, hp: (b, 0, 0))
    ccol_spec = pl.BlockSpec((dec_seq, LANES), lambda b, hp: (b, 0))
    out_spec = pl.BlockSpec((dec_seq, LANES), lambda b, hp: (b, hp))
    out_shape = [jax.ShapeDtypeStruct((streams * dec_seq, W_ATT), F32)] * 2
    return pl.pallas_call(
        functools.partial(_sample_attn_kernel, blk=ATT_TILE),
        grid=(streams, N_PAIR), name="sample_attention",
        in_specs=[new_spec(0), new_spec(0), new_spec(0), cache_spec, cache_spec, drow_spec,
                  cnrow_spec, ccol_spec, new_spec(1), new_spec(1), new_spec(1), cache_spec,
                  cache_spec],
        out_specs=[out_spec, out_spec], out_shape=out_shape,
        compiler_params=_params(("parallel", "parallel"), 48),
    )(q, kb, vb, cache_fk, cache_fv, drow, cnrow, ccol, q, kb, vb, cache_sk, cache_sv)


def _merge_kernel(x_ref, of_ref, os_ref, u_ref, vn_ref, ws_ref, bs_ref, gmix_ref, wout_ref,
                  g_ref, b_ref, o_ref, *, alpha):
    tm = x_ref.shape[0]
    r = lax.broadcasted_iota(jnp.int32, (SGU_CHUNK, SGU_CHUNK), 0)
    c = lax.broadcasted_iota(jnp.int32, (SGU_CHUNK, SGU_CHUNK), 1)
    tril = r >= c
    head_mean = jnp.where(r // D_HEAD == c // D_HEAD, 1.0 / D_HEAD, 0.0).astype(BF16)

    gate_pieces = []
    for p in range(W_SGU // LANES):
        w0 = jnp.where(tril, ws_ref[2 * p], 0.0).astype(BF16)
        w1 = jnp.where(tril, ws_ref[2 * p + 1], 0.0).astype(BF16)
        bias = bs_ref[:, p * LANES:(p + 1) * LANES]
        chunks = []
        for ch in range(tm // SGU_CHUNK):
            rows = slice(ch * SGU_CHUNK, (ch + 1) * SGU_CHUNK)
            vp = vn_ref[rows, p * LANES:(p + 1) * LANES].astype(BF16)
            s = _merge_heads(_dot(w0, vp), _dot(w1, vp)) + bias
            chunks.append(u_ref[rows, p * LANES:(p + 1) * LANES] * s)
        gate_pieces.append(jnp.concatenate(chunks, axis=0))

    pieces = ([of_ref[:, k * LANES:(k + 1) * LANES] for k in range(N_PAIR)] + gate_pieces
              + [os_ref[:, k * LANES:(k + 1) * LANES] for k in range(N_PAIR)])
    normed = []
    for k, o in enumerate(pieces):
        ms = _dot((o * o).astype(BF16), head_mean)
        gain = gmix_ref[:, k * LANES:(k + 1) * LANES]
        normed.append((o * lax.rsqrt(ms + RMS_EPS) * gain).astype(BF16))
    mix = _dot(jnp.concatenate(normed, axis=1), wout_ref[...])
    o_ref[...] = _layer_norm(alpha * x_ref[...] + mix, g_ref[...], b_ref[...])


def _merge(x, of, os_, u, vn, ws, bs, gmix, wout, g, b, *, alpha, name):
    n, d = x.shape
    tm = ROW_TILE
    row = lambda w: pl.BlockSpec((tm, w), lambda i: (i, 0))
    return pl.pallas_call(
        functools.partial(_merge_kernel, alpha=alpha), grid=(n // tm,), name=name,
        in_specs=[row(d), row(W_ATT), row(W_ATT), row(W_SGU), row(W_SGU)]
        + [_resident(a.shape) for a in (ws, bs, gmix, wout, g, b)],
        out_specs=row(d), out_shape=jax.ShapeDtypeStruct((n, d), F32),
        compiler_params=_params(("parallel",), 48),
    )(x, of, os_, u, vn, ws, bs, gmix, wout, g, b)


def _mlp_kernel(x_ref, wup_ref, wdn_ref, g_ref, b_ref, o_ref, *, alpha, ff_chunk):
    x = x_ref[...]
    xb = x.astype(BF16)
    acc = alpha * x
    for c0 in range(0, wup_ref.shape[1], ff_chunk):
        h = jnp.maximum(_dot(xb, wup_ref[:, c0:c0 + ff_chunk]), 0.0)
        acc = acc + _dot((h * h).astype(BF16), wdn_ref[c0:c0 + ff_chunk, :])
    o_ref[...] = _layer_norm(acc, g_ref[...], b_ref[...])


def _mlp(x, wup, wdn, g, b, *, alpha, name):
    n, d = x.shape
    tm = ROW_TILE
    row = pl.BlockSpec((tm, d), lambda i: (i, 0))
    return pl.pallas_call(
        functools.partial(_mlp_kernel, alpha=alpha, ff_chunk=1024), grid=(n // tm,), name=name,
        in_specs=[row] + [_resident(a.shape) for a in (wup, wdn, g, b)],
        out_specs=row, out_shape=jax.ShapeDtypeStruct((n, d), F32),
        compiler_params=_params(("parallel",), 56),
    )(x, wup, wdn, g, b)


def _pack_w_in(w):
    sizes = [W_ATT, W_ATT, W_ATT, H_FOX, W_SGU, W_SGU, W_ATT, W_ATT, W_ATT]
    offs = [0]
    for s in sizes:
        offs.append(offs[-1] + s)
    q_f, k_f, v_f, f_lg, u_g, v_g, q_s, k_s, v_s = (w[:, offs[i]:offs[i + 1]] for i in range(9))
    w_rows = jnp.concatenate([q_f, q_s, k_f, k_s, v_f, v_s, u_g, v_g,
                              jnp.pad(f_lg, ((0, 0), (0, LANES - H_FOX)))], axis=1).astype(BF16)
    w_cols = jnp.concatenate([k_f, k_s, v_f, v_s,
                              jnp.pad(f_lg, ((0, 0), (0, F_ROWS - H_FOX)))], axis=1).T.astype(BF16)
    return w_rows, w_cols


def kernel(x_prompt, x_sample, cache_fox_k, cache_fox_v, cache_fox_logf, cache_sb_k, cache_sb_v,
           w_in, b_f, g_v, b_v, w_s, b_s, g_mix, w_out, ln1_g, ln1_b, w_up, w_down, ln2_g, ln2_b):
    depth = w_in.shape[0]
    batch, seq, d_model = x_prompt.shape
    streams, dec_seq, _ = x_sample.shape
    past = cache_fox_k.shape[2]
    n_prompt = batch * seq
    n_sample = streams * dec_seq
    alpha = (2 * depth) ** 0.25
    assert seq % ROW_TILE == 0 and n_sample % ROW_TILE == 0 and ROW_TILE % dec_seq == 0
    assert dec_seq == D_HEAD and 2 * dec_seq == SGU_CHUNK and past % ATT_TILE == 0
    assert streams % SUBLANES == 0

    xp = x_prompt.reshape(n_prompt, d_model)
    xs = x_sample.reshape(n_sample, d_model)
    row2 = lambda a: a.reshape(1, -1)
    cache_t = lambda c: jnp.transpose(c, (0, 2, 3, 1)).reshape(streams, W_ATT, past)

    outs = [[] for _ in range(11)]
    for l in range(depth):
        w_rows, w_cols = _pack_w_in(w_in[l])
        bfc = jnp.pad(b_f[l], (0, LANES - H_FOX)).reshape(1, LANES)
        bfr = jnp.broadcast_to(jnp.pad(b_f[l], (0, SUBLANES - H_FOX)).reshape(SUBLANES, 1),
                               (SUBLANES, LANES))
        gv, bv = row2(g_v[l]), row2(b_v[l])
        wout = w_out[l].astype(BF16)
        wup, wdn = w_up[l].astype(BF16), w_down[l].astype(BF16)
        gmix, g1, b1, g2, b2 = (row2(a[l]) for a in (g_mix, ln1_g, ln1_b, ln2_g, ln2_b))

        w_l = w_s[l]
        half = w_l[:, :dec_seq, :dec_seq]
        zeros = jnp.zeros_like(half)
        w_blockdiag = jnp.concatenate([jnp.concatenate([half, zeros], axis=2),
                                       jnp.concatenate([zeros, half], axis=2)], axis=1)
        bias_full = jnp.repeat(b_s[l].T, D_HEAD, axis=1)
        bias_half = jnp.concatenate([bias_full[:dec_seq], bias_full[:dec_seq]], axis=0)

        (q, ktf, kts, vtf, vts, ktbf, ktbs, vb, lfrow, crow, ccol, u, vn) = _in_proj_prompt(
            xp, w_rows, w_cols, bfc, bfr, gv, bv, batch=batch, seq=seq)
        of = _prompt_attention(q, ktbf, vb, 0, batch=batch, seq=seq, crow=crow, ccol=ccol)
        os_ = _prompt_attention(q, ktbs, vb, 1, batch=batch, seq=seq)
        x1 = _merge(xp, of, os_, u, vn, w_l, bias_full, gmix, wout, g1, b1, alpha=alpha,
                    name="merge_prompt")
        xp = _mlp(x1, wup, wdn, g2, b2, alpha=alpha, name="mlp_prompt")
        heads_t = lambda a: jnp.transpose(a.reshape(batch, H_FOX, D_HEAD, seq), (0, 3, 1, 2))
        p_out = (heads_t(ktf), heads_t(vtf),
                 jnp.transpose(lfrow[:H_FOX].reshape(H_FOX, batch, seq), (1, 2, 0)),
                 heads_t(kts), heads_t(vts))

        (q, kb, vb, k, v, lfcol, ccol, cnrow, u, vn) = _in_proj_sample(
            xs, w_rows, w_cols, bfc, bfr, gv, bv, dec_seq=dec_seq)
        lf_cache = jnp.transpose(cache_fox_logf[l].astype(F32), (2, 0, 1))
        drow = _suffix_sum(lf_cache.reshape(H_FOX * streams, past)).reshape(H_FOX, streams, past)
        of, os_ = _sample_attention(
            q, kb, vb, cache_t(cache_fox_k[l]), cache_t(cache_fox_v[l]), drow, cnrow, ccol,
            cache_t(cache_sb_k[l]), cache_t(cache_sb_v[l]), streams=streams, dec_seq=dec_seq)
        x1 = _merge(xs, of, os_, u, vn, w_blockdiag, bias_half, gmix, wout, g1, b1, alpha=alpha,
                    name="merge_sample")
        xs = _mlp(x1, wup, wdn, g2, b2, alpha=alpha, name="mlp_sample")
        heads = lambda a, g: a[:, g * W_ATT:(g + 1) * W_ATT].reshape(streams, dec_seq, H_FOX, D_HEAD)
        s_out = (heads(k, 0), heads(v, 0), lfcol[:, :H_FOX].reshape(streams, dec_seq, H_FOX),
                 heads(k, 1), heads(v, 1), vn.reshape(streams, dec_seq, W_SGU))

        for acc, val in zip(outs, p_out + s_out):
            acc.append(val)

    return (xp.reshape(batch, seq, d_model), xs.reshape(streams, dec_seq, d_model),
            *(jnp.stack(o) for o in outs))
```

```python
import functools
import math

import jax
import jax.numpy as jnp
from jax import lax
from jax.experimental import pallas as pl
from jax.experimental.pallas import tpu as pltpu

F32 = jnp.float32
BF16 = jnp.bfloat16

LANES = 128
SUBLANES = 8
VMEM_BYTES_V7X = 64 * 1024 * 1024

D_HEAD = 64
H_FOX = 6
H_SB = 6
H_SGU = 4
W_ATT = H_FOX * D_HEAD
W_SGU = H_SGU * D_HEAD
N_PAIR = W_ATT // LANES
SGU_CHUNK = 128
LN_EPS = 1e-5
RMS_EPS = 1e-6
NEG_INF = -1e30
SB_CLOSED = 110.0
FOX_DEAD = 110.0

ROW_TILE = 512
ATT_TILE = 256

_C_Q = 0
_C_K = 2 * W_ATT
_C_V = 4 * W_ATT
_C_G = 6 * W_ATT
_C_F = _C_G + 2 * W_SGU
_C_FP = _C_F + LANES
W_ROWS = _C_FP + N_PAIR * LANES
_R_Q = 0
_R_K = 2 * W_ATT
_R_V = 4 * W_ATT
_R_F = 6 * W_ATT
F_ROWS = 2 * SUBLANES
W_COLS = _R_F + F_ROWS


def _params(semantics, vmem_mib, flags=None):
    assert vmem_mib * 1024 * 1024 < VMEM_BYTES_V7X
    return pltpu.CompilerParams(dimension_semantics=semantics,
                                vmem_limit_bytes=vmem_mib * 1024 * 1024, flags=flags)


def _resident(shape):
    nd = len(shape)
    return pl.BlockSpec(shape, lambda *_: (0,) * nd, pipeline_mode=pl.Buffered(1))


def _split3(x):
    hi = x.astype(BF16)
    r1 = x - hi.astype(F32)
    mid = r1.astype(BF16)
    lo = (r1 - mid.astype(F32)).astype(BF16)
    return hi, mid, lo


def _dot(a, b):
    return jnp.dot(a, b, preferred_element_type=F32)


def _dot_nt(a, b):
    return lax.dot_general(a, b, (((1,), (1,)), ((), ())), preferred_element_type=F32)


def _ones_dot3(x, ones_mat, left):
    if left:
        return sum(_dot(ones_mat, p) for p in _split3(x))
    return sum(_dot(p, ones_mat) for p in _split3(x))


def _gelu_tanh(x):
    return 0.5 * x * (1.0 + jnp.tanh(0.7978845608028654 * (x + 0.044715 * (x * x * x))))


def _log_sigmoid(x):
    return jnp.minimum(x, 0.0) - jnp.log1p(jnp.exp(-jnp.abs(x)))


def _softplus(z):
    return jnp.maximum(z, 0.0) + jnp.log(1.0 + jnp.exp(-jnp.abs(z)))


def _layer_norm(x, g, b):
    mu = jnp.mean(x, axis=-1, keepdims=True)
    xc = x - mu
    var = jnp.mean(xc * xc, axis=-1, keepdims=True)
    return xc * lax.rsqrt(var + LN_EPS) * g + b


def _tri_ones(n, kind, seg=None):
    r = lax.broadcasted_iota(jnp.int32, (n, n), 0)
    c = lax.broadcasted_iota(jnp.int32, (n, n), 1)
    keep = {"lower": c <= r, "upper": r <= c, "strict_lower": r > c, "strict_upper": r < c}[kind]
    if seg is not None:
        keep = jnp.logical_and(keep, r // seg == c // seg)
    return jnp.where(keep, 1.0, 0.0).astype(BF16)


def _pick_lane(x, idx):
    lane = lax.broadcasted_iota(jnp.int32, x.shape, 1)
    return jnp.sum(jnp.where(lane == idx, x, 0.0), axis=1, keepdims=True)


def _pick_row(x, idx):
    sub = lax.broadcasted_iota(jnp.int32, x.shape, 0)
    return jnp.sum(jnp.where(sub == idx, x, 0.0), axis=0, keepdims=True)


def _gates(xb, wr_ref, gv_ref, bv_ref, u_ref, vn_ref):
    u_ref[...] = _gelu_tanh(_dot(xb, wr_ref[:, _C_G:_C_G + W_SGU]))
    vg = _dot(xb, wr_ref[:, _C_G + W_SGU:_C_G + 2 * W_SGU])
    vn_ref[...] = _layer_norm(_gelu_tanh(vg), gv_ref[...], bv_ref[...])


def _log_forget_rows(xb, wc_ref, bfr_ref):
    sub = lax.broadcasted_iota(jnp.int32, (SUBLANES, xb.shape[0]), 0)
    z = _dot_nt(wc_ref[_R_F:_R_F + F_ROWS, :], xb)[:SUBLANES] + bfr_ref[:, 0:1]
    return jnp.where(sub < H_FOX, _log_sigmoid(z), 0.0)


def _in_proj_prompt_kernel(x_ref, wr_ref, wc_ref, bfp_ref, bfr_ref, gv_ref, bv_ref,
                           qt_ref, k_ref, ktf_ref, kts_ref, vtf_ref, vts_ref, vtbf_ref, vtbs_ref,
                           lfrow_ref, crow_ref, ccolp_ref, u_ref, vn_ref, knmax_ref,
                           carry_col_ref, carry_row_ref):
    i = pl.program_id(1)
    tm = x_ref.shape[0]
    tk = vtbf_ref.shape[3]
    xb = x_ref[...].astype(BF16)

    def blocks(zt):
        return [zt[:, kb * tk:(kb + 1) * tk].astype(BF16) for kb in range(tm // tk)]

    scale = 1.0 / math.sqrt(D_HEAD)
    for kb, blk in enumerate(blocks(_dot_nt(wc_ref[_R_Q:_R_Q + 2 * W_ATT, :], xb) * scale)):
        qt_ref[0, kb] = blk
    for r0, full_ref, blk_ref in ((_R_K, ktf_ref, None), (_R_K + W_ATT, kts_ref, None),
                                  (_R_V, vtf_ref, vtbf_ref), (_R_V + W_ATT, vts_ref, vtbs_ref)):
        zt = _dot_nt(wc_ref[r0:r0 + W_ATT, :], xb)
        full_ref[0] = zt
        if blk_ref is not None:
            for kb, blk in enumerate(blocks(zt)):
                blk_ref[0, kb] = blk
    k_rows = _dot(xb, wr_ref[:, _C_K:_C_K + 2 * W_ATT])
    k_ref[...] = k_rows.astype(BF16)

    _gates(xb, wr_ref, gv_ref, bv_ref, u_ref, vn_ref)

    @pl.when(i == 0)
    def _():
        carry_col_ref[...] = jnp.zeros_like(carry_col_ref)
        carry_row_ref[...] = jnp.zeros_like(carry_row_ref)
        knmax_ref[...] = jnp.zeros_like(knmax_ref)

    r = lax.broadcasted_iota(jnp.int32, (W_ATT, LANES), 0)
    c = lax.broadcasted_iota(jnp.int32, (W_ATT, LANES), 1)
    head_sum = jnp.where(r // D_HEAD == c, 1.0, 0.0).astype(BF16)
    kf = k_rows[:, :W_ATT]
    kn2 = jnp.max(_dot((kf * kf).astype(BF16), head_sum), axis=0, keepdims=True)
    knmax_ref[0] = jnp.maximum(knmax_ref[0], jnp.broadcast_to(kn2, (SUBLANES, LANES)))


    lane = lax.broadcasted_iota(jnp.int32, (tm, N_PAIR * LANES), 1)
    z = _dot(xb, wr_ref[:, _C_FP:_C_FP + N_PAIR * LANES]) + bfp_ref[...]
    lf_col = jnp.where(lane % LANES < 2, _log_sigmoid(z), 0.0)
    c_col = _ones_dot3(lf_col, _tri_ones(tm, "lower"), left=True) + carry_col_ref[...]
    ccolp_ref[...] = c_col
    carry_col_ref[...] = c_col[tm - 1:tm, :]

    lf_row = _log_forget_rows(xb, wc_ref, bfr_ref)
    lfrow_ref[...] = lf_row
    c_row = _ones_dot3(lf_row, _tri_ones(tm, "upper"), left=False) + carry_row_ref[:, 0:1]
    for kb in range(tm // tk):
        crow_ref[kb] = c_row[:, kb * tk:(kb + 1) * tk]
    carry_row_ref[...] = jnp.broadcast_to(c_row[:, tm - 1:tm], carry_row_ref.shape)


def _in_proj_prompt(x, w_rows, w_cols, bfp, bfr, gv, bv, *, batch, seq):
    n, d = x.shape
    tm, tk = ROW_TILE, ATT_TILE
    nt = seq // tm
    row = lambda w: pl.BlockSpec((tm, w), lambda b, i: (b * nt + i, 0))
    t_spec = pl.BlockSpec((1, W_ATT, tm), lambda b, i: (b, 0, i))
    t_shape = jax.ShapeDtypeStruct((batch, W_ATT, seq), F32)
    blk_spec = lambda w: pl.BlockSpec((1, tm // tk, w, tk), lambda b, i: (b, i, 0, 0))
    blk_shape = lambda w: jax.ShapeDtypeStruct((batch, seq // tk, w, tk), BF16)
    out_shape = [blk_shape(2 * W_ATT), jax.ShapeDtypeStruct((n, 2 * W_ATT), BF16),
                 t_shape, t_shape, t_shape, t_shape, blk_shape(W_ATT), blk_shape(W_ATT),
                 jax.ShapeDtypeStruct((SUBLANES, n), F32),
                 jax.ShapeDtypeStruct((n // tk, SUBLANES, tk), F32),
                 jax.ShapeDtypeStruct((n, N_PAIR * LANES), F32),
                 jax.ShapeDtypeStruct((n, W_SGU), F32), jax.ShapeDtypeStruct((n, W_SGU), F32),
                 jax.ShapeDtypeStruct((batch, SUBLANES, LANES), F32)]
    out_specs = [blk_spec(2 * W_ATT), row(2 * W_ATT), t_spec, t_spec, t_spec, t_spec,
                 blk_spec(W_ATT), blk_spec(W_ATT),
                 pl.BlockSpec((SUBLANES, tm), lambda b, i: (0, b * nt + i)),
                 pl.BlockSpec((tm // tk, SUBLANES, tk), lambda b, i: (b * nt + i, 0, 0)),
                 row(N_PAIR * LANES), row(W_SGU), row(W_SGU),
                 pl.BlockSpec((1, SUBLANES, LANES), lambda b, i: (b, 0, 0))]
    return pl.pallas_call(
        _in_proj_prompt_kernel, grid=(batch, nt), name="in_proj_prompt",
        in_specs=[row(d)] + [_resident(a.shape) for a in (w_rows, w_cols, bfp, bfr, gv, bv)],
        out_specs=out_specs, out_shape=out_shape,
        scratch_shapes=[pltpu.VMEM((1, N_PAIR * LANES), F32), pltpu.VMEM((SUBLANES, LANES), F32)],
        compiler_params=_params(("parallel", "arbitrary"), 52),
    )(x, w_rows, w_cols, bfp, bfr, gv, bv)


def _in_proj_sample_kernel(x_ref, wr_ref, wc_ref, bfc_ref, bfr_ref, gv_ref, bv_ref,
                           q_ref, kb_ref, vb_ref, k_ref, v_ref, lfcol_ref, ccol_ref, crow_ref,
                           u_ref, vn_ref, *, dec_seq):
    tm = x_ref.shape[0]
    xb = x_ref[...].astype(BF16)
    scale = 1.0 / math.sqrt(D_HEAD)
    q_ref[...] = (_dot(xb, wr_ref[:, _C_Q:_C_Q + 2 * W_ATT]) * scale).astype(BF16)
    for c0, full_ref, half_ref in ((_C_K, k_ref, kb_ref), (_C_V, v_ref, vb_ref)):
        z = _dot(xb, wr_ref[:, c0:c0 + 2 * W_ATT])
        full_ref[...] = z
        half_ref[...] = z.astype(BF16)
    _gates(xb, wr_ref, gv_ref, bv_ref, u_ref, vn_ref)

    lane = lax.broadcasted_iota(jnp.int32, (tm, LANES), 1)
    z = _dot(xb, wr_ref[:, _C_F:_C_F + LANES]) + bfc_ref[...]
    lf_col = jnp.where(lane < H_FOX, _log_sigmoid(z), 0.0)
    lfcol_ref[...] = lf_col
    ccol_ref[...] = _ones_dot3(lf_col, _tri_ones(tm, "lower", dec_seq), left=True)
    lf_row = _log_forget_rows(xb, wc_ref, bfr_ref)
    c_row = _ones_dot3(lf_row, _tri_ones(tm, "upper", dec_seq), left=False)
    for s in range(tm // dec_seq):
        crow_ref[s] = c_row[:, s * dec_seq:(s + 1) * dec_seq]


def _in_proj_sample(x, w_rows, w_cols, bfc, bfr, gv, bv, *, dec_seq):
    n, d = x.shape
    tm = ROW_TILE
    row = lambda w: pl.BlockSpec((tm, w), lambda i: (i, 0))
    sds = lambda w, dt: jax.ShapeDtypeStruct((n, w), dt)
    out_shape = [sds(2 * W_ATT, BF16)] * 3 + [sds(2 * W_ATT, F32)] * 2 + [sds(LANES, F32)] * 2 + [
        jax.ShapeDtypeStruct((n // dec_seq, SUBLANES, dec_seq), F32), sds(W_SGU, F32), sds(W_SGU, F32)]
    out_specs = [row(2 * W_ATT)] * 5 + [row(LANES)] * 2 + [
        pl.BlockSpec((tm // dec_seq, SUBLANES, dec_seq), lambda i: (i, 0, 0)), row(W_SGU), row(W_SGU)]
    return pl.pallas_call(
        functools.partial(_in_proj_sample_kernel, dec_seq=dec_seq), grid=(n // tm,),
        name="in_proj_sample",
        in_specs=[row(d)] + [_resident(a.shape) for a in (w_rows, w_cols, bfc, bfr, gv, bv)],
        out_specs=out_specs, out_shape=out_shape,
        compiler_params=_params(("parallel",), 52),
    )(x, w_rows, w_cols, bfc, bfr, gv, bv)


def _split_head_rows(qt):
    sub = lax.broadcasted_iota(jnp.int32, qt.shape, 0)
    low = sub < D_HEAD
    zero = jnp.zeros_like(qt)
    return jnp.where(low, qt, zero), jnp.where(low, zero, qt)


def _fox_update_t(s, ct, pv, state, valid):
    m, l, acc = state
    if valid is not None:
        s = jnp.where(valid, s, NEG_INF)
    m_new = jnp.maximum(m, jnp.max(s, axis=0, keepdims=True) + ct)
    p = jnp.exp(s + (ct - m_new))
    alpha = jnp.exp(m - m_new)
    l = alpha * l + jnp.sum(p, axis=0, keepdims=True)
    acc = alpha * acc + pv(p.astype(BF16))
    return m_new, l, acc


def _sb_update_t(z, pv, suffix_ones, state, valid):
    carry, acc = state
    r = _softplus(z)
    if valid is not None:
        r = jnp.where(valid, r, 0.0)
    r_hi = r.astype(BF16)
    r_lo = (r - r_hi.astype(F32)).astype(BF16)
    after = _dot(suffix_ones, r_hi) + _dot(suffix_ones, r_lo) + carry
    a = jnp.exp((z - r) - after)
    if valid is not None:
        a = jnp.where(valid, a, 0.0)
    acc = acc + pv(a.astype(BF16))
    carry = carry + jnp.sum(r, axis=0, keepdims=True)
    return carry, acc


def _fox_prompt_kernel(qt_ref, k_ref, vt_ref, crow_ref, ccolp_ref, knmax_ref, o_ref):
    tq = qt_ref.shape[3]
    tk = vt_ref.shape[3]
    hp = pl.program_id(1)
    i = pl.program_id(2)
    q2 = jnp.concatenate(_split_head_rows(qt_ref[0, 0]), axis=1)
    c_q = crow_ref[i]
    ct = jnp.concatenate([_pick_row(c_q, 2 * hp), _pick_row(c_q, 2 * hp + 1)], axis=1)

    def block(j, state, valid):
        start = pl.multiple_of(j * tk, tk)
        kb = k_ref[pl.ds(start, tk), :]
        cc = ccolp_ref[pl.ds(start, tk), :]
        vt = vt_ref[0, j]
        ck = jnp.concatenate([jnp.broadcast_to(cc[:, 0:1], (tk, tq)),
                              jnp.broadcast_to(cc[:, 1:2], (tk, tq))], axis=1)

        def pv(p):
            return jnp.concatenate([_dot(vt[:D_HEAD], p[:, :tq]), _dot(vt[D_HEAD:], p[:, tq:])],
                                   axis=1)

        return _fox_update_t(_dot(kb, q2) - ck, ct, pv, state, valid)

    def pair_row(a, b):
        return jnp.concatenate([jnp.broadcast_to(a, (1, tq)), jnp.broadcast_to(b, (1, tq))], axis=1)

    qf = q2.astype(F32)
    q_norm = jnp.sqrt(jnp.sum(qf * qf, axis=0, keepdims=True))
    kn2 = knmax_ref[0][0:1, :]
    qk_bound = q_norm * jnp.sqrt(pair_row(_pick_lane(kn2, 2 * hp), _pick_lane(kn2, 2 * hp + 1))) * 1.02

    def reaches(j, m):
        c_end = crow_ref[jnp.maximum(j, 0)][:, tk - 1:tk]
        bound = qk_bound + (ct - pair_row(_pick_row(c_end, 2 * hp), _pick_row(c_end, 2 * hp + 1)))
        return jnp.max(bound - m) > -FOX_DEAD

    init = (jnp.full((1, 2 * tq), NEG_INF, F32), jnp.zeros((1, 2 * tq), F32),
            jnp.zeros((D_HEAD, 2 * tq), F32))
    key = lax.broadcasted_iota(jnp.int32, (tk, 2 * tq), 0)
    qry = lax.broadcasted_iota(jnp.int32, (tk, 2 * tq), 1) % tq
    state = block(i, init, key <= qry)

    def cond(c):
        j, go, _ = c
        return jnp.logical_and(j >= 0, go)

    def body(c):
        j, _, st = c
        st = block(j, st, None)
        return j - 1, reaches(j - 1, st[0]), st

    _, _, (_, l, acc) = lax.while_loop(cond, body, (i - 1, reaches(i - 1, state[0]), state))
    o = acc / l
    o_ref[...] = jnp.concatenate([o[:, :tq], o[:, tq:]], axis=0).T


def _sb_prompt_kernel(qt_ref, k_ref, vt_ref, o_ref):
    tq = qt_ref.shape[3]
    tk = vt_ref.shape[3]
    i = pl.program_id(2)
    q2 = jnp.concatenate(_split_head_rows(qt_ref[0, 0]), axis=1)
    suffix_ones = _tri_ones(tk, "strict_upper")

    def block(j, state, valid):
        kb = k_ref[pl.ds(pl.multiple_of(j * tk, tk), tk), :]
        vt = vt_ref[0, j]

        def pv(a):
            return jnp.concatenate([_dot(vt[:D_HEAD], a[:, :tq]), _dot(vt[D_HEAD:], a[:, tq:])],
                                   axis=1)

        return _sb_update_t(_dot(kb, q2), pv, suffix_ones, state, valid)

    def still_open(carry):
        return jnp.min(carry) < SB_CLOSED

    key = lax.broadcasted_iota(jnp.int32, (tk, 2 * tq), 0)
    qry = lax.broadcasted_iota(jnp.int32, (tk, 2 * tq), 1) % tq
    state = block(i, (jnp.zeros((1, 2 * tq), F32), jnp.zeros((D_HEAD, 2 * tq), F32)), key < qry)

    def cond(c):
        j, go, _ = c
        return jnp.logical_and(j >= 0, go)

    def body(c):
        j, _, st = c
        st = block(j, st, None)
        return j - 1, still_open(st[0]), st

    _, _, (_, acc) = lax.while_loop(cond, body, (i - 1, still_open(state[0]), state))
    o_ref[...] = jnp.concatenate([acc[:, :tq], acc[:, tq:]], axis=0).T


def _prompt_attention(qt, k, vtb, group, *, batch, seq, crow=None, ccolp=None, knmax=None):
    tq = tk = ATT_TILE
    nq = seq // tq
    col0 = group * N_PAIR
    qt_spec = pl.BlockSpec((1, 1, LANES, tq), lambda b, hp, i: (b, i, col0 + hp, 0))
    k_spec = pl.BlockSpec((seq, LANES), lambda b, hp, i: (b, col0 + hp))
    vt_spec = pl.BlockSpec((1, seq // tk, LANES, tk), lambda b, hp, i: (b, 0, hp, 0))
    out_spec = pl.BlockSpec((tq, LANES), lambda b, hp, i: (b * nq + i, hp))
    out_shape = jax.ShapeDtypeStruct((batch * seq, W_ATT), F32)
    params = _params(("parallel", "parallel", "arbitrary"), 40)
    grid = (batch, N_PAIR, nq)
    if group == 1:
        return pl.pallas_call(
            _sb_prompt_kernel, grid=grid, name="sb_prompt",
            in_specs=[qt_spec, k_spec, vt_spec], out_specs=out_spec, out_shape=out_shape,
            compiler_params=params)(qt, k, vtb)
    crow_spec = pl.BlockSpec((seq // tk, SUBLANES, tk), lambda b, hp, i: (b, 0, 0))
    ccolp_spec = pl.BlockSpec((seq, LANES), lambda b, hp, i: (b, hp))
    knmax_spec = pl.BlockSpec((1, SUBLANES, LANES), lambda b, hp, i: (b, 0, 0))
    return pl.pallas_call(
        _fox_prompt_kernel, grid=grid, name="fox_prompt",
        in_specs=[qt_spec, k_spec, vt_spec, crow_spec, ccolp_spec, knmax_spec],
        out_specs=out_spec, out_shape=out_shape,
        compiler_params=params)(qt, k, vtb, crow, ccolp, knmax)


def _suffix_sum_kernel(x_ref, o_ref):
    rows, width = x_ref.shape
    ones = _tri_ones(LANES, "strict_lower")
    carry = jnp.zeros((rows, 1), F32)
    for kb in reversed(range(width // LANES)):
        xb = x_ref[:, kb * LANES:(kb + 1) * LANES]
        o_ref[:, kb * LANES:(kb + 1) * LANES] = _ones_dot3(xb, ones, left=False) + carry
        carry = carry + jnp.sum(xb, axis=1, keepdims=True)


def _suffix_sum(x):
    return pl.pallas_call(
        _suffix_sum_kernel, name="cache_suffix_sum",
        out_shape=jax.ShapeDtypeStruct(x.shape, F32),
        compiler_params=_params(None, 32))(x)


def _split_heads(q2):
    lane = lax.broadcasted_iota(jnp.int32, q2.shape, 1)
    low = lane < D_HEAD
    zero = jnp.zeros_like(q2)
    return jnp.where(low, q2, zero), jnp.where(low, zero, q2)


def _stack_heads(q2):
    return jnp.concatenate(_split_heads(q2), axis=0)


def _unstack_heads(o):
    half = o.shape[0] // 2
    lane = lax.broadcasted_iota(jnp.int32, (half, o.shape[1]), 1)
    return jnp.where(lane < D_HEAD, o[:half], o[half:])


def _sb_update(z, pv, suffix_ones, state, valid):
    carry, acc = state
    r = _softplus(z)
    if valid is not None:
        r = jnp.where(valid, r, 0.0)
    r_hi = r.astype(BF16)
    r_lo = (r - r_hi.astype(F32)).astype(BF16)
    after = _dot(r_hi, suffix_ones) + _dot(r_lo, suffix_ones) + carry
    a = jnp.exp((z - r) - after)
    if valid is not None:
        a = jnp.where(valid, a, 0.0)
    acc = acc + pv(a.astype(BF16))
    carry = carry + jnp.sum(r, axis=1, keepdims=True)
    return carry, acc


def _sample_attn_kernel(qf_ref, kfn_ref, vfn_ref, kfc_ref, vfc_ref, drow_ref, cnrow_ref, ccol_ref,
                        qs_ref, ksn_ref, vsn_ref, ksc_ref, vsc_ref, of_ref, os_ref,
                        carry_ref, acc_ref, *, blk):
    b = pl.program_id(0)
    hp = pl.program_id(1)
    tq = qf_ref.shape[0]
    past = kfc_ref.shape[3]
    row = lax.broadcasted_iota(jnp.int32, (2 * tq, tq), 0) % tq
    col = lax.broadcasted_iota(jnp.int32, (2 * tq, tq), 1)

    def per_head(fn):
        return jnp.concatenate([fn(0), fn(1)], axis=0)

    q2 = _stack_heads(qf_ref[...])
    kct = kfc_ref[0, 0].astype(BF16)
    vct = vfc_ref[0, 0].astype(BF16)
    cc = ccol_ref[...]
    cn = cnrow_ref[0]
    cq = per_head(lambda h: _pick_lane(cc, 2 * hp + h))
    d_cache = per_head(lambda h: jnp.broadcast_to(
        _pick_row(drow_ref[0, 2 * hp + h], b % SUBLANES), (tq, past)))
    c_new = per_head(lambda h: jnp.broadcast_to(_pick_row(cn, 2 * hp + h), (tq, tq)))
    s_c = _dot(q2, kct) + d_cache
    s_n = jnp.where(col <= row, _dot_nt(q2, kfn_ref[...]) - c_new, NEG_INF)
    m = jnp.maximum(jnp.max(s_c, axis=1, keepdims=True), jnp.max(s_n, axis=1, keepdims=True)) + cq
    shift = cq - m
    p_c = jnp.exp(s_c + shift)
    p_n = jnp.exp(s_n + shift)
    l = jnp.sum(p_c, axis=1, keepdims=True) + jnp.sum(p_n, axis=1, keepdims=True)
    o = _dot_nt(p_c.astype(BF16), vct) + _dot(p_n.astype(BF16), vfn_ref[...])
    of_ref[...] = _unstack_heads(o / l)

    q2 = _stack_heads(qs_ref[...])
    vn = vsn_ref[...]
    state = (jnp.zeros((2 * tq, 1), F32), jnp.zeros((2 * tq, LANES), F32))
    carry_ref[...], acc_ref[...] = _sb_update(
        _dot_nt(q2, ksn_ref[...]), lambda a: _dot(a, vn), _tri_ones(tq, "strict_lower"), state,
        col < row)
    suffix_ones = _tri_ones(blk, "strict_lower")
    for jb in reversed(range(past // blk)):
        @pl.when(jnp.min(carry_ref[...]) < SB_CLOSED)
        def _():
            kt = ksc_ref[0, 0, :, jb * blk:(jb + 1) * blk].astype(BF16)
            vt = vsc_ref[0, 0, :, jb * blk:(jb + 1) * blk].astype(BF16)
            carry_ref[...], acc_ref[...] = _sb_update(
                _dot(q2, kt), lambda a: _dot_nt(a, vt), suffix_ones,
                (carry_ref[...], acc_ref[...]), None)
    os_ref[...] = _unstack_heads(acc_ref[...])


def _sample_attention(q, kb, vb, cache_fk, cache_fv, drow, cnrow, ccol, cache_sk, cache_sv,
                      *, layer, streams, dec_seq):
    past = cache_fk.shape[3]
    new_spec = lambda g: pl.BlockSpec((dec_seq, LANES), lambda b, hp: (b, g * N_PAIR + hp))
    cache_spec = pl.BlockSpec((1, 1, LANES, past), lambda b, hp: (layer, b, hp, 0))
    drow_spec = pl.BlockSpec((1, H_FOX, SUBLANES, past), lambda b, hp: (layer, 0, b // SUBLANES, 0))
    cnrow_spec = pl.BlockSpec((1, SUBLANES, dec_seq), lambda b, hp: (b, 0, 0))
    ccol_spec = pl.BlockSpec((dec_seq, LANES), lambda b, hp: (b, 0))
    out_spec = pl.BlockSpec((dec_seq, LANES), lambda b, hp: (b, hp))
    out_shape = [jax.ShapeDtypeStruct((streams * dec_seq, W_ATT), F32)] * 2
    return pl.pallas_call(
        functools.partial(_sample_attn_kernel, blk=ATT_TILE),
        grid=(streams, N_PAIR), name="sample_attention",
        in_specs=[new_spec(0), new_spec(0), new_spec(0), cache_spec, cache_spec, drow_spec,
                  cnrow_spec, ccol_spec, new_spec(1), new_spec(1), new_spec(1), cache_spec,
                  cache_spec],
        out_specs=[out_spec, out_spec], out_shape=out_shape,
        scratch_shapes=[pltpu.VMEM((2 * dec_seq, 1), F32), pltpu.VMEM((2 * dec_seq, LANES), F32)],
        compiler_params=_params(("parallel", "parallel"), 48),
    )(q, kb, vb, cache_fk, cache_fv, drow, cnrow, ccol, q, kb, vb, cache_sk, cache_sv)


def _merge_kernel(x_ref, of_ref, os_ref, u_ref, vn_ref, ws_ref, bs_ref, gmix_ref, wout_ref,
                  g_ref, b_ref, o_ref, *, alpha):
    tm = x_ref.shape[0]
    r = lax.broadcasted_iota(jnp.int32, (SGU_CHUNK, SGU_CHUNK), 0)
    c = lax.broadcasted_iota(jnp.int32, (SGU_CHUNK, SGU_CHUNK), 1)
    tril = r >= c
    head_mean = jnp.where(r // D_HEAD == c // D_HEAD, 1.0 / D_HEAD, 0.0).astype(BF16)
    lane = lax.broadcasted_iota(jnp.int32, (SGU_CHUNK, LANES), 1)

    gate_pieces = []
    for p in range(W_SGU // LANES):
        w0 = jnp.where(tril, ws_ref[2 * p], 0.0).astype(BF16)
        w1 = jnp.where(tril, ws_ref[2 * p + 1], 0.0).astype(BF16)
        bias = bs_ref[:, p * LANES:(p + 1) * LANES]
        chunks = []
        for ch in range(tm // SGU_CHUNK):
            rows = slice(ch * SGU_CHUNK, (ch + 1) * SGU_CHUNK)
            vp = vn_ref[rows, p * LANES:(p + 1) * LANES].astype(BF16)
            s = jnp.where(lane < D_HEAD, _dot(w0, vp), _dot(w1, vp)) + bias
            chunks.append(u_ref[rows, p * LANES:(p + 1) * LANES] * s)
        gate_pieces.append(jnp.concatenate(chunks, axis=0))

    pieces = ([of_ref[:, k * LANES:(k + 1) * LANES] for k in range(N_PAIR)] + gate_pieces
              + [os_ref[:, k * LANES:(k + 1) * LANES] for k in range(N_PAIR)])
    normed = []
    for k, o in enumerate(pieces):
        ms = _dot((o * o).astype(BF16), head_mean)
        gain = gmix_ref[:, k * LANES:(k + 1) * LANES]
        normed.append((o * lax.rsqrt(ms + RMS_EPS) * gain).astype(BF16))
    mix = _dot(jnp.concatenate(normed, axis=1), wout_ref[...])
    o_ref[...] = _layer_norm(alpha * x_ref[...] + mix, g_ref[...], b_ref[...])


def _merge(x, of, os_, u, vn, ws, bs, gmix, wout, g, b, *, alpha, name):
    n, d = x.shape
    tm = ROW_TILE
    row = lambda w: pl.BlockSpec((tm, w), lambda i: (i, 0))
    return pl.pallas_call(
        functools.partial(_merge_kernel, alpha=alpha), grid=(n // tm,), name=name,
        in_specs=[row(d), row(W_ATT), row(W_ATT), row(W_SGU), row(W_SGU)]
        + [_resident(a.shape) for a in (ws, bs, gmix, wout, g, b)],
        out_specs=row(d), out_shape=jax.ShapeDtypeStruct((n, d), F32),
        compiler_params=_params(("parallel",), 48),
    )(x, of, os_, u, vn, ws, bs, gmix, wout, g, b)


def _mlp_kernel(x_ref, wup_ref, wdn_ref, g_ref, b_ref, o_ref, *, alpha, ff_chunk):
    x = x_ref[...]
    xb = x.astype(BF16)
    acc = alpha * x
    for c0 in range(0, wup_ref.shape[1], ff_chunk):
        h = jnp.maximum(_dot(xb, wup_ref[:, c0:c0 + ff_chunk]), 0.0)
        acc = acc + _dot((h * h).astype(BF16), wdn_ref[c0:c0 + ff_chunk, :])
    o_ref[...] = _layer_norm(acc, g_ref[...], b_ref[...])


def _mlp(x, wup, wdn, g, b, *, alpha, name):
    n, d = x.shape
    tm = ROW_TILE
    row = pl.BlockSpec((tm, d), lambda i: (i, 0))
    return pl.pallas_call(
        functools.partial(_mlp_kernel, alpha=alpha, ff_chunk=1024), grid=(n // tm,), name=name,
        in_specs=[row] + [_resident(a.shape) for a in (wup, wdn, g, b)],
        out_specs=row, out_shape=jax.ShapeDtypeStruct((n, d), F32),
        compiler_params=_params(("parallel",), 56),
    )(x, wup, wdn, g, b)


def _pack_w_in(w, b_f):
    sizes = [W_ATT, W_ATT, W_ATT, H_FOX, W_SGU, W_SGU, W_ATT, W_ATT, W_ATT]
    offs = [0]
    for s in sizes:
        offs.append(offs[-1] + s)
    q_f, k_f, v_f, f_lg, u_g, v_g, q_s, k_s, v_s = (w[:, offs[i]:offs[i + 1]] for i in range(9))
    pad_cols = lambda a, width: jnp.pad(a, ((0, 0), (0, width - a.shape[1])))
    f_pairs = [pad_cols(f_lg[:, 2 * p:2 * p + 2], LANES) for p in range(N_PAIR)]
    w_rows = jnp.concatenate([q_f, q_s, k_f, k_s, v_f, v_s, u_g, v_g, pad_cols(f_lg, LANES)]
                             + f_pairs, axis=1).astype(BF16)
    w_cols = jnp.concatenate([q_f, q_s, k_f, k_s, v_f, v_s, pad_cols(f_lg, F_ROWS)],
                             axis=1).T.astype(BF16)
    bfc = jnp.pad(b_f, (0, LANES - H_FOX)).reshape(1, LANES)
    bfp = jnp.concatenate([jnp.pad(b_f[2 * p:2 * p + 2], (0, LANES - 2)) for p in range(N_PAIR)]
                          ).reshape(1, N_PAIR * LANES)
    bfr = jnp.broadcast_to(jnp.pad(b_f, (0, SUBLANES - H_FOX)).reshape(SUBLANES, 1),
                           (SUBLANES, LANES))
    return w_rows, w_cols, bfc, bfp, bfr


def kernel(x_prompt, x_sample, cache_fox_k, cache_fox_v, cache_fox_logf, cache_sb_k, cache_sb_v,
           w_in, b_f, g_v, b_v, w_s, b_s, g_mix, w_out, ln1_g, ln1_b, w_up, w_down, ln2_g, ln2_b):
    depth = w_in.shape[0]
    batch, seq, d_model = x_prompt.shape
    streams, dec_seq, _ = x_sample.shape
    past = cache_fox_k.shape[2]
    n_prompt = batch * seq
    n_sample = streams * dec_seq
    alpha = (2 * depth) ** 0.25
    assert seq % ROW_TILE == 0 and n_sample % ROW_TILE == 0 and ROW_TILE % dec_seq == 0
    assert dec_seq == D_HEAD and 2 * dec_seq == SGU_CHUNK and past % ATT_TILE == 0
    assert streams % SUBLANES == 0

    xp = x_prompt.reshape(n_prompt, d_model)
    xs = x_sample.reshape(n_sample, d_model)
    row2 = lambda a: a.reshape(1, -1)
    cache_t = lambda c: jnp.transpose(c, (0, 1, 3, 4, 2)).reshape(depth, streams, W_ATT, past)
    ck_f, cv_f, ck_s, cv_s = (cache_t(c) for c in (cache_fox_k, cache_fox_v, cache_sb_k, cache_sb_v))
    lf_cache = jnp.transpose(cache_fox_logf.astype(F32), (0, 3, 1, 2))
    drow = _suffix_sum(lf_cache.reshape(depth * H_FOX * streams, past)).reshape(
        depth, H_FOX, streams, past)

    outs = [[] for _ in range(11)]
    for l in range(depth):
        w_rows, w_cols, bfc, bfp, bfr = _pack_w_in(w_in[l], b_f[l])
        gv, bv = row2(g_v[l]), row2(b_v[l])
        wout = w_out[l].astype(BF16)
        wup, wdn = w_up[l].astype(BF16), w_down[l].astype(BF16)
        gmix, g1, b1, g2, b2 = (row2(a[l]) for a in (g_mix, ln1_g, ln1_b, ln2_g, ln2_b))

        w_l = w_s[l]
        half = w_l[:, :dec_seq, :dec_seq]
        zeros = jnp.zeros_like(half)
        w_blockdiag = jnp.concatenate([jnp.concatenate([half, zeros], axis=2),
                                       jnp.concatenate([zeros, half], axis=2)], axis=1)
        bias_full = jnp.repeat(b_s[l].T, D_HEAD, axis=1)
        bias_half = jnp.concatenate([bias_full[:dec_seq], bias_full[:dec_seq]], axis=0)

        (qt, kb, ktf, kts, vtf, vts, vtbf, vtbs, lfrow, crow, ccolp, u, vn, knmax) = _in_proj_prompt(
            xp, w_rows, w_cols, bfp, bfr, gv, bv, batch=batch, seq=seq)
        of = _prompt_attention(qt, kb, vtbf, 0, batch=batch, seq=seq, crow=crow, ccolp=ccolp,
                               knmax=knmax)
        os_ = _prompt_attention(qt, kb, vtbs, 1, batch=batch, seq=seq)
        x1 = _merge(xp, of, os_, u, vn, w_l, bias_full, gmix, wout, g1, b1, alpha=alpha,
                    name="merge_prompt")
        xp = _mlp(x1, wup, wdn, g2, b2, alpha=alpha, name="mlp_prompt")
        heads_t = lambda a: jnp.transpose(a.reshape(batch, H_FOX, D_HEAD, seq), (0, 3, 1, 2))
        p_out = (heads_t(ktf), heads_t(vtf),
                 jnp.transpose(lfrow[:H_FOX].reshape(H_FOX, batch, seq), (1, 2, 0)),
                 heads_t(kts), heads_t(vts))

        (q, kb, vb, k, v, lfcol, ccol, cnrow, u, vn) = _in_proj_sample(
            xs, w_rows, w_cols, bfc, bfr, gv, bv, dec_seq=dec_seq)
        of, os_ = _sample_attention(q, kb, vb, ck_f, cv_f, drow, cnrow, ccol, ck_s, cv_s,
                                    layer=l, streams=streams, dec_seq=dec_seq)
        x1 = _merge(xs, of, os_, u, vn, w_blockdiag, bias_half, gmix, wout, g1, b1, alpha=alpha,
                    name="merge_sample")
        xs = _mlp(x1, wup, wdn, g2, b2, alpha=alpha, name="mlp_sample")
        heads = lambda a, g: a[:, g * W_ATT:(g + 1) * W_ATT].reshape(streams, dec_seq, H_FOX, D_HEAD)
        s_out = (heads(k, 0), heads(v, 0), lfcol[:, :H_FOX].reshape(streams, dec_seq, H_FOX),
                 heads(k, 1), heads(v, 1), vn.reshape(streams, dec_seq, W_SGU))

        for acc, val in zip(outs, p_out + s_out):
            acc.append(val)

    return (xp.reshape(batch, seq, d_model), xs.reshape(streams, dec_seq, d_model),
            *(jnp.stack(o) for o in outs))
```

```python
import functools
import math

import jax
import jax.numpy as jnp
from jax import lax
from jax.experimental import pallas as pl
from jax.experimental.pallas import tpu as pltpu

F32 = jnp.float32
BF16 = jnp.bfloat16

LANES = 128
SUBLANES = 8
VMEM_BYTES_V7X = 64 * 1024 * 1024

D_HEAD = 64
H_FOX = 6
H_SB = 6
H_SGU = 4
W_ATT = H_FOX * D_HEAD
W_SGU = H_SGU * D_HEAD
N_PAIR = W_ATT // LANES
SGU_CHUNK = 128
LN_EPS = 1e-5
RMS_EPS = 1e-6
NEG_INF = -1e30
SB_CLOSED = 110.0
FOX_DEAD = 110.0
FOX_L_FLOOR = 1e-25
FOX_WINDOW = 5
SB_WINDOW = 2

ROW_TILE = 512
ATT_TILE = 256

_C_Q = 0
_C_K = 2 * W_ATT
_C_V = 4 * W_ATT
_C_G = 6 * W_ATT
_C_F = _C_G + 2 * W_SGU
_C_FP = _C_F + LANES
W_ROWS = _C_FP + N_PAIR * LANES
_R_Q = 0
_R_K = 2 * W_ATT
_R_V = 4 * W_ATT
_R_F = 6 * W_ATT
F_ROWS = 2 * SUBLANES
W_COLS = _R_F + F_ROWS


def _params(semantics, vmem_mib, flags=None):
    assert vmem_mib * 1024 * 1024 < VMEM_BYTES_V7X
    return pltpu.CompilerParams(dimension_semantics=semantics,
                                vmem_limit_bytes=vmem_mib * 1024 * 1024, flags=flags)


def _resident(shape):
    nd = len(shape)
    return pl.BlockSpec(shape, lambda *_: (0,) * nd, pipeline_mode=pl.Buffered(1))


def _split3(x):
    hi = x.astype(BF16)
    r1 = x - hi.astype(F32)
    mid = r1.astype(BF16)
    lo = (r1 - mid.astype(F32)).astype(BF16)
    return hi, mid, lo


def _dot(a, b):
    return jnp.dot(a, b, preferred_element_type=F32)


def _dot_nt(a, b):
    return lax.dot_general(a, b, (((1,), (1,)), ((), ())), preferred_element_type=F32)


def _ones_dot3(x, ones_mat, left):
    if left:
        return sum(_dot(ones_mat, p) for p in _split3(x))
    return sum(_dot(p, ones_mat) for p in _split3(x))


def _gelu_tanh(x):
    return 0.5 * x * (1.0 + jnp.tanh(0.7978845608028654 * (x + 0.044715 * (x * x * x))))


def _log_sigmoid(x):
    return jnp.minimum(x, 0.0) - jnp.log1p(jnp.exp(-jnp.abs(x)))


def _softplus(z):
    return jnp.maximum(z, 0.0) + jnp.log(1.0 + jnp.exp(-jnp.abs(z)))


def _layer_norm(x, g, b):
    mu = jnp.mean(x, axis=-1, keepdims=True)
    xc = x - mu
    var = jnp.mean(xc * xc, axis=-1, keepdims=True)
    return xc * lax.rsqrt(var + LN_EPS) * g + b


def _tri_ones(n, kind, seg=None):
    r = lax.broadcasted_iota(jnp.int32, (n, n), 0)
    c = lax.broadcasted_iota(jnp.int32, (n, n), 1)
    keep = {"lower": c <= r, "upper": r <= c, "strict_lower": r > c, "strict_upper": r < c}[kind]
    if seg is not None:
        keep = jnp.logical_and(keep, r // seg == c // seg)
    return jnp.where(keep, 1.0, 0.0).astype(BF16)


def _pick_lane(x, idx):
    lane = lax.broadcasted_iota(jnp.int32, x.shape, 1)
    return jnp.sum(jnp.where(lane == idx, x, 0.0), axis=1, keepdims=True)


def _pick_row(x, idx):
    sub = lax.broadcasted_iota(jnp.int32, x.shape, 0)
    return jnp.sum(jnp.where(sub == idx, x, 0.0), axis=0, keepdims=True)


def _gates(xb, wr_ref, gv_ref, bv_ref, u_ref, vn_ref):
    u_ref[...] = _gelu_tanh(_dot(xb, wr_ref[:, _C_G:_C_G + W_SGU]))
    vg = _dot(xb, wr_ref[:, _C_G + W_SGU:_C_G + 2 * W_SGU])
    vn_ref[...] = _layer_norm(_gelu_tanh(vg), gv_ref[...], bv_ref[...])


def _log_forget_rows(xb, wc_ref, bfr_ref):
    sub = lax.broadcasted_iota(jnp.int32, (SUBLANES, xb.shape[0]), 0)
    z = _dot_nt(wc_ref[_R_F:_R_F + F_ROWS, :], xb)[:SUBLANES] + bfr_ref[:, 0:1]
    return jnp.where(sub < H_FOX, _log_sigmoid(z), 0.0)


def _in_proj_prompt_kernel(x_ref, wr_ref, wc_ref, bfp_ref, bfr_ref, gv_ref, bv_ref,
                           qt_ref, k_ref, ktf_ref, kts_ref, vtf_ref, vts_ref, vtbf_ref, vtbs_ref,
                           lfrow_ref, crow_ref, ccolp_ref, u_ref, vn_ref, knmax_ref,
                           carry_col_ref, carry_row_ref):
    i = pl.program_id(1)
    tm = x_ref.shape[0]
    tk = vtbf_ref.shape[3]
    xb = x_ref[...].astype(BF16)

    def blocks(zt):
        return [zt[:, kb * tk:(kb + 1) * tk].astype(BF16) for kb in range(tm // tk)]

    scale = 1.0 / math.sqrt(D_HEAD)
    for kb, blk in enumerate(blocks(_dot_nt(wc_ref[_R_Q:_R_Q + 2 * W_ATT, :], xb) * scale)):
        qt_ref[0, kb] = blk
    for r0, full_ref, blk_ref in ((_R_K, ktf_ref, None), (_R_K + W_ATT, kts_ref, None),
                                  (_R_V, vtf_ref, vtbf_ref), (_R_V + W_ATT, vts_ref, vtbs_ref)):
        zt = _dot_nt(wc_ref[r0:r0 + W_ATT, :], xb)
        full_ref[0] = zt
        if blk_ref is not None:
            for kb, blk in enumerate(blocks(zt)):
                blk_ref[0, kb] = blk
    k_rows = _dot(xb, wr_ref[:, _C_K:_C_K + 2 * W_ATT])
    k_ref[...] = k_rows.astype(BF16)

    _gates(xb, wr_ref, gv_ref, bv_ref, u_ref, vn_ref)

    @pl.when(i == 0)
    def _():
        carry_col_ref[...] = jnp.zeros_like(carry_col_ref)
        carry_row_ref[...] = jnp.zeros_like(carry_row_ref)
        knmax_ref[...] = jnp.zeros_like(knmax_ref)

    r = lax.broadcasted_iota(jnp.int32, (W_ATT, LANES), 0)
    c = lax.broadcasted_iota(jnp.int32, (W_ATT, LANES), 1)
    head_sum = jnp.where(r // D_HEAD == c, 1.0, 0.0).astype(BF16)
    kf = k_rows[:, :W_ATT]
    kn2 = jnp.max(_dot((kf * kf).astype(BF16), head_sum), axis=0, keepdims=True)
    knmax_ref[0] = jnp.maximum(knmax_ref[0], jnp.broadcast_to(kn2, (SUBLANES, LANES)))


    lane = lax.broadcasted_iota(jnp.int32, (tm, N_PAIR * LANES), 1)
    z = _dot(xb, wr_ref[:, _C_FP:_C_FP + N_PAIR * LANES]) + bfp_ref[...]
    lf_col = jnp.where(lane % LANES < 2, _log_sigmoid(z), 0.0)
    c_col = _ones_dot3(lf_col, _tri_ones(tm, "lower"), left=True) + carry_col_ref[...]
    ccolp_ref[...] = c_col
    carry_col_ref[...] = c_col[tm - 1:tm, :]

    lf_row = _log_forget_rows(xb, wc_ref, bfr_ref)
    lfrow_ref[...] = lf_row
    c_row = _ones_dot3(lf_row, _tri_ones(tm, "upper"), left=False) + carry_row_ref[:, 0:1]
    for kb in range(tm // tk):
        crow_ref[kb] = c_row[:, kb * tk:(kb + 1) * tk]
    carry_row_ref[...] = jnp.broadcast_to(c_row[:, tm - 1:tm], carry_row_ref.shape)


def _in_proj_prompt(x, w_rows, w_cols, bfp, bfr, gv, bv, *, batch, seq):
    n, d = x.shape
    tm, tk = ROW_TILE, ATT_TILE
    nt = seq // tm
    row = lambda w: pl.BlockSpec((tm, w), lambda b, i: (b * nt + i, 0))
    t_spec = pl.BlockSpec((1, W_ATT, tm), lambda b, i: (b, 0, i))
    t_shape = jax.ShapeDtypeStruct((batch, W_ATT, seq), F32)
    blk_spec = lambda w: pl.BlockSpec((1, tm // tk, w, tk), lambda b, i: (b, i, 0, 0))
    blk_shape = lambda w: jax.ShapeDtypeStruct((batch, seq // tk, w, tk), BF16)
    out_shape = [blk_shape(2 * W_ATT), jax.ShapeDtypeStruct((n, 2 * W_ATT), BF16),
                 t_shape, t_shape, t_shape, t_shape, blk_shape(W_ATT), blk_shape(W_ATT),
                 jax.ShapeDtypeStruct((SUBLANES, n), F32),
                 jax.ShapeDtypeStruct((n // tk, SUBLANES, tk), F32),
                 jax.ShapeDtypeStruct((n, N_PAIR * LANES), F32),
                 jax.ShapeDtypeStruct((n, W_SGU), F32), jax.ShapeDtypeStruct((n, W_SGU), F32),
                 jax.ShapeDtypeStruct((batch, SUBLANES, LANES), F32)]
    out_specs = [blk_spec(2 * W_ATT), row(2 * W_ATT), t_spec, t_spec, t_spec, t_spec,
                 blk_spec(W_ATT), blk_spec(W_ATT),
                 pl.BlockSpec((SUBLANES, tm), lambda b, i: (0, b * nt + i)),
                 pl.BlockSpec((tm // tk, SUBLANES, tk), lambda b, i: (b * nt + i, 0, 0)),
                 row(N_PAIR * LANES), row(W_SGU), row(W_SGU),
                 pl.BlockSpec((1, SUBLANES, LANES), lambda b, i: (b, 0, 0))]
    return pl.pallas_call(
        _in_proj_prompt_kernel, grid=(batch, nt), name="in_proj_prompt",
        in_specs=[row(d)] + [_resident(a.shape) for a in (w_rows, w_cols, bfp, bfr, gv, bv)],
        out_specs=out_specs, out_shape=out_shape,
        scratch_shapes=[pltpu.VMEM((1, N_PAIR * LANES), F32), pltpu.VMEM((SUBLANES, LANES), F32)],
        compiler_params=_params(("parallel", "arbitrary"), 52),
    )(x, w_rows, w_cols, bfp, bfr, gv, bv)


def _in_proj_sample_kernel(x_ref, wr_ref, wc_ref, bfc_ref, bfr_ref, gv_ref, bv_ref,
                           q_ref, kb_ref, vb_ref, k_ref, v_ref, lfcol_ref, ccol_ref, crow_ref,
                           u_ref, vn_ref, *, dec_seq):
    tm = x_ref.shape[0]
    xb = x_ref[...].astype(BF16)
    scale = 1.0 / math.sqrt(D_HEAD)
    q_ref[...] = (_dot(xb, wr_ref[:, _C_Q:_C_Q + 2 * W_ATT]) * scale).astype(BF16)
    for c0, full_ref, half_ref in ((_C_K, k_ref, kb_ref), (_C_V, v_ref, vb_ref)):
        z = _dot(xb, wr_ref[:, c0:c0 + 2 * W_ATT])
        full_ref[...] = z
        half_ref[...] = z.astype(BF16)
    _gates(xb, wr_ref, gv_ref, bv_ref, u_ref, vn_ref)

    lane = lax.broadcasted_iota(jnp.int32, (tm, LANES), 1)
    z = _dot(xb, wr_ref[:, _C_F:_C_F + LANES]) + bfc_ref[...]
    lf_col = jnp.where(lane < H_FOX, _log_sigmoid(z), 0.0)
    lfcol_ref[...] = lf_col
    ccol_ref[...] = _ones_dot3(lf_col, _tri_ones(tm, "lower", dec_seq), left=True)
    lf_row = _log_forget_rows(xb, wc_ref, bfr_ref)
    c_row = _ones_dot3(lf_row, _tri_ones(tm, "upper", dec_seq), left=False)
    for s in range(tm // dec_seq):
        crow_ref[s] = c_row[:, s * dec_seq:(s + 1) * dec_seq]


def _in_proj_sample(x, w_rows, w_cols, bfc, bfr, gv, bv, *, dec_seq):
    n, d = x.shape
    tm = ROW_TILE
    row = lambda w: pl.BlockSpec((tm, w), lambda i: (i, 0))
    sds = lambda w, dt: jax.ShapeDtypeStruct((n, w), dt)
    out_shape = [sds(2 * W_ATT, BF16)] * 3 + [sds(2 * W_ATT, F32)] * 2 + [sds(LANES, F32)] * 2 + [
        jax.ShapeDtypeStruct((n // dec_seq, SUBLANES, dec_seq), F32), sds(W_SGU, F32), sds(W_SGU, F32)]
    out_specs = [row(2 * W_ATT)] * 5 + [row(LANES)] * 2 + [
        pl.BlockSpec((tm // dec_seq, SUBLANES, dec_seq), lambda i: (i, 0, 0)), row(W_SGU), row(W_SGU)]
    return pl.pallas_call(
        functools.partial(_in_proj_sample_kernel, dec_seq=dec_seq), grid=(n // tm,),
        name="in_proj_sample",
        in_specs=[row(d)] + [_resident(a.shape) for a in (w_rows, w_cols, bfc, bfr, gv, bv)],
        out_specs=out_specs, out_shape=out_shape,
        compiler_params=_params(("parallel",), 52),
    )(x, w_rows, w_cols, bfc, bfr, gv, bv)


def _split_head_rows(qt):
    sub = lax.broadcasted_iota(jnp.int32, qt.shape, 0)
    low = sub < D_HEAD
    zero = jnp.zeros_like(qt)
    return jnp.where(low, qt, zero), jnp.where(low, zero, qt)


def _fox_update_t(s, ct, pv, state, valid):
    m, l, acc = state
    if valid is not None:
        s = jnp.where(valid, s, NEG_INF)
    m_new = jnp.maximum(m, jnp.max(s, axis=0, keepdims=True) + ct)
    p = jnp.exp(s + (ct - m_new))
    alpha = jnp.exp(m - m_new)
    l = alpha * l + jnp.sum(p, axis=0, keepdims=True)
    acc = alpha * acc + pv(p.astype(BF16))
    return m_new, l, acc


def _sb_update_t(z, pv, suffix_ones, state, valid):
    carry, acc = state
    r = _softplus(z)
    if valid is not None:
        r = jnp.where(valid, r, 0.0)
    r_hi = r.astype(BF16)
    r_lo = (r - r_hi.astype(F32)).astype(BF16)
    after = _dot(suffix_ones, r_hi) + _dot(suffix_ones, r_lo) + carry
    a = jnp.exp((z - r) - after)
    if valid is not None:
        a = jnp.where(valid, a, 0.0)
    acc = acc + pv(a.astype(BF16))
    carry = carry + jnp.sum(r, axis=0, keepdims=True)
    return carry, acc


def _fox_prompt_kernel(qt_ref, k_ref, vt_ref, crow_ref, ccolp_ref, knmax_ref, o_ref, *,
                       window_blocks):
    tq = qt_ref.shape[3]
    tk = vt_ref.shape[3]
    hp = pl.program_id(1)
    i = pl.program_id(2)
    q2 = jnp.concatenate(_split_head_rows(qt_ref[0, 0]), axis=1)
    c_q = crow_ref[i]
    ct = jnp.concatenate([_pick_row(c_q, 2 * hp), _pick_row(c_q, 2 * hp + 1)], axis=1)

    key = lax.broadcasted_iota(jnp.int32, (tk, 2 * tq), 0)
    qry = lax.broadcasted_iota(jnp.int32, (tk, 2 * tq), 1) % tq
    causal = key <= qry

    def scores(j0, nblk):
        start = pl.multiple_of(j0 * tk, tk)
        kb = k_ref[pl.ds(start, nblk * tk), :]
        cc = ccolp_ref[pl.ds(start, nblk * tk), :]
        ck = jnp.concatenate([jnp.broadcast_to(cc[:, 0:1], (nblk * tk, tq)),
                              jnp.broadcast_to(cc[:, 1:2], (nblk * tk, tq))], axis=1)
        return _dot(kb, q2) - ck

    def pv(p, j0, nblk):
        vts = [vt_ref[0, j0 + w] for w in range(nblk)]
        outs = []
        for h in range(2):
            vth = jnp.concatenate([v[h * D_HEAD:(h + 1) * D_HEAD] for v in vts], axis=1)
            outs.append(_dot(vth, p[:, h * tq:(h + 1) * tq]))
        return jnp.concatenate(outs, axis=1)

    def pair_row(a, b):
        return jnp.concatenate([jnp.broadcast_to(a, (1, tq)), jnp.broadcast_to(b, (1, tq))], axis=1)

    qf = q2.astype(F32)
    q_norm = jnp.sqrt(jnp.sum(qf * qf, axis=0, keepdims=True))
    kn2 = knmax_ref[0][0:1, :]
    qk_bound = q_norm * jnp.sqrt(pair_row(_pick_lane(kn2, 2 * hp), _pick_lane(kn2, 2 * hp + 1))) * 1.02

    shift = ct - qk_bound

    def reaches(j):
        c_end = crow_ref[jnp.maximum(j, 0)][:, tk - 1:tk]
        decay = ct - pair_row(_pick_row(c_end, 2 * hp), _pick_row(c_end, 2 * hp + 1))
        return jnp.max(decay) > -FOX_DEAD

    def window(nblk):
        j0 = i - (nblk - 1)
        p = jnp.exp(scores(j0, nblk) + shift)
        body_rows = (nblk - 1) * tk
        diag = jnp.where(causal, p[body_rows:], 0.0)
        p = diag if nblk == 1 else jnp.concatenate([p[:body_rows], diag], axis=0)
        return (j0 - 1, jnp.sum(p, axis=0, keepdims=True), pv(p.astype(BF16), j0, nblk))

    j, l, acc = lax.cond(i >= window_blocks - 1, lambda: window(window_blocks), lambda: window(1))

    def cond(c):
        j, go, _, _ = c
        return jnp.logical_and(j >= 0, go)

    def body(c):
        j, _, l, acc = c
        p = jnp.exp(scores(j, 1) + shift)
        return (j - 1, reaches(j - 1), l + jnp.sum(p, axis=0, keepdims=True),
                acc + pv(p.astype(BF16), j, 1))

    _, _, l, acc = lax.while_loop(cond, body, (j, reaches(j), l, acc))

    def exact():
        def step(t, state):
            jj = i - 1 - t
            return _fox_update_t(scores(jj, 1), ct, lambda p: pv(p, jj, 1), state, None)

        init = (jnp.full((1, 2 * tq), NEG_INF, F32), jnp.zeros((1, 2 * tq), F32),
                jnp.zeros((D_HEAD, 2 * tq), F32))
        state = _fox_update_t(scores(i, 1), ct, lambda p: pv(p, i, 1), init, causal)
        _, l_x, acc_x = lax.fori_loop(0, i, step, state)
        return l_x, acc_x

    l, acc = lax.cond(jnp.min(l) < FOX_L_FLOOR, exact, lambda: (l, acc))
    o = acc / l
    o_ref[...] = jnp.concatenate([o[:, :tq], o[:, tq:]], axis=0).T


def _sb_prompt_kernel(qt_ref, k_ref, vt_ref, o_ref, *, window_blocks):
    tq = qt_ref.shape[3]
    tk = vt_ref.shape[3]
    i = pl.program_id(2)
    q2 = jnp.concatenate(_split_head_rows(qt_ref[0, 0]), axis=1)
    suffix_ones = _tri_ones(tk, "strict_upper")

    def block(j, state, valid):
        kb = k_ref[pl.ds(pl.multiple_of(j * tk, tk), tk), :]
        vt = vt_ref[0, j]

        def pv(a):
            return jnp.concatenate([_dot(vt[:D_HEAD], a[:, :tq]), _dot(vt[D_HEAD:], a[:, tq:])],
                                   axis=1)

        return _sb_update_t(_dot(kb, q2), pv, suffix_ones, state, valid)

    def still_open(carry):
        return jnp.min(carry) < SB_CLOSED

    key = lax.broadcasted_iota(jnp.int32, (tk, 2 * tq), 0)
    qry = lax.broadcasted_iota(jnp.int32, (tk, 2 * tq), 1) % tq
    causal = key < qry

    def window(nblk):
        j0 = i - (nblk - 1)
        z = _dot(k_ref[pl.ds(pl.multiple_of(j0 * tk, tk), nblk * tk), :], q2)
        r = _softplus(z)
        carry = jnp.zeros((1, 2 * tq), F32)
        a_blocks = [None] * nblk
        for w in reversed(range(nblk)):
            rows = slice(w * tk, (w + 1) * tk)
            r_w = jnp.where(causal, r[rows], 0.0) if w == nblk - 1 else r[rows]
            r_hi = r_w.astype(BF16)
            r_lo = (r_w - r_hi.astype(F32)).astype(BF16)
            after = _dot(suffix_ones, r_hi) + _dot(suffix_ones, r_lo) + carry
            a = jnp.exp((z[rows] - r_w) - after)
            a_blocks[w] = (jnp.where(causal, a, 0.0) if w == nblk - 1 else a).astype(BF16)
            carry = carry + jnp.sum(r_w, axis=0, keepdims=True)
        a = a_blocks[0] if nblk == 1 else jnp.concatenate(a_blocks, axis=0)
        vts = [vt_ref[0, j0 + w] for w in range(nblk)]
        outs = []
        for h in range(2):
            vth = jnp.concatenate([v[h * D_HEAD:(h + 1) * D_HEAD] for v in vts], axis=1)
            outs.append(_dot(vth, a[:, h * tq:(h + 1) * tq]))
        return j0 - 1, (carry, jnp.concatenate(outs, axis=1))

    j, state = lax.cond(i >= window_blocks - 1, lambda: window(window_blocks), lambda: window(1))

    def cond(c):
        j, go, _ = c
        return jnp.logical_and(j >= 0, go)

    def body(c):
        j, _, st = c
        st = block(j, st, None)
        return j - 1, still_open(st[0]), st

    _, _, (_, acc) = lax.while_loop(cond, body, (j, still_open(state[0]), state))
    o_ref[...] = jnp.concatenate([acc[:, :tq], acc[:, tq:]], axis=0).T


def _prompt_attention(qt, k, vtb, group, *, batch, seq, crow=None, ccolp=None, knmax=None):
    tq = tk = ATT_TILE
    nq = seq // tq
    col0 = group * N_PAIR
    qt_spec = pl.BlockSpec((1, 1, LANES, tq), lambda b, hp, i: (b, i, col0 + hp, 0))
    k_spec = pl.BlockSpec((seq, LANES), lambda b, hp, i: (b, col0 + hp))
    vt_spec = pl.BlockSpec((1, seq // tk, LANES, tk), lambda b, hp, i: (b, 0, hp, 0))
    out_spec = pl.BlockSpec((tq, LANES), lambda b, hp, i: (b * nq + i, hp))
    out_shape = jax.ShapeDtypeStruct((batch * seq, W_ATT), F32)
    params = _params(("parallel", "parallel", "arbitrary"), 40)
    grid = (batch, N_PAIR, nq)
    if group == 1:
        return pl.pallas_call(
            functools.partial(_sb_prompt_kernel, window_blocks=SB_WINDOW), grid=grid,
            name="sb_prompt",
            in_specs=[qt_spec, k_spec, vt_spec], out_specs=out_spec, out_shape=out_shape,
            compiler_params=params)(qt, k, vtb)
    crow_spec = pl.BlockSpec((seq // tk, SUBLANES, tk), lambda b, hp, i: (b, 0, 0))
    ccolp_spec = pl.BlockSpec((seq, LANES), lambda b, hp, i: (b, hp))
    knmax_spec = pl.BlockSpec((1, SUBLANES, LANES), lambda b, hp, i: (b, 0, 0))
    return pl.pallas_call(
        functools.partial(_fox_prompt_kernel, window_blocks=FOX_WINDOW), grid=grid,
        name="fox_prompt",
        in_specs=[qt_spec, k_spec, vt_spec, crow_spec, ccolp_spec, knmax_spec],
        out_specs=out_spec, out_shape=out_shape,
        compiler_params=params)(qt, k, vtb, crow, ccolp, knmax)


def _suffix_sum_kernel(x_ref, o_ref):
    rows, width = x_ref.shape
    ones = _tri_ones(LANES, "strict_lower")
    carry = jnp.zeros((rows, 1), F32)
    for kb in reversed(range(width // LANES)):
        xb = x_ref[:, kb * LANES:(kb + 1) * LANES]
        o_ref[:, kb * LANES:(kb + 1) * LANES] = _ones_dot3(xb, ones, left=False) + carry
        carry = carry + jnp.sum(xb, axis=1, keepdims=True)


def _suffix_sum(x):
    return pl.pallas_call(
        _suffix_sum_kernel, name="cache_suffix_sum",
        out_shape=jax.ShapeDtypeStruct(x.shape, F32),
        compiler_params=_params(None, 32))(x)


def _split_heads(q2):
    lane = lax.broadcasted_iota(jnp.int32, q2.shape, 1)
    low = lane < D_HEAD
    zero = jnp.zeros_like(q2)
    return jnp.where(low, q2, zero), jnp.where(low, zero, q2)


def _stack_heads(q2):
    return jnp.concatenate(_split_heads(q2), axis=0)


def _unstack_heads(o):
    half = o.shape[0] // 2
    lane = lax.broadcasted_iota(jnp.int32, (half, o.shape[1]), 1)
    return jnp.where(lane < D_HEAD, o[:half], o[half:])


def _sb_update(z, pv, suffix_ones, state, valid):
    carry, acc = state
    r = _softplus(z)
    if valid is not None:
        r = jnp.where(valid, r, 0.0)
    r_hi = r.astype(BF16)
    r_lo = (r - r_hi.astype(F32)).astype(BF16)
    after = _dot(r_hi, suffix_ones) + _dot(r_lo, suffix_ones) + carry
    a = jnp.exp((z - r) - after)
    if valid is not None:
        a = jnp.where(valid, a, 0.0)
    acc = acc + pv(a.astype(BF16))
    carry = carry + jnp.sum(r, axis=1, keepdims=True)
    return carry, acc


def _sample_attn_kernel(qf_ref, kfn_ref, vfn_ref, kfc_ref, vfc_ref, drow_ref, cnrow_ref, ccol_ref,
                        qs_ref, ksn_ref, vsn_ref, ksc_ref, vsc_ref, of_ref, os_ref,
                        carry_ref, acc_ref, *, blk):
    b = pl.program_id(0)
    hp = pl.program_id(1)
    tq = qf_ref.shape[0]
    past = kfc_ref.shape[3]
    row = lax.broadcasted_iota(jnp.int32, (2 * tq, tq), 0) % tq
    col = lax.broadcasted_iota(jnp.int32, (2 * tq, tq), 1)

    def per_head(fn):
        return jnp.concatenate([fn(0), fn(1)], axis=0)

    q2 = _stack_heads(qf_ref[...])
    kct = kfc_ref[0, 0].astype(BF16)
    vct = vfc_ref[0, 0].astype(BF16)
    cc = ccol_ref[...]
    cn = cnrow_ref[0]
    cq = per_head(lambda h: _pick_lane(cc, 2 * hp + h))
    d_cache = per_head(lambda h: jnp.broadcast_to(
        _pick_row(drow_ref[0, 2 * hp + h], b % SUBLANES), (tq, past)))
    c_new = per_head(lambda h: jnp.broadcast_to(_pick_row(cn, 2 * hp + h), (tq, tq)))
    s_c = _dot(q2, kct) + d_cache
    s_n = jnp.where(col <= row, _dot_nt(q2, kfn_ref[...]) - c_new, NEG_INF)
    m = jnp.maximum(jnp.max(s_c, axis=1, keepdims=True), jnp.max(s_n, axis=1, keepdims=True)) + cq
    shift = cq - m
    p_c = jnp.exp(s_c + shift)
    p_n = jnp.exp(s_n + shift)
    l = jnp.sum(p_c, axis=1, keepdims=True) + jnp.sum(p_n, axis=1, keepdims=True)
    o = _dot_nt(p_c.astype(BF16), vct) + _dot(p_n.astype(BF16), vfn_ref[...])
    of_ref[...] = _unstack_heads(o / l)

    q2 = _stack_heads(qs_ref[...])
    vn = vsn_ref[...]
    state = (jnp.zeros((2 * tq, 1), F32), jnp.zeros((2 * tq, LANES), F32))
    carry_ref[...], acc_ref[...] = _sb_update(
        _dot_nt(q2, ksn_ref[...]), lambda a: _dot(a, vn), _tri_ones(tq, "strict_lower"), state,
        col < row)
    suffix_ones = _tri_ones(blk, "strict_lower")
    for jb in reversed(range(past // blk)):
        @pl.when(jnp.min(carry_ref[...]) < SB_CLOSED)
        def _():
            kt = ksc_ref[0, 0, :, jb * blk:(jb + 1) * blk].astype(BF16)
            vt = vsc_ref[0, 0, :, jb * blk:(jb + 1) * blk].astype(BF16)
            carry_ref[...], acc_ref[...] = _sb_update(
                _dot(q2, kt), lambda a: _dot_nt(a, vt), suffix_ones,
                (carry_ref[...], acc_ref[...]), None)
    os_ref[...] = _unstack_heads(acc_ref[...])


def _sample_attention(q, kb, vb, cache_fk, cache_fv, drow, cnrow, ccol, cache_sk, cache_sv,
                      *, layer, streams, dec_seq):
    past = cache_fk.shape[3]
    new_spec = lambda g: pl.BlockSpec((dec_seq, LANES), lambda b, hp: (b, g * N_PAIR + hp))
    cache_spec = pl.BlockSpec((1, 1, LANES, past), lambda b, hp: (layer, b, hp, 0))
    drow_spec = pl.BlockSpec((1, H_FOX, SUBLANES, past), lambda b, hp: (layer, 0, b // SUBLANES, 0))
    cnrow_spec = pl.BlockSpec((1, SUBLANES, dec_seq), lambda b, hp: (b, 0, 0))
    ccol_spec = pl.BlockSpec((dec_seq, LANES), lambda b, hp: (b, 0))
    out_spec = pl.BlockSpec((dec_seq, LANES), lambda b, hp: (b, hp))
    out_shape = [jax.ShapeDtypeStruct((streams * dec_seq, W_ATT), F32)] * 2
    return pl.pallas_call(
        functools.partial(_sample_attn_kernel, blk=ATT_TILE),
        grid=(streams, N_PAIR), name="sample_attention",
        in_specs=[new_spec(0), new_spec(0), new_spec(0), cache_spec, cache_spec, drow_spec,
                  cnrow_spec, ccol_spec, new_spec(1), new_spec(1), new_spec(1), cache_spec,
                  cache_spec],
        out_specs=[out_spec, out_spec], out_shape=out_shape,
        scratch_shapes=[pltpu.VMEM((2 * dec_seq, 1), F32), pltpu.VMEM((2 * dec_seq, LANES), F32)],
        compiler_params=_params(("parallel", "parallel"), 48),
    )(q, kb, vb, cache_fk, cache_fv, drow, cnrow, ccol, q, kb, vb, cache_sk, cache_sv)


def _merge_kernel(x_ref, of_ref, os_ref, u_ref, vn_ref, ws_ref, bs_ref, gmix_ref, wout_ref,
                  g_ref, b_ref, o_ref, *, alpha):
    tm = x_ref.shape[0]
    r = lax.broadcasted_iota(jnp.int32, (SGU_CHUNK, SGU_CHUNK), 0)
    c = lax.broadcasted_iota(jnp.int32, (SGU_CHUNK, SGU_CHUNK), 1)
    tril = r >= c
    head_mean = jnp.where(r // D_HEAD == c // D_HEAD, 1.0 / D_HEAD, 0.0).astype(BF16)
    lane = lax.broadcasted_iota(jnp.int32, (SGU_CHUNK, LANES), 1)

    gate_pieces = []
    for p in range(W_SGU // LANES):
        w0 = jnp.where(tril, ws_ref[2 * p], 0.0).astype(BF16)
        w1 = jnp.where(tril, ws_ref[2 * p + 1], 0.0).astype(BF16)
        bias = bs_ref[:, p * LANES:(p + 1) * LANES]
        chunks = []
        for ch in range(tm // SGU_CHUNK):
            rows = slice(ch * SGU_CHUNK, (ch + 1) * SGU_CHUNK)
            vp = vn_ref[rows, p * LANES:(p + 1) * LANES].astype(BF16)
            s = jnp.where(lane < D_HEAD, _dot(w0, vp), _dot(w1, vp)) + bias
            chunks.append(u_ref[rows, p * LANES:(p + 1) * LANES] * s)
        gate_pieces.append(jnp.concatenate(chunks, axis=0))

    pieces = ([of_ref[:, k * LANES:(k + 1) * LANES] for k in range(N_PAIR)] + gate_pieces
              + [os_ref[:, k * LANES:(k + 1) * LANES] for k in range(N_PAIR)])
    normed = []
    for k, o in enumerate(pieces):
        ms = _dot((o * o).astype(BF16), head_mean)
        gain = gmix_ref[:, k * LANES:(k + 1) * LANES]
        normed.append((o * lax.rsqrt(ms + RMS_EPS) * gain).astype(BF16))
    mix = _dot(jnp.concatenate(normed, axis=1), wout_ref[...])
    o_ref[...] = _layer_norm(alpha * x_ref[...] + mix, g_ref[...], b_ref[...])


def _merge(x, of, os_, u, vn, ws, bs, gmix, wout, g, b, *, alpha, name):
    n, d = x.shape
    tm = ROW_TILE
    row = lambda w: pl.BlockSpec((tm, w), lambda i: (i, 0))
    return pl.pallas_call(
        functools.partial(_merge_kernel, alpha=alpha), grid=(n // tm,), name=name,
        in_specs=[row(d), row(W_ATT), row(W_ATT), row(W_SGU), row(W_SGU)]
        + [_resident(a.shape) for a in (ws, bs, gmix, wout, g, b)],
        out_specs=row(d), out_shape=jax.ShapeDtypeStruct((n, d), F32),
        compiler_params=_params(("parallel",), 48),
    )(x, of, os_, u, vn, ws, bs, gmix, wout, g, b)


def _mlp_kernel(x_ref, wup_ref, wdn_ref, g_ref, b_ref, o_ref, *, alpha, ff_chunk):
    x = x_ref[...]
    xb = x.astype(BF16)
    acc = alpha * x
    for c0 in range(0, wup_ref.shape[1], ff_chunk):
        h = jnp.maximum(_dot(xb, wup_ref[:, c0:c0 + ff_chunk]), 0.0)
        acc = acc + _dot((h * h).astype(BF16), wdn_ref[c0:c0 + ff_chunk, :])
    o_ref[...] = _layer_norm(acc, g_ref[...], b_ref[...])


def _mlp(x, wup, wdn, g, b, *, alpha, name):
    n, d = x.shape
    tm = ROW_TILE
    row = pl.BlockSpec((tm, d), lambda i: (i, 0))
    return pl.pallas_call(
        functools.partial(_mlp_kernel, alpha=alpha, ff_chunk=1024), grid=(n // tm,), name=name,
        in_specs=[row] + [_resident(a.shape) for a in (wup, wdn, g, b)],
        out_specs=row, out_shape=jax.ShapeDtypeStruct((n, d), F32),
        compiler_params=_params(("parallel",), 56),
    )(x, wup, wdn, g, b)


def _pack_w_in(w, b_f):
    sizes = [W_ATT, W_ATT, W_ATT, H_FOX, W_SGU, W_SGU, W_ATT, W_ATT, W_ATT]
    offs = [0]
    for s in sizes:
        offs.append(offs[-1] + s)
    q_f, k_f, v_f, f_lg, u_g, v_g, q_s, k_s, v_s = (w[:, offs[i]:offs[i + 1]] for i in range(9))
    pad_cols = lambda a, width: jnp.pad(a, ((0, 0), (0, width - a.shape[1])))
    f_pairs = [pad_cols(f_lg[:, 2 * p:2 * p + 2], LANES) for p in range(N_PAIR)]
    w_rows = jnp.concatenate([q_f, q_s, k_f, k_s, v_f, v_s, u_g, v_g, pad_cols(f_lg, LANES)]
                             + f_pairs, axis=1).astype(BF16)
    w_cols = jnp.concatenate([q_f, q_s, k_f, k_s, v_f, v_s, pad_cols(f_lg, F_ROWS)],
                             axis=1).T.astype(BF16)
    bfc = jnp.pad(b_f, (0, LANES - H_FOX)).reshape(1, LANES)
    bfp = jnp.concatenate([jnp.pad(b_f[2 * p:2 * p + 2], (0, LANES - 2)) for p in range(N_PAIR)]
                          ).reshape(1, N_PAIR * LANES)
    bfr = jnp.broadcast_to(jnp.pad(b_f, (0, SUBLANES - H_FOX)).reshape(SUBLANES, 1),
                           (SUBLANES, LANES))
    return w_rows, w_cols, bfc, bfp, bfr


def kernel(x_prompt, x_sample, cache_fox_k, cache_fox_v, cache_fox_logf, cache_sb_k, cache_sb_v,
           w_in, b_f, g_v, b_v, w_s, b_s, g_mix, w_out, ln1_g, ln1_b, w_up, w_down, ln2_g, ln2_b):
    depth = w_in.shape[0]
    batch, seq, d_model = x_prompt.shape
    streams, dec_seq, _ = x_sample.shape
    past = cache_fox_k.shape[2]
    n_prompt = batch * seq
    n_sample = streams * dec_seq
    alpha = (2 * depth) ** 0.25
    assert seq % ROW_TILE == 0 and n_sample % ROW_TILE == 0 and ROW_TILE % dec_seq == 0
    assert dec_seq == D_HEAD and 2 * dec_seq == SGU_CHUNK and past % ATT_TILE == 0
    assert streams % SUBLANES == 0

    xp = x_prompt.reshape(n_prompt, d_model)
    xs = x_sample.reshape(n_sample, d_model)
    row2 = lambda a: a.reshape(1, -1)
    cache_t = lambda c: jnp.transpose(c, (0, 1, 3, 4, 2)).reshape(depth, streams, W_ATT, past)
    ck_f, cv_f, ck_s, cv_s = (cache_t(c) for c in (cache_fox_k, cache_fox_v, cache_sb_k, cache_sb_v))
    lf_cache = jnp.transpose(cache_fox_logf.astype(F32), (0, 3, 1, 2))
    drow = _suffix_sum(lf_cache.reshape(depth * H_FOX * streams, past)).reshape(
        depth, H_FOX, streams, past)

    outs = [[] for _ in range(11)]
    for l in range(depth):
        w_rows, w_cols, bfc, bfp, bfr = _pack_w_in(w_in[l], b_f[l])
        gv, bv = row2(g_v[l]), row2(b_v[l])
        wout = w_out[l].astype(BF16)
        wup, wdn = w_up[l].astype(BF16), w_down[l].astype(BF16)
        gmix, g1, b1, g2, b2 = (row2(a[l]) for a in (g_mix, ln1_g, ln1_b, ln2_g, ln2_b))

        w_l = w_s[l]
        half = w_l[:, :dec_seq, :dec_seq]
        zeros = jnp.zeros_like(half)
        w_blockdiag = jnp.concatenate([jnp.concatenate([half, zeros], axis=2),
                                       jnp.concatenate([zeros, half], axis=2)], axis=1)
        bias_full = jnp.repeat(b_s[l].T, D_HEAD, axis=1)
        bias_half = jnp.concatenate([bias_full[:dec_seq], bias_full[:dec_seq]], axis=0)

        (qt, kb, ktf, kts, vtf, vts, vtbf, vtbs, lfrow, crow, ccolp, u, vn, knmax) = _in_proj_prompt(
            xp, w_rows, w_cols, bfp, bfr, gv, bv, batch=batch, seq=seq)
        of = _prompt_attention(qt, kb, vtbf, 0, batch=batch, seq=seq, crow=crow, ccolp=ccolp,
                               knmax=knmax)
        os_ = _prompt_attention(qt, kb, vtbs, 1, batch=batch, seq=seq)
        x1 = _merge(xp, of, os_, u, vn, w_l, bias_full, gmix, wout, g1, b1, alpha=alpha,
                    name="merge_prompt")
        xp = _mlp(x1, wup, wdn, g2, b2, alpha=alpha, name="mlp_prompt")
        heads_t = lambda a: jnp.transpose(a.reshape(batch, H_FOX, D_HEAD, seq), (0, 3, 1, 2))
        p_out = (heads_t(ktf), heads_t(vtf),
                 jnp.transpose(lfrow[:H_FOX].reshape(H_FOX, batch, seq), (1, 2, 0)),
                 heads_t(kts), heads_t(vts))

        (q, kb, vb, k, v, lfcol, ccol, cnrow, u, vn) = _in_proj_sample(
            xs, w_rows, w_cols, bfc, bfr, gv, bv, dec_seq=dec_seq)
        of, os_ = _sample_attention(q, kb, vb, ck_f, cv_f, drow, cnrow, ccol, ck_s, cv_s,
                                    layer=l, streams=streams, dec_seq=dec_seq)
        x1 = _merge(xs, of, os_, u, vn, w_blockdiag, bias_half, gmix, wout, g1, b1, alpha=alpha,
                    name="merge_sample")
        xs = _mlp(x1, wup, wdn, g2, b2, alpha=alpha, name="mlp_sample")
        heads = lambda a, g: a[:, g * W_ATT:(g + 1) * W_ATT].reshape(streams, dec_seq, H_FOX, D_HEAD)
        s_out = (heads(k, 0), heads(v, 0), lfcol[:, :H_FOX].reshape(streams, dec_seq, H_FOX),
                 heads(k, 1), heads(v, 1), vn.reshape(streams, dec_seq, W_SGU))

        for acc, val in zip(outs, p_out + s_out):
            acc.append(val)

    return (xp.reshape(batch, seq, d_model), xs.reshape(streams, dec_seq, d_model),
            *(jnp.stack(o) for o in outs))
```

```python
import functools
import math

import jax
import jax.numpy as jnp
from jax import lax
from jax.experimental import pallas as pl
from jax.experimental.pallas import tpu as pltpu

F32 = jnp.float32
BF16 = jnp.bfloat16

LANES = 128
SUBLANES = 8
VMEM_BYTES_V7X = 64 * 1024 * 1024

D_HEAD = 64
H_FOX = 6
H_SB = 6
H_SGU = 4
W_ATT = H_FOX * D_HEAD
W_SGU = H_SGU * D_HEAD
N_PAIR = W_ATT // LANES
SGU_CHUNK = 128
LN_EPS = 1e-5
RMS_EPS = 1e-6
NEG_INF = -1e30
SB_CLOSED = 110.0
FOX_DEAD = 110.0
FOX_L_FLOOR = 1e-25
FOX_WINDOW = 5
SB_WINDOW = 2

ROW_TILE = 512
ATT_TILE = 256

_C_Q = 0
_C_K = 2 * W_ATT
_C_V = 4 * W_ATT
_C_G = 6 * W_ATT
_C_F = _C_G + 2 * W_SGU
W_ROWS = _C_F + LANES
_R_Q = 0
_R_K = 2 * W_ATT
_R_V = 4 * W_ATT
_R_F = 6 * W_ATT
F_ROWS = 2 * SUBLANES
W_COLS = _R_F + F_ROWS


def _params(semantics, vmem_mib, flags=None):
    assert vmem_mib * 1024 * 1024 < VMEM_BYTES_V7X
    return pltpu.CompilerParams(dimension_semantics=semantics,
                                vmem_limit_bytes=vmem_mib * 1024 * 1024, flags=flags)


def _resident(shape):
    nd = len(shape)
    return pl.BlockSpec(shape, lambda *_: (0,) * nd, pipeline_mode=pl.Buffered(1))


def _split3(x):
    hi = x.astype(BF16)
    r1 = x - hi.astype(F32)
    mid = r1.astype(BF16)
    lo = (r1 - mid.astype(F32)).astype(BF16)
    return hi, mid, lo


def _dot(a, b):
    return jnp.dot(a, b, preferred_element_type=F32)


def _dot_nt(a, b):
    return lax.dot_general(a, b, (((1,), (1,)), ((), ())), preferred_element_type=F32)


def _ones_dot3(x, ones_mat, left):
    if left:
        return sum(_dot(ones_mat, p) for p in _split3(x))
    return sum(_dot(p, ones_mat) for p in _split3(x))


def _gelu_tanh(x):
    return 0.5 * x * (1.0 + jnp.tanh(0.7978845608028654 * (x + 0.044715 * (x * x * x))))


def _log_sigmoid(x):
    return jnp.minimum(x, 0.0) - jnp.log1p(jnp.exp(-jnp.abs(x)))


def _softplus(z):
    return jnp.maximum(z, 0.0) + jnp.log(1.0 + jnp.exp(-jnp.abs(z)))


def _layer_norm(x, g, b):
    mu = jnp.mean(x, axis=-1, keepdims=True)
    xc = x - mu
    var = jnp.mean(xc * xc, axis=-1, keepdims=True)
    return xc * lax.rsqrt(var + LN_EPS) * g + b


def _tri_ones(n, kind, seg=None):
    r = lax.broadcasted_iota(jnp.int32, (n, n), 0)
    c = lax.broadcasted_iota(jnp.int32, (n, n), 1)
    keep = {"lower": c <= r, "upper": r <= c, "strict_lower": r > c, "strict_upper": r < c}[kind]
    if seg is not None:
        keep = jnp.logical_and(keep, r // seg == c // seg)
    return jnp.where(keep, 1.0, 0.0).astype(BF16)


def _pick_lane(x, idx):
    lane = lax.broadcasted_iota(jnp.int32, x.shape, 1)
    return jnp.sum(jnp.where(lane == idx, x, 0.0), axis=1, keepdims=True)


def _pick_row(x, idx):
    sub = lax.broadcasted_iota(jnp.int32, x.shape, 0)
    return jnp.sum(jnp.where(sub == idx, x, 0.0), axis=0, keepdims=True)


def _gates(xb, wr_ref, gv_ref, bv_ref, u_ref, vn_ref):
    u_ref[...] = _gelu_tanh(_dot(xb, wr_ref[:, _C_G:_C_G + W_SGU]))
    vg = _dot(xb, wr_ref[:, _C_G + W_SGU:_C_G + 2 * W_SGU])
    vn_ref[...] = _layer_norm(_gelu_tanh(vg), gv_ref[...], bv_ref[...])


def _log_forget_rows(xb, wc_ref, bfr_ref):
    sub = lax.broadcasted_iota(jnp.int32, (SUBLANES, xb.shape[0]), 0)
    z = _dot_nt(wc_ref[_R_F:_R_F + F_ROWS, :], xb)[:SUBLANES] + bfr_ref[:, 0:1]
    return jnp.where(sub < H_FOX, _log_sigmoid(z), 0.0)


def _in_proj_prompt_kernel(x_ref, wr_ref, wc_ref, bfr_ref, gv_ref, bv_ref,
                           qt_ref, k_ref, ktf_ref, kts_ref, vtf_ref, vts_ref, vtbf_ref, vtbs_ref,
                           lfrow_ref, crow_ref, ccolp_ref, u_ref, vn_ref, knmax_ref, qn2row_ref,
                           carry_row_ref):
    i = pl.program_id(1)
    tm = x_ref.shape[0]
    tk = vtbf_ref.shape[3]
    xb = x_ref[...].astype(BF16)

    def blocks(zt):
        return [zt[:, kb * tk:(kb + 1) * tk].astype(BF16) for kb in range(tm // tk)]

    scale = 1.0 / math.sqrt(D_HEAD)
    for kb, blk in enumerate(blocks(_dot_nt(wc_ref[_R_Q:_R_Q + 2 * W_ATT, :], xb) * scale)):
        qt_ref[0, kb] = blk
        qf = blk[:W_ATT].astype(F32)
        qn2 = jnp.sum((qf * qf).reshape(H_FOX, D_HEAD, tk), axis=1)
        qn2row_ref[kb] = jnp.concatenate([qn2, jnp.zeros((SUBLANES - H_FOX, tk), F32)], axis=0)
    for r0, full_ref, blk_ref in ((_R_V, vtf_ref, vtbf_ref), (_R_V + W_ATT, vts_ref, vtbs_ref)):
        zt = _dot_nt(wc_ref[r0:r0 + W_ATT, :], xb)
        full_ref[0] = zt
        for kb, blk in enumerate(blocks(zt)):
            blk_ref[0, kb] = blk
    k_rows = _dot(xb, wr_ref[:, _C_K:_C_K + 2 * W_ATT])
    k_ref[...] = k_rows.astype(BF16)
    kt_f = k_rows[:, :W_ATT].T
    ktf_ref[0] = kt_f
    kts_ref[0] = k_rows[:, W_ATT:].T

    _gates(xb, wr_ref, gv_ref, bv_ref, u_ref, vn_ref)

    @pl.when(i == 0)
    def _():
        carry_row_ref[...] = jnp.zeros_like(carry_row_ref)
        knmax_ref[...] = jnp.zeros_like(knmax_ref)

    kn2 = jnp.sum((kt_f * kt_f).reshape(H_FOX, D_HEAD, tm), axis=1)
    kn2 = jnp.broadcast_to(jnp.max(kn2, axis=1, keepdims=True), (H_FOX, LANES))
    kn2 = jnp.concatenate([kn2, jnp.zeros((SUBLANES - H_FOX, LANES), F32)], axis=0)
    knmax_ref[0] = jnp.maximum(knmax_ref[0], kn2)

    lf_row = _log_forget_rows(xb, wc_ref, bfr_ref)
    lfrow_ref[...] = lf_row
    c_row = _ones_dot3(lf_row, _tri_ones(tm, "upper"), left=False) + carry_row_ref[:, 0:1]
    for kb in range(tm // tk):
        crow_ref[kb] = c_row[:, kb * tk:(kb + 1) * tk]
    carry_row_ref[...] = jnp.broadcast_to(c_row[:, tm - 1:tm], carry_row_ref.shape)
    pad = jnp.zeros((LANES - SUBLANES, tm), F32)
    for p in range(N_PAIR):
        rows = c_row if p == 0 else jnp.concatenate([c_row[2 * p:], c_row[:2 * p]], axis=0)
        ccolp_ref[:, p * LANES:(p + 1) * LANES] = jnp.concatenate([rows, pad], axis=0).T


def _in_proj_prompt(x, w_rows, w_cols, bfr, gv, bv, *, batch, seq):
    n, d = x.shape
    tm, tk = ROW_TILE, ATT_TILE
    nt = seq // tm
    row = lambda w: pl.BlockSpec((tm, w), lambda b, i: (b * nt + i, 0))
    t_spec = pl.BlockSpec((1, W_ATT, tm), lambda b, i: (b, 0, i))
    t_shape = jax.ShapeDtypeStruct((batch, W_ATT, seq), F32)
    blk_spec = lambda w: pl.BlockSpec((1, tm // tk, w, tk), lambda b, i: (b, i, 0, 0))
    blk_shape = lambda w: jax.ShapeDtypeStruct((batch, seq // tk, w, tk), BF16)
    out_shape = [blk_shape(2 * W_ATT), jax.ShapeDtypeStruct((n, 2 * W_ATT), BF16),
                 t_shape, t_shape, t_shape, t_shape, blk_shape(W_ATT), blk_shape(W_ATT),
                 jax.ShapeDtypeStruct((SUBLANES, n), F32),
                 jax.ShapeDtypeStruct((n // tk, SUBLANES, tk), F32),
                 jax.ShapeDtypeStruct((n, N_PAIR * LANES), F32),
                 jax.ShapeDtypeStruct((n, W_SGU), F32), jax.ShapeDtypeStruct((n, W_SGU), F32),
                 jax.ShapeDtypeStruct((batch, SUBLANES, LANES), F32),
                 jax.ShapeDtypeStruct((n // tk, SUBLANES, tk), F32)]
    out_specs = [blk_spec(2 * W_ATT), row(2 * W_ATT), t_spec, t_spec, t_spec, t_spec,
                 blk_spec(W_ATT), blk_spec(W_ATT),
                 pl.BlockSpec((SUBLANES, tm), lambda b, i: (0, b * nt + i)),
                 pl.BlockSpec((tm // tk, SUBLANES, tk), lambda b, i: (b * nt + i, 0, 0)),
                 row(N_PAIR * LANES), row(W_SGU), row(W_SGU),
                 pl.BlockSpec((1, SUBLANES, LANES), lambda b, i: (b, 0, 0)),
                 pl.BlockSpec((tm // tk, SUBLANES, tk), lambda b, i: (b * nt + i, 0, 0))]
    return pl.pallas_call(
        _in_proj_prompt_kernel, grid=(batch, nt), name="in_proj_prompt",
        in_specs=[row(d)] + [_resident(a.shape) for a in (w_rows, w_cols, bfr, gv, bv)],
        out_specs=out_specs, out_shape=out_shape,
        scratch_shapes=[pltpu.VMEM((SUBLANES, LANES), F32)],
        compiler_params=_params(("parallel", "arbitrary"), 52),
    )(x, w_rows, w_cols, bfr, gv, bv)


def _in_proj_sample_kernel(x_ref, wr_ref, wc_ref, bfc_ref, bfr_ref, gv_ref, bv_ref,
                           q_ref, kb_ref, vb_ref, k_ref, v_ref, lfcol_ref, ccol_ref, crow_ref,
                           u_ref, vn_ref, *, dec_seq):
    tm = x_ref.shape[0]
    xb = x_ref[...].astype(BF16)
    scale = 1.0 / math.sqrt(D_HEAD)
    q_ref[...] = (_dot(xb, wr_ref[:, _C_Q:_C_Q + 2 * W_ATT]) * scale).astype(BF16)
    for c0, full_ref, half_ref in ((_C_K, k_ref, kb_ref), (_C_V, v_ref, vb_ref)):
        z = _dot(xb, wr_ref[:, c0:c0 + 2 * W_ATT])
        full_ref[...] = z
        half_ref[...] = z.astype(BF16)
    _gates(xb, wr_ref, gv_ref, bv_ref, u_ref, vn_ref)

    lane = lax.broadcasted_iota(jnp.int32, (tm, LANES), 1)
    z = _dot(xb, wr_ref[:, _C_F:_C_F + LANES]) + bfc_ref[...]
    lf_col = jnp.where(lane < H_FOX, _log_sigmoid(z), 0.0)
    lfcol_ref[...] = lf_col
    ccol_ref[...] = _ones_dot3(lf_col, _tri_ones(tm, "lower", dec_seq), left=True)
    lf_row = _log_forget_rows(xb, wc_ref, bfr_ref)
    c_row = _ones_dot3(lf_row, _tri_ones(tm, "upper", dec_seq), left=False)
    for s in range(tm // dec_seq):
        crow_ref[s] = c_row[:, s * dec_seq:(s + 1) * dec_seq]


def _in_proj_sample(x, w_rows, w_cols, bfc, bfr, gv, bv, *, dec_seq):
    n, d = x.shape
    tm = ROW_TILE
    row = lambda w: pl.BlockSpec((tm, w), lambda i: (i, 0))
    sds = lambda w, dt: jax.ShapeDtypeStruct((n, w), dt)
    out_shape = [sds(2 * W_ATT, BF16)] * 3 + [sds(2 * W_ATT, F32)] * 2 + [sds(LANES, F32)] * 2 + [
        jax.ShapeDtypeStruct((n // dec_seq, SUBLANES, dec_seq), F32), sds(W_SGU, F32), sds(W_SGU, F32)]
    out_specs = [row(2 * W_ATT)] * 5 + [row(LANES)] * 2 + [
        pl.BlockSpec((tm // dec_seq, SUBLANES, dec_seq), lambda i: (i, 0, 0)), row(W_SGU), row(W_SGU)]
    return pl.pallas_call(
        functools.partial(_in_proj_sample_kernel, dec_seq=dec_seq), grid=(n // tm,),
        name="in_proj_sample",
        in_specs=[row(d)] + [_resident(a.shape) for a in (w_rows, w_cols, bfc, bfr, gv, bv)],
        out_specs=out_specs, out_shape=out_shape,
        compiler_params=_params(("parallel",), 52),
    )(x, w_rows, w_cols, bfc, bfr, gv, bv)


def _split_head_rows(qt):
    sub = lax.broadcasted_iota(jnp.int32, qt.shape, 0)
    low = sub < D_HEAD
    zero = jnp.zeros_like(qt)
    return jnp.where(low, qt, zero), jnp.where(low, zero, qt)


def _fox_update_t(s, ct, pv, state, valid):
    m, l, acc = state
    if valid is not None:
        s = jnp.where(valid, s, NEG_INF)
    m_new = jnp.maximum(m, jnp.max(s, axis=0, keepdims=True) + ct)
    p = jnp.exp(s + (ct - m_new))
    alpha = jnp.exp(m - m_new)
    l = alpha * l + jnp.sum(p, axis=0, keepdims=True)
    acc = alpha * acc + pv(p.astype(BF16))
    return m_new, l, acc


def _sb_update_t(z, pv, suffix_ones, state, valid):
    carry, acc = state
    r = _softplus(z)
    if valid is not None:
        r = jnp.where(valid, r, 0.0)
    r_hi = r.astype(BF16)
    r_lo = (r - r_hi.astype(F32)).astype(BF16)
    after = _dot(suffix_ones, r_hi) + _dot(suffix_ones, r_lo) + carry
    a = jnp.exp((z - r) - after)
    if valid is not None:
        a = jnp.where(valid, a, 0.0)
    acc = acc + pv(a.astype(BF16))
    carry = carry + jnp.sum(r, axis=0, keepdims=True)
    return carry, acc


def _fox_prompt_kernel(qt_ref, k_ref, vt_ref, crow_ref, ccolp_ref, knmax_ref, qn2row_ref, o_ref,
                       *, window_blocks):
    tq = qt_ref.shape[3]
    tk = vt_ref.shape[3]
    hp = pl.program_id(1)
    i = pl.program_id(2)
    q2 = jnp.concatenate(_split_head_rows(qt_ref[0, 0]), axis=1)
    c_q = crow_ref[i]
    ct = jnp.concatenate([_pick_row(c_q, 2 * hp), _pick_row(c_q, 2 * hp + 1)], axis=1)

    key = lax.broadcasted_iota(jnp.int32, (tk, 2 * tq), 0)
    qry = lax.broadcasted_iota(jnp.int32, (tk, 2 * tq), 1) % tq
    causal = key <= qry

    def scores(j0, nblk):
        start = pl.multiple_of(j0 * tk, tk)
        kb = k_ref[pl.ds(start, nblk * tk), :]
        cc = ccolp_ref[pl.ds(start, nblk * tk), :]
        ck = jnp.concatenate([jnp.broadcast_to(cc[:, 0:1], (nblk * tk, tq)),
                              jnp.broadcast_to(cc[:, 1:2], (nblk * tk, tq))], axis=1)
        return _dot(kb, q2) - ck

    def pv(p, j0, nblk):
        vts = [vt_ref[0, j0 + w] for w in range(nblk)]
        outs = []
        for h in range(2):
            vth = jnp.concatenate([v[h * D_HEAD:(h + 1) * D_HEAD] for v in vts], axis=1)
            outs.append(_dot(vth, p[:, h * tq:(h + 1) * tq]))
        return jnp.concatenate(outs, axis=1)

    def pair_row(a, b):
        return jnp.concatenate([jnp.broadcast_to(a, (1, tq)), jnp.broadcast_to(b, (1, tq))], axis=1)

    qn2 = qn2row_ref[i]
    qn2 = jnp.concatenate([_pick_row(qn2, 2 * hp), _pick_row(qn2, 2 * hp + 1)], axis=1)
    kn2 = knmax_ref[0]
    kn2 = jnp.concatenate([_pick_row(kn2, 2 * hp)] * (tq // LANES)
                          + [_pick_row(kn2, 2 * hp + 1)] * (tq // LANES), axis=1)
    qk_bound = jnp.sqrt(qn2 * kn2) * 1.02

    shift = ct - qk_bound

    def reaches(j):
        return jnp.max(decay(j)) > -FOX_DEAD

    def decay(j):
        c_end = crow_ref[jnp.maximum(j, 0)][:, tk - 1:tk]
        return ct - pair_row(_pick_row(c_end, 2 * hp), _pick_row(c_end, 2 * hp + 1))

    def window(nblk):
        j0 = i - (nblk - 1)
        p = jnp.exp(scores(j0, nblk) + shift)
        body_rows = (nblk - 1) * tk
        diag = jnp.where(causal, p[body_rows:], 0.0)
        p = diag if nblk == 1 else jnp.concatenate([p[:body_rows], diag], axis=0)
        return (j0 - 1, jnp.sum(p, axis=0, keepdims=True), pv(p.astype(BF16), j0, nblk))

    j, l, acc = lax.switch(jnp.minimum(i, window_blocks - 1),
                           [functools.partial(window, n) for n in range(1, window_blocks + 1)])

    def cond(c):
        j, go, _, _ = c
        return jnp.logical_and(j >= 0, go)

    def body(c):
        j, _, l, acc = c
        p = jnp.exp(scores(j, 1) + shift)
        return (j - 1, reaches(j - 1), l + jnp.sum(p, axis=0, keepdims=True),
                acc + pv(p.astype(BF16), j, 1))

    def exact():
        def step(t, state):
            jj = i - 1 - t
            return _fox_update_t(scores(jj, 1), ct, lambda p: pv(p, jj, 1), state, None)

        init = (jnp.full((1, 2 * tq), NEG_INF, F32), jnp.zeros((1, 2 * tq), F32),
                jnp.zeros((D_HEAD, 2 * tq), F32))
        state = _fox_update_t(scores(i, 1), ct, lambda p: pv(p, i, 1), init, causal)
        _, l_x, acc_x = lax.fori_loop(0, i, step, state)
        return l_x, acc_x

    def beyond_window():
        _, _, l_w, acc_w = lax.while_loop(cond, body, (j, reaches(j), l, acc))
        return lax.cond(jnp.min(l_w) < FOX_L_FLOOR, exact, lambda: (l_w, acc_w))

    live_left = jnp.logical_and(decay(j) > -FOX_DEAD, j >= 0)
    uncommon = jnp.max(jnp.where(jnp.logical_or(live_left, l < FOX_L_FLOOR), 1.0, 0.0)) > 0.0
    l, acc = lax.cond(uncommon, beyond_window, lambda: (l, acc))
    o = acc / l
    o_ref[...] = jnp.concatenate([o[:, :tq], o[:, tq:]], axis=0).T


def _sb_prompt_kernel(qt_ref, k_ref, vt_ref, o_ref, *, window_blocks):
    tq = qt_ref.shape[3]
    tk = vt_ref.shape[3]
    i = pl.program_id(2)
    q2 = jnp.concatenate(_split_head_rows(qt_ref[0, 0]), axis=1)
    suffix_ones = _tri_ones(tk, "strict_upper")

    def block(j, state, valid):
        kb = k_ref[pl.ds(pl.multiple_of(j * tk, tk), tk), :]
        vt = vt_ref[0, j]

        def pv(a):
            return jnp.concatenate([_dot(vt[:D_HEAD], a[:, :tq]), _dot(vt[D_HEAD:], a[:, tq:])],
                                   axis=1)

        return _sb_update_t(_dot(kb, q2), pv, suffix_ones, state, valid)

    def still_open(carry):
        return jnp.min(carry) < SB_CLOSED

    key = lax.broadcasted_iota(jnp.int32, (tk, 2 * tq), 0)
    qry = lax.broadcasted_iota(jnp.int32, (tk, 2 * tq), 1) % tq
    causal = key < qry

    def window(nblk):
        j0 = i - (nblk - 1)
        z = _dot(k_ref[pl.ds(pl.multiple_of(j0 * tk, tk), nblk * tk), :], q2)
        r = _softplus(z)
        carry = jnp.zeros((1, 2 * tq), F32)
        a_blocks = [None] * nblk
        for w in reversed(range(nblk)):
            rows = slice(w * tk, (w + 1) * tk)
            r_w = jnp.where(causal, r[rows], 0.0) if w == nblk - 1 else r[rows]
            r_hi = r_w.astype(BF16)
            r_lo = (r_w - r_hi.astype(F32)).astype(BF16)
            after = _dot(suffix_ones, r_hi) + _dot(suffix_ones, r_lo) + carry
            a = jnp.exp((z[rows] - r_w) - after)
            a_blocks[w] = (jnp.where(causal, a, 0.0) if w == nblk - 1 else a).astype(BF16)
            carry = carry + jnp.sum(r_w, axis=0, keepdims=True)
        a = a_blocks[0] if nblk == 1 else jnp.concatenate(a_blocks, axis=0)
        vts = [vt_ref[0, j0 + w] for w in range(nblk)]
        outs = []
        for h in range(2):
            vth = jnp.concatenate([v[h * D_HEAD:(h + 1) * D_HEAD] for v in vts], axis=1)
            outs.append(_dot(vth, a[:, h * tq:(h + 1) * tq]))
        return j0 - 1, (carry, jnp.concatenate(outs, axis=1))

    j, state = lax.cond(i >= window_blocks - 1, lambda: window(window_blocks), lambda: window(1))

    def cond(c):
        j, go, _ = c
        return jnp.logical_and(j >= 0, go)

    def body(c):
        j, _, st = c
        st = block(j, st, None)
        return j - 1, still_open(st[0]), st

    _, _, (_, acc) = lax.while_loop(cond, body, (j, still_open(state[0]), state))
    o_ref[...] = jnp.concatenate([acc[:, :tq], acc[:, tq:]], axis=0).T


def _prompt_attention(qt, k, vtb, group, *, batch, seq, crow=None, ccolp=None, knmax=None,
                      qn2row=None):
    tq = tk = ATT_TILE
    nq = seq // tq
    col0 = group * N_PAIR
    qt_spec = pl.BlockSpec((1, 1, LANES, tq), lambda b, hp, i: (b, i, col0 + hp, 0))
    k_spec = pl.BlockSpec((seq, LANES), lambda b, hp, i: (b, col0 + hp))
    vt_spec = pl.BlockSpec((1, seq // tk, LANES, tk), lambda b, hp, i: (b, 0, hp, 0))
    out_spec = pl.BlockSpec((tq, LANES), lambda b, hp, i: (b * nq + i, hp))
    out_shape = jax.ShapeDtypeStruct((batch * seq, W_ATT), F32)
    params = _params(("parallel", "parallel", "arbitrary"), 40)
    grid = (batch, N_PAIR, nq)
    if group == 1:
        return pl.pallas_call(
            functools.partial(_sb_prompt_kernel, window_blocks=SB_WINDOW), grid=grid,
            name="sb_prompt",
            in_specs=[qt_spec, k_spec, vt_spec], out_specs=out_spec, out_shape=out_shape,
            compiler_params=params)(qt, k, vtb)
    crow_spec = pl.BlockSpec((seq // tk, SUBLANES, tk), lambda b, hp, i: (b, 0, 0))
    ccolp_spec = pl.BlockSpec((seq, LANES), lambda b, hp, i: (b, hp))
    knmax_spec = pl.BlockSpec((1, SUBLANES, LANES), lambda b, hp, i: (b, 0, 0))
    return pl.pallas_call(
        functools.partial(_fox_prompt_kernel, window_blocks=FOX_WINDOW), grid=grid,
        name="fox_prompt",
        in_specs=[qt_spec, k_spec, vt_spec, crow_spec, ccolp_spec, knmax_spec, crow_spec],
        out_specs=out_spec, out_shape=out_shape,
        compiler_params=params)(qt, k, vtb, crow, ccolp, knmax, qn2row)


def _suffix_sum_kernel(x_ref, o_ref):
    rows, width = x_ref.shape
    ones = _tri_ones(LANES, "strict_lower")
    carry = jnp.zeros((rows, 1), F32)
    for kb in reversed(range(width // LANES)):
        xb = x_ref[:, kb * LANES:(kb + 1) * LANES]
        o_ref[:, kb * LANES:(kb + 1) * LANES] = _ones_dot3(xb, ones, left=False) + carry
        carry = carry + jnp.sum(xb, axis=1, keepdims=True)


def _suffix_sum(x):
    return pl.pallas_call(
        _suffix_sum_kernel, name="cache_suffix_sum",
        out_shape=jax.ShapeDtypeStruct(x.shape, F32),
        compiler_params=_params(None, 32))(x)


def _split_heads(q2):
    lane = lax.broadcasted_iota(jnp.int32, q2.shape, 1)
    low = lane < D_HEAD
    zero = jnp.zeros_like(q2)
    return jnp.where(low, q2, zero), jnp.where(low, zero, q2)


def _stack_heads(q2):
    return jnp.concatenate(_split_heads(q2), axis=0)


def _unstack_heads(o):
    half = o.shape[0] // 2
    lane = lax.broadcasted_iota(jnp.int32, (half, o.shape[1]), 1)
    return jnp.where(lane < D_HEAD, o[:half], o[half:])


def _sb_update(z, pv, suffix_ones, state, valid):
    carry, acc = state
    r = _softplus(z)
    if valid is not None:
        r = jnp.where(valid, r, 0.0)
    r_hi = r.astype(BF16)
    r_lo = (r - r_hi.astype(F32)).astype(BF16)
    after = _dot(r_hi, suffix_ones) + _dot(r_lo, suffix_ones) + carry
    a = jnp.exp((z - r) - after)
    if valid is not None:
        a = jnp.where(valid, a, 0.0)
    acc = acc + pv(a.astype(BF16))
    carry = carry + jnp.sum(r, axis=1, keepdims=True)
    return carry, acc


def _sample_attn_kernel(qf_ref, kfn_ref, vfn_ref, kfc_ref, vfc_ref, drow_ref, cnrow_ref, ccol_ref,
                        qs_ref, ksn_ref, vsn_ref, ksc_ref, vsc_ref, of_ref, os_ref,
                        carry_ref, acc_ref, *, blk):
    b = pl.program_id(0)
    hp = pl.program_id(1)
    tq = qf_ref.shape[0]
    past = kfc_ref.shape[3]
    row = lax.broadcasted_iota(jnp.int32, (2 * tq, tq), 0) % tq
    col = lax.broadcasted_iota(jnp.int32, (2 * tq, tq), 1)

    def per_head(fn):
        return jnp.concatenate([fn(0), fn(1)], axis=0)

    q2 = _stack_heads(qf_ref[...])
    kct = kfc_ref[0, 0].astype(BF16)
    vct = vfc_ref[0, 0].astype(BF16)
    cc = ccol_ref[...]
    cn = cnrow_ref[0]
    cq = per_head(lambda h: _pick_lane(cc, 2 * hp + h))
    qk_c = _dot(q2, kct)
    qk_n = _dot_nt(q2, kfn_ref[...])
    s_c = per_head(lambda h: qk_c[h * tq:(h + 1) * tq]
                   + _pick_row(drow_ref[0, 2 * hp + h], b % SUBLANES))
    s_n = per_head(lambda h: qk_n[h * tq:(h + 1) * tq] - _pick_row(cn, 2 * hp + h))
    s_n = jnp.where(col <= row, s_n, NEG_INF)
    m = jnp.maximum(jnp.max(s_c, axis=1, keepdims=True), jnp.max(s_n, axis=1, keepdims=True)) + cq
    shift = cq - m
    p_c = jnp.exp(s_c + shift)
    p_n = jnp.exp(s_n + shift)
    l = jnp.sum(p_c, axis=1, keepdims=True) + jnp.sum(p_n, axis=1, keepdims=True)
    o = _dot_nt(p_c.astype(BF16), vct) + _dot(p_n.astype(BF16), vfn_ref[...])
    of_ref[...] = _unstack_heads(o / l)

    q2 = _stack_heads(qs_ref[...])
    vn = vsn_ref[...]
    suffix_ones = _tri_ones(blk, "strict_lower")

    def cache_block(jb, state):
        kt = ksc_ref[0, 0, :, jb * blk:(jb + 1) * blk].astype(BF16)
        vt = vsc_ref[0, 0, :, jb * blk:(jb + 1) * blk].astype(BF16)
        return _sb_update(_dot(q2, kt), lambda a: _dot_nt(a, vt), suffix_ones, state, None)

    n_blocks = past // blk
    state = (jnp.zeros((2 * tq, 1), F32), jnp.zeros((2 * tq, LANES), F32))
    state = _sb_update(_dot_nt(q2, ksn_ref[...]), lambda a: _dot(a, vn),
                       _tri_ones(tq, "strict_lower"), state, col < row)
    carry_ref[...], acc_ref[...] = cache_block(n_blocks - 1, state)

    def still_open():
        return jnp.min(carry_ref[...]) < SB_CLOSED

    @pl.when(still_open())
    def _():
        for jb in reversed(range(n_blocks - 1)):
            @pl.when(still_open())
            def _():
                carry_ref[...], acc_ref[...] = cache_block(jb, (carry_ref[...], acc_ref[...]))

    os_ref[...] = _unstack_heads(acc_ref[...])


def _sample_attention(q, kb, vb, cache_fk, cache_fv, drow, cnrow, ccol, cache_sk, cache_sv,
                      *, layer, streams, dec_seq):
    past = cache_fk.shape[3]
    new_spec = lambda g: pl.BlockSpec((dec_seq, LANES), lambda b, hp: (b, g * N_PAIR + hp))
    cache_spec = pl.BlockSpec((1, 1, LANES, past), lambda b, hp: (layer, b, hp, 0))
    drow_spec = pl.BlockSpec((1, H_FOX, SUBLANES, past), lambda b, hp: (layer, 0, b // SUBLANES, 0))
    cnrow_spec = pl.BlockSpec((1, SUBLANES, dec_seq), lambda b, hp: (b, 0, 0))
    ccol_spec = pl.BlockSpec((dec_seq, LANES), lambda b, hp: (b, 0))
    out_spec = pl.BlockSpec((dec_seq, LANES), lambda b, hp: (b, hp))
    out_shape = [jax.ShapeDtypeStruct((streams * dec_seq, W_ATT), F32)] * 2
    return pl.pallas_call(
        functools.partial(_sample_attn_kernel, blk=ATT_TILE),
        grid=(streams, N_PAIR), name="sample_attention",
        in_specs=[new_spec(0), new_spec(0), new_spec(0), cache_spec, cache_spec, drow_spec,
                  cnrow_spec, ccol_spec, new_spec(1), new_spec(1), new_spec(1), cache_spec,
                  cache_spec],
        out_specs=[out_spec, out_spec], out_shape=out_shape,
        scratch_shapes=[pltpu.VMEM((2 * dec_seq, 1), F32), pltpu.VMEM((2 * dec_seq, LANES), F32)],
        compiler_params=_params(("parallel", "parallel"), 48),
    )(q, kb, vb, cache_fk, cache_fv, drow, cnrow, ccol, q, kb, vb, cache_sk, cache_sv)


def _merge_kernel(x_ref, of_ref, os_ref, u_ref, vn_ref, ws_ref, bs_ref, gmix_ref, wout_ref,
                  g_ref, b_ref, o_ref, *, alpha):
    tm = x_ref.shape[0]
    r = lax.broadcasted_iota(jnp.int32, (SGU_CHUNK, SGU_CHUNK), 0)
    c = lax.broadcasted_iota(jnp.int32, (SGU_CHUNK, SGU_CHUNK), 1)
    tril = r >= c
    head_mean = jnp.where(r // D_HEAD == c // D_HEAD, 1.0 / D_HEAD, 0.0).astype(BF16)
    lane = lax.broadcasted_iota(jnp.int32, (SGU_CHUNK, LANES), 1)

    gate_pieces = []
    for p in range(W_SGU // LANES):
        w0 = jnp.where(tril, ws_ref[2 * p], 0.0).astype(BF16)
        w1 = jnp.where(tril, ws_ref[2 * p + 1], 0.0).astype(BF16)
        bias = bs_ref[:, p * LANES:(p + 1) * LANES]
        chunks = []
        for ch in range(tm // SGU_CHUNK):
            rows = slice(ch * SGU_CHUNK, (ch + 1) * SGU_CHUNK)
            vp = vn_ref[rows, p * LANES:(p + 1) * LANES].astype(BF16)
            s = jnp.where(lane < D_HEAD, _dot(w0, vp), _dot(w1, vp)) + bias
            chunks.append(u_ref[rows, p * LANES:(p + 1) * LANES] * s)
        gate_pieces.append(jnp.concatenate(chunks, axis=0))

    pieces = ([of_ref[:, k * LANES:(k + 1) * LANES] for k in range(N_PAIR)] + gate_pieces
              + [os_ref[:, k * LANES:(k + 1) * LANES] for k in range(N_PAIR)])
    normed = []
    for k, o in enumerate(pieces):
        ms = _dot((o * o).astype(BF16), head_mean)
        gain = gmix_ref[:, k * LANES:(k + 1) * LANES]
        normed.append((o * lax.rsqrt(ms + RMS_EPS) * gain).astype(BF16))
    mix = _dot(jnp.concatenate(normed, axis=1), wout_ref[...])
    o_ref[...] = _layer_norm(alpha * x_ref[...] + mix, g_ref[...], b_ref[...])


def _merge(x, of, os_, u, vn, ws, bs, gmix, wout, g, b, *, alpha, name):
    n, d = x.shape
    tm = ROW_TILE
    row = lambda w: pl.BlockSpec((tm, w), lambda i: (i, 0))
    return pl.pallas_call(
        functools.partial(_merge_kernel, alpha=alpha), grid=(n // tm,), name=name,
        in_specs=[row(d), row(W_ATT), row(W_ATT), row(W_SGU), row(W_SGU)]
        + [_resident(a.shape) for a in (ws, bs, gmix, wout, g, b)],
        out_specs=row(d), out_shape=jax.ShapeDtypeStruct((n, d), F32),
        compiler_params=_params(("parallel",), 48),
    )(x, of, os_, u, vn, ws, bs, gmix, wout, g, b)


def _mlp_kernel(x_ref, wup_ref, wdn_ref, g_ref, b_ref, o_ref, *, alpha, ff_chunk):
    x = x_ref[...]
    xb = x.astype(BF16)
    acc = alpha * x
    for c0 in range(0, wup_ref.shape[1], ff_chunk):
        h = jnp.maximum(_dot(xb, wup_ref[:, c0:c0 + ff_chunk]), 0.0)
        acc = acc + _dot((h * h).astype(BF16), wdn_ref[c0:c0 + ff_chunk, :])
    o_ref[...] = _layer_norm(acc, g_ref[...], b_ref[...])


def _mlp(x, wup, wdn, g, b, *, alpha, name):
    n, d = x.shape
    tm = ROW_TILE
    row = pl.BlockSpec((tm, d), lambda i: (i, 0))
    return pl.pallas_call(
        functools.partial(_mlp_kernel, alpha=alpha, ff_chunk=1024), grid=(n // tm,), name=name,
        in_specs=[row] + [_resident(a.shape) for a in (wup, wdn, g, b)],
        out_specs=row, out_shape=jax.ShapeDtypeStruct((n, d), F32),
        compiler_params=_params(("parallel",), 56),
    )(x, wup, wdn, g, b)


def _pack_w_in(w, b_f):
    sizes = [W_ATT, W_ATT, W_ATT, H_FOX, W_SGU, W_SGU, W_ATT, W_ATT, W_ATT]
    offs = [0]
    for s in sizes:
        offs.append(offs[-1] + s)
    q_f, k_f, v_f, f_lg, u_g, v_g, q_s, k_s, v_s = (w[:, offs[i]:offs[i + 1]] for i in range(9))
    pad_cols = lambda a, width: jnp.pad(a, ((0, 0), (0, width - a.shape[1])))
    w_rows = jnp.concatenate([q_f, q_s, k_f, k_s, v_f, v_s, u_g, v_g, pad_cols(f_lg, LANES)],
                             axis=1).astype(BF16)
    w_cols = jnp.concatenate([q_f, q_s, k_f, k_s, v_f, v_s, pad_cols(f_lg, F_ROWS)],
                             axis=1).T.astype(BF16)
    bfc = jnp.pad(b_f, (0, LANES - H_FOX)).reshape(1, LANES)
    bfr = jnp.broadcast_to(jnp.pad(b_f, (0, SUBLANES - H_FOX)).reshape(SUBLANES, 1),
                           (SUBLANES, LANES))
    return w_rows, w_cols, bfc, bfr


def kernel(x_prompt, x_sample, cache_fox_k, cache_fox_v, cache_fox_logf, cache_sb_k, cache_sb_v,
           w_in, b_f, g_v, b_v, w_s, b_s, g_mix, w_out, ln1_g, ln1_b, w_up, w_down, ln2_g, ln2_b):
    depth = w_in.shape[0]
    batch, seq, d_model = x_prompt.shape
    streams, dec_seq, _ = x_sample.shape
    past = cache_fox_k.shape[2]
    n_prompt = batch * seq
    n_sample = streams * dec_seq
    alpha = (2 * depth) ** 0.25
    assert seq % ROW_TILE == 0 and n_sample % ROW_TILE == 0 and ROW_TILE % dec_seq == 0
    assert dec_seq == D_HEAD and 2 * dec_seq == SGU_CHUNK and past % ATT_TILE == 0
    assert streams % SUBLANES == 0

    xp = x_prompt.reshape(n_prompt, d_model)
    xs = x_sample.reshape(n_sample, d_model)
    row2 = lambda a: a.reshape(1, -1)
    cache_t = lambda c: jnp.transpose(c, (0, 1, 3, 4, 2)).reshape(depth, streams, W_ATT, past)
    ck_f, cv_f, ck_s, cv_s = (cache_t(c) for c in (cache_fox_k, cache_fox_v, cache_sb_k, cache_sb_v))
    lf_cache = jnp.transpose(cache_fox_logf.astype(F32), (0, 3, 1, 2))
    drow = _suffix_sum(lf_cache.reshape(depth * H_FOX * streams, past)).reshape(
        depth, H_FOX, streams, past)

    outs = [[] for _ in range(11)]
    for l in range(depth):
        w_rows, w_cols, bfc, bfr = _pack_w_in(w_in[l], b_f[l])
        gv, bv = row2(g_v[l]), row2(b_v[l])
        wout = w_out[l].astype(BF16)
        wup, wdn = w_up[l].astype(BF16), w_down[l].astype(BF16)
        gmix, g1, b1, g2, b2 = (row2(a[l]) for a in (g_mix, ln1_g, ln1_b, ln2_g, ln2_b))

        w_l = w_s[l]
        half = w_l[:, :dec_seq, :dec_seq]
        zeros = jnp.zeros_like(half)
        w_blockdiag = jnp.concatenate([jnp.concatenate([half, zeros], axis=2),
                                       jnp.concatenate([zeros, half], axis=2)], axis=1)
        bias_full = jnp.repeat(b_s[l].T, D_HEAD, axis=1)
        bias_half = jnp.concatenate([bias_full[:dec_seq], bias_full[:dec_seq]], axis=0)

        (qt, kb, ktf, kts, vtf, vts, vtbf, vtbs, lfrow, crow, ccolp, u, vn, knmax,
         qn2row) = _in_proj_prompt(
            xp, w_rows, w_cols, bfr, gv, bv, batch=batch, seq=seq)
        of = _prompt_attention(qt, kb, vtbf, 0, batch=batch, seq=seq, crow=crow, ccolp=ccolp,
                               knmax=knmax, qn2row=qn2row)
        os_ = _prompt_attention(qt, kb, vtbs, 1, batch=batch, seq=seq)
        x1 = _merge(xp, of, os_, u, vn, w_l, bias_full, gmix, wout, g1, b1, alpha=alpha,
                    name="merge_prompt")
        xp = _mlp(x1, wup, wdn, g2, b2, alpha=alpha, name="mlp_prompt")
        heads_t = lambda a: jnp.transpose(a.reshape(batch, H_FOX, D_HEAD, seq), (0, 3, 1, 2))
        p_out = (heads_t(ktf), heads_t(vtf),
                 jnp.transpose(lfrow[:H_FOX].reshape(H_FOX, batch, seq), (1, 2, 0)),
                 heads_t(kts), heads_t(vts))

        (q, kb, vb, k, v, lfcol, ccol, cnrow, u, vn) = _in_proj_sample(
            xs, w_rows, w_cols, bfc, bfr, gv, bv, dec_seq=dec_seq)
        of, os_ = _sample_attention(q, kb, vb, ck_f, cv_f, drow, cnrow, ccol, ck_s, cv_s,
                                    layer=l, streams=streams, dec_seq=dec_seq)
        x1 = _merge(xs, of, os_, u, vn, w_blockdiag, bias_half, gmix, wout, g1, b1, alpha=alpha,
                    name="merge_sample")
        xs = _mlp(x1, wup, wdn, g2, b2, alpha=alpha, name="mlp_sample")
        heads = lambda a, g: a[:, g * W_ATT:(g + 1) * W_ATT].reshape(streams, dec_seq, H_FOX, D_HEAD)
        s_out = (heads(k, 0), heads(v, 0), lfcol[:, :H_FOX].reshape(streams, dec_seq, H_FOX),
                 heads(k, 1), heads(v, 1), vn.reshape(streams, dec_seq, W_SGU))

        for acc, val in zip(outs, p_out + s_out):
            acc.append(val)

    return (xp.reshape(batch, seq, d_model), xs.reshape(streams, dec_seq, d_model),
            *(jnp.stack(o) for o in outs))
```

```python
import functools
import math

import jax
import jax.numpy as jnp
from jax import lax
from jax.experimental import pallas as pl
from jax.experimental.pallas import tpu as pltpu

F32 = jnp.float32
BF16 = jnp.bfloat16

LANES = 128
SUBLANES = 8
VMEM_BYTES_V7X = 64 * 1024 * 1024

D_HEAD = 64
H_FOX = 6
H_SB = 6
H_SGU = 4
W_ATT = H_FOX * D_HEAD
W_SGU = H_SGU * D_HEAD
N_PAIR = W_ATT // LANES
SGU_CHUNK = 128
LN_EPS = 1e-5
RMS_EPS = 1e-6
NEG_INF = -1e30
SB_CLOSED = 110.0
FOX_DEAD = 110.0
LOG2E = math.log2(math.e)
FOX_L_FLOOR = 1e-25
FOX_WINDOW = 5
SB_WINDOW = 2
SB_SUFFIX_BLOCK = 128

ROW_TILE = 512
ATT_TILE = 256

_C_Q = 0
_C_K = 2 * W_ATT
_C_V = 4 * W_ATT
_C_G = 6 * W_ATT
_C_F = _C_G + 2 * W_SGU
W_ROWS = _C_F + LANES
_R_Q = 0
_R_K = 2 * W_ATT
_R_V = 4 * W_ATT
_R_F = 6 * W_ATT
F_ROWS = 2 * SUBLANES
W_COLS = _R_F + F_ROWS


def _params(semantics, vmem_mib, flags=None):
    assert vmem_mib * 1024 * 1024 < VMEM_BYTES_V7X
    return pltpu.CompilerParams(dimension_semantics=semantics,
                                vmem_limit_bytes=vmem_mib * 1024 * 1024, flags=flags)


def _resident(shape):
    nd = len(shape)
    return pl.BlockSpec(shape, lambda *_: (0,) * nd, pipeline_mode=pl.Buffered(1))


def _split3(x):
    hi = x.astype(BF16)
    r1 = x - hi.astype(F32)
    mid = r1.astype(BF16)
    lo = (r1 - mid.astype(F32)).astype(BF16)
    return hi, mid, lo


def _dot(a, b):
    return jnp.dot(a, b, preferred_element_type=F32)


def _dot_nt(a, b):
    return lax.dot_general(a, b, (((1,), (1,)), ((), ())), preferred_element_type=F32)


def _ones_dot3(x, ones_mat, left):
    if left:
        return sum(_dot(ones_mat, p) for p in _split3(x))
    return sum(_dot(p, ones_mat) for p in _split3(x))


def _gelu_tanh(x):
    return 0.5 * x * (1.0 + jnp.tanh(0.7978845608028654 * (x + 0.044715 * (x * x * x))))


def _log_sigmoid(x):
    return jnp.minimum(x, 0.0) - jnp.log1p(jnp.exp(-jnp.abs(x)))


def _softplus(z):
    return jnp.maximum(z, 0.0) + jnp.log(1.0 + jnp.exp(-jnp.abs(z)))


def _softplus2(z2):
    neg_abs = pltpu.bitcast(pltpu.bitcast(z2, jnp.uint32) | jnp.uint32(0x80000000), F32)
    return jnp.maximum(z2, 0.0) + jnp.log2(1.0 + jnp.exp2(neg_abs))


def _layer_norm(x, g, b):
    mu = jnp.mean(x, axis=-1, keepdims=True)
    xc = x - mu
    var = jnp.mean(xc * xc, axis=-1, keepdims=True)
    return xc * lax.rsqrt(var + LN_EPS) * g + b


def _tri_ones(n, kind, seg=None):
    r = lax.broadcasted_iota(jnp.int32, (n, n), 0)
    c = lax.broadcasted_iota(jnp.int32, (n, n), 1)
    keep = {"lower": c <= r, "upper": r <= c, "strict_lower": r > c, "strict_upper": r < c}[kind]
    if seg is not None:
        keep = jnp.logical_and(keep, r // seg == c // seg)
    return jnp.where(keep, 1.0, 0.0).astype(BF16)


def _pick_lane(x, idx):
    lane = lax.broadcasted_iota(jnp.int32, x.shape, 1)
    return jnp.sum(jnp.where(lane == idx, x, 0.0), axis=1, keepdims=True)


def _pick_row(x, idx):
    sub = lax.broadcasted_iota(jnp.int32, x.shape, 0)
    return jnp.sum(jnp.where(sub == idx, x, 0.0), axis=0, keepdims=True)


def _gates(xb, wr_ref, gv_ref, bv_ref, u_ref, vn_ref):
    u_ref[...] = _gelu_tanh(_dot(xb, wr_ref[:, _C_G:_C_G + W_SGU]))
    vg = _dot(xb, wr_ref[:, _C_G + W_SGU:_C_G + 2 * W_SGU])
    vn_ref[...] = _layer_norm(_gelu_tanh(vg), gv_ref[...], bv_ref[...])


def _log_forget_rows(xb, wc_ref, bfr_ref):
    sub = lax.broadcasted_iota(jnp.int32, (SUBLANES, xb.shape[0]), 0)
    z = _dot_nt(wc_ref[_R_F:_R_F + F_ROWS, :], xb)[:SUBLANES] + bfr_ref[:, 0:1]
    return jnp.where(sub < H_FOX, _log_sigmoid(z), 0.0)


def _in_proj_prompt_kernel(*refs, layer, depth):
    n_in = 6 if layer == 0 else 10
    x_ref, wr_ref, wc_ref, bfr_ref, gv_ref, bv_ref = refs[:6]
    (qt_ref, k_ref, ktf_ref, kts_ref, vtf_ref, vts_ref, vtbf_ref, vtbs_ref,
     lfrow_ref, crow_ref, ccolp_ref, u_ref, vn_ref, knmax_ref, qn2row_ref,
     carry_row_ref) = refs[n_in:]
    i = pl.program_id(1)
    tm = x_ref.shape[0]
    tk = vtbf_ref.shape[3]
    xb = x_ref[...].astype(BF16)

    def blocks(zt):
        return [zt[:, kb * tk:(kb + 1) * tk].astype(BF16) for kb in range(tm // tk)]

    def store_cache_rows(ref, zt):
        if layer == 0:
            for l in range(depth):
                ref[l, 0] = zt if l == 0 else jnp.zeros_like(zt)
        else:
            ref[0, 0] = zt

    scale = LOG2E / math.sqrt(D_HEAD)
    for kb, blk in enumerate(blocks(_dot_nt(wc_ref[_R_Q:_R_Q + 2 * W_ATT, :], xb) * scale)):
        qt_ref[0, kb] = blk
        qf = blk[:W_ATT].astype(F32)
        qn2 = jnp.sum((qf * qf).reshape(H_FOX, D_HEAD, tk), axis=1)
        qn2row_ref[kb] = jnp.concatenate([qn2, jnp.zeros((SUBLANES - H_FOX, tk), F32)], axis=0)
    for r0, full_ref, blk_ref in ((_R_V, vtf_ref, vtbf_ref), (_R_V + W_ATT, vts_ref, vtbs_ref)):
        zt = _dot_nt(wc_ref[r0:r0 + W_ATT, :], xb)
        store_cache_rows(full_ref, zt)
        for kb, blk in enumerate(blocks(zt)):
            blk_ref[0, kb] = blk
    k_rows = _dot(xb, wr_ref[:, _C_K:_C_K + 2 * W_ATT])
    k_ref[...] = k_rows.astype(BF16)
    kt_f = k_rows[:, :W_ATT].T
    store_cache_rows(ktf_ref, kt_f)
    store_cache_rows(kts_ref, k_rows[:, W_ATT:].T)

    _gates(xb, wr_ref, gv_ref, bv_ref, u_ref, vn_ref)

    @pl.when(i == 0)
    def _():
        carry_row_ref[...] = jnp.zeros_like(carry_row_ref)
        knmax_ref[...] = jnp.zeros_like(knmax_ref)

    kn2 = jnp.sum((kt_f * kt_f).reshape(H_FOX, D_HEAD, tm), axis=1)
    kn2 = jnp.broadcast_to(jnp.max(kn2, axis=1, keepdims=True), (H_FOX, LANES))
    kn2 = jnp.concatenate([kn2, jnp.zeros((SUBLANES - H_FOX, LANES), F32)], axis=0)
    knmax_ref[0] = jnp.maximum(knmax_ref[0], kn2)

    lf_row = _log_forget_rows(xb, wc_ref, bfr_ref)
    lfrow_ref[...] = lf_row
    c_row = _ones_dot3(lf_row, _tri_ones(tm, "upper"), left=False) + carry_row_ref[:, 0:1]
    carry_row_ref[...] = jnp.broadcast_to(c_row[:, tm - 1:tm], carry_row_ref.shape)
    c_row = c_row * LOG2E
    for kb in range(tm // tk):
        crow_ref[kb] = c_row[:, kb * tk:(kb + 1) * tk]
    pad = jnp.zeros((LANES - SUBLANES, tm), F32)
    for p in range(N_PAIR):
        rows = c_row if p == 0 else jnp.concatenate([c_row[2 * p:], c_row[:2 * p]], axis=0)
        ccolp_ref[:, p * LANES:(p + 1) * LANES] = jnp.concatenate([rows, pad], axis=0).T


def _in_proj_prompt(x, w_rows, w_cols, bfr, gv, bv, *, batch, seq, layer, depth, cache_rows):
    n, d = x.shape
    tm, tk = ROW_TILE, ATT_TILE
    nt = seq // tm
    row = lambda w: pl.BlockSpec((tm, w), lambda b, i: (b * nt + i, 0))
    if layer == 0:
        t_spec = pl.BlockSpec((depth, 1, W_ATT, tm), lambda b, i: (0, b, 0, i))
        extra_in, extra_specs, aliases = (), [], {}
    else:
        t_spec = pl.BlockSpec((1, 1, W_ATT, tm), lambda b, i: (layer, b, 0, i))
        extra_in = tuple(cache_rows)
        extra_specs = [pl.BlockSpec(memory_space=pl.ANY)] * 4
        aliases = {6 + k: 2 + k for k in range(4)}
    t_shape = jax.ShapeDtypeStruct((depth, batch, W_ATT, seq), F32)
    blk_spec = lambda w: pl.BlockSpec((1, tm // tk, w, tk), lambda b, i: (b, i, 0, 0))
    blk_shape = lambda w: jax.ShapeDtypeStruct((batch, seq // tk, w, tk), BF16)
    out_shape = [blk_shape(2 * W_ATT), jax.ShapeDtypeStruct((n, 2 * W_ATT), BF16),
                 t_shape, t_shape, t_shape, t_shape, blk_shape(W_ATT), blk_shape(W_ATT),
                 jax.ShapeDtypeStruct((SUBLANES, n), F32),
                 jax.ShapeDtypeStruct((n // tk, SUBLANES, tk), F32),
                 jax.ShapeDtypeStruct((n, N_PAIR * LANES), F32),
                 jax.ShapeDtypeStruct((n, W_SGU), F32), jax.ShapeDtypeStruct((n, W_SGU), F32),
                 jax.ShapeDtypeStruct((batch, SUBLANES, LANES), F32),
                 jax.ShapeDtypeStruct((n // tk, SUBLANES, tk), F32)]
    out_specs = [blk_spec(2 * W_ATT), row(2 * W_ATT), t_spec, t_spec, t_spec, t_spec,
                 blk_spec(W_ATT), blk_spec(W_ATT),
                 pl.BlockSpec((SUBLANES, tm), lambda b, i: (0, b * nt + i)),
                 pl.BlockSpec((tm // tk, SUBLANES, tk), lambda b, i: (b * nt + i, 0, 0)),
                 row(N_PAIR * LANES), row(W_SGU), row(W_SGU),
                 pl.BlockSpec((1, SUBLANES, LANES), lambda b, i: (b, 0, 0)),
                 pl.BlockSpec((tm // tk, SUBLANES, tk), lambda b, i: (b * nt + i, 0, 0))]
    return pl.pallas_call(
        functools.partial(_in_proj_prompt_kernel, layer=layer, depth=depth),
        grid=(batch, nt), name="in_proj_prompt",
        in_specs=[row(d)] + [_resident(a.shape) for a in (w_rows, w_cols, bfr, gv, bv)]
        + extra_specs,
        out_specs=out_specs, out_shape=out_shape, input_output_aliases=aliases,
        scratch_shapes=[pltpu.VMEM((SUBLANES, LANES), F32)],
        compiler_params=_params(("parallel", "arbitrary"), 56),
    )(x, w_rows, w_cols, bfr, gv, bv, *extra_in)


def _in_proj_sample_kernel(*refs, dec_seq, layer, depth):
    n_in = 7 if layer == 0 else 11
    x_ref, wr_ref, wc_ref, bfc_ref, bfr_ref, gv_ref, bv_ref = refs[:7]
    (q_ref, kb_ref, vb_ref, kf_ref, ks_ref, vf_ref, vs_ref, lfcol_ref, ccol_ref, crow_ref,
     u_ref, vn_ref) = refs[n_in:]
    tm = x_ref.shape[0]
    xb = x_ref[...].astype(BF16)
    scale = 1.0 / math.sqrt(D_HEAD)
    q_ref[...] = (_dot(xb, wr_ref[:, _C_Q:_C_Q + 2 * W_ATT]) * scale).astype(BF16)

    def store_cache_rows(ref, z):
        for s in range(tm // dec_seq):
            for h in range(W_ATT // D_HEAD):
                blk = z[s * dec_seq:(s + 1) * dec_seq, h * D_HEAD:(h + 1) * D_HEAD]
                ref[0, s, h] = blk
                if layer == 0:
                    for l in range(1, depth):
                        ref[l, s, h] = jnp.zeros_like(blk)

    for c0, refs_fs, half_ref in ((_C_K, (kf_ref, ks_ref), kb_ref), (_C_V, (vf_ref, vs_ref), vb_ref)):
        z = _dot(xb, wr_ref[:, c0:c0 + 2 * W_ATT])
        half_ref[...] = z.astype(BF16)
        store_cache_rows(refs_fs[0], z[:, :W_ATT])
        store_cache_rows(refs_fs[1], z[:, W_ATT:])
    _gates(xb, wr_ref, gv_ref, bv_ref, u_ref, vn_ref)

    lane = lax.broadcasted_iota(jnp.int32, (tm, LANES), 1)
    z = _dot(xb, wr_ref[:, _C_F:_C_F + LANES]) + bfc_ref[...]
    lf_col = jnp.where(lane < H_FOX, _log_sigmoid(z), 0.0)
    lfcol_ref[...] = lf_col
    ccol_ref[...] = _ones_dot3(lf_col, _tri_ones(tm, "lower", dec_seq), left=True)
    lf_row = _log_forget_rows(xb, wc_ref, bfr_ref)
    c_row = _ones_dot3(lf_row, _tri_ones(tm, "upper", dec_seq), left=False)
    for s in range(tm // dec_seq):
        crow_ref[s] = c_row[:, s * dec_seq:(s + 1) * dec_seq]


def _in_proj_sample(x, w_rows, w_cols, bfc, bfr, gv, bv, *, dec_seq, layer, depth, cache_rows):
    n, d = x.shape
    tm = ROW_TILE
    heads = W_ATT // D_HEAD
    row = lambda w: pl.BlockSpec((tm, w), lambda i: (i, 0))
    sds = lambda w, dt: jax.ShapeDtypeStruct((n, w), dt)
    if layer == 0:
        c_spec = pl.BlockSpec((depth, tm // dec_seq, heads, dec_seq, D_HEAD), lambda i: (0, i, 0, 0, 0))
        extra_in, extra_specs, aliases = (), [], {}
    else:
        c_spec = pl.BlockSpec((1, tm // dec_seq, heads, dec_seq, D_HEAD), lambda i: (layer, i, 0, 0, 0))
        extra_in = tuple(cache_rows)
        extra_specs = [pl.BlockSpec(memory_space=pl.ANY)] * 4
        aliases = {7 + k: 3 + k for k in range(4)}
    c_shape = jax.ShapeDtypeStruct((depth, n // dec_seq, heads, dec_seq, D_HEAD), F32)
    out_shape = [sds(2 * W_ATT, BF16)] * 3 + [c_shape] * 4 + [sds(LANES, F32)] * 2 + [
        jax.ShapeDtypeStruct((n // dec_seq, SUBLANES, dec_seq), F32), sds(W_SGU, F32), sds(W_SGU, F32)]
    out_specs = [row(2 * W_ATT)] * 3 + [c_spec] * 4 + [row(LANES)] * 2 + [
        pl.BlockSpec((tm // dec_seq, SUBLANES, dec_seq), lambda i: (i, 0, 0)), row(W_SGU), row(W_SGU)]
    return pl.pallas_call(
        functools.partial(_in_proj_sample_kernel, dec_seq=dec_seq, layer=layer, depth=depth),
        grid=(n // tm,), name="in_proj_sample",
        in_specs=[row(d)] + [_resident(a.shape) for a in (w_rows, w_cols, bfc, bfr, gv, bv)]
        + extra_specs,
        out_specs=out_specs, out_shape=out_shape, input_output_aliases=aliases,
        compiler_params=_params(("parallel",), 56),
    )(x, w_rows, w_cols, bfc, bfr, gv, bv, *extra_in)


def _split_head_rows(qt):
    sub = lax.broadcasted_iota(jnp.int32, qt.shape, 0)
    low = sub < D_HEAD
    zero = jnp.zeros_like(qt)
    return jnp.where(low, qt, zero), jnp.where(low, zero, qt)


def _fox_update_t(s, ct, pv, state, valid):
    m, l, acc = state
    if valid is not None:
        s = jnp.where(valid, s, NEG_INF)
    m_new = jnp.maximum(m, jnp.max(s, axis=0, keepdims=True) + ct)
    p = jnp.exp2(s + (ct - m_new))
    alpha = jnp.exp2(m - m_new)
    l = alpha * l + jnp.sum(p, axis=0, keepdims=True)
    acc = alpha * acc + pv(p.astype(BF16))
    return m_new, l, acc


def _sb_update_t(z, pv, suffix_ones, state, valid):
    carry, acc = state
    r = _softplus2(z)
    if valid is not None:
        r = jnp.where(valid, r, 0.0)
    r_hi = r.astype(BF16)
    r_lo = (r - r_hi.astype(F32)).astype(BF16)
    after = _dot(suffix_ones, r_hi) + _dot(suffix_ones, r_lo) + carry
    a = jnp.exp2((z - r) - after)
    if valid is not None:
        a = jnp.where(valid, a, 0.0)
    acc = acc + pv(a.astype(BF16))
    carry = carry + jnp.sum(r, axis=0, keepdims=True)
    return carry, acc


def _fox_prompt_kernel(qt_ref, k_ref, vt_ref, crow_ref, ccolp_ref, knmax_ref, qn2row_ref, o_ref,
                       *, window_blocks):
    tq = qt_ref.shape[3]
    tk = vt_ref.shape[3]
    hp = pl.program_id(1)
    i = pl.program_id(2)
    q2 = jnp.concatenate(_split_head_rows(qt_ref[0, 0]), axis=1)
    c_q = crow_ref[i]
    ct = jnp.concatenate([_pick_row(c_q, 2 * hp), _pick_row(c_q, 2 * hp + 1)], axis=1)

    key = lax.broadcasted_iota(jnp.int32, (tk, 2 * tq), 0)
    qry = lax.broadcasted_iota(jnp.int32, (tk, 2 * tq), 1) % tq
    causal = key <= qry

    def scores(j0, nblk):
        start = pl.multiple_of(j0 * tk, tk)
        kb = k_ref[pl.ds(start, nblk * tk), :]
        cc = ccolp_ref[pl.ds(start, nblk * tk), :]
        ck = jnp.concatenate([jnp.broadcast_to(cc[:, 0:1], (nblk * tk, tq)),
                              jnp.broadcast_to(cc[:, 1:2], (nblk * tk, tq))], axis=1)
        return _dot(kb, q2) - ck

    def pv(p, j0, nblk):
        vts = [vt_ref[0, j0 + w] for w in range(nblk)]
        outs = []
        for h in range(2):
            vth = jnp.concatenate([v[h * D_HEAD:(h + 1) * D_HEAD] for v in vts], axis=1)
            outs.append(_dot(vth, p[:, h * tq:(h + 1) * tq]))
        return jnp.concatenate(outs, axis=1)

    def pair_row(a, b):
        return jnp.concatenate([jnp.broadcast_to(a, (1, tq)), jnp.broadcast_to(b, (1, tq))], axis=1)

    qn2 = qn2row_ref[i]
    qn2 = jnp.concatenate([_pick_row(qn2, 2 * hp), _pick_row(qn2, 2 * hp + 1)], axis=1)
    kn2 = knmax_ref[0]
    kn2 = jnp.concatenate([_pick_row(kn2, 2 * hp)] * (tq // LANES)
                          + [_pick_row(kn2, 2 * hp + 1)] * (tq // LANES), axis=1)
    qk_bound = jnp.sqrt(qn2 * kn2) * 1.02

    shift = ct - qk_bound

    def reaches(j):
        return jnp.max(decay(j)) > -FOX_DEAD * LOG2E

    def decay(j):
        c_end = crow_ref[jnp.maximum(j, 0)][:, tk - 1:tk]
        return ct - pair_row(_pick_row(c_end, 2 * hp), _pick_row(c_end, 2 * hp + 1))

    def window(nblk):
        j0 = i - (nblk - 1)
        p = jnp.exp2(scores(j0, nblk) + shift)
        body_rows = (nblk - 1) * tk
        diag = jnp.where(causal, p[body_rows:], 0.0)
        p = diag if nblk == 1 else jnp.concatenate([p[:body_rows], diag], axis=0)
        return (j0 - 1, jnp.sum(p, axis=0, keepdims=True), pv(p.astype(BF16), j0, nblk))

    j, l, acc = lax.switch(jnp.minimum(i, window_blocks - 1),
                           [functools.partial(window, n) for n in range(1, window_blocks + 1)])

    def cond(c):
        j, go, _, _ = c
        return jnp.logical_and(j >= 0, go)

    def body(c):
        j, _, l, acc = c
        p = jnp.exp2(scores(j, 1) + shift)
        return (j - 1, reaches(j - 1), l + jnp.sum(p, axis=0, keepdims=True),
                acc + pv(p.astype(BF16), j, 1))

    def exact():
        def step(t, state):
            jj = i - 1 - t
            return _fox_update_t(scores(jj, 1), ct, lambda p: pv(p, jj, 1), state, None)

        init = (jnp.full((1, 2 * tq), NEG_INF, F32), jnp.zeros((1, 2 * tq), F32),
                jnp.zeros((D_HEAD, 2 * tq), F32))
        state = _fox_update_t(scores(i, 1), ct, lambda p: pv(p, i, 1), init, causal)
        _, l_x, acc_x = lax.fori_loop(0, i, step, state)
        return l_x, acc_x

    def beyond_window():
        _, _, l_w, acc_w = lax.while_loop(cond, body, (j, reaches(j), l, acc))
        return lax.cond(jnp.min(l_w) < FOX_L_FLOOR, exact, lambda: (l_w, acc_w))

    live_left = jnp.logical_and(decay(j) > -FOX_DEAD * LOG2E, j >= 0)
    uncommon = jnp.max(jnp.where(jnp.logical_or(live_left, l < FOX_L_FLOOR), 1.0, 0.0)) > 0.0
    l, acc = lax.cond(uncommon, beyond_window, lambda: (l, acc))
    o = acc / l
    o_ref[...] = jnp.concatenate([o[:, :tq], o[:, tq:]], axis=0).T


def _sb_prompt_kernel(qt_ref, k_ref, vt_ref, o_ref, *, window_blocks):
    tq = qt_ref.shape[3]
    tk = vt_ref.shape[3]
    i = pl.program_id(2)
    q2 = jnp.concatenate(_split_head_rows(qt_ref[0, 0]), axis=1)
    suffix_ones = _tri_ones(tk, "strict_upper")
    ts = SB_SUFFIX_BLOCK
    sub_ones = _tri_ones(ts, "strict_upper")

    def block(j, state, valid):
        kb = k_ref[pl.ds(pl.multiple_of(j * tk, tk), tk), :]
        vt = vt_ref[0, j]

        def pv(a):
            return jnp.concatenate([_dot(vt[:D_HEAD], a[:, :tq]), _dot(vt[D_HEAD:], a[:, tq:])],
                                   axis=1)

        return _sb_update_t(_dot(kb, q2), pv, suffix_ones, state, valid)

    def still_open(carry):
        return jnp.min(carry) < SB_CLOSED * LOG2E

    key = lax.broadcasted_iota(jnp.int32, (tk, 2 * tq), 0)
    qry = lax.broadcasted_iota(jnp.int32, (tk, 2 * tq), 1) % tq
    causal = key < qry

    def window(nblk):
        j0 = i - (nblk - 1)
        z = _dot(k_ref[pl.ds(pl.multiple_of(j0 * tk, tk), nblk * tk), :], q2)
        r = _softplus2(z)
        body_rows = (nblk - 1) * tk
        diag = jnp.where(causal, r[body_rows:], 0.0)
        r = diag if nblk == 1 else jnp.concatenate([r[:body_rows], diag], axis=0)
        carry = jnp.zeros((1, 2 * tq), F32)
        n_sub = nblk * tk // ts
        a_blocks = [None] * n_sub
        for w in reversed(range(n_sub)):
            rows = slice(w * ts, (w + 1) * ts)
            r_w = r[rows]
            after = _dot(sub_ones, r_w.astype(BF16)) + carry
            a_blocks[w] = jnp.exp2((z[rows] - r_w) - after)
            carry = carry + jnp.sum(r_w, axis=0, keepdims=True)
        a = jnp.concatenate(a_blocks, axis=0)
        diag = jnp.where(causal, a[body_rows:], 0.0)
        a = (diag if nblk == 1 else jnp.concatenate([a[:body_rows], diag], axis=0)).astype(BF16)
        vts = [vt_ref[0, j0 + w] for w in range(nblk)]
        outs = []
        for h in range(2):
            vth = jnp.concatenate([v[h * D_HEAD:(h + 1) * D_HEAD] for v in vts], axis=1)
            outs.append(_dot(vth, a[:, h * tq:(h + 1) * tq]))
        return j0 - 1, (carry, jnp.concatenate(outs, axis=1))

    j, state = lax.cond(i >= window_blocks - 1, lambda: window(window_blocks), lambda: window(1))

    def cond(c):
        j, go, _ = c
        return jnp.logical_and(j >= 0, go)

    def body(c):
        j, _, st = c
        st = block(j, st, None)
        return j - 1, still_open(st[0]), st

    _, _, (_, acc) = lax.while_loop(cond, body, (j, still_open(state[0]), state))
    o_ref[...] = jnp.concatenate([acc[:, :tq], acc[:, tq:]], axis=0).T


def _prompt_attention(qt, k, vtb, group, *, batch, seq, crow=None, ccolp=None, knmax=None,
                      qn2row=None):
    tq = tk = ATT_TILE
    nq = seq // tq
    col0 = group * N_PAIR
    qt_spec = pl.BlockSpec((1, 1, LANES, tq), lambda b, hp, i: (b, i, col0 + hp, 0))
    k_spec = pl.BlockSpec((seq, LANES), lambda b, hp, i: (b, col0 + hp))
    vt_spec = pl.BlockSpec((1, seq // tk, LANES, tk), lambda b, hp, i: (b, 0, hp, 0))
    out_spec = pl.BlockSpec((tq, LANES), lambda b, hp, i: (b * nq + i, hp))
    out_shape = jax.ShapeDtypeStruct((batch * seq, W_ATT), F32)
    params = _params(("parallel", "parallel", "arbitrary"), 40)
    grid = (batch, N_PAIR, nq)
    if group == 1:
        return pl.pallas_call(
            functools.partial(_sb_prompt_kernel, window_blocks=SB_WINDOW), grid=grid,
            name="sb_prompt",
            in_specs=[qt_spec, k_spec, vt_spec], out_specs=out_spec, out_shape=out_shape,
            compiler_params=params)(qt, k, vtb)
    crow_spec = pl.BlockSpec((seq // tk, SUBLANES, tk), lambda b, hp, i: (b, 0, 0))
    ccolp_spec = pl.BlockSpec((seq, LANES), lambda b, hp, i: (b, hp))
    knmax_spec = pl.BlockSpec((1, SUBLANES, LANES), lambda b, hp, i: (b, 0, 0))
    return pl.pallas_call(
        functools.partial(_fox_prompt_kernel, window_blocks=FOX_WINDOW), grid=grid,
        name="fox_prompt",
        in_specs=[qt_spec, k_spec, vt_spec, crow_spec, ccolp_spec, knmax_spec, crow_spec],
        out_specs=out_spec, out_shape=out_shape,
        compiler_params=params)(qt, k, vtb, crow, ccolp, knmax, qn2row)


def _suffix_sum_kernel(x_ref, o_ref):
    rows, width = x_ref.shape
    ones = _tri_ones(LANES, "strict_lower")
    carry = jnp.zeros((rows, 1), F32)
    for kb in reversed(range(width // LANES)):
        xb = x_ref[:, kb * LANES:(kb + 1) * LANES]
        o_ref[:, kb * LANES:(kb + 1) * LANES] = _ones_dot3(xb, ones, left=False) + carry
        carry = carry + jnp.sum(xb, axis=1, keepdims=True)


def _suffix_sum(x):
    return pl.pallas_call(
        _suffix_sum_kernel, name="cache_suffix_sum",
        out_shape=jax.ShapeDtypeStruct(x.shape, F32),
        compiler_params=_params(None, 32))(x)


def _split_heads(q2):
    lane = lax.broadcasted_iota(jnp.int32, q2.shape, 1)
    low = lane < D_HEAD
    zero = jnp.zeros_like(q2)
    return jnp.where(low, q2, zero), jnp.where(low, zero, q2)


def _stack_heads(q2):
    return jnp.concatenate(_split_heads(q2), axis=0)


def _unstack_heads(o):
    half = o.shape[0] // 2
    lane = lax.broadcasted_iota(jnp.int32, (half, o.shape[1]), 1)
    return jnp.where(lane < D_HEAD, o[:half], o[half:])


def _sb_update(z, pv, suffix_ones, state, valid):
    carry, acc = state
    r = _softplus(z)
    if valid is not None:
        r = jnp.where(valid, r, 0.0)
    r_hi = r.astype(BF16)
    r_lo = (r - r_hi.astype(F32)).astype(BF16)
    after = _dot(r_hi, suffix_ones) + _dot(r_lo, suffix_ones) + carry
    a = jnp.exp((z - r) - after)
    if valid is not None:
        a = jnp.where(valid, a, 0.0)
    acc = acc + pv(a.astype(BF16))
    carry = carry + jnp.sum(r, axis=1, keepdims=True)
    return carry, acc


def _sample_attn_kernel(qf_ref, kfn_ref, vfn_ref, kfc_ref, vfc_ref, drow_ref, cnrow_ref, ccol_ref,
                        qs_ref, ksn_ref, vsn_ref, ksc_ref, vsc_ref, of_ref, os_ref,
                        carry_ref, acc_ref, *, blk):
    b = pl.program_id(0)
    hp = pl.program_id(1)
    tq = qf_ref.shape[0]
    past = kfc_ref.shape[3]
    row = lax.broadcasted_iota(jnp.int32, (2 * tq, tq), 0) % tq
    col = lax.broadcasted_iota(jnp.int32, (2 * tq, tq), 1)

    def per_head(fn):
        return jnp.concatenate([fn(0), fn(1)], axis=0)

    q2 = _stack_heads(qf_ref[...])
    kct = kfc_ref[0, 0].astype(BF16)
    vct = vfc_ref[0, 0].astype(BF16)
    cc = ccol_ref[...]
    cn = cnrow_ref[0]
    cq = per_head(lambda h: _pick_lane(cc, 2 * hp + h))
    qk_c = _dot(q2, kct)
    qk_n = _dot_nt(q2, kfn_ref[...])
    s_c = per_head(lambda h: qk_c[h * tq:(h + 1) * tq]
                   + _pick_row(drow_ref[0, 2 * hp + h], b % SUBLANES))
    s_n = per_head(lambda h: qk_n[h * tq:(h + 1) * tq] - _pick_row(cn, 2 * hp + h))
    s_n = jnp.where(col <= row, s_n, NEG_INF)
    m = jnp.maximum(jnp.max(s_c, axis=1, keepdims=True), jnp.max(s_n, axis=1, keepdims=True)) + cq
    shift = cq - m
    p_c = jnp.exp(s_c + shift)
    p_n = jnp.exp(s_n + shift)
    l = jnp.sum(p_c, axis=1, keepdims=True) + jnp.sum(p_n, axis=1, keepdims=True)
    o = _dot_nt(p_c.astype(BF16), vct) + _dot(p_n.astype(BF16), vfn_ref[...])
    of_ref[...] = _unstack_heads(o / l)

    q2 = _stack_heads(qs_ref[...])
    vn = vsn_ref[...]
    suffix_ones = _tri_ones(blk, "strict_lower")

    def cache_block(jb, state):
        kt = ksc_ref[0, 0, :, jb * blk:(jb + 1) * blk].astype(BF16)
        vt = vsc_ref[0, 0, :, jb * blk:(jb + 1) * blk].astype(BF16)
        return _sb_update(_dot(q2, kt), lambda a: _dot_nt(a, vt), suffix_ones, state, None)

    n_blocks = past // blk
    state = (jnp.zeros((2 * tq, 1), F32), jnp.zeros((2 * tq, LANES), F32))
    state = _sb_update(_dot_nt(q2, ksn_ref[...]), lambda a: _dot(a, vn),
                       _tri_ones(tq, "strict_lower"), state, col < row)
    carry_ref[...], acc_ref[...] = cache_block(n_blocks - 1, state)

    def still_open():
        return jnp.min(carry_ref[...]) < SB_CLOSED

    @pl.when(still_open())
    def _():
        for jb in reversed(range(n_blocks - 1)):
            @pl.when(still_open())
            def _():
                carry_ref[...], acc_ref[...] = cache_block(jb, (carry_ref[...], acc_ref[...]))

    os_ref[...] = _unstack_heads(acc_ref[...])


def _sample_attention(q, kb, vb, cache_fk, cache_fv, drow, cnrow, ccol, cache_sk, cache_sv,
                      *, layer, streams, dec_seq):
    past = cache_fk.shape[3]
    new_spec = lambda g: pl.BlockSpec((dec_seq, LANES), lambda b, hp: (b, g * N_PAIR + hp))
    cache_spec = pl.BlockSpec((1, 1, LANES, past), lambda b, hp: (layer, b, hp, 0))
    drow_spec = pl.BlockSpec((1, H_FOX, SUBLANES, past), lambda b, hp: (layer, 0, b // SUBLANES, 0))
    cnrow_spec = pl.BlockSpec((1, SUBLANES, dec_seq), lambda b, hp: (b, 0, 0))
    ccol_spec = pl.BlockSpec((dec_seq, LANES), lambda b, hp: (b, 0))
    out_spec = pl.BlockSpec((dec_seq, LANES), lambda b, hp: (b, hp))
    out_shape = [jax.ShapeDtypeStruct((streams * dec_seq, W_ATT), F32)] * 2
    return pl.pallas_call(
        functools.partial(_sample_attn_kernel, blk=ATT_TILE),
        grid=(streams, N_PAIR), name="sample_attention",
        in_specs=[new_spec(0), new_spec(0), new_spec(0), cache_spec, cache_spec, drow_spec,
                  cnrow_spec, ccol_spec, new_spec(1), new_spec(1), new_spec(1), cache_spec,
                  cache_spec],
        out_specs=[out_spec, out_spec], out_shape=out_shape,
        scratch_shapes=[pltpu.VMEM((2 * dec_seq, 1), F32), pltpu.VMEM((2 * dec_seq, LANES), F32)],
        compiler_params=_params(("parallel", "parallel"), 48),
    )(q, kb, vb, cache_fk, cache_fv, drow, cnrow, ccol, q, kb, vb, cache_sk, cache_sv)


def _merge_kernel(x_ref, of_ref, os_ref, u_ref, vn_ref, ws_ref, bs_ref, gmix_ref, wout_ref,
                  g_ref, b_ref, o_ref, *, alpha):
    tm = x_ref.shape[0]
    r = lax.broadcasted_iota(jnp.int32, (SGU_CHUNK, SGU_CHUNK), 0)
    c = lax.broadcasted_iota(jnp.int32, (SGU_CHUNK, SGU_CHUNK), 1)
    tril = r >= c
    head_mean = jnp.where(r // D_HEAD == c // D_HEAD, 1.0 / D_HEAD, 0.0).astype(BF16)
    lane = lax.broadcasted_iota(jnp.int32, (SGU_CHUNK, LANES), 1)

    gate_pieces = []
    for p in range(W_SGU // LANES):
        w0 = jnp.where(tril, ws_ref[2 * p], 0.0).astype(BF16)
        w1 = jnp.where(tril, ws_ref[2 * p + 1], 0.0).astype(BF16)
        bias = bs_ref[:, p * LANES:(p + 1) * LANES]
        chunks = []
        for ch in range(tm // SGU_CHUNK):
            rows = slice(ch * SGU_CHUNK, (ch + 1) * SGU_CHUNK)
            vp = vn_ref[rows, p * LANES:(p + 1) * LANES].astype(BF16)
            s = jnp.where(lane < D_HEAD, _dot(w0, vp), _dot(w1, vp)) + bias
            chunks.append(u_ref[rows, p * LANES:(p + 1) * LANES] * s)
        gate_pieces.append(jnp.concatenate(chunks, axis=0))

    pieces = ([of_ref[:, k * LANES:(k + 1) * LANES] for k in range(N_PAIR)] + gate_pieces
              + [os_ref[:, k * LANES:(k + 1) * LANES] for k in range(N_PAIR)])
    normed = []
    for k, o in enumerate(pieces):
        ms = _dot((o * o).astype(BF16), head_mean)
        gain = gmix_ref[:, k * LANES:(k + 1) * LANES]
        normed.append((o * lax.rsqrt(ms + RMS_EPS) * gain).astype(BF16))
    mix = _dot(jnp.concatenate(normed, axis=1), wout_ref[...])
    o_ref[...] = _layer_norm(alpha * x_ref[...] + mix, g_ref[...], b_ref[...])


def _merge(x, of, os_, u, vn, ws, bs, gmix, wout, g, b, *, alpha, name):
    n, d = x.shape
    tm = ROW_TILE
    row = lambda w: pl.BlockSpec((tm, w), lambda i: (i, 0))
    return pl.pallas_call(
        functools.partial(_merge_kernel, alpha=alpha), grid=(n // tm,), name=name,
        in_specs=[row(d), row(W_ATT), row(W_ATT), row(W_SGU), row(W_SGU)]
        + [_resident(a.shape) for a in (ws, bs, gmix, wout, g, b)],
        out_specs=row(d), out_shape=jax.ShapeDtypeStruct((n, d), F32),
        compiler_params=_params(("parallel",), 48),
    )(x, of, os_, u, vn, ws, bs, gmix, wout, g, b)


def _mlp_kernel(x_ref, wup_ref, wdn_ref, g_ref, b_ref, o_ref, *, alpha, ff_chunk):
    x = x_ref[...]
    xb = x.astype(BF16)
    acc = alpha * x
    for c0 in range(0, wup_ref.shape[1], ff_chunk):
        h = jnp.maximum(_dot(xb, wup_ref[:, c0:c0 + ff_chunk]), 0.0)
        acc = acc + _dot((h * h).astype(BF16), wdn_ref[c0:c0 + ff_chunk, :])
    o_ref[...] = _layer_norm(acc, g_ref[...], b_ref[...])


def _mlp(x, wup, wdn, g, b, *, alpha, name):
    n, d = x.shape
    tm = ROW_TILE
    row = pl.BlockSpec((tm, d), lambda i: (i, 0))
    return pl.pallas_call(
        functools.partial(_mlp_kernel, alpha=alpha, ff_chunk=1024), grid=(n // tm,), name=name,
        in_specs=[row] + [_resident(a.shape) for a in (wup, wdn, g, b)],
        out_specs=row, out_shape=jax.ShapeDtypeStruct((n, d), F32),
        compiler_params=_params(("parallel",), 56),
    )(x, wup, wdn, g, b)


def _pack_w_in(w, b_f):
    sizes = [W_ATT, W_ATT, W_ATT, H_FOX, W_SGU, W_SGU, W_ATT, W_ATT, W_ATT]
    offs = [0]
    for s in sizes:
        offs.append(offs[-1] + s)
    q_f, k_f, v_f, f_lg, u_g, v_g, q_s, k_s, v_s = (w[:, offs[i]:offs[i + 1]] for i in range(9))
    pad_cols = lambda a, width: jnp.pad(a, ((0, 0), (0, width - a.shape[1])))
    w_rows = jnp.concatenate([q_f, q_s, k_f, k_s, v_f, v_s, u_g, v_g, pad_cols(f_lg, LANES)],
                             axis=1).astype(BF16)
    w_cols = jnp.concatenate([q_f, q_s, k_f, k_s, v_f, v_s, pad_cols(f_lg, F_ROWS)],
                             axis=1).T.astype(BF16)
    bfc = jnp.pad(b_f, (0, LANES - H_FOX)).reshape(1, LANES)
    bfr = jnp.broadcast_to(jnp.pad(b_f, (0, SUBLANES - H_FOX)).reshape(SUBLANES, 1),
                           (SUBLANES, LANES))
    return w_rows, w_cols, bfc, bfr


def kernel(x_prompt, x_sample, cache_fox_k, cache_fox_v, cache_fox_logf, cache_sb_k, cache_sb_v,
           w_in, b_f, g_v, b_v, w_s, b_s, g_mix, w_out, ln1_g, ln1_b, w_up, w_down, ln2_g, ln2_b):
    depth = w_in.shape[0]
    batch, seq, d_model = x_prompt.shape
    streams, dec_seq, _ = x_sample.shape
    past = cache_fox_k.shape[2]
    n_prompt = batch * seq
    n_sample = streams * dec_seq
    alpha = (2 * depth) ** 0.25
    assert seq % ROW_TILE == 0 and n_sample % ROW_TILE == 0 and ROW_TILE % dec_seq == 0
    assert dec_seq == D_HEAD and 2 * dec_seq == SGU_CHUNK and past % ATT_TILE == 0
    assert streams % SUBLANES == 0

    xp = x_prompt.reshape(n_prompt, d_model)
    xs = x_sample.reshape(n_sample, d_model)
    row2 = lambda a: a.reshape(1, -1)
    cache_t = lambda c: jnp.transpose(c, (0, 1, 3, 4, 2)).reshape(depth, streams, W_ATT, past)
    ck_f, cv_f, ck_s, cv_s = (cache_t(c) for c in (cache_fox_k, cache_fox_v, cache_sb_k, cache_sb_v))
    lf_cache = jnp.transpose(cache_fox_logf.astype(F32), (0, 3, 1, 2))
    drow = _suffix_sum(lf_cache.reshape(depth * H_FOX * streams, past)).reshape(
        depth, H_FOX, streams, past)

    p_rows = s_rows = None
    p_logf, s_logf, s_gate = [], [], []
    for l in range(depth):
        w_rows, w_cols, bfc, bfr = _pack_w_in(w_in[l], b_f[l])
        gv, bv = row2(g_v[l]), row2(b_v[l])
        wout = w_out[l].astype(BF16)
        wup, wdn = w_up[l].astype(BF16), w_down[l].astype(BF16)
        gmix, g1, b1, g2, b2 = (row2(a[l]) for a in (g_mix, ln1_g, ln1_b, ln2_g, ln2_b))

        w_l = w_s[l]
        half = w_l[:, :dec_seq, :dec_seq]
        zeros = jnp.zeros_like(half)
        w_blockdiag = jnp.concatenate([jnp.concatenate([half, zeros], axis=2),
                                       jnp.concatenate([zeros, half], axis=2)], axis=1)
        bias_full = jnp.repeat(b_s[l].T, D_HEAD, axis=1)
        bias_half = jnp.concatenate([bias_full[:dec_seq], bias_full[:dec_seq]], axis=0)

        (qt, kb, *p_rows, vtbf, vtbs, lfrow, crow, ccolp, u, vn, knmax, qn2row) = _in_proj_prompt(
            xp, w_rows, w_cols, bfr, gv, bv, batch=batch, seq=seq, layer=l, depth=depth,
            cache_rows=p_rows)
        of = _prompt_attention(qt, kb, vtbf, 0, batch=batch, seq=seq, crow=crow, ccolp=ccolp,
                               knmax=knmax, qn2row=qn2row)
        os_ = _prompt_attention(qt, kb, vtbs, 1, batch=batch, seq=seq)
        x1 = _merge(xp, of, os_, u, vn, w_l, bias_full, gmix, wout, g1, b1, alpha=alpha,
                    name="merge_prompt")
        xp = _mlp(x1, wup, wdn, g2, b2, alpha=alpha, name="mlp_prompt")
        p_logf.append(jnp.transpose(lfrow[:H_FOX].reshape(H_FOX, batch, seq), (1, 2, 0)))

        (q, kb, vb, *s_rows, lfcol, ccol, cnrow, u, vn) = _in_proj_sample(
            xs, w_rows, w_cols, bfc, bfr, gv, bv, dec_seq=dec_seq, layer=l, depth=depth,
            cache_rows=s_rows)
        of, os_ = _sample_attention(q, kb, vb, ck_f, cv_f, drow, cnrow, ccol, ck_s, cv_s,
                                    layer=l, streams=streams, dec_seq=dec_seq)
        x1 = _merge(xs, of, os_, u, vn, w_blockdiag, bias_half, gmix, wout, g1, b1, alpha=alpha,
                    name="merge_sample")
        xs = _mlp(x1, wup, wdn, g2, b2, alpha=alpha, name="mlp_sample")
        s_logf.append(lfcol[:, :H_FOX].reshape(streams, dec_seq, H_FOX))
        s_gate.append(vn.reshape(streams, dec_seq, W_SGU))

    p_kf, p_ks, p_vf, p_vs = (jnp.transpose(a.reshape(depth, batch, H_FOX, D_HEAD, seq),
                                            (0, 1, 4, 2, 3)) for a in p_rows)
    s_kf, s_ks, s_vf, s_vs = (jnp.transpose(a, (0, 1, 3, 2, 4)) for a in s_rows)
    return (xp.reshape(batch, seq, d_model), xs.reshape(streams, dec_seq, d_model),
            p_kf, p_vf, jnp.stack(p_logf), p_ks, p_vs,
            s_kf, s_vf, jnp.stack(s_logf), s_ks, s_vs, jnp.stack(s_gate))
```

```python
import functools
import math

import jax
import jax.numpy as jnp
from jax import lax
from jax.experimental import pallas as pl
from jax.experimental.pallas import tpu as pltpu

F32 = jnp.float32
BF16 = jnp.bfloat16

LANES = 128
SUBLANES = 8
VMEM_BYTES_V7X = 64 * 1024 * 1024

D_HEAD = 64
H_FOX = 6
H_SB = 6
H_SGU = 4
W_ATT = H_FOX * D_HEAD
W_SGU = H_SGU * D_HEAD
N_PAIR = W_ATT // LANES
SGU_CHUNK = 128
LN_EPS = 1e-5
RMS_EPS = 1e-6
NEG_INF = -1e30
SB_CLOSED = 110.0
FOX_DEAD = 110.0
LOG2E = math.log2(math.e)
FOX_L_FLOOR = 1e-25
FOX_WINDOW = 5
SB_WINDOW = 2
SB_SUFFIX_BLOCK = 128

ROW_TILE = 512
ATT_TILE = 256

_C_Q = 0
_C_K = 2 * W_ATT
_C_V = 4 * W_ATT
_C_G = 6 * W_ATT
_C_F = _C_G + 2 * W_SGU
W_ROWS = _C_F + LANES
_R_Q = 0
_R_K = 2 * W_ATT
_R_V = 4 * W_ATT
_R_F = 6 * W_ATT
F_ROWS = 2 * SUBLANES
W_COLS = _R_F + F_ROWS


def _params(semantics, vmem_mib, flags=None):
    assert vmem_mib * 1024 * 1024 < VMEM_BYTES_V7X
    return pltpu.CompilerParams(dimension_semantics=semantics,
                                vmem_limit_bytes=vmem_mib * 1024 * 1024, flags=flags)


def _resident(shape):
    nd = len(shape)
    return pl.BlockSpec(shape, lambda *_: (0,) * nd, pipeline_mode=pl.Buffered(1))


def _split3(x):
    hi = x.astype(BF16)
    r1 = x - hi.astype(F32)
    mid = r1.astype(BF16)
    lo = (r1 - mid.astype(F32)).astype(BF16)
    return hi, mid, lo


def _dot(a, b):
    return jnp.dot(a, b, preferred_element_type=F32)


def _dot_nt(a, b):
    return lax.dot_general(a, b, (((1,), (1,)), ((), ())), preferred_element_type=F32)


def _ones_dot3(x, ones_mat, left):
    if left:
        return sum(_dot(ones_mat, p) for p in _split3(x))
    return sum(_dot(p, ones_mat) for p in _split3(x))


def _gelu_tanh(x):
    return 0.5 * x * (1.0 + jnp.tanh(0.7978845608028654 * (x + 0.044715 * (x * x * x))))


def _log_sigmoid(x):
    return jnp.minimum(x, 0.0) - jnp.log1p(jnp.exp(-jnp.abs(x)))


def _softplus(z):
    return jnp.maximum(z, 0.0) + jnp.log(1.0 + jnp.exp(-jnp.abs(z)))


def _softplus2(z2):
    return jnp.maximum(z2, 0.0) + jnp.log2(1.0 + jnp.exp2(-jnp.abs(z2)))


def _layer_norm(x, g, b):
    mu = jnp.mean(x, axis=-1, keepdims=True)
    xc = x - mu
    var = jnp.mean(xc * xc, axis=-1, keepdims=True)
    return xc * lax.rsqrt(var + LN_EPS) * g + b


def _tri_ones(n, kind, seg=None):
    r = lax.broadcasted_iota(jnp.int32, (n, n), 0)
    c = lax.broadcasted_iota(jnp.int32, (n, n), 1)
    keep = {"lower": c <= r, "upper": r <= c, "strict_lower": r > c, "strict_upper": r < c}[kind]
    if seg is not None:
        keep = jnp.logical_and(keep, r // seg == c // seg)
    return jnp.where(keep, 1.0, 0.0).astype(BF16)


def _pick_lane(x, idx):
    lane = lax.broadcasted_iota(jnp.int32, x.shape, 1)
    return jnp.sum(jnp.where(lane == idx, x, 0.0), axis=1, keepdims=True)


def _pick_row(x, idx):
    sub = lax.broadcasted_iota(jnp.int32, x.shape, 0)
    return jnp.sum(jnp.where(sub == idx, x, 0.0), axis=0, keepdims=True)


def _gates(xb, wr_ref, gv_ref, bv_ref, u_ref, vn_ref):
    u_ref[...] = _gelu_tanh(_dot(xb, wr_ref[:, _C_G:_C_G + W_SGU]))
    vg = _dot(xb, wr_ref[:, _C_G + W_SGU:_C_G + 2 * W_SGU])
    vn_ref[...] = _layer_norm(_gelu_tanh(vg), gv_ref[...], bv_ref[...])


def _log_forget_rows(xb, wc_ref, bfr_ref):
    sub = lax.broadcasted_iota(jnp.int32, (SUBLANES, xb.shape[0]), 0)
    z = _dot_nt(wc_ref[_R_F:_R_F + F_ROWS, :], xb)[:SUBLANES] + bfr_ref[:, 0:1]
    return jnp.where(sub < H_FOX, _log_sigmoid(z), 0.0)


def _in_proj_prompt_kernel(*refs, layer, depth):
    n_in = 6 if layer == 0 else 10
    x_ref, wr_ref, wc_ref, bfr_ref, gv_ref, bv_ref = refs[:6]
    (qt_ref, k_ref, ktf_ref, kts_ref, vtf_ref, vts_ref, vtbf_ref, vtbs_ref,
     lfrow_ref, crow_ref, ccolp_ref, u_ref, vn_ref, knmax_ref, qn2row_ref,
     carry_row_ref) = refs[n_in:]
    i = pl.program_id(1)
    tm = x_ref.shape[0]
    tk = vtbf_ref.shape[3]
    xb = x_ref[...].astype(BF16)

    def blocks(zt):
        return [zt[:, kb * tk:(kb + 1) * tk].astype(BF16) for kb in range(tm // tk)]

    def store_cache_rows(ref, zt):
        if layer == 0:
            for l in range(depth):
                ref[l, 0] = zt if l == 0 else jnp.zeros_like(zt)
        else:
            ref[0, 0] = zt

    scale = LOG2E / math.sqrt(D_HEAD)
    for kb, blk in enumerate(blocks(_dot_nt(wc_ref[_R_Q:_R_Q + 2 * W_ATT, :], xb) * scale)):
        qt_ref[0, kb] = blk
        qf = blk[:W_ATT].astype(F32)
        qn2 = jnp.sum((qf * qf).reshape(H_FOX, D_HEAD, tk), axis=1)
        qn2row_ref[kb] = jnp.concatenate([qn2, jnp.zeros((SUBLANES - H_FOX, tk), F32)], axis=0)
    for r0, full_ref, blk_ref in ((_R_V, vtf_ref, vtbf_ref), (_R_V + W_ATT, vts_ref, vtbs_ref)):
        zt = _dot_nt(wc_ref[r0:r0 + W_ATT, :], xb)
        store_cache_rows(full_ref, zt)
        for kb, blk in enumerate(blocks(zt)):
            blk_ref[0, kb] = blk
    k_rows = _dot(xb, wr_ref[:, _C_K:_C_K + 2 * W_ATT])
    k_ref[...] = k_rows.astype(BF16)
    kt_f = k_rows[:, :W_ATT].T
    store_cache_rows(ktf_ref, kt_f)
    store_cache_rows(kts_ref, k_rows[:, W_ATT:].T)

    _gates(xb, wr_ref, gv_ref, bv_ref, u_ref, vn_ref)

    @pl.when(i == 0)
    def _():
        carry_row_ref[...] = jnp.zeros_like(carry_row_ref)
        knmax_ref[...] = jnp.zeros_like(knmax_ref)

    kn2 = jnp.sum((kt_f * kt_f).reshape(H_FOX, D_HEAD, tm), axis=1)
    kn2 = jnp.broadcast_to(jnp.max(kn2, axis=1, keepdims=True), (H_FOX, LANES))
    kn2 = jnp.concatenate([kn2, jnp.zeros((SUBLANES - H_FOX, LANES), F32)], axis=0)
    knmax_ref[0] = jnp.maximum(knmax_ref[0], kn2)

    lf_row = _log_forget_rows(xb, wc_ref, bfr_ref)
    lfrow_ref[...] = lf_row
    c_row = _ones_dot3(lf_row, _tri_ones(tm, "upper"), left=False) + carry_row_ref[:, 0:1]
    carry_row_ref[...] = jnp.broadcast_to(c_row[:, tm - 1:tm], carry_row_ref.shape)
    c_row = c_row * LOG2E
    for kb in range(tm // tk):
        crow_ref[kb] = c_row[:, kb * tk:(kb + 1) * tk]
    pad = jnp.zeros((LANES - SUBLANES, tm), F32)
    for p in range(N_PAIR):
        rows = c_row if p == 0 else jnp.concatenate([c_row[2 * p:], c_row[:2 * p]], axis=0)
        ccolp_ref[:, p * LANES:(p + 1) * LANES] = jnp.concatenate([rows, pad], axis=0).T


def _in_proj_prompt(x, w_rows, w_cols, bfr, gv, bv, *, batch, seq, layer, depth, cache_rows):
    n, d = x.shape
    tm, tk = ROW_TILE, ATT_TILE
    nt = seq // tm
    row = lambda w: pl.BlockSpec((tm, w), lambda b, i: (b * nt + i, 0))
    if layer == 0:
        t_spec = pl.BlockSpec((depth, 1, W_ATT, tm), lambda b, i: (0, b, 0, i))
        extra_in, extra_specs, aliases = (), [], {}
    else:
        t_spec = pl.BlockSpec((1, 1, W_ATT, tm), lambda b, i: (layer, b, 0, i))
        extra_in = tuple(cache_rows)
        extra_specs = [pl.BlockSpec(memory_space=pl.ANY)] * 4
        aliases = {6 + k: 2 + k for k in range(4)}
    t_shape = jax.ShapeDtypeStruct((depth, batch, W_ATT, seq), F32)
    blk_spec = lambda w: pl.BlockSpec((1, tm // tk, w, tk), lambda b, i: (b, i, 0, 0))
    blk_shape = lambda w: jax.ShapeDtypeStruct((batch, seq // tk, w, tk), BF16)
    out_shape = [blk_shape(2 * W_ATT), jax.ShapeDtypeStruct((n, 2 * W_ATT), BF16),
                 t_shape, t_shape, t_shape, t_shape, blk_shape(W_ATT), blk_shape(W_ATT),
                 jax.ShapeDtypeStruct((SUBLANES, n), F32),
                 jax.ShapeDtypeStruct((n // tk, SUBLANES, tk), F32),
                 jax.ShapeDtypeStruct((n, N_PAIR * LANES), F32),
                 jax.ShapeDtypeStruct((n, W_SGU), F32), jax.ShapeDtypeStruct((n, W_SGU), F32),
                 jax.ShapeDtypeStruct((batch, SUBLANES, LANES), F32),
                 jax.ShapeDtypeStruct((n // tk, SUBLANES, tk), F32)]
    out_specs = [blk_spec(2 * W_ATT), row(2 * W_ATT), t_spec, t_spec, t_spec, t_spec,
                 blk_spec(W_ATT), blk_spec(W_ATT),
                 pl.BlockSpec((SUBLANES, tm), lambda b, i: (0, b * nt + i)),
                 pl.BlockSpec((tm // tk, SUBLANES, tk), lambda b, i: (b * nt + i, 0, 0)),
                 row(N_PAIR * LANES), row(W_SGU), row(W_SGU),
                 pl.BlockSpec((1, SUBLANES, LANES), lambda b, i: (b, 0, 0)),
                 pl.BlockSpec((tm // tk, SUBLANES, tk), lambda b, i: (b * nt + i, 0, 0))]
    return pl.pallas_call(
        functools.partial(_in_proj_prompt_kernel, layer=layer, depth=depth),
        grid=(batch, nt), name="in_proj_prompt",
        in_specs=[row(d)] + [_resident(a.shape) for a in (w_rows, w_cols, bfr, gv, bv)]
        + extra_specs,
        out_specs=out_specs, out_shape=out_shape, input_output_aliases=aliases,
        scratch_shapes=[pltpu.VMEM((SUBLANES, LANES), F32)],
        compiler_params=_params(("parallel", "arbitrary"), 56),
    )(x, w_rows, w_cols, bfr, gv, bv, *extra_in)


def _in_proj_sample_kernel(*refs, dec_seq, layer, depth):
    n_in = 7 if layer == 0 else 11
    x_ref, wr_ref, wc_ref, bfc_ref, bfr_ref, gv_ref, bv_ref = refs[:7]
    (q_ref, kb_ref, vb_ref, kf_ref, ks_ref, vf_ref, vs_ref, lfcol_ref, ccol_ref, crow_ref,
     u_ref, vn_ref) = refs[n_in:]
    tm = x_ref.shape[0]
    xb = x_ref[...].astype(BF16)
    scale = 1.0 / math.sqrt(D_HEAD)
    q_ref[...] = (_dot(xb, wr_ref[:, _C_Q:_C_Q + 2 * W_ATT]) * scale).astype(BF16)

    def store_cache_rows(ref, z):
        for s in range(tm // dec_seq):
            for h in range(W_ATT // D_HEAD):
                blk = z[s * dec_seq:(s + 1) * dec_seq, h * D_HEAD:(h + 1) * D_HEAD]
                ref[0, s, h] = blk
                if layer == 0:
                    for l in range(1, depth):
                        ref[l, s, h] = jnp.zeros_like(blk)

    for c0, refs_fs, half_ref in ((_C_K, (kf_ref, ks_ref), kb_ref), (_C_V, (vf_ref, vs_ref), vb_ref)):
        z = _dot(xb, wr_ref[:, c0:c0 + 2 * W_ATT])
        half_ref[...] = z.astype(BF16)
        store_cache_rows(refs_fs[0], z[:, :W_ATT])
        store_cache_rows(refs_fs[1], z[:, W_ATT:])
    _gates(xb, wr_ref, gv_ref, bv_ref, u_ref, vn_ref)

    lane = lax.broadcasted_iota(jnp.int32, (tm, LANES), 1)
    z = _dot(xb, wr_ref[:, _C_F:_C_F + LANES]) + bfc_ref[...]
    lf_col = jnp.where(lane < H_FOX, _log_sigmoid(z), 0.0)
    lfcol_ref[...] = lf_col
    ccol_ref[...] = _ones_dot3(lf_col, _tri_ones(tm, "lower", dec_seq), left=True)
    lf_row = _log_forget_rows(xb, wc_ref, bfr_ref)
    c_row = _ones_dot3(lf_row, _tri_ones(tm, "upper", dec_seq), left=False)
    for s in range(tm // dec_seq):
        crow_ref[s] = c_row[:, s * dec_seq:(s + 1) * dec_seq]


def _in_proj_sample(x, w_rows, w_cols, bfc, bfr, gv, bv, *, dec_seq, layer, depth, cache_rows):
    n, d = x.shape
    tm = ROW_TILE
    heads = W_ATT // D_HEAD
    row = lambda w: pl.BlockSpec((tm, w), lambda i: (i, 0))
    sds = lambda w, dt: jax.ShapeDtypeStruct((n, w), dt)
    if layer == 0:
        c_spec = pl.BlockSpec((depth, tm // dec_seq, heads, dec_seq, D_HEAD), lambda i: (0, i, 0, 0, 0))
        extra_in, extra_specs, aliases = (), [], {}
    else:
        c_spec = pl.BlockSpec((1, tm // dec_seq, heads, dec_seq, D_HEAD), lambda i: (layer, i, 0, 0, 0))
        extra_in = tuple(cache_rows)
        extra_specs = [pl.BlockSpec(memory_space=pl.ANY)] * 4
        aliases = {7 + k: 3 + k for k in range(4)}
    c_shape = jax.ShapeDtypeStruct((depth, n // dec_seq, heads, dec_seq, D_HEAD), F32)
    out_shape = [sds(2 * W_ATT, BF16)] * 3 + [c_shape] * 4 + [sds(LANES, F32)] * 2 + [
        jax.ShapeDtypeStruct((n // dec_seq, SUBLANES, dec_seq), F32), sds(W_SGU, F32), sds(W_SGU, F32)]
    out_specs = [row(2 * W_ATT)] * 3 + [c_spec] * 4 + [row(LANES)] * 2 + [
        pl.BlockSpec((tm // dec_seq, SUBLANES, dec_seq), lambda i: (i, 0, 0)), row(W_SGU), row(W_SGU)]
    return pl.pallas_call(
        functools.partial(_in_proj_sample_kernel, dec_seq=dec_seq, layer=layer, depth=depth),
        grid=(n // tm,), name="in_proj_sample",
        in_specs=[row(d)] + [_resident(a.shape) for a in (w_rows, w_cols, bfc, bfr, gv, bv)]
        + extra_specs,
        out_specs=out_specs, out_shape=out_shape, input_output_aliases=aliases,
        compiler_params=_params(("parallel",), 56),
    )(x, w_rows, w_cols, bfc, bfr, gv, bv, *extra_in)


def _split_head_rows(qt):
    sub = lax.broadcasted_iota(jnp.int32, qt.shape, 0)
    low = sub < D_HEAD
    zero = jnp.zeros_like(qt)
    return jnp.where(low, qt, zero), jnp.where(low, zero, qt)


def _fox_update_t(s, ct, pv, state, valid):
    m, l, acc = state
    if valid is not None:
        s = jnp.where(valid, s, NEG_INF)
    m_new = jnp.maximum(m, jnp.max(s, axis=0, keepdims=True) + ct)
    p = jnp.exp2(s + (ct - m_new))
    alpha = jnp.exp2(m - m_new)
    l = alpha * l + jnp.sum(p, axis=0, keepdims=True)
    acc = alpha * acc + pv(p.astype(BF16))
    return m_new, l, acc


def _sb_update_t(z, pv, suffix_ones, state, valid):
    carry, acc = state
    r = _softplus2(z)
    if valid is not None:
        r = jnp.where(valid, r, 0.0)
    r_hi = r.astype(BF16)
    r_lo = (r - r_hi.astype(F32)).astype(BF16)
    after = _dot(suffix_ones, r_hi) + _dot(suffix_ones, r_lo) + carry
    a = jnp.exp2((z - r) - after)
    if valid is not None:
        a = jnp.where(valid, a, 0.0)
    acc = acc + pv(a.astype(BF16))
    carry = carry + jnp.sum(r, axis=0, keepdims=True)
    return carry, acc


def _fox_prompt_kernel(qt_ref, k_ref, vt_ref, crow_ref, ccolp_ref, knmax_ref, qn2row_ref, o_ref,
                       *, window_blocks):
    tq = qt_ref.shape[3]
    tk = vt_ref.shape[3]
    hp = pl.program_id(1)
    i = pl.program_id(2)
    q2 = jnp.concatenate(_split_head_rows(qt_ref[0, 0]), axis=1)
    c_q = crow_ref[i]
    ct = jnp.concatenate([_pick_row(c_q, 2 * hp), _pick_row(c_q, 2 * hp + 1)], axis=1)

    key = lax.broadcasted_iota(jnp.int32, (tk, 2 * tq), 0)
    qry = lax.broadcasted_iota(jnp.int32, (tk, 2 * tq), 1) % tq
    causal = key <= qry

    def scores(j0, nblk):
        start = pl.multiple_of(j0 * tk, tk)
        kb = k_ref[pl.ds(start, nblk * tk), :]
        cc = ccolp_ref[pl.ds(start, nblk * tk), :]
        ck = jnp.concatenate([jnp.broadcast_to(cc[:, 0:1], (nblk * tk, tq)),
                              jnp.broadcast_to(cc[:, 1:2], (nblk * tk, tq))], axis=1)
        return _dot(kb, q2) - ck

    def pv(p, j0, nblk):
        vts = [vt_ref[0, j0 + w] for w in range(nblk)]
        outs = []
        for h in range(2):
            vth = jnp.concatenate([v[h * D_HEAD:(h + 1) * D_HEAD] for v in vts], axis=1)
            outs.append(_dot(vth, p[:, h * tq:(h + 1) * tq]))
        return jnp.concatenate(outs, axis=1)

    def pair_row(a, b):
        return jnp.concatenate([jnp.broadcast_to(a, (1, tq)), jnp.broadcast_to(b, (1, tq))], axis=1)

    qn2 = qn2row_ref[i]
    qn2 = jnp.concatenate([_pick_row(qn2, 2 * hp), _pick_row(qn2, 2 * hp + 1)], axis=1)
    kn2 = knmax_ref[0]
    kn2 = jnp.concatenate([_pick_row(kn2, 2 * hp)] * (tq // LANES)
                          + [_pick_row(kn2, 2 * hp + 1)] * (tq // LANES), axis=1)
    qk_bound = jnp.sqrt(qn2 * kn2) * 1.02

    shift = ct - qk_bound

    def reaches(j):
        return jnp.max(decay(j)) > -FOX_DEAD * LOG2E

    def decay(j):
        c_end = crow_ref[jnp.maximum(j, 0)][:, tk - 1:tk]
        return ct - pair_row(_pick_row(c_end, 2 * hp), _pick_row(c_end, 2 * hp + 1))

    def window(nblk):
        j0 = i - (nblk - 1)
        p = jnp.exp2(scores(j0, nblk) + shift)
        body_rows = (nblk - 1) * tk
        diag = jnp.where(causal, p[body_rows:], 0.0)
        p = diag if nblk == 1 else jnp.concatenate([p[:body_rows], diag], axis=0)
        return (j0 - 1, jnp.sum(p, axis=0, keepdims=True), pv(p.astype(BF16), j0, nblk))

    j, l, acc = lax.switch(jnp.minimum(i, window_blocks - 1),
                           [functools.partial(window, n) for n in range(1, window_blocks + 1)])

    def cond(c):
        j, go, _, _ = c
        return jnp.logical_and(j >= 0, go)

    def body(c):
        j, _, l, acc = c
        p = jnp.exp2(scores(j, 1) + shift)
        return (j - 1, reaches(j - 1), l + jnp.sum(p, axis=0, keepdims=True),
                acc + pv(p.astype(BF16), j, 1))

    def exact():
        def step(t, state):
            jj = i - 1 - t
            return _fox_update_t(scores(jj, 1), ct, lambda p: pv(p, jj, 1), state, None)

        init = (jnp.full((1, 2 * tq), NEG_INF, F32), jnp.zeros((1, 2 * tq), F32),
                jnp.zeros((D_HEAD, 2 * tq), F32))
        state = _fox_update_t(scores(i, 1), ct, lambda p: pv(p, i, 1), init, causal)
        _, l_x, acc_x = lax.fori_loop(0, i, step, state)
        return l_x, acc_x

    def beyond_window():
        _, _, l_w, acc_w = lax.while_loop(cond, body, (j, reaches(j), l, acc))
        return lax.cond(jnp.min(l_w) < FOX_L_FLOOR, exact, lambda: (l_w, acc_w))

    live_left = jnp.logical_and(decay(j) > -FOX_DEAD * LOG2E, j >= 0)
    uncommon = jnp.max(jnp.where(jnp.logical_or(live_left, l < FOX_L_FLOOR), 1.0, 0.0)) > 0.0
    l, acc = lax.cond(uncommon, beyond_window, lambda: (l, acc))
    o = acc / l
    o_ref[0] = jnp.concatenate([o[:, :tq], o[:, tq:]], axis=0)


def _sb_prompt_kernel(qt_ref, k_ref, vt_ref, o_ref, *, window_blocks):
    tq = qt_ref.shape[3]
    tk = vt_ref.shape[3]
    i = pl.program_id(2)
    q2 = jnp.concatenate(_split_head_rows(qt_ref[0, 0]), axis=1)
    suffix_ones = _tri_ones(tk, "strict_upper")
    ts = SB_SUFFIX_BLOCK
    sub_ones = _tri_ones(ts, "strict_upper")

    def block(j, state, valid):
        kb = k_ref[pl.ds(pl.multiple_of(j * tk, tk), tk), :]
        vt = vt_ref[0, j]

        def pv(a):
            return jnp.concatenate([_dot(vt[:D_HEAD], a[:, :tq]), _dot(vt[D_HEAD:], a[:, tq:])],
                                   axis=1)

        return _sb_update_t(_dot(kb, q2), pv, suffix_ones, state, valid)

    def still_open(carry):
        return jnp.min(carry) < SB_CLOSED * LOG2E

    key = lax.broadcasted_iota(jnp.int32, (tk, 2 * tq), 0)
    qry = lax.broadcasted_iota(jnp.int32, (tk, 2 * tq), 1) % tq
    causal = key < qry

    def window(nblk):
        j0 = i - (nblk - 1)
        z = _dot(k_ref[pl.ds(pl.multiple_of(j0 * tk, tk), nblk * tk), :], q2)
        r = _softplus2(z)
        body_rows = (nblk - 1) * tk
        diag = jnp.where(causal, r[body_rows:], 0.0)
        r = diag if nblk == 1 else jnp.concatenate([r[:body_rows], diag], axis=0)
        carry = jnp.zeros((1, 2 * tq), F32)
        n_sub = nblk * tk // ts
        a_blocks = [None] * n_sub
        for w in reversed(range(n_sub)):
            rows = slice(w * ts, (w + 1) * ts)
            r_w = r[rows]
            after = _dot(sub_ones, r_w.astype(BF16)) + carry
            a_blocks[w] = jnp.exp2((z[rows] - r_w) - after)
            carry = carry + jnp.sum(r_w, axis=0, keepdims=True)
        a = jnp.concatenate(a_blocks, axis=0)
        diag = jnp.where(causal, a[body_rows:], 0.0)
        a = (diag if nblk == 1 else jnp.concatenate([a[:body_rows], diag], axis=0)).astype(BF16)
        vts = [vt_ref[0, j0 + w] for w in range(nblk)]
        outs = []
        for h in range(2):
            vth = jnp.concatenate([v[h * D_HEAD:(h + 1) * D_HEAD] for v in vts], axis=1)
            outs.append(_dot(vth, a[:, h * tq:(h + 1) * tq]))
        return j0 - 1, (carry, jnp.concatenate(outs, axis=1))

    j, state = lax.cond(i >= window_blocks - 1, lambda: window(window_blocks), lambda: window(1))

    def cond(c):
        j, go, _ = c
        return jnp.logical_and(j >= 0, go)

    def body(c):
        j, _, st = c
        st = block(j, st, None)
        return j - 1, still_open(st[0]), st

    _, _, (_, acc) = lax.while_loop(cond, body, (j, still_open(state[0]), state))
    o_ref[0] = jnp.concatenate([acc[:, :tq], acc[:, tq:]], axis=0)


def _prompt_attention(qt, k, vtb, group, *, batch, seq, crow=None, ccolp=None, knmax=None,
                      qn2row=None):
    tq = tk = ATT_TILE
    nq = seq // tq
    col0 = group * N_PAIR
    qt_spec = pl.BlockSpec((1, 1, LANES, tq), lambda b, hp, i: (b, i, col0 + hp, 0))
    k_spec = pl.BlockSpec((seq, LANES), lambda b, hp, i: (b, col0 + hp))
    vt_spec = pl.BlockSpec((1, seq // tk, LANES, tk), lambda b, hp, i: (b, 0, hp, 0))
    out_spec = pl.BlockSpec((1, LANES, tq), lambda b, hp, i: (b * nq + i, hp, 0))
    out_shape = jax.ShapeDtypeStruct((batch * nq, W_ATT, tq), F32)
    params = _params(("parallel", "parallel", "arbitrary"), 40)
    grid = (batch, N_PAIR, nq)
    if group == 1:
        return pl.pallas_call(
            functools.partial(_sb_prompt_kernel, window_blocks=SB_WINDOW), grid=grid,
            name="sb_prompt",
            in_specs=[qt_spec, k_spec, vt_spec], out_specs=out_spec, out_shape=out_shape,
            compiler_params=params)(qt, k, vtb)
    crow_spec = pl.BlockSpec((seq // tk, SUBLANES, tk), lambda b, hp, i: (b, 0, 0))
    ccolp_spec = pl.BlockSpec((seq, LANES), lambda b, hp, i: (b, hp))
    knmax_spec = pl.BlockSpec((1, SUBLANES, LANES), lambda b, hp, i: (b, 0, 0))
    return pl.pallas_call(
        functools.partial(_fox_prompt_kernel, window_blocks=FOX_WINDOW), grid=grid,
        name="fox_prompt",
        in_specs=[qt_spec, k_spec, vt_spec, crow_spec, ccolp_spec, knmax_spec, crow_spec],
        out_specs=out_spec, out_shape=out_shape,
        compiler_params=params)(qt, k, vtb, crow, ccolp, knmax, qn2row)


def _suffix_sum_kernel(x_ref, o_ref):
    rows, width = x_ref.shape
    ones = _tri_ones(LANES, "strict_lower")
    carry = jnp.zeros((rows, 1), F32)
    for kb in reversed(range(width // LANES)):
        xb = x_ref[:, kb * LANES:(kb + 1) * LANES]
        o_ref[:, kb * LANES:(kb + 1) * LANES] = _ones_dot3(xb, ones, left=False) + carry
        carry = carry + jnp.sum(xb, axis=1, keepdims=True)


def _suffix_sum(x):
    return pl.pallas_call(
        _suffix_sum_kernel, name="cache_suffix_sum",
        out_shape=jax.ShapeDtypeStruct(x.shape, F32),
        compiler_params=_params(None, 32))(x)


def _split_heads(q2):
    lane = lax.broadcasted_iota(jnp.int32, q2.shape, 1)
    low = lane < D_HEAD
    zero = jnp.zeros_like(q2)
    return jnp.where(low, q2, zero), jnp.where(low, zero, q2)


def _stack_heads(q2):
    return jnp.concatenate(_split_heads(q2), axis=0)


def _unstack_heads(o):
    half = o.shape[0] // 2
    lane = lax.broadcasted_iota(jnp.int32, (half, o.shape[1]), 1)
    return jnp.where(lane < D_HEAD, o[:half], o[half:])


def _sb_update(z, pv, suffix_ones, state, valid):
    carry, acc = state
    r = _softplus(z)
    if valid is not None:
        r = jnp.where(valid, r, 0.0)
    r_hi = r.astype(BF16)
    r_lo = (r - r_hi.astype(F32)).astype(BF16)
    after = _dot(r_hi, suffix_ones) + _dot(r_lo, suffix_ones) + carry
    a = jnp.exp((z - r) - after)
    if valid is not None:
        a = jnp.where(valid, a, 0.0)
    acc = acc + pv(a.astype(BF16))
    carry = carry + jnp.sum(r, axis=1, keepdims=True)
    return carry, acc


def _sample_attn_kernel(qf_ref, kfn_ref, vfn_ref, kfc_ref, vfc_ref, drow_ref, cnrow_ref, ccol_ref,
                        qs_ref, ksn_ref, vsn_ref, ksc_last_ref, vsc_last_ref, ksc_hbm_ref,
                        vsc_hbm_ref, of_ref, os_ref,
                        carry_ref, acc_ref, kbuf_ref, vbuf_ref, sem_ref, *, blk, layer):
    b = pl.program_id(0)
    hp = pl.program_id(1)
    tq = qf_ref.shape[0]
    past = kfc_ref.shape[3]
    row = lax.broadcasted_iota(jnp.int32, (2 * tq, tq), 0) % tq
    col = lax.broadcasted_iota(jnp.int32, (2 * tq, tq), 1)

    def per_head(fn):
        return jnp.concatenate([fn(0), fn(1)], axis=0)

    q2 = _stack_heads(qf_ref[...])
    kct = kfc_ref[0, 0].astype(BF16)
    vct = vfc_ref[0, 0].astype(BF16)
    cc = ccol_ref[...]
    cn = cnrow_ref[0]
    cq = per_head(lambda h: _pick_lane(cc, 2 * hp + h))
    qk_c = _dot(q2, kct)
    qk_n = _dot_nt(q2, kfn_ref[...])
    s_c = per_head(lambda h: qk_c[h * tq:(h + 1) * tq]
                   + _pick_row(drow_ref[0, 2 * hp + h], b % SUBLANES))
    s_n = per_head(lambda h: qk_n[h * tq:(h + 1) * tq] - _pick_row(cn, 2 * hp + h))
    s_n = jnp.where(col <= row, s_n, NEG_INF)
    m = jnp.maximum(jnp.max(s_c, axis=1, keepdims=True), jnp.max(s_n, axis=1, keepdims=True)) + cq
    shift = cq - m
    p_c = jnp.exp(s_c + shift)
    p_n = jnp.exp(s_n + shift)
    l = jnp.sum(p_c, axis=1, keepdims=True) + jnp.sum(p_n, axis=1, keepdims=True)
    o = _dot_nt(p_c.astype(BF16), vct) + _dot(p_n.astype(BF16), vfn_ref[...])
    of_ref[...] = _unstack_heads(o / l)

    q2 = _stack_heads(qs_ref[...])
    vn = vsn_ref[...]
    suffix_ones = _tri_ones(blk, "strict_lower")

    def cache_block(kt_f32, vt_f32, state):
        kt = kt_f32.astype(BF16)
        vt = vt_f32.astype(BF16)
        return _sb_update(_dot(q2, kt), lambda a: _dot_nt(a, vt), suffix_ones, state, None)

    n_blocks = past // blk
    state = (jnp.zeros((2 * tq, 1), F32), jnp.zeros((2 * tq, LANES), F32))
    state = _sb_update(_dot_nt(q2, ksn_ref[...]), lambda a: _dot(a, vn),
                       _tri_ones(tq, "strict_lower"), state, col < row)
    carry_ref[...], acc_ref[...] = cache_block(ksc_last_ref[0, 0], vsc_last_ref[0, 0], state)

    def still_open():
        return jnp.min(carry_ref[...]) < SB_CLOSED

    def fetch(src_ref, jb, buf_ref, sem):
        rows = pl.ds(pl.multiple_of(hp * LANES, LANES), LANES)
        return pltpu.make_async_copy(src_ref.at[layer, b, rows, pl.ds(jb * blk, blk)], buf_ref, sem)

    @pl.when(still_open())
    def _():
        for jb in reversed(range(n_blocks - 1)):
            @pl.when(still_open())
            def _():
                copies = (fetch(ksc_hbm_ref, jb, kbuf_ref, sem_ref.at[0]),
                          fetch(vsc_hbm_ref, jb, vbuf_ref, sem_ref.at[1]))
                for cp in copies:
                    cp.start()
                for cp in copies:
                    cp.wait()
                carry_ref[...], acc_ref[...] = cache_block(
                    kbuf_ref[...], vbuf_ref[...], (carry_ref[...], acc_ref[...]))

    os_ref[...] = _unstack_heads(acc_ref[...])


def _sample_attention(q, kb, vb, cache_fk, cache_fv, drow, cnrow, ccol, cache_sk, cache_sv,
                      *, layer, streams, dec_seq):
    past = cache_fk.shape[3]
    new_spec = lambda g: pl.BlockSpec((dec_seq, LANES), lambda b, hp: (b, g * N_PAIR + hp))
    cache_spec = pl.BlockSpec((1, 1, LANES, past), lambda b, hp: (layer, b, hp, 0))
    drow_spec = pl.BlockSpec((1, H_FOX, SUBLANES, past), lambda b, hp: (layer, 0, b // SUBLANES, 0))
    cnrow_spec = pl.BlockSpec((1, SUBLANES, dec_seq), lambda b, hp: (b, 0, 0))
    ccol_spec = pl.BlockSpec((dec_seq, LANES), lambda b, hp: (b, 0))
    out_spec = pl.BlockSpec((dec_seq, LANES), lambda b, hp: (b, hp))
    out_shape = [jax.ShapeDtypeStruct((streams * dec_seq, W_ATT), F32)] * 2
    blk = ATT_TILE
    newest_spec = pl.BlockSpec((1, 1, LANES, blk), lambda b, hp: (layer, b, hp, past // blk - 1))
    hbm_spec = pl.BlockSpec(memory_space=pl.ANY)
    return pl.pallas_call(
        functools.partial(_sample_attn_kernel, blk=blk, layer=layer),
        grid=(streams, N_PAIR), name="sample_attention",
        in_specs=[new_spec(0), new_spec(0), new_spec(0), cache_spec, cache_spec, drow_spec,
                  cnrow_spec, ccol_spec, new_spec(1), new_spec(1), new_spec(1), newest_spec,
                  newest_spec, hbm_spec, hbm_spec],
        out_specs=[out_spec, out_spec], out_shape=out_shape,
        scratch_shapes=[pltpu.VMEM((2 * dec_seq, 1), F32), pltpu.VMEM((2 * dec_seq, LANES), F32),
                        pltpu.VMEM((LANES, blk), F32), pltpu.VMEM((LANES, blk), F32),
                        pltpu.SemaphoreType.DMA((2,))],
        compiler_params=_params(("parallel", "parallel"), 48),
    )(q, kb, vb, cache_fk, cache_fv, drow, cnrow, ccol, q, kb, vb, cache_sk, cache_sv,
      cache_sk, cache_sv)


def _merge_kernel(x_ref, of_ref, os_ref, u_ref, vn_ref, ws_ref, bs_ref, gmix_ref, wout_ref,
                  g_ref, b_ref, o_ref, *, alpha):
    tm = x_ref.shape[0]
    r = lax.broadcasted_iota(jnp.int32, (SGU_CHUNK, SGU_CHUNK), 0)
    c = lax.broadcasted_iota(jnp.int32, (SGU_CHUNK, SGU_CHUNK), 1)
    tril = r >= c
    head_mean = jnp.where(r // D_HEAD == c // D_HEAD, 1.0 / D_HEAD, 0.0).astype(BF16)
    lane = lax.broadcasted_iota(jnp.int32, (SGU_CHUNK, LANES), 1)

    gate_pieces = []
    for p in range(W_SGU // LANES):
        w0 = jnp.where(tril, ws_ref[2 * p], 0.0).astype(BF16)
        w1 = jnp.where(tril, ws_ref[2 * p + 1], 0.0).astype(BF16)
        bias = bs_ref[:, p * LANES:(p + 1) * LANES]
        chunks = []
        for ch in range(tm // SGU_CHUNK):
            rows = slice(ch * SGU_CHUNK, (ch + 1) * SGU_CHUNK)
            vp = vn_ref[rows, p * LANES:(p + 1) * LANES].astype(BF16)
            s = jnp.where(lane < D_HEAD, _dot(w0, vp), _dot(w1, vp)) + bias
            chunks.append(u_ref[rows, p * LANES:(p + 1) * LANES] * s)
        gate_pieces.append(jnp.concatenate(chunks, axis=0))

    def attention_pieces(ref):
        if ref.ndim == 2:
            return [ref[:, k * LANES:(k + 1) * LANES] for k in range(N_PAIR)]
        return [jnp.concatenate([ref[qb, k * LANES:(k + 1) * LANES, :].T
                                 for qb in range(ref.shape[0])], axis=0) for k in range(N_PAIR)]

    pieces = attention_pieces(of_ref) + gate_pieces + attention_pieces(os_ref)
    normed = []
    for k, o in enumerate(pieces):
        ms = _dot((o * o).astype(BF16), head_mean)
        gain = gmix_ref[:, k * LANES:(k + 1) * LANES]
        normed.append((o * lax.rsqrt(ms + RMS_EPS) * gain).astype(BF16))
    mix = _dot(jnp.concatenate(normed, axis=1), wout_ref[...])
    o_ref[...] = _layer_norm(alpha * x_ref[...] + mix, g_ref[...], b_ref[...])


def _merge(x, of, os_, u, vn, ws, bs, gmix, wout, g, b, *, alpha, name):
    n, d = x.shape
    tm = ROW_TILE
    row = lambda w: pl.BlockSpec((tm, w), lambda i: (i, 0))
    if of.ndim == 2:
        att = row(W_ATT)
    else:
        tq = of.shape[2]
        att = pl.BlockSpec((tm // tq, W_ATT, tq), lambda i: (i, 0, 0))
    return pl.pallas_call(
        functools.partial(_merge_kernel, alpha=alpha), grid=(n // tm,), name=name,
        in_specs=[row(d), att, att, row(W_SGU), row(W_SGU)]
        + [_resident(a.shape) for a in (ws, bs, gmix, wout, g, b)],
        out_specs=row(d), out_shape=jax.ShapeDtypeStruct((n, d), F32),
        compiler_params=_params(("parallel",), 48),
    )(x, of, os_, u, vn, ws, bs, gmix, wout, g, b)


def _mlp_kernel(x_ref, wup_ref, wdn_ref, g_ref, b_ref, o_ref, *, alpha, ff_chunk):
    x = x_ref[...]
    xb = x.astype(BF16)
    acc = alpha * x
    for c0 in range(0, wup_ref.shape[1], ff_chunk):
        h = jnp.maximum(_dot(xb, wup_ref[:, c0:c0 + ff_chunk]), 0.0)
        acc = acc + _dot((h * h).astype(BF16), wdn_ref[c0:c0 + ff_chunk, :])
    o_ref[...] = _layer_norm(acc, g_ref[...], b_ref[...])


def _mlp(x, wup, wdn, g, b, *, alpha, name):
    n, d = x.shape
    tm = ROW_TILE
    row = pl.BlockSpec((tm, d), lambda i: (i, 0))
    return pl.pallas_call(
        functools.partial(_mlp_kernel, alpha=alpha, ff_chunk=1024), grid=(n // tm,), name=name,
        in_specs=[row] + [_resident(a.shape) for a in (wup, wdn, g, b)],
        out_specs=row, out_shape=jax.ShapeDtypeStruct((n, d), F32),
        compiler_params=_params(("parallel",), 56),
    )(x, wup, wdn, g, b)


def _pack_w_in(w, b_f):
    sizes = [W_ATT, W_ATT, W_ATT, H_FOX, W_SGU, W_SGU, W_ATT, W_ATT, W_ATT]
    offs = [0]
    for s in sizes:
        offs.append(offs[-1] + s)
    q_f, k_f, v_f, f_lg, u_g, v_g, q_s, k_s, v_s = (w[:, offs[i]:offs[i + 1]] for i in range(9))
    pad_cols = lambda a, width: jnp.pad(a, ((0, 0), (0, width - a.shape[1])))
    w_rows = jnp.concatenate([q_f, q_s, k_f, k_s, v_f, v_s, u_g, v_g, pad_cols(f_lg, LANES)],
                             axis=1).astype(BF16)
    w_cols = jnp.concatenate([q_f, q_s, k_f, k_s, v_f, v_s, pad_cols(f_lg, F_ROWS)],
                             axis=1).T.astype(BF16)
    bfc = jnp.pad(b_f, (0, LANES - H_FOX)).reshape(1, LANES)
    bfr = jnp.broadcast_to(jnp.pad(b_f, (0, SUBLANES - H_FOX)).reshape(SUBLANES, 1),
                           (SUBLANES, LANES))
    return w_rows, w_cols, bfc, bfr


def kernel(x_prompt, x_sample, cache_fox_k, cache_fox_v, cache_fox_logf, cache_sb_k, cache_sb_v,
           w_in, b_f, g_v, b_v, w_s, b_s, g_mix, w_out, ln1_g, ln1_b, w_up, w_down, ln2_g, ln2_b):
    depth = w_in.shape[0]
    batch, seq, d_model = x_prompt.shape
    streams, dec_seq, _ = x_sample.shape
    past = cache_fox_k.shape[2]
    n_prompt = batch * seq
    n_sample = streams * dec_seq
    alpha = (2 * depth) ** 0.25
    assert seq % ROW_TILE == 0 and n_sample % ROW_TILE == 0 and ROW_TILE % dec_seq == 0
    assert dec_seq == D_HEAD and 2 * dec_seq == SGU_CHUNK and past % ATT_TILE == 0
    assert streams % SUBLANES == 0

    xp = x_prompt.reshape(n_prompt, d_model)
    xs = x_sample.reshape(n_sample, d_model)
    row2 = lambda a: a.reshape(1, -1)
    cache_t = lambda c: jnp.transpose(c, (0, 1, 3, 4, 2)).reshape(depth, streams, W_ATT, past)
    ck_f, cv_f, ck_s, cv_s = (cache_t(c) for c in (cache_fox_k, cache_fox_v, cache_sb_k, cache_sb_v))
    lf_cache = jnp.transpose(cache_fox_logf.astype(F32), (0, 3, 1, 2))
    drow = _suffix_sum(lf_cache.reshape(depth * H_FOX * streams, past)).reshape(
        depth, H_FOX, streams, past)

    p_rows = s_rows = None
    p_logf, s_logf, s_gate = [], [], []
    for l in range(depth):
        w_rows, w_cols, bfc, bfr = _pack_w_in(w_in[l], b_f[l])
        gv, bv = row2(g_v[l]), row2(b_v[l])
        wout = w_out[l].astype(BF16)
        wup, wdn = w_up[l].astype(BF16), w_down[l].astype(BF16)
        gmix, g1, b1, g2, b2 = (row2(a[l]) for a in (g_mix, ln1_g, ln1_b, ln2_g, ln2_b))

        w_l = w_s[l]
        half = w_l[:, :dec_seq, :dec_seq]
        zeros = jnp.zeros_like(half)
        w_blockdiag = jnp.concatenate([jnp.concatenate([half, zeros], axis=2),
                                       jnp.concatenate([zeros, half], axis=2)], axis=1)
        bias_full = jnp.repeat(b_s[l].T, D_HEAD, axis=1)
        bias_half = jnp.concatenate([bias_full[:dec_seq], bias_full[:dec_seq]], axis=0)

        (qt, kb, *p_rows, vtbf, vtbs, lfrow, crow, ccolp, u, vn, knmax, qn2row) = _in_proj_prompt(
            xp, w_rows, w_cols, bfr, gv, bv, batch=batch, seq=seq, layer=l, depth=depth,
            cache_rows=p_rows)
        of = _prompt_attention(qt, kb, vtbf, 0, batch=batch, seq=seq, crow=crow, ccolp=ccolp,
                               knmax=knmax, qn2row=qn2row)
        os_ = _prompt_attention(qt, kb, vtbs, 1, batch=batch, seq=seq)
        x1 = _merge(xp, of, os_, u, vn, w_l, bias_full, gmix, wout, g1, b1, alpha=alpha,
                    name="merge_prompt")
        xp = _mlp(x1, wup, wdn, g2, b2, alpha=alpha, name="mlp_prompt")
        p_logf.append(jnp.transpose(lfrow[:H_FOX].reshape(H_FOX, batch, seq), (1, 2, 0)))

        (q, kb, vb, *s_rows, lfcol, ccol, cnrow, u, vn) = _in_proj_sample(
            xs, w_rows, w_cols, bfc, bfr, gv, bv, dec_seq=dec_seq, layer=l, depth=depth,
            cache_rows=s_rows)
        of, os_ = _sample_attention(q, kb, vb, ck_f, cv_f, drow, cnrow, ccol, ck_s, cv_s,
                                    layer=l, streams=streams, dec_seq=dec_seq)
        x1 = _merge(xs, of, os_, u, vn, w_blockdiag, bias_half, gmix, wout, g1, b1, alpha=alpha,
                    name="merge_sample")
        xs = _mlp(x1, wup, wdn, g2, b2, alpha=alpha, name="mlp_sample")
        s_logf.append(lfcol[:, :H_FOX].reshape(streams, dec_seq, H_FOX))
        s_gate.append(vn.reshape(streams, dec_seq, W_SGU))

    p_kf, p_ks, p_vf, p_vs = (jnp.transpose(a.reshape(depth, batch, H_FOX, D_HEAD, seq),
                                            (0, 1, 4, 2, 3)) for a in p_rows)
    s_kf, s_ks, s_vf, s_vs = (jnp.transpose(a, (0, 1, 3, 2, 4)) for a in s_rows)
    return (xp.reshape(batch, seq, d_model), xs.reshape(streams, dec_seq, d_model),
            p_kf, p_vf, jnp.stack(p_logf), p_ks, p_vs,
            s_kf, s_vf, jnp.stack(s_logf), s_ks, s_vs, jnp.stack(s_gate))
```

```python
import functools
import math

import jax
import jax.numpy as jnp
from jax import lax
from jax.experimental import pallas as pl
from jax.experimental.pallas import tpu as pltpu

F32 = jnp.float32
BF16 = jnp.bfloat16

LANES = 128
SUBLANES = 8
VMEM_BYTES_V7X = 64 * 1024 * 1024

D_HEAD = 64
H_FOX = 6
H_SB = 6
H_SGU = 4
W_ATT = H_FOX * D_HEAD
W_SGU = H_SGU * D_HEAD
N_PAIR = W_ATT // LANES
SGU_CHUNK = 128
LN_EPS = 1e-5
RMS_EPS = 1e-6
NEG_INF = -1e30
SB_CLOSED = 110.0
FOX_DEAD = 110.0
LOG2E = math.log2(math.e)
FOX_L_FLOOR = 1e-25
FOX_BOUND_SAFE = 41.0
FOX_WINDOW = 5
SB_WINDOW = 2
SB_SUFFIX_BLOCK = 128

ROW_TILE = 512
ATT_TILE = 256

_C_Q = 0
_C_K = 2 * W_ATT
_C_V = 4 * W_ATT
_C_G = 6 * W_ATT
_C_F = _C_G + 2 * W_SGU
W_ROWS = _C_F + LANES
_R_Q = 0
_R_K = 2 * W_ATT
_R_V = 4 * W_ATT
_R_F = 6 * W_ATT
F_ROWS = 2 * SUBLANES
W_COLS = _R_F + F_ROWS


def _params(semantics, vmem_mib, flags=None):
    assert vmem_mib * 1024 * 1024 < VMEM_BYTES_V7X
    return pltpu.CompilerParams(dimension_semantics=semantics,
                                vmem_limit_bytes=vmem_mib * 1024 * 1024, flags=flags)


def _resident(shape):
    nd = len(shape)
    return pl.BlockSpec(shape, lambda *_: (0,) * nd, pipeline_mode=pl.Buffered(1))


def _split3(x):
    hi = x.astype(BF16)
    r1 = x - hi.astype(F32)
    mid = r1.astype(BF16)
    lo = (r1 - mid.astype(F32)).astype(BF16)
    return hi, mid, lo


def _dot(a, b):
    return jnp.dot(a, b, preferred_element_type=F32)


def _dot_nt(a, b):
    return lax.dot_general(a, b, (((1,), (1,)), ((), ())), preferred_element_type=F32)


def _ones_dot3(x, ones_mat, left):
    if left:
        return sum(_dot(ones_mat, p) for p in _split3(x))
    return sum(_dot(p, ones_mat) for p in _split3(x))


def _gelu_tanh(x):
    return 0.5 * x * (1.0 + jnp.tanh(0.7978845608028654 * (x + 0.044715 * (x * x * x))))


def _log_sigmoid(x):
    return jnp.minimum(x, 0.0) - jnp.log1p(jnp.exp(-jnp.abs(x)))


def _softplus(z):
    return jnp.maximum(z, 0.0) + jnp.log(1.0 + jnp.exp(-jnp.abs(z)))


def _softplus2(z2):
    return jnp.maximum(z2, 0.0) + jnp.log2(1.0 + jnp.exp2(-jnp.abs(z2)))


def _layer_norm(x, g, b):
    mu = jnp.mean(x, axis=-1, keepdims=True)
    xc = x - mu
    var = jnp.mean(xc * xc, axis=-1, keepdims=True)
    return xc * lax.rsqrt(var + LN_EPS) * g + b


def _tri_ones(n, kind, seg=None):
    r = lax.broadcasted_iota(jnp.int32, (n, n), 0)
    c = lax.broadcasted_iota(jnp.int32, (n, n), 1)
    keep = {"lower": c <= r, "upper": r <= c, "strict_lower": r > c, "strict_upper": r < c}[kind]
    if seg is not None:
        keep = jnp.logical_and(keep, r // seg == c // seg)
    return jnp.where(keep, 1.0, 0.0).astype(BF16)


def _pick_lane(x, idx):
    lane = lax.broadcasted_iota(jnp.int32, x.shape, 1)
    return jnp.sum(jnp.where(lane == idx, x, 0.0), axis=1, keepdims=True)


def _pick_row(x, idx):
    sub = lax.broadcasted_iota(jnp.int32, x.shape, 0)
    return jnp.sum(jnp.where(sub == idx, x, 0.0), axis=0, keepdims=True)


def _gates(xb, wr_ref, gv_ref, bv_ref, u_ref, vn_ref):
    u_ref[...] = _gelu_tanh(_dot(xb, wr_ref[:, _C_G:_C_G + W_SGU]))
    vg = _dot(xb, wr_ref[:, _C_G + W_SGU:_C_G + 2 * W_SGU])
    vn_ref[...] = _layer_norm(_gelu_tanh(vg), gv_ref[...], bv_ref[...])


def _log_forget_rows(xb, wc_ref, bfr_ref):
    sub = lax.broadcasted_iota(jnp.int32, (SUBLANES, xb.shape[0]), 0)
    z = _dot_nt(wc_ref[_R_F:_R_F + F_ROWS, :], xb)[:SUBLANES] + bfr_ref[:, 0:1]
    return jnp.where(sub < H_FOX, _log_sigmoid(z), 0.0)


def _in_proj_prompt_kernel(*refs, layer, depth):
    n_in = 6 if layer == 0 else 10
    x_ref, wr_ref, wc_ref, bfr_ref, gv_ref, bv_ref = refs[:6]
    (qt_ref, k_ref, ktf_ref, kts_ref, vtf_ref, vts_ref, vtbf_ref, vtbs_ref,
     lfrow_ref, crow_ref, ccolp_ref, u_ref, vn_ref, shiftrow_ref, flags_ref,
     carry_row_ref, knmax_ref, cend_ref) = refs[n_in:]
    i = pl.program_id(1)
    tm = x_ref.shape[0]
    tk = vtbf_ref.shape[3]
    xb = x_ref[...].astype(BF16)

    def blocks(zt):
        return [zt[:, kb * tk:(kb + 1) * tk].astype(BF16) for kb in range(tm // tk)]

    def store_cache_rows(ref, zt):
        if layer == 0:
            for l in range(depth):
                ref[l, 0] = zt if l == 0 else jnp.zeros_like(zt)
        else:
            ref[0, 0] = zt

    scale = LOG2E / math.sqrt(D_HEAD)
    qn2_rows = []
    for kb, blk in enumerate(blocks(_dot_nt(wc_ref[_R_Q:_R_Q + 2 * W_ATT, :], xb) * scale)):
        qt_ref[0, kb] = blk
        qf = blk[:W_ATT].astype(F32)
        qn2 = jnp.sum((qf * qf).reshape(H_FOX, D_HEAD, tk), axis=1)
        qn2_rows.append(jnp.concatenate([qn2, jnp.zeros((SUBLANES - H_FOX, tk), F32)], axis=0))
    for r0, full_ref, blk_ref in ((_R_V, vtf_ref, vtbf_ref), (_R_V + W_ATT, vts_ref, vtbs_ref)):
        zt = _dot_nt(wc_ref[r0:r0 + W_ATT, :], xb)
        store_cache_rows(full_ref, zt)
        for kb, blk in enumerate(blocks(zt)):
            blk_ref[0, kb] = blk
    k_rows = _dot(xb, wr_ref[:, _C_K:_C_K + 2 * W_ATT])
    k_ref[...] = k_rows.astype(BF16)
    kt_f = k_rows[:, :W_ATT].T
    store_cache_rows(ktf_ref, kt_f)
    store_cache_rows(kts_ref, k_rows[:, W_ATT:].T)

    _gates(xb, wr_ref, gv_ref, bv_ref, u_ref, vn_ref)

    @pl.when(i == 0)
    def _():
        carry_row_ref[...] = jnp.zeros_like(carry_row_ref)
        knmax_ref[...] = jnp.zeros_like(knmax_ref)

    kn2 = jnp.sum((kt_f * kt_f).reshape(H_FOX, D_HEAD, tm), axis=1)
    kn2 = jnp.broadcast_to(jnp.max(kn2, axis=1, keepdims=True), (H_FOX, LANES))
    kn2 = jnp.concatenate([kn2, jnp.zeros((SUBLANES - H_FOX, LANES), F32)], axis=0)
    knmax_ref[...] = jnp.maximum(knmax_ref[...], kn2)

    lf_row = _log_forget_rows(xb, wc_ref, bfr_ref)
    lfrow_ref[...] = lf_row
    c_row = _ones_dot3(lf_row, _tri_ones(tm, "upper"), left=False) + carry_row_ref[:, 0:1]
    carry_row_ref[...] = jnp.broadcast_to(c_row[:, tm - 1:tm], carry_row_ref.shape)
    c_row = c_row * LOG2E
    kn_run = jnp.concatenate([knmax_ref[...]] * (tk // LANES), axis=1)
    for kb in range(tm // tk):
        c_blk = c_row[:, kb * tk:(kb + 1) * tk]
        crow_ref[kb] = c_blk
        bound = jnp.sqrt(qn2_rows[kb] * kn_run) * 1.02
        shiftrow_ref[kb] = c_blk - bound
        blk_id = i * (tm // tk) + kb
        cend_ref[blk_id] = jnp.broadcast_to(c_blk[:, tk - 1:tk], (SUBLANES, LANES))
        left = blk_id - FOX_WINDOW
        decay = c_blk[:, 0:1] - cend_ref[jnp.maximum(left, 0)][:, 0:1]
        live = jnp.logical_and(decay > -FOX_DEAD * LOG2E, left >= 0)
        risky = jnp.max(bound, axis=1, keepdims=True) > FOX_BOUND_SAFE
        flags_ref[kb] = jnp.broadcast_to(
            jnp.where(jnp.logical_or(live, risky), 1, 0).astype(jnp.int32), (SUBLANES, LANES))
    pad = jnp.zeros((LANES - SUBLANES, tm), F32)
    for p in range(N_PAIR):
        rows = c_row if p == 0 else jnp.concatenate([c_row[2 * p:], c_row[:2 * p]], axis=0)
        ccolp_ref[:, p * LANES:(p + 1) * LANES] = jnp.concatenate([rows, pad], axis=0).T


def _in_proj_prompt(x, w_rows, w_cols, bfr, gv, bv, *, batch, seq, layer, depth, cache_rows):
    n, d = x.shape
    tm, tk = ROW_TILE, ATT_TILE
    nt = seq // tm
    row = lambda w: pl.BlockSpec((tm, w), lambda b, i: (b * nt + i, 0))
    if layer == 0:
        t_spec = pl.BlockSpec((depth, 1, W_ATT, tm), lambda b, i: (0, b, 0, i))
        extra_in, extra_specs, aliases = (), [], {}
    else:
        t_spec = pl.BlockSpec((1, 1, W_ATT, tm), lambda b, i: (layer, b, 0, i))
        extra_in = tuple(cache_rows)
        extra_specs = [pl.BlockSpec(memory_space=pl.ANY)] * 4
        aliases = {6 + k: 2 + k for k in range(4)}
    t_shape = jax.ShapeDtypeStruct((depth, batch, W_ATT, seq), F32)
    blk_spec = lambda w: pl.BlockSpec((1, tm // tk, w, tk), lambda b, i: (b, i, 0, 0))
    blk_shape = lambda w: jax.ShapeDtypeStruct((batch, seq // tk, w, tk), BF16)
    out_shape = [blk_shape(2 * W_ATT), jax.ShapeDtypeStruct((n, 2 * W_ATT), BF16),
                 t_shape, t_shape, t_shape, t_shape, blk_shape(W_ATT), blk_shape(W_ATT),
                 jax.ShapeDtypeStruct((SUBLANES, n), F32),
                 jax.ShapeDtypeStruct((n // tk, SUBLANES, tk), F32),
                 jax.ShapeDtypeStruct((n, N_PAIR * LANES), F32),
                 jax.ShapeDtypeStruct((n, W_SGU), F32), jax.ShapeDtypeStruct((n, W_SGU), F32),
                 jax.ShapeDtypeStruct((n // tk, SUBLANES, tk), F32),
                 jax.ShapeDtypeStruct((n // tk, SUBLANES, LANES), jnp.int32)]
    out_specs = [blk_spec(2 * W_ATT), row(2 * W_ATT), t_spec, t_spec, t_spec, t_spec,
                 blk_spec(W_ATT), blk_spec(W_ATT),
                 pl.BlockSpec((SUBLANES, tm), lambda b, i: (0, b * nt + i)),
                 pl.BlockSpec((tm // tk, SUBLANES, tk), lambda b, i: (b * nt + i, 0, 0)),
                 row(N_PAIR * LANES), row(W_SGU), row(W_SGU),
                 pl.BlockSpec((tm // tk, SUBLANES, tk), lambda b, i: (b * nt + i, 0, 0)),
                 pl.BlockSpec((tm // tk, SUBLANES, LANES), lambda b, i: (b * nt + i, 0, 0))]
    return pl.pallas_call(
        functools.partial(_in_proj_prompt_kernel, layer=layer, depth=depth),
        grid=(batch, nt), name="in_proj_prompt",
        in_specs=[row(d)] + [_resident(a.shape) for a in (w_rows, w_cols, bfr, gv, bv)]
        + extra_specs,
        out_specs=out_specs, out_shape=out_shape, input_output_aliases=aliases,
        scratch_shapes=[pltpu.VMEM((SUBLANES, LANES), F32), pltpu.VMEM((SUBLANES, LANES), F32),
                        pltpu.VMEM((seq // tk, SUBLANES, LANES), F32)],
        compiler_params=_params(("parallel", "arbitrary"), 56),
    )(x, w_rows, w_cols, bfr, gv, bv, *extra_in)


def _in_proj_sample_kernel(*refs, dec_seq, layer, depth):
    n_in = 7 if layer == 0 else 11
    x_ref, wr_ref, wc_ref, bfc_ref, bfr_ref, gv_ref, bv_ref = refs[:7]
    (q_ref, kb_ref, vb_ref, kf_ref, ks_ref, vf_ref, vs_ref, lfcol_ref, ccol_ref, crow_ref,
     u_ref, vn_ref) = refs[n_in:]
    tm = x_ref.shape[0]
    xb = x_ref[...].astype(BF16)
    scale = 1.0 / math.sqrt(D_HEAD)
    q_ref[...] = (_dot(xb, wr_ref[:, _C_Q:_C_Q + 2 * W_ATT]) * scale).astype(BF16)

    def store_cache_rows(ref, z):
        for s in range(tm // dec_seq):
            for h in range(W_ATT // D_HEAD):
                blk = z[s * dec_seq:(s + 1) * dec_seq, h * D_HEAD:(h + 1) * D_HEAD]
                ref[0, s, h] = blk
                if layer == 0:
                    for l in range(1, depth):
                        ref[l, s, h] = jnp.zeros_like(blk)

    for c0, refs_fs, half_ref in ((_C_K, (kf_ref, ks_ref), kb_ref), (_C_V, (vf_ref, vs_ref), vb_ref)):
        z = _dot(xb, wr_ref[:, c0:c0 + 2 * W_ATT])
        half_ref[...] = z.astype(BF16)
        store_cache_rows(refs_fs[0], z[:, :W_ATT])
        store_cache_rows(refs_fs[1], z[:, W_ATT:])
    _gates(xb, wr_ref, gv_ref, bv_ref, u_ref, vn_ref)

    lane = lax.broadcasted_iota(jnp.int32, (tm, LANES), 1)
    z = _dot(xb, wr_ref[:, _C_F:_C_F + LANES]) + bfc_ref[...]
    lf_col = jnp.where(lane < H_FOX, _log_sigmoid(z), 0.0)
    lfcol_ref[...] = lf_col
    ccol_ref[...] = _ones_dot3(lf_col, _tri_ones(tm, "lower", dec_seq), left=True)
    lf_row = _log_forget_rows(xb, wc_ref, bfr_ref)
    c_row = _ones_dot3(lf_row, _tri_ones(tm, "upper", dec_seq), left=False)
    for s in range(tm // dec_seq):
        crow_ref[s] = c_row[:, s * dec_seq:(s + 1) * dec_seq]


def _in_proj_sample(x, w_rows, w_cols, bfc, bfr, gv, bv, *, dec_seq, layer, depth, cache_rows):
    n, d = x.shape
    tm = ROW_TILE
    heads = W_ATT // D_HEAD
    row = lambda w: pl.BlockSpec((tm, w), lambda i: (i, 0))
    sds = lambda w, dt: jax.ShapeDtypeStruct((n, w), dt)
    if layer == 0:
        c_spec = pl.BlockSpec((depth, tm // dec_seq, heads, dec_seq, D_HEAD), lambda i: (0, i, 0, 0, 0))
        extra_in, extra_specs, aliases = (), [], {}
    else:
        c_spec = pl.BlockSpec((1, tm // dec_seq, heads, dec_seq, D_HEAD), lambda i: (layer, i, 0, 0, 0))
        extra_in = tuple(cache_rows)
        extra_specs = [pl.BlockSpec(memory_space=pl.ANY)] * 4
        aliases = {7 + k: 3 + k for k in range(4)}
    c_shape = jax.ShapeDtypeStruct((depth, n // dec_seq, heads, dec_seq, D_HEAD), F32)
    out_shape = [sds(2 * W_ATT, BF16)] * 3 + [c_shape] * 4 + [sds(LANES, F32)] * 2 + [
        jax.ShapeDtypeStruct((n // dec_seq, SUBLANES, dec_seq), F32), sds(W_SGU, F32), sds(W_SGU, F32)]
    out_specs = [row(2 * W_ATT)] * 3 + [c_spec] * 4 + [row(LANES)] * 2 + [
        pl.BlockSpec((tm // dec_seq, SUBLANES, dec_seq), lambda i: (i, 0, 0)), row(W_SGU), row(W_SGU)]
    return pl.pallas_call(
        functools.partial(_in_proj_sample_kernel, dec_seq=dec_seq, layer=layer, depth=depth),
        grid=(n // tm,), name="in_proj_sample",
        in_specs=[row(d)] + [_resident(a.shape) for a in (w_rows, w_cols, bfc, bfr, gv, bv)]
        + extra_specs,
        out_specs=out_specs, out_shape=out_shape, input_output_aliases=aliases,
        compiler_params=_params(("parallel",), 56),
    )(x, w_rows, w_cols, bfc, bfr, gv, bv, *extra_in)


def _split_head_rows(qt):
    sub = lax.broadcasted_iota(jnp.int32, qt.shape, 0)
    low = sub < D_HEAD
    zero = jnp.zeros_like(qt)
    return jnp.where(low, qt, zero), jnp.where(low, zero, qt)


def _fox_update_t(s, ct, pv, state, valid):
    m, l, acc = state
    if valid is not None:
        s = jnp.where(valid, s, NEG_INF)
    m_new = jnp.maximum(m, jnp.max(s, axis=0, keepdims=True) + ct)
    p = jnp.exp2(s + (ct - m_new))
    alpha = jnp.exp2(m - m_new)
    l = alpha * l + jnp.sum(p, axis=0, keepdims=True)
    acc = alpha * acc + pv(p.astype(BF16))
    return m_new, l, acc


def _sb_update_t(z, pv, suffix_ones, state, valid):
    carry, acc = state
    r = _softplus2(z)
    if valid is not None:
        r = jnp.where(valid, r, 0.0)
    r_hi = r.astype(BF16)
    r_lo = (r - r_hi.astype(F32)).astype(BF16)
    after = _dot(suffix_ones, r_hi) + _dot(suffix_ones, r_lo) + carry
    a = jnp.exp2((z - r) - after)
    if valid is not None:
        a = jnp.where(valid, a, 0.0)
    acc = acc + pv(a.astype(BF16))
    carry = carry + jnp.sum(r, axis=0, keepdims=True)
    return carry, acc


def _fox_prompt_kernel(flags_ref, qt_ref, k_ref, vt_ref, crow_ref, ccolp_ref, shiftrow_ref, o_ref,
                       *, window_blocks):
    tq = qt_ref.shape[3]
    tk = vt_ref.shape[3]
    hp = pl.program_id(1)
    i = pl.program_id(2)
    q2 = jnp.concatenate(_split_head_rows(qt_ref[0, 0]), axis=1)
    c_q = crow_ref[i]
    ct = jnp.concatenate([_pick_row(c_q, 2 * hp), _pick_row(c_q, 2 * hp + 1)], axis=1)

    key = lax.broadcasted_iota(jnp.int32, (tk, 2 * tq), 0)
    qry = lax.broadcasted_iota(jnp.int32, (tk, 2 * tq), 1) % tq
    causal = key <= qry

    def scores(j0, nblk):
        start = pl.multiple_of(j0 * tk, tk)
        kb = k_ref[pl.ds(start, nblk * tk), :]
        cc = ccolp_ref[pl.ds(start, nblk * tk), :]
        ck = jnp.concatenate([jnp.broadcast_to(cc[:, 0:1], (nblk * tk, tq)),
                              jnp.broadcast_to(cc[:, 1:2], (nblk * tk, tq))], axis=1)
        return _dot(kb, q2) - ck

    def pv(p, j0, nblk):
        vts = [vt_ref[0, j0 + w] for w in range(nblk)]
        outs = []
        for h in range(2):
            vth = jnp.concatenate([v[h * D_HEAD:(h + 1) * D_HEAD] for v in vts], axis=1)
            outs.append(_dot(vth, p[:, h * tq:(h + 1) * tq]))
        return jnp.concatenate(outs, axis=1)

    def pair_row(a, b):
        return jnp.concatenate([jnp.broadcast_to(a, (1, tq)), jnp.broadcast_to(b, (1, tq))], axis=1)

    sh = shiftrow_ref[i]
    shift = jnp.concatenate([_pick_row(sh, 2 * hp), _pick_row(sh, 2 * hp + 1)], axis=1)

    def reaches(j):
        return jnp.max(decay(j)) > -FOX_DEAD * LOG2E

    def decay(j):
        c_end = crow_ref[jnp.maximum(j, 0)][:, tk - 1:tk]
        return ct - pair_row(_pick_row(c_end, 2 * hp), _pick_row(c_end, 2 * hp + 1))

    def window(nblk):
        j0 = i - (nblk - 1)
        p = jnp.exp2(scores(j0, nblk) + shift)
        body_rows = (nblk - 1) * tk
        diag = jnp.where(causal, p[body_rows:], 0.0)
        p = diag if nblk == 1 else jnp.concatenate([p[:body_rows], diag], axis=0)
        return (j0 - 1, jnp.sum(p, axis=0, keepdims=True), pv(p.astype(BF16), j0, nblk))

    j, l, acc = lax.switch(jnp.minimum(i, window_blocks - 1),
                           [functools.partial(window, n) for n in range(1, window_blocks + 1)])

    def cond(c):
        j, go, _, _ = c
        return jnp.logical_and(j >= 0, go)

    def body(c):
        j, _, l, acc = c
        p = jnp.exp2(scores(j, 1) + shift)
        return (j - 1, reaches(j - 1), l + jnp.sum(p, axis=0, keepdims=True),
                acc + pv(p.astype(BF16), j, 1))

    def exact():
        def step(t, state):
            jj = i - 1 - t
            return _fox_update_t(scores(jj, 1), ct, lambda p: pv(p, jj, 1), state, None)

        init = (jnp.full((1, 2 * tq), NEG_INF, F32), jnp.zeros((1, 2 * tq), F32),
                jnp.zeros((D_HEAD, 2 * tq), F32))
        state = _fox_update_t(scores(i, 1), ct, lambda p: pv(p, i, 1), init, causal)
        _, l_x, acc_x = lax.fori_loop(0, i, step, state)
        return l_x, acc_x

    def beyond_window():
        _, _, l_w, acc_w = lax.while_loop(cond, body, (j, reaches(j), l, acc))
        return lax.cond(jnp.min(l_w) < FOX_L_FLOOR, exact, lambda: (l_w, acc_w))

    b = pl.program_id(0)
    uncommon = flags_ref[(b * pl.num_programs(2) + i) * N_PAIR + hp] != 0
    l, acc = lax.cond(uncommon, beyond_window, lambda: (l, acc))
    o = acc / l
    o_ref[0] = jnp.concatenate([o[:, :tq], o[:, tq:]], axis=0)


def _sb_prompt_kernel(qt_ref, k_ref, vt_ref, o_ref, *, window_blocks):
    tq = qt_ref.shape[3]
    tk = vt_ref.shape[3]
    i = pl.program_id(2)
    q2 = jnp.concatenate(_split_head_rows(qt_ref[0, 0]), axis=1)
    suffix_ones = _tri_ones(tk, "strict_upper")
    ts = SB_SUFFIX_BLOCK
    sub_ones = _tri_ones(ts, "strict_upper")

    def block(j, state, valid):
        kb = k_ref[pl.ds(pl.multiple_of(j * tk, tk), tk), :]
        vt = vt_ref[0, j]

        def pv(a):
            return jnp.concatenate([_dot(vt[:D_HEAD], a[:, :tq]), _dot(vt[D_HEAD:], a[:, tq:])],
                                   axis=1)

        return _sb_update_t(_dot(kb, q2), pv, suffix_ones, state, valid)

    def still_open(carry):
        return jnp.min(carry) < SB_CLOSED * LOG2E

    key = lax.broadcasted_iota(jnp.int32, (tk, 2 * tq), 0)
    qry = lax.broadcasted_iota(jnp.int32, (tk, 2 * tq), 1) % tq
    causal = key < qry

    def window(nblk):
        j0 = i - (nblk - 1)
        z = _dot(k_ref[pl.ds(pl.multiple_of(j0 * tk, tk), nblk * tk), :], q2)
        r = _softplus2(z)
        body_rows = (nblk - 1) * tk
        diag = jnp.where(causal, r[body_rows:], 0.0)
        r = diag if nblk == 1 else jnp.concatenate([r[:body_rows], diag], axis=0)
        carry = jnp.zeros((1, 2 * tq), F32)
        n_sub = nblk * tk // ts
        a_blocks = [None] * n_sub
        for w in reversed(range(n_sub)):
            rows = slice(w * ts, (w + 1) * ts)
            r_w = r[rows]
            after = _dot(sub_ones, r_w.astype(BF16)) + carry
            a_blocks[w] = jnp.exp2((z[rows] - r_w) - after)
            carry = carry + jnp.sum(r_w, axis=0, keepdims=True)
        a = jnp.concatenate(a_blocks, axis=0)
        diag = jnp.where(causal, a[body_rows:], 0.0)
        a = (diag if nblk == 1 else jnp.concatenate([a[:body_rows], diag], axis=0)).astype(BF16)
        vts = [vt_ref[0, j0 + w] for w in range(nblk)]
        outs = []
        for h in range(2):
            vth = jnp.concatenate([v[h * D_HEAD:(h + 1) * D_HEAD] for v in vts], axis=1)
            outs.append(_dot(vth, a[:, h * tq:(h + 1) * tq]))
        return j0 - 1, (carry, jnp.concatenate(outs, axis=1))

    j, state = lax.cond(i >= window_blocks - 1, lambda: window(window_blocks), lambda: window(1))

    def cond(c):
        j, go, _ = c
        return jnp.logical_and(j >= 0, go)

    def body(c):
        j, _, st = c
        st = block(j, st, None)
        return j - 1, still_open(st[0]), st

    _, _, (_, acc) = lax.while_loop(cond, body, (j, still_open(state[0]), state))
    o_ref[0] = jnp.concatenate([acc[:, :tq], acc[:, tq:]], axis=0)


def _prompt_attention(qt, k, vtb, group, *, batch, seq, crow=None, ccolp=None, shiftrow=None,
                      flags=None):
    tq = tk = ATT_TILE
    nq = seq // tq
    col0 = group * N_PAIR
    qt_spec = pl.BlockSpec((1, 1, LANES, tq), lambda b, hp, i, *_: (b, i, col0 + hp, 0))
    k_spec = pl.BlockSpec((seq, LANES), lambda b, hp, i, *_: (b, col0 + hp))
    vt_spec = pl.BlockSpec((1, seq // tk, LANES, tk), lambda b, hp, i, *_: (b, 0, hp, 0))
    out_spec = pl.BlockSpec((1, LANES, tq), lambda b, hp, i, *_: (b * nq + i, hp, 0))
    out_shape = jax.ShapeDtypeStruct((batch * nq, W_ATT, tq), F32)
    params = _params(("parallel", "parallel", "arbitrary"), 40)
    grid = (batch, N_PAIR, nq)
    if group == 1:
        return pl.pallas_call(
            functools.partial(_sb_prompt_kernel, window_blocks=SB_WINDOW), grid=grid,
            name="sb_prompt",
            in_specs=[qt_spec, k_spec, vt_spec], out_specs=out_spec, out_shape=out_shape,
            compiler_params=params)(qt, k, vtb)
    crow_spec = pl.BlockSpec((seq // tk, SUBLANES, tk), lambda b, hp, i, *_: (b, 0, 0))
    ccolp_spec = pl.BlockSpec((seq, LANES), lambda b, hp, i, *_: (b, hp))
    grid_spec = pltpu.PrefetchScalarGridSpec(
        num_scalar_prefetch=1, grid=grid,
        in_specs=[qt_spec, k_spec, vt_spec, crow_spec, ccolp_spec, crow_spec],
        out_specs=out_spec)
    return pl.pallas_call(
        functools.partial(_fox_prompt_kernel, window_blocks=FOX_WINDOW), grid_spec=grid_spec,
        name="fox_prompt", out_shape=out_shape,
        compiler_params=params)(flags, qt, k, vtb, crow, ccolp, shiftrow)


def _suffix_sum_kernel(x_ref, o_ref):
    rows, width = x_ref.shape
    ones = _tri_ones(LANES, "strict_lower")
    carry = jnp.zeros((rows, 1), F32)
    for kb in reversed(range(width // LANES)):
        xb = x_ref[:, kb * LANES:(kb + 1) * LANES]
        o_ref[:, kb * LANES:(kb + 1) * LANES] = _ones_dot3(xb, ones, left=False) + carry
        carry = carry + jnp.sum(xb, axis=1, keepdims=True)


def _suffix_sum(x):
    return pl.pallas_call(
        _suffix_sum_kernel, name="cache_suffix_sum",
        out_shape=jax.ShapeDtypeStruct(x.shape, F32),
        compiler_params=_params(None, 32))(x)


def _split_heads(q2):
    lane = lax.broadcasted_iota(jnp.int32, q2.shape, 1)
    low = lane < D_HEAD
    zero = jnp.zeros_like(q2)
    return jnp.where(low, q2, zero), jnp.where(low, zero, q2)


def _stack_heads(q2):
    return jnp.concatenate(_split_heads(q2), axis=0)


def _unstack_heads(o):
    half = o.shape[0] // 2
    lane = lax.broadcasted_iota(jnp.int32, (half, o.shape[1]), 1)
    return jnp.where(lane < D_HEAD, o[:half], o[half:])


def _sb_update(z, pv, suffix_ones, state, valid):
    carry, acc = state
    r = _softplus(z)
    if valid is not None:
        r = jnp.where(valid, r, 0.0)
    r_hi = r.astype(BF16)
    r_lo = (r - r_hi.astype(F32)).astype(BF16)
    after = _dot(r_hi, suffix_ones) + _dot(r_lo, suffix_ones) + carry
    a = jnp.exp((z - r) - after)
    if valid is not None:
        a = jnp.where(valid, a, 0.0)
    acc = acc + pv(a.astype(BF16))
    carry = carry + jnp.sum(r, axis=1, keepdims=True)
    return carry, acc


def _sample_attn_kernel(qf_ref, kfn_ref, vfn_ref, kfc_ref, vfc_ref, drow_ref, cnrow_ref, ccol_ref,
                        qs_ref, ksn_ref, vsn_ref, ksc_last_ref, vsc_last_ref, ksc_hbm_ref,
                        vsc_hbm_ref, of_ref, os_ref,
                        carry_ref, acc_ref, kbuf_ref, vbuf_ref, sem_ref, *, blk, layer):
    b = pl.program_id(0)
    hp = pl.program_id(1)
    tq = qf_ref.shape[0]
    past = kfc_ref.shape[3]
    row = lax.broadcasted_iota(jnp.int32, (2 * tq, tq), 0) % tq
    col = lax.broadcasted_iota(jnp.int32, (2 * tq, tq), 1)

    def per_head(fn):
        return jnp.concatenate([fn(0), fn(1)], axis=0)

    q2 = _stack_heads(qf_ref[...])
    kct = kfc_ref[0, 0].astype(BF16)
    vct = vfc_ref[0, 0].astype(BF16)
    cc = ccol_ref[...]
    cn = cnrow_ref[0]
    cq = per_head(lambda h: _pick_lane(cc, 2 * hp + h))
    qk_c = _dot(q2, kct)
    qk_n = _dot_nt(q2, kfn_ref[...])
    s_c = per_head(lambda h: qk_c[h * tq:(h + 1) * tq]
                   + _pick_row(drow_ref[0, 2 * hp + h], b % SUBLANES))
    s_n = per_head(lambda h: qk_n[h * tq:(h + 1) * tq] - _pick_row(cn, 2 * hp + h))
    s_n = jnp.where(col <= row, s_n, NEG_INF)
    m = jnp.maximum(jnp.max(s_c, axis=1, keepdims=True), jnp.max(s_n, axis=1, keepdims=True)) + cq
    shift = cq - m
    p_c = jnp.exp(s_c + shift)
    p_n = jnp.exp(s_n + shift)
    l = jnp.sum(p_c, axis=1, keepdims=True) + jnp.sum(p_n, axis=1, keepdims=True)
    o = _dot_nt(p_c.astype(BF16), vct) + _dot(p_n.astype(BF16), vfn_ref[...])
    of_ref[...] = _unstack_heads(o / l)

    q2 = _stack_heads(qs_ref[...])
    vn = vsn_ref[...]
    suffix_ones = _tri_ones(blk, "strict_lower")

    def cache_block(kt_f32, vt_f32, state):
        kt = kt_f32.astype(BF16)
        vt = vt_f32.astype(BF16)
        return _sb_update(_dot(q2, kt), lambda a: _dot_nt(a, vt), suffix_ones, state, None)

    n_blocks = past // blk
    state = (jnp.zeros((2 * tq, 1), F32), jnp.zeros((2 * tq, LANES), F32))
    state = _sb_update(_dot_nt(q2, ksn_ref[...]), lambda a: _dot(a, vn),
                       _tri_ones(tq, "strict_lower"), state, col < row)
    carry_ref[...], acc_ref[...] = cache_block(ksc_last_ref[0, 0], vsc_last_ref[0, 0], state)

    def still_open():
        return jnp.min(carry_ref[...]) < SB_CLOSED

    def fetch(src_ref, jb, buf_ref, sem):
        rows = pl.ds(pl.multiple_of(hp * LANES, LANES), LANES)
        return pltpu.make_async_copy(src_ref.at[layer, b, rows, pl.ds(jb * blk, blk)], buf_ref, sem)

    @pl.when(still_open())
    def _():
        for jb in reversed(range(n_blocks - 1)):
            @pl.when(still_open())
            def _():
                copies = (fetch(ksc_hbm_ref, jb, kbuf_ref, sem_ref.at[0]),
                          fetch(vsc_hbm_ref, jb, vbuf_ref, sem_ref.at[1]))
                for cp in copies:
                    cp.start()
                for cp in copies:
                    cp.wait()
                carry_ref[...], acc_ref[...] = cache_block(
                    kbuf_ref[...], vbuf_ref[...], (carry_ref[...], acc_ref[...]))

    os_ref[...] = _unstack_heads(acc_ref[...])


def _sample_attention(q, kb, vb, cache_fk, cache_fv, drow, cnrow, ccol, cache_sk, cache_sv,
                      *, layer, streams, dec_seq):
    past = cache_fk.shape[3]
    new_spec = lambda g: pl.BlockSpec((dec_seq, LANES), lambda b, hp: (b, g * N_PAIR + hp))
    cache_spec = pl.BlockSpec((1, 1, LANES, past), lambda b, hp: (layer, b, hp, 0))
    drow_spec = pl.BlockSpec((1, H_FOX, SUBLANES, past), lambda b, hp: (layer, 0, b // SUBLANES, 0))
    cnrow_spec = pl.BlockSpec((1, SUBLANES, dec_seq), lambda b, hp: (b, 0, 0))
    ccol_spec = pl.BlockSpec((dec_seq, LANES), lambda b, hp: (b, 0))
    out_spec = pl.BlockSpec((dec_seq, LANES), lambda b, hp: (b, hp))
    out_shape = [jax.ShapeDtypeStruct((streams * dec_seq, W_ATT), F32)] * 2
    blk = ATT_TILE
    newest_spec = pl.BlockSpec((1, 1, LANES, blk), lambda b, hp: (layer, b, hp, past // blk - 1))
    hbm_spec = pl.BlockSpec(memory_space=pl.ANY)
    return pl.pallas_call(
        functools.partial(_sample_attn_kernel, blk=blk, layer=layer),
        grid=(streams, N_PAIR), name="sample_attention",
        in_specs=[new_spec(0), new_spec(0), new_spec(0), cache_spec, cache_spec, drow_spec,
                  cnrow_spec, ccol_spec, new_spec(1), new_spec(1), new_spec(1), newest_spec,
                  newest_spec, hbm_spec, hbm_spec],
        out_specs=[out_spec, out_spec], out_shape=out_shape,
        scratch_shapes=[pltpu.VMEM((2 * dec_seq, 1), F32), pltpu.VMEM((2 * dec_seq, LANES), F32),
                        pltpu.VMEM((LANES, blk), F32), pltpu.VMEM((LANES, blk), F32),
                        pltpu.SemaphoreType.DMA((2,))],
        compiler_params=_params(("parallel", "parallel"), 48),
    )(q, kb, vb, cache_fk, cache_fv, drow, cnrow, ccol, q, kb, vb, cache_sk, cache_sv,
      cache_sk, cache_sv)


def _merge_kernel(x_ref, of_ref, os_ref, u_ref, vn_ref, ws_ref, bs_ref, gmix_ref, wout_ref,
                  g_ref, b_ref, o_ref, *, alpha):
    tm = x_ref.shape[0]
    r = lax.broadcasted_iota(jnp.int32, (SGU_CHUNK, SGU_CHUNK), 0)
    c = lax.broadcasted_iota(jnp.int32, (SGU_CHUNK, SGU_CHUNK), 1)
    tril = r >= c
    head_mean = jnp.where(r // D_HEAD == c // D_HEAD, 1.0 / D_HEAD, 0.0).astype(BF16)
    lane = lax.broadcasted_iota(jnp.int32, (SGU_CHUNK, LANES), 1)

    gate_pieces = []
    for p in range(W_SGU // LANES):
        w0 = jnp.where(tril, ws_ref[2 * p], 0.0).astype(BF16)
        w1 = jnp.where(tril, ws_ref[2 * p + 1], 0.0).astype(BF16)
        bias = bs_ref[:, p * LANES:(p + 1) * LANES]
        chunks = []
        for ch in range(tm // SGU_CHUNK):
            rows = slice(ch * SGU_CHUNK, (ch + 1) * SGU_CHUNK)
            vp = vn_ref[rows, p * LANES:(p + 1) * LANES].astype(BF16)
            s = jnp.where(lane < D_HEAD, _dot(w0, vp), _dot(w1, vp)) + bias
            chunks.append(u_ref[rows, p * LANES:(p + 1) * LANES] * s)
        gate_pieces.append(jnp.concatenate(chunks, axis=0))

    def attention_pieces(ref):
        if ref.ndim == 2:
            return [ref[:, k * LANES:(k + 1) * LANES] for k in range(N_PAIR)]
        return [jnp.concatenate([ref[qb, k * LANES:(k + 1) * LANES, :].T
                                 for qb in range(ref.shape[0])], axis=0) for k in range(N_PAIR)]

    pieces = attention_pieces(of_ref) + gate_pieces + attention_pieces(os_ref)
    normed = []
    for k, o in enumerate(pieces):
        ms = _dot((o * o).astype(BF16), head_mean)
        gain = gmix_ref[:, k * LANES:(k + 1) * LANES]
        normed.append((o * lax.rsqrt(ms + RMS_EPS) * gain).astype(BF16))
    mix = _dot(jnp.concatenate(normed, axis=1), wout_ref[...])
    o_ref[...] = _layer_norm(alpha * x_ref[...] + mix, g_ref[...], b_ref[...])


def _merge(x, of, os_, u, vn, ws, bs, gmix, wout, g, b, *, alpha, name):
    n, d = x.shape
    tm = ROW_TILE
    row = lambda w: pl.BlockSpec((tm, w), lambda i: (i, 0))
    if of.ndim == 2:
        att = row(W_ATT)
    else:
        tq = of.shape[2]
        att = pl.BlockSpec((tm // tq, W_ATT, tq), lambda i: (i, 0, 0))
    return pl.pallas_call(
        functools.partial(_merge_kernel, alpha=alpha), grid=(n // tm,), name=name,
        in_specs=[row(d), att, att, row(W_SGU), row(W_SGU)]
        + [_resident(a.shape) for a in (ws, bs, gmix, wout, g, b)],
        out_specs=row(d), out_shape=jax.ShapeDtypeStruct((n, d), F32),
        compiler_params=_params(("parallel",), 48),
    )(x, of, os_, u, vn, ws, bs, gmix, wout, g, b)


def _mlp_kernel(x_ref, wup_ref, wdn_ref, g_ref, b_ref, o_ref, *, alpha, ff_chunk):
    x = x_ref[...]
    xb = x.astype(BF16)
    acc = alpha * x
    for c0 in range(0, wup_ref.shape[1], ff_chunk):
        h = jnp.maximum(_dot(xb, wup_ref[:, c0:c0 + ff_chunk]), 0.0)
        acc = acc + _dot((h * h).astype(BF16), wdn_ref[c0:c0 + ff_chunk, :])
    o_ref[...] = _layer_norm(acc, g_ref[...], b_ref[...])


def _mlp(x, wup, wdn, g, b, *, alpha, name):
    n, d = x.shape
    tm = ROW_TILE
    row = pl.BlockSpec((tm, d), lambda i: (i, 0))
    return pl.pallas_call(
        functools.partial(_mlp_kernel, alpha=alpha, ff_chunk=1024), grid=(n // tm,), name=name,
        in_specs=[row] + [_resident(a.shape) for a in (wup, wdn, g, b)],
        out_specs=row, out_shape=jax.ShapeDtypeStruct((n, d), F32),
        compiler_params=_params(("parallel",), 56),
    )(x, wup, wdn, g, b)


def _pack_w_in(w, b_f):
    sizes = [W_ATT, W_ATT, W_ATT, H_FOX, W_SGU, W_SGU, W_ATT, W_ATT, W_ATT]
    offs = [0]
    for s in sizes:
        offs.append(offs[-1] + s)
    q_f, k_f, v_f, f_lg, u_g, v_g, q_s, k_s, v_s = (w[:, offs[i]:offs[i + 1]] for i in range(9))
    pad_cols = lambda a, width: jnp.pad(a, ((0, 0), (0, width - a.shape[1])))
    w_rows = jnp.concatenate([q_f, q_s, k_f, k_s, v_f, v_s, u_g, v_g, pad_cols(f_lg, LANES)],
                             axis=1).astype(BF16)
    w_cols = jnp.concatenate([q_f, q_s, k_f, k_s, v_f, v_s, pad_cols(f_lg, F_ROWS)],
                             axis=1).T.astype(BF16)
    bfc = jnp.pad(b_f, (0, LANES - H_FOX)).reshape(1, LANES)
    bfr = jnp.broadcast_to(jnp.pad(b_f, (0, SUBLANES - H_FOX)).reshape(SUBLANES, 1),
                           (SUBLANES, LANES))
    return w_rows, w_cols, bfc, bfr


def kernel(x_prompt, x_sample, cache_fox_k, cache_fox_v, cache_fox_logf, cache_sb_k, cache_sb_v,
           w_in, b_f, g_v, b_v, w_s, b_s, g_mix, w_out, ln1_g, ln1_b, w_up, w_down, ln2_g, ln2_b):
    depth = w_in.shape[0]
    batch, seq, d_model = x_prompt.shape
    streams, dec_seq, _ = x_sample.shape
    past = cache_fox_k.shape[2]
    n_prompt = batch * seq
    n_sample = streams * dec_seq
    alpha = (2 * depth) ** 0.25
    assert seq % ROW_TILE == 0 and n_sample % ROW_TILE == 0 and ROW_TILE % dec_seq == 0
    assert dec_seq == D_HEAD and 2 * dec_seq == SGU_CHUNK and past % ATT_TILE == 0
    assert streams % SUBLANES == 0

    xp = x_prompt.reshape(n_prompt, d_model)
    xs = x_sample.reshape(n_sample, d_model)
    row2 = lambda a: a.reshape(1, -1)
    cache_t = lambda c: jnp.transpose(c, (0, 1, 3, 4, 2)).reshape(depth, streams, W_ATT, past)
    ck_f, cv_f, ck_s, cv_s = (cache_t(c) for c in (cache_fox_k, cache_fox_v, cache_sb_k, cache_sb_v))
    lf_cache = jnp.transpose(cache_fox_logf.astype(F32), (0, 3, 1, 2))
    drow = _suffix_sum(lf_cache.reshape(depth * H_FOX * streams, past)).reshape(
        depth, H_FOX, streams, past)

    p_rows = s_rows = None
    p_logf, s_logf, s_gate = [], [], []
    for l in range(depth):
        w_rows, w_cols, bfc, bfr = _pack_w_in(w_in[l], b_f[l])
        gv, bv = row2(g_v[l]), row2(b_v[l])
        wout = w_out[l].astype(BF16)
        wup, wdn = w_up[l].astype(BF16), w_down[l].astype(BF16)
        gmix, g1, b1, g2, b2 = (row2(a[l]) for a in (g_mix, ln1_g, ln1_b, ln2_g, ln2_b))

        w_l = w_s[l]
        half = w_l[:, :dec_seq, :dec_seq]
        zeros = jnp.zeros_like(half)
        w_blockdiag = jnp.concatenate([jnp.concatenate([half, zeros], axis=2),
                                       jnp.concatenate([zeros, half], axis=2)], axis=1)
        bias_full = jnp.repeat(b_s[l].T, D_HEAD, axis=1)
        bias_half = jnp.concatenate([bias_full[:dec_seq], bias_full[:dec_seq]], axis=0)

        (qt, kb, *p_rows, vtbf, vtbs, lfrow, crow, ccolp, u, vn, shiftrow,
         head_flags) = _in_proj_prompt(
            xp, w_rows, w_cols, bfr, gv, bv, batch=batch, seq=seq, layer=l, depth=depth,
            cache_rows=p_rows)
        pair_flags = jnp.max(head_flags[:, :H_FOX, 0].reshape(-1, N_PAIR, 2), axis=2).reshape(-1)
        of = _prompt_attention(qt, kb, vtbf, 0, batch=batch, seq=seq, crow=crow, ccolp=ccolp,
                               shiftrow=shiftrow, flags=pair_flags)
        os_ = _prompt_attention(qt, kb, vtbs, 1, batch=batch, seq=seq)
        x1 = _merge(xp, of, os_, u, vn, w_l, bias_full, gmix, wout, g1, b1, alpha=alpha,
                    name="merge_prompt")
        xp = _mlp(x1, wup, wdn, g2, b2, alpha=alpha, name="mlp_prompt")
        p_logf.append(jnp.transpose(lfrow[:H_FOX].reshape(H_FOX, batch, seq), (1, 2, 0)))

        (q, kb, vb, *s_rows, lfcol, ccol, cnrow, u, vn) = _in_proj_sample(
            xs, w_rows, w_cols, bfc, bfr, gv, bv, dec_seq=dec_seq, layer=l, depth=depth,
            cache_rows=s_rows)
        of, os_ = _sample_attention(q, kb, vb, ck_f, cv_f, drow, cnrow, ccol, ck_s, cv_s,
                                    layer=l, streams=streams, dec_seq=dec_seq)
        x1 = _merge(xs, of, os_, u, vn, w_blockdiag, bias_half, gmix, wout, g1, b1, alpha=alpha,
                    name="merge_sample")
        xs = _mlp(x1, wup, wdn, g2, b2, alpha=alpha, name="mlp_sample")
        s_logf.append(lfcol[:, :H_FOX].reshape(streams, dec_seq, H_FOX))
        s_gate.append(vn.reshape(streams, dec_seq, W_SGU))

    p_kf, p_ks, p_vf, p_vs = (jnp.transpose(a.reshape(depth, batch, H_FOX, D_HEAD, seq),
                                            (0, 1, 4, 2, 3)) for a in p_rows)
    s_kf, s_ks, s_vf, s_vs = (jnp.transpose(a, (0, 1, 3, 2, 4)) for a in s_rows)
    return (xp.reshape(batch, seq, d_model), xs.reshape(streams, dec_seq, d_model),
            p_kf, p_vf, jnp.stack(p_logf), p_ks, p_vs,
            s_kf, s_vf, jnp.stack(s_logf), s_ks, s_vs, jnp.stack(s_gate))
```

```python
import functools
import math

import jax
import jax.numpy as jnp
from jax import lax
from jax.experimental import pallas as pl
from jax.experimental.pallas import tpu as pltpu

F32 = jnp.float32
BF16 = jnp.bfloat16

LANES = 128
SUBLANES = 8
VMEM_BYTES_V7X = 64 * 1024 * 1024

D_HEAD = 64
H_FOX = 6
H_SB = 6
H_SGU = 4
W_ATT = H_FOX * D_HEAD
W_SGU = H_SGU * D_HEAD
N_PAIR = W_ATT // LANES
SGU_CHUNK = 128
LN_EPS = 1e-5
RMS_EPS = 1e-6
NEG_INF = -1e30
SB_CLOSED = 110.0
FOX_DEAD = 110.0
LOG2E = math.log2(math.e)
FOX_L_FLOOR = 1e-25
FOX_BOUND_SAFE = 41.0
FOX_WINDOW = 5
SB_WINDOW = 2
SB_SUFFIX_BLOCK = 128

ROW_TILE = 512
ATT_TILE = 256

_C_Q = 0
_C_K = 2 * W_ATT
_C_V = 4 * W_ATT
_C_G = 6 * W_ATT
_C_F = _C_G + 2 * W_SGU
W_ROWS = _C_F + LANES
_R_Q = 0
_R_K = 2 * W_ATT
_R_V = 4 * W_ATT
_R_F = 6 * W_ATT
F_ROWS = 2 * SUBLANES
W_COLS = _R_F + F_ROWS


def _params(semantics, vmem_mib, flags=None):
    assert vmem_mib * 1024 * 1024 < VMEM_BYTES_V7X
    return pltpu.CompilerParams(dimension_semantics=semantics,
                                vmem_limit_bytes=vmem_mib * 1024 * 1024, flags=flags)


def _resident(shape):
    nd = len(shape)
    return pl.BlockSpec(shape, lambda *_: (0,) * nd, pipeline_mode=pl.Buffered(1))


def _split3(x):
    hi = x.astype(BF16)
    r1 = x - hi.astype(F32)
    mid = r1.astype(BF16)
    lo = (r1 - mid.astype(F32)).astype(BF16)
    return hi, mid, lo


def _dot(a, b):
    return jnp.dot(a, b, preferred_element_type=F32)


def _dot_nt(a, b):
    return lax.dot_general(a, b, (((1,), (1,)), ((), ())), preferred_element_type=F32)


def _lane_cumsum(x, seg=None):
    n = x.shape[1]
    lane = lax.broadcasted_iota(jnp.int32, x.shape, 1)
    pos = lane if seg is None else lane % seg
    step = 1
    while step < (n if seg is None else seg):
        x = x + jnp.where(pos >= step, pltpu.roll(x, step, axis=1), 0.0)
        step *= 2
    return x


def _ones_dot3(x, ones_mat, left):
    if left:
        return sum(_dot(ones_mat, p) for p in _split3(x))
    return sum(_dot(p, ones_mat) for p in _split3(x))


def _gelu_tanh(x):
    return 0.5 * x * (1.0 + jnp.tanh(0.7978845608028654 * (x + 0.044715 * (x * x * x))))


def _log_sigmoid(x):
    return jnp.minimum(x, 0.0) - jnp.log1p(jnp.exp(-jnp.abs(x)))


def _softplus(z):
    return jnp.maximum(z, 0.0) + jnp.log(1.0 + jnp.exp(-jnp.abs(z)))


def _softplus2(z2):
    return jnp.maximum(z2, 0.0) + jnp.log(1.0 + jnp.exp2(-jnp.abs(z2))) * LOG2E


def _layer_norm(x, g, b):
    mu = jnp.mean(x, axis=-1, keepdims=True)
    xc = x - mu
    var = jnp.mean(xc * xc, axis=-1, keepdims=True)
    return xc * lax.rsqrt(var + LN_EPS) * g + b


def _tri_ones(n, kind, seg=None):
    r = lax.broadcasted_iota(jnp.int32, (n, n), 0)
    c = lax.broadcasted_iota(jnp.int32, (n, n), 1)
    keep = {"lower": c <= r, "upper": r <= c, "strict_lower": r > c, "strict_upper": r < c}[kind]
    if seg is not None:
        keep = jnp.logical_and(keep, r // seg == c // seg)
    return jnp.where(keep, 1.0, 0.0).astype(BF16)


def _pick_lane(x, idx):
    lane = lax.broadcasted_iota(jnp.int32, x.shape, 1)
    return jnp.sum(jnp.where(lane == idx, x, 0.0), axis=1, keepdims=True)


def _pick_row(x, idx):
    sub = lax.broadcasted_iota(jnp.int32, x.shape, 0)
    return jnp.sum(jnp.where(sub == idx, x, 0.0), axis=0, keepdims=True)


def _gates(xb, wr_ref, gv_ref, bv_ref, u_ref, vn_ref):
    u_ref[...] = _gelu_tanh(_dot(xb, wr_ref[:, _C_G:_C_G + W_SGU]))
    vg = _dot(xb, wr_ref[:, _C_G + W_SGU:_C_G + 2 * W_SGU])
    vn_ref[...] = _layer_norm(_gelu_tanh(vg), gv_ref[...], bv_ref[...])


def _log_forget_rows(xb, wc_ref, bfr_ref):
    sub = lax.broadcasted_iota(jnp.int32, (SUBLANES, xb.shape[0]), 0)
    z = _dot_nt(wc_ref[_R_F:_R_F + F_ROWS, :], xb)[:SUBLANES] + bfr_ref[:, 0:1]
    return jnp.where(sub < H_FOX, _log_sigmoid(z), 0.0)


def _in_proj_prompt_kernel(*refs, layer, depth):
    n_in = 6 if layer == 0 else 10
    x_ref, wr_ref, wc_ref, bfr_ref, gv_ref, bv_ref = refs[:6]
    (qt_ref, k_ref, ktf_ref, kts_ref, vtf_ref, vts_ref, vtbf_ref, vtbs_ref,
     lfrow_ref, crow_ref, ccolp_ref, u_ref, vn_ref, shiftrow_ref, flags_ref,
     carry_row_ref, knmax_ref, cend_ref) = refs[n_in:]
    i = pl.program_id(1)
    tm = x_ref.shape[0]
    tk = vtbf_ref.shape[3]

    @pl.when(i == 0)
    def _():
        carry_row_ref[...] = jnp.zeros_like(carry_row_ref)
        knmax_ref[...] = jnp.zeros_like(knmax_ref)

    xb = x_ref[...].astype(BF16)

    def blocks(zt):
        return [zt[:, kb * tk:(kb + 1) * tk].astype(BF16) for kb in range(tm // tk)]

    def store_cache_rows(ref, zt):
        if layer == 0:
            for l in range(depth):
                ref[l, 0] = zt if l == 0 else jnp.zeros_like(zt)
        else:
            ref[0, 0] = zt

    scale = LOG2E / math.sqrt(D_HEAD)
    qn2_rows = []
    for kb, blk in enumerate(blocks(_dot_nt(wc_ref[_R_Q:_R_Q + 2 * W_ATT, :], xb) * scale)):
        qt_ref[0, kb] = blk
        qf = blk[:W_ATT].astype(F32)
        qn2 = jnp.sum((qf * qf).reshape(H_FOX, D_HEAD, tk), axis=1)
        qn2_rows.append(jnp.concatenate([qn2, jnp.zeros((SUBLANES - H_FOX, tk), F32)], axis=0))
    for r0, full_ref, blk_ref in ((_R_V, vtf_ref, vtbf_ref), (_R_V + W_ATT, vts_ref, vtbs_ref)):
        zt = _dot_nt(wc_ref[r0:r0 + W_ATT, :], xb)
        store_cache_rows(full_ref, zt)
        for kb, blk in enumerate(blocks(zt)):
            blk_ref[0, kb] = blk
    k_rows = _dot(xb, wr_ref[:, _C_K:_C_K + 2 * W_ATT])
    k_ref[...] = k_rows.astype(BF16)
    kt_f = k_rows[:, :W_ATT].T
    store_cache_rows(ktf_ref, kt_f)
    store_cache_rows(kts_ref, k_rows[:, W_ATT:].T)

    _gates(xb, wr_ref, gv_ref, bv_ref, u_ref, vn_ref)

    kn2 = jnp.sum((kt_f * kt_f).reshape(H_FOX, D_HEAD, tm), axis=1)
    kn2 = jnp.broadcast_to(jnp.max(kn2, axis=1, keepdims=True), (H_FOX, LANES))
    kn2 = jnp.concatenate([kn2, jnp.zeros((SUBLANES - H_FOX, LANES), F32)], axis=0)
    knmax_ref[...] = jnp.maximum(knmax_ref[...], kn2)

    lf_row = _log_forget_rows(xb, wc_ref, bfr_ref)
    lfrow_ref[...] = lf_row
    c_row = _ones_dot3(lf_row, _tri_ones(tm, "upper"), left=False) + carry_row_ref[:, 0:1]
    carry_row_ref[...] = jnp.broadcast_to(c_row[:, tm - 1:tm], carry_row_ref.shape)
    c_row = c_row * LOG2E
    kn_run = jnp.concatenate([knmax_ref[...]] * (tk // LANES), axis=1)
    for kb in range(tm // tk):
        c_blk = c_row[:, kb * tk:(kb + 1) * tk]
        crow_ref[kb] = c_blk
        bound = jnp.sqrt(qn2_rows[kb] * kn_run) * 1.02
        shiftrow_ref[kb] = c_blk - bound
        blk_id = i * (tm // tk) + kb
        cend_ref[blk_id] = jnp.broadcast_to(c_blk[:, tk - 1:tk], (SUBLANES, LANES))
        left = blk_id - FOX_WINDOW
        decay = c_blk[:, 0:1] - cend_ref[jnp.maximum(left, 0)][:, 0:1]
        live = jnp.logical_and(decay > -FOX_DEAD * LOG2E, left >= 0)
        risky = jnp.max(bound, axis=1, keepdims=True) > FOX_BOUND_SAFE
        flags_ref[kb] = jnp.broadcast_to(
            jnp.where(jnp.logical_or(live, risky), 1, 0).astype(jnp.int32), (SUBLANES, LANES))
    pad = jnp.zeros((LANES - SUBLANES, tm), F32)
    for p in range(N_PAIR):
        rows = c_row if p == 0 else jnp.concatenate([c_row[2 * p:], c_row[:2 * p]], axis=0)
        ccolp_ref[:, p * LANES:(p + 1) * LANES] = jnp.concatenate([rows, pad], axis=0).T


def _in_proj_prompt(x, w_rows, w_cols, bfr, gv, bv, *, batch, seq, layer, depth, cache_rows):
    n, d = x.shape
    tm, tk = ROW_TILE, ATT_TILE
    nt = seq // tm
    row = lambda w: pl.BlockSpec((tm, w), lambda b, i: (b * nt + i, 0))
    if layer == 0:
        t_spec = pl.BlockSpec((depth, 1, W_ATT, tm), lambda b, i: (0, b, 0, i))
        extra_in, extra_specs, aliases = (), [], {}
    else:
        t_spec = pl.BlockSpec((1, 1, W_ATT, tm), lambda b, i: (layer, b, 0, i))
        extra_in = tuple(cache_rows)
        extra_specs = [pl.BlockSpec(memory_space=pl.ANY)] * 4
        aliases = {6 + k: 2 + k for k in range(4)}
    t_shape = jax.ShapeDtypeStruct((depth, batch, W_ATT, seq), F32)
    blk_spec = lambda w: pl.BlockSpec((1, tm // tk, w, tk), lambda b, i: (b, i, 0, 0))
    blk_shape = lambda w: jax.ShapeDtypeStruct((batch, seq // tk, w, tk), BF16)
    out_shape = [blk_shape(2 * W_ATT), jax.ShapeDtypeStruct((n, 2 * W_ATT), BF16),
                 t_shape, t_shape, t_shape, t_shape, blk_shape(W_ATT), blk_shape(W_ATT),
                 jax.ShapeDtypeStruct((SUBLANES, n), F32),
                 jax.ShapeDtypeStruct((n // tk, SUBLANES, tk), F32),
                 jax.ShapeDtypeStruct((n, N_PAIR * LANES), F32),
                 jax.ShapeDtypeStruct((n, W_SGU), F32), jax.ShapeDtypeStruct((n, W_SGU), F32),
                 jax.ShapeDtypeStruct((n // tk, SUBLANES, tk), F32),
                 jax.ShapeDtypeStruct((n // tk, SUBLANES, LANES), jnp.int32)]
    out_specs = [blk_spec(2 * W_ATT), row(2 * W_ATT), t_spec, t_spec, t_spec, t_spec,
                 blk_spec(W_ATT), blk_spec(W_ATT),
                 pl.BlockSpec((SUBLANES, tm), lambda b, i: (0, b * nt + i)),
                 pl.BlockSpec((tm // tk, SUBLANES, tk), lambda b, i: (b * nt + i, 0, 0)),
                 row(N_PAIR * LANES), row(W_SGU), row(W_SGU),
                 pl.BlockSpec((tm // tk, SUBLANES, tk), lambda b, i: (b * nt + i, 0, 0)),
                 pl.BlockSpec((tm // tk, SUBLANES, LANES), lambda b, i: (b * nt + i, 0, 0))]
    return pl.pallas_call(
        functools.partial(_in_proj_prompt_kernel, layer=layer, depth=depth),
        grid=(batch, nt), name="in_proj_prompt",
        in_specs=[row(d)] + [_resident(a.shape) for a in (w_rows, w_cols, bfr, gv, bv)]
        + extra_specs,
        out_specs=out_specs, out_shape=out_shape, input_output_aliases=aliases,
        scratch_shapes=[pltpu.VMEM((SUBLANES, LANES), F32), pltpu.VMEM((SUBLANES, LANES), F32),
                        pltpu.VMEM((seq // tk, SUBLANES, LANES), F32)],
        compiler_params=_params(("parallel", "arbitrary"), 56),
    )(x, w_rows, w_cols, bfr, gv, bv, *extra_in)


def _in_proj_sample_kernel(*refs, dec_seq, layer, depth):
    n_in = 7 if layer == 0 else 11
    x_ref, wr_ref, wc_ref, bfc_ref, bfr_ref, gv_ref, bv_ref = refs[:7]
    (q_ref, kb_ref, vb_ref, kf_ref, ks_ref, vf_ref, vs_ref, lfcol_ref, ccol_ref, crow_ref,
     u_ref, vn_ref) = refs[n_in:]
    tm = x_ref.shape[0]
    xb = x_ref[...].astype(BF16)
    scale = 1.0 / math.sqrt(D_HEAD)
    q_ref[...] = (_dot(xb, wr_ref[:, _C_Q:_C_Q + 2 * W_ATT]) * scale).astype(BF16)

    def store_cache_rows(ref, z):
        for s in range(tm // dec_seq):
            for h in range(W_ATT // D_HEAD):
                blk = z[s * dec_seq:(s + 1) * dec_seq, h * D_HEAD:(h + 1) * D_HEAD]
                ref[0, s, h] = blk
                if layer == 0:
                    for l in range(1, depth):
                        ref[l, s, h] = jnp.zeros_like(blk)

    for c0, refs_fs, half_ref in ((_C_K, (kf_ref, ks_ref), kb_ref), (_C_V, (vf_ref, vs_ref), vb_ref)):
        z = _dot(xb, wr_ref[:, c0:c0 + 2 * W_ATT])
        half_ref[...] = z.astype(BF16)
        store_cache_rows(refs_fs[0], z[:, :W_ATT])
        store_cache_rows(refs_fs[1], z[:, W_ATT:])
    _gates(xb, wr_ref, gv_ref, bv_ref, u_ref, vn_ref)

    lane = lax.broadcasted_iota(jnp.int32, (tm, LANES), 1)
    z = _dot(xb, wr_ref[:, _C_F:_C_F + LANES]) + bfc_ref[...]
    lf_col = jnp.where(lane < H_FOX, _log_sigmoid(z), 0.0)
    lfcol_ref[...] = lf_col
    ccol_ref[...] = _ones_dot3(lf_col, _tri_ones(tm, "lower", dec_seq), left=True)
    lf_row = _log_forget_rows(xb, wc_ref, bfr_ref)
    c_row = _lane_cumsum(lf_row, dec_seq)
    for s in range(tm // dec_seq):
        crow_ref[s] = c_row[:, s * dec_seq:(s + 1) * dec_seq]


def _in_proj_sample(x, w_rows, w_cols, bfc, bfr, gv, bv, *, dec_seq, layer, depth, cache_rows):
    n, d = x.shape
    tm = ROW_TILE
    heads = W_ATT // D_HEAD
    row = lambda w: pl.BlockSpec((tm, w), lambda i: (i, 0))
    sds = lambda w, dt: jax.ShapeDtypeStruct((n, w), dt)
    if layer == 0:
        c_spec = pl.BlockSpec((depth, tm // dec_seq, heads, dec_seq, D_HEAD), lambda i: (0, i, 0, 0, 0))
        extra_in, extra_specs, aliases = (), [], {}
    else:
        c_spec = pl.BlockSpec((1, tm // dec_seq, heads, dec_seq, D_HEAD), lambda i: (layer, i, 0, 0, 0))
        extra_in = tuple(cache_rows)
        extra_specs = [pl.BlockSpec(memory_space=pl.ANY)] * 4
        aliases = {7 + k: 3 + k for k in range(4)}
    c_shape = jax.ShapeDtypeStruct((depth, n // dec_seq, heads, dec_seq, D_HEAD), F32)
    out_shape = [sds(2 * W_ATT, BF16)] * 3 + [c_shape] * 4 + [sds(LANES, F32)] * 2 + [
        jax.ShapeDtypeStruct((n // dec_seq, SUBLANES, dec_seq), F32), sds(W_SGU, F32), sds(W_SGU, F32)]
    out_specs = [row(2 * W_ATT)] * 3 + [c_spec] * 4 + [row(LANES)] * 2 + [
        pl.BlockSpec((tm // dec_seq, SUBLANES, dec_seq), lambda i: (i, 0, 0)), row(W_SGU), row(W_SGU)]
    return pl.pallas_call(
        functools.partial(_in_proj_sample_kernel, dec_seq=dec_seq, layer=layer, depth=depth),
        grid=(n // tm,), name="in_proj_sample",
        in_specs=[row(d)] + [_resident(a.shape) for a in (w_rows, w_cols, bfc, bfr, gv, bv)]
        + extra_specs,
        out_specs=out_specs, out_shape=out_shape, input_output_aliases=aliases,
        compiler_params=_params(("parallel",), 56),
    )(x, w_rows, w_cols, bfc, bfr, gv, bv, *extra_in)


def _split_head_rows(qt):
    sub = lax.broadcasted_iota(jnp.int32, qt.shape, 0)
    low = sub < D_HEAD
    zero = jnp.zeros_like(qt)
    return jnp.where(low, qt, zero), jnp.where(low, zero, qt)


def _fox_update_t(s, ct, pv, state, valid):
    m, l, acc = state
    if valid is not None:
        s = jnp.where(valid, s, NEG_INF)
    m_new = jnp.maximum(m, jnp.max(s, axis=0, keepdims=True) + ct)
    p = jnp.exp2(s + (ct - m_new))
    alpha = jnp.exp2(m - m_new)
    l = alpha * l + jnp.sum(p, axis=0, keepdims=True)
    acc = alpha * acc + pv(p.astype(BF16))
    return m_new, l, acc


def _sb_update_t(z, pv, suffix_ones, state, valid):
    carry, acc = state
    r = _softplus2(z)
    if valid is not None:
        r = jnp.where(valid, r, 0.0)
    r_hi = r.astype(BF16)
    r_lo = (r - r_hi.astype(F32)).astype(BF16)
    after = _dot(suffix_ones, r_hi) + _dot(suffix_ones, r_lo) + carry
    a = jnp.exp2((z - r) - after)
    if valid is not None:
        a = jnp.where(valid, a, 0.0)
    acc = acc + pv(a.astype(BF16))
    carry = carry + jnp.sum(r, axis=0, keepdims=True)
    return carry, acc


def _fox_prompt_kernel(*refs, window_blocks):
    lax.switch(jnp.minimum(pl.program_id(2), window_blocks - 1),
               [functools.partial(_fox_prompt_body, n, *refs) for n in range(1, window_blocks + 1)])


def _fox_prompt_body(nblk, flags_ref, qt_ref, k_ref, vt_ref, crow_ref, ccolp_ref, shiftrow_ref,
                     o_ref):
    tq = qt_ref.shape[3]
    tk = vt_ref.shape[3]
    hp = pl.program_id(1)
    i = pl.program_id(2)
    q2 = jnp.concatenate(_split_head_rows(qt_ref[0, 0]), axis=1)
    c_q = crow_ref[i]
    ct = jnp.concatenate([_pick_row(c_q, 2 * hp), _pick_row(c_q, 2 * hp + 1)], axis=1)

    key = lax.broadcasted_iota(jnp.int32, (tk, 2 * tq), 0)
    qry = lax.broadcasted_iota(jnp.int32, (tk, 2 * tq), 1) % tq
    causal = key <= qry

    def scores(j0, nblk):
        start = pl.multiple_of(j0 * tk, tk)
        kb = k_ref[pl.ds(start, nblk * tk), :]
        cc = ccolp_ref[pl.ds(start, nblk * tk), :]
        ck = jnp.concatenate([jnp.broadcast_to(cc[:, 0:1], (nblk * tk, tq)),
                              jnp.broadcast_to(cc[:, 1:2], (nblk * tk, tq))], axis=1)
        return _dot(kb, q2) - ck

    def pv(p, j0, nblk):
        vts = [vt_ref[0, j0 + w] for w in range(nblk)]
        outs = []
        for h in range(2):
            vth = jnp.concatenate([v[h * D_HEAD:(h + 1) * D_HEAD] for v in vts], axis=1)
            outs.append(_dot(vth, p[:, h * tq:(h + 1) * tq]))
        return jnp.concatenate(outs, axis=1)

    def pair_row(a, b):
        return jnp.concatenate([jnp.broadcast_to(a, (1, tq)), jnp.broadcast_to(b, (1, tq))], axis=1)

    sh = shiftrow_ref[i]
    shift = jnp.concatenate([_pick_row(sh, 2 * hp), _pick_row(sh, 2 * hp + 1)], axis=1)

    def reaches(j):
        return jnp.max(decay(j)) > -FOX_DEAD * LOG2E

    def decay(j):
        c_end = crow_ref[jnp.maximum(j, 0)][:, tk - 1:tk]
        return ct - pair_row(_pick_row(c_end, 2 * hp), _pick_row(c_end, 2 * hp + 1))

    def window(nblk):
        j0 = i - (nblk - 1)
        p = jnp.exp2(scores(j0, nblk) + shift)
        body_rows = (nblk - 1) * tk
        diag = jnp.where(causal, p[body_rows:], 0.0)
        p = diag if nblk == 1 else jnp.concatenate([p[:body_rows], diag], axis=0)
        return (j0 - 1, jnp.sum(p, axis=0, keepdims=True), pv(p.astype(BF16), j0, nblk))

    j, l, acc = window(nblk)

    def cond(c):
        j, go, _, _ = c
        return jnp.logical_and(j >= 0, go)

    def body(c):
        j, _, l, acc = c
        p = jnp.exp2(scores(j, 1) + shift)
        return (j - 1, reaches(j - 1), l + jnp.sum(p, axis=0, keepdims=True),
                acc + pv(p.astype(BF16), j, 1))

    def exact():
        def step(t, state):
            jj = i - 1 - t
            return _fox_update_t(scores(jj, 1), ct, lambda p: pv(p, jj, 1), state, None)

        init = (jnp.full((1, 2 * tq), NEG_INF, F32), jnp.zeros((1, 2 * tq), F32),
                jnp.zeros((D_HEAD, 2 * tq), F32))
        state = _fox_update_t(scores(i, 1), ct, lambda p: pv(p, i, 1), init, causal)
        _, l_x, acc_x = lax.fori_loop(0, i, step, state)
        return l_x, acc_x

    def beyond_window():
        _, _, l_w, acc_w = lax.while_loop(cond, body, (j, reaches(j), l, acc))
        return lax.cond(jnp.min(l_w) < FOX_L_FLOOR, exact, lambda: (l_w, acc_w))

    b = pl.program_id(0)
    uncommon = flags_ref[(b * pl.num_programs(2) + i) * N_PAIR + hp] != 0
    l, acc = lax.cond(uncommon, beyond_window, lambda: (l, acc))
    o = acc / l
    o_ref[0] = jnp.concatenate([o[:, :tq], o[:, tq:]], axis=0)


def _sb_prompt_kernel(*refs, window_blocks):
    lax.switch(jnp.minimum(pl.program_id(2), window_blocks - 1),
               [functools.partial(_sb_prompt_body, n, *refs) for n in range(1, window_blocks + 1)])


def _sb_prompt_body(nblk, qt_ref, k_ref, vt_ref, o_ref):
    tq = qt_ref.shape[3]
    tk = vt_ref.shape[3]
    i = pl.program_id(2)
    q2 = jnp.concatenate(_split_head_rows(qt_ref[0, 0]), axis=1)
    ts = SB_SUFFIX_BLOCK
    sub_ones = _tri_ones(ts, "strict_upper")

    def block(j, state, valid):
        kb = k_ref[pl.ds(pl.multiple_of(j * tk, tk), tk), :]
        vt = vt_ref[0, j]

        def pv(a):
            return jnp.concatenate([_dot(vt[:D_HEAD], a[:, :tq]), _dot(vt[D_HEAD:], a[:, tq:])],
                                   axis=1)

        return _sb_update_t(_dot(kb, q2), pv, _tri_ones(tk, "strict_upper"), state, valid)

    def still_open(carry):
        return jnp.min(carry) < SB_CLOSED * LOG2E

    key = lax.broadcasted_iota(jnp.int32, (tk, 2 * tq), 0)
    qry = lax.broadcasted_iota(jnp.int32, (tk, 2 * tq), 1) % tq
    causal = key < qry

    def window(nblk):
        j0 = i - (nblk - 1)
        z = _dot(k_ref[pl.ds(pl.multiple_of(j0 * tk, tk), nblk * tk), :], q2)
        r = _softplus2(z)
        body_rows = (nblk - 1) * tk
        diag = jnp.where(causal, r[body_rows:], 0.0)
        r = diag if nblk == 1 else jnp.concatenate([r[:body_rows], diag], axis=0)
        carry = jnp.zeros((1, 2 * tq), F32)
        n_sub = nblk * tk // ts
        a_blocks = [None] * n_sub
        for w in reversed(range(n_sub)):
            rows = slice(w * ts, (w + 1) * ts)
            r_w = r[rows]
            after = _dot(sub_ones, r_w.astype(BF16)) + carry
            a_blocks[w] = jnp.exp2((z[rows] - r_w) - after)
            carry = carry + jnp.sum(r_w, axis=0, keepdims=True)
        a = jnp.concatenate(a_blocks, axis=0)
        diag = jnp.where(causal, a[body_rows:], 0.0)
        a = (diag if nblk == 1 else jnp.concatenate([a[:body_rows], diag], axis=0)).astype(BF16)
        vts = [vt_ref[0, j0 + w] for w in range(nblk)]
        outs = []
        for h in range(2):
            vth = jnp.concatenate([v[h * D_HEAD:(h + 1) * D_HEAD] for v in vts], axis=1)
            outs.append(_dot(vth, a[:, h * tq:(h + 1) * tq]))
        return j0 - 1, still_open(carry), (carry, jnp.concatenate(outs, axis=1))

    j, go, state = window(nblk)

    def cond(c):
        j, go, _ = c
        return jnp.logical_and(j >= 0, go)

    def body(c):
        j, _, st = c
        st = block(j, st, None)
        return j - 1, still_open(st[0]), st

    _, _, (_, acc) = lax.while_loop(cond, body, (j, go, state))
    o_ref[0] = jnp.concatenate([acc[:, :tq], acc[:, tq:]], axis=0)


def _prompt_attention(qt, k, vtb, group, *, batch, seq, crow=None, ccolp=None, shiftrow=None,
                      flags=None):
    tq = tk = ATT_TILE
    nq = seq // tq
    col0 = group * N_PAIR
    qt_spec = pl.BlockSpec((1, 1, LANES, tq), lambda b, hp, i, *_: (b, i, col0 + hp, 0))
    k_spec = pl.BlockSpec((seq, LANES), lambda b, hp, i, *_: (b, col0 + hp))
    vt_spec = pl.BlockSpec((1, seq // tk, LANES, tk), lambda b, hp, i, *_: (b, 0, hp, 0))
    out_spec = pl.BlockSpec((1, LANES, tq), lambda b, hp, i, *_: (b * nq + i, hp, 0))
    out_shape = jax.ShapeDtypeStruct((batch * nq, W_ATT, tq), F32)
    params = _params(("parallel", "parallel", "arbitrary"), 40)
    grid = (batch, N_PAIR, nq)
    if group == 1:
        return pl.pallas_call(
            functools.partial(_sb_prompt_kernel, window_blocks=SB_WINDOW), grid=grid,
            name="sb_prompt",
            in_specs=[qt_spec, k_spec, vt_spec], out_specs=out_spec, out_shape=out_shape,
            compiler_params=params)(qt, k, vtb)
    crow_spec = pl.BlockSpec((seq // tk, SUBLANES, tk), lambda b, hp, i, *_: (b, 0, 0))
    ccolp_spec = pl.BlockSpec((seq, LANES), lambda b, hp, i, *_: (b, hp))
    grid_spec = pltpu.PrefetchScalarGridSpec(
        num_scalar_prefetch=1, grid=grid,
        in_specs=[qt_spec, k_spec, vt_spec, crow_spec, ccolp_spec, crow_spec],
        out_specs=out_spec)
    return pl.pallas_call(
        functools.partial(_fox_prompt_kernel, window_blocks=FOX_WINDOW), grid_spec=grid_spec,
        name="fox_prompt", out_shape=out_shape,
        compiler_params=params)(flags, qt, k, vtb, crow, ccolp, shiftrow)


def _suffix_sum_kernel(x_ref, o_ref):
    rows, width = x_ref.shape
    ones = _tri_ones(LANES, "strict_lower")
    carry = jnp.zeros((rows, 1), F32)
    for kb in reversed(range(width // LANES)):
        xb = x_ref[:, kb * LANES:(kb + 1) * LANES]
        o_ref[:, kb * LANES:(kb + 1) * LANES] = _ones_dot3(xb, ones, left=False) + carry
        carry = carry + jnp.sum(xb, axis=1, keepdims=True)


def _suffix_sum(x):
    return pl.pallas_call(
        _suffix_sum_kernel, name="cache_suffix_sum",
        out_shape=jax.ShapeDtypeStruct(x.shape, F32),
        compiler_params=_params(None, 32))(x)


def _split_heads(q2):
    lane = lax.broadcasted_iota(jnp.int32, q2.shape, 1)
    low = lane < D_HEAD
    zero = jnp.zeros_like(q2)
    return jnp.where(low, q2, zero), jnp.where(low, zero, q2)


def _stack_heads(q2):
    return jnp.concatenate(_split_heads(q2), axis=0)


def _unstack_heads(o):
    half = o.shape[0] // 2
    lane = lax.broadcasted_iota(jnp.int32, (half, o.shape[1]), 1)
    return jnp.where(lane < D_HEAD, o[:half], o[half:])


def _sb_update(z, pv, suffix_ones, state, valid):
    carry, acc = state
    r = _softplus(z)
    if valid is not None:
        r = jnp.where(valid, r, 0.0)
    r_hi = r.astype(BF16)
    r_lo = (r - r_hi.astype(F32)).astype(BF16)
    after = _dot(r_hi, suffix_ones) + _dot(r_lo, suffix_ones) + carry
    a = jnp.exp((z - r) - after)
    if valid is not None:
        a = jnp.where(valid, a, 0.0)
    acc = acc + pv(a.astype(BF16))
    carry = carry + jnp.sum(r, axis=1, keepdims=True)
    return carry, acc


def _sample_attn_kernel(qf_ref, kfn_ref, vfn_ref, kfc_ref, vfc_ref, drow_ref, cnrow_ref, ccol_ref,
                        qs_ref, ksn_ref, vsn_ref, ksc_last_ref, vsc_last_ref, ksc_hbm_ref,
                        vsc_hbm_ref, of_ref, os_ref,
                        carry_ref, acc_ref, kbuf_ref, vbuf_ref, sem_ref, *, blk, layer):
    b = pl.program_id(0)
    hp = pl.program_id(1)
    tq = qf_ref.shape[0]
    past = kfc_ref.shape[3]
    row = lax.broadcasted_iota(jnp.int32, (2 * tq, tq), 0) % tq
    col = lax.broadcasted_iota(jnp.int32, (2 * tq, tq), 1)

    def per_head(fn):
        return jnp.concatenate([fn(0), fn(1)], axis=0)

    q2 = _stack_heads(qf_ref[...])
    kct = kfc_ref[0, 0].astype(BF16)
    vct = vfc_ref[0, 0].astype(BF16)
    cc = ccol_ref[...]
    cn = cnrow_ref[0]
    cq = per_head(lambda h: _pick_lane(cc, 2 * hp + h))
    qk_c = _dot(q2, kct)
    qk_n = _dot_nt(q2, kfn_ref[...])
    s_c = per_head(lambda h: qk_c[h * tq:(h + 1) * tq]
                   + _pick_row(drow_ref[0, 2 * hp + h], b % SUBLANES))
    s_n = per_head(lambda h: qk_n[h * tq:(h + 1) * tq] - _pick_row(cn, 2 * hp + h))
    s_n = jnp.where(col <= row, s_n, NEG_INF)
    m = jnp.maximum(jnp.max(s_c, axis=1, keepdims=True), jnp.max(s_n, axis=1, keepdims=True)) + cq
    shift = cq - m
    p_c = jnp.exp(s_c + shift)
    p_n = jnp.exp(s_n + shift)
    l = jnp.sum(p_c, axis=1, keepdims=True) + jnp.sum(p_n, axis=1, keepdims=True)
    o = _dot_nt(p_c.astype(BF16), vct) + _dot(p_n.astype(BF16), vfn_ref[...])
    of_ref[...] = _unstack_heads(o / l)

    q2 = _stack_heads(qs_ref[...])
    vn = vsn_ref[...]
    suffix_ones = _tri_ones(blk, "strict_lower")

    def cache_block(kt_f32, vt_f32, state):
        kt = kt_f32.astype(BF16)
        vt = vt_f32.astype(BF16)
        return _sb_update(_dot(q2, kt), lambda a: _dot_nt(a, vt), suffix_ones, state, None)

    n_blocks = past // blk
    state = (jnp.zeros((2 * tq, 1), F32), jnp.zeros((2 * tq, LANES), F32))
    state = _sb_update(_dot_nt(q2, ksn_ref[...]), lambda a: _dot(a, vn),
                       _tri_ones(tq, "strict_lower"), state, col < row)
    carry_ref[...], acc_ref[...] = cache_block(ksc_last_ref[0, 0], vsc_last_ref[0, 0], state)

    def still_open():
        return jnp.min(carry_ref[...]) < SB_CLOSED

    def fetch(src_ref, jb, buf_ref, sem):
        rows = pl.ds(pl.multiple_of(hp * LANES, LANES), LANES)
        return pltpu.make_async_copy(src_ref.at[layer, b, rows, pl.ds(jb * blk, blk)], buf_ref, sem)

    @pl.when(still_open())
    def _():
        for jb in reversed(range(n_blocks - 1)):
            @pl.when(still_open())
            def _():
                copies = (fetch(ksc_hbm_ref, jb, kbuf_ref, sem_ref.at[0]),
                          fetch(vsc_hbm_ref, jb, vbuf_ref, sem_ref.at[1]))
                for cp in copies:
                    cp.start()
                for cp in copies:
                    cp.wait()
                carry_ref[...], acc_ref[...] = cache_block(
                    kbuf_ref[...], vbuf_ref[...], (carry_ref[...], acc_ref[...]))

    os_ref[...] = _unstack_heads(acc_ref[...])


def _sample_attention(q, kb, vb, cache_fk, cache_fv, drow, cnrow, ccol, cache_sk, cache_sv,
                      *, layer, streams, dec_seq):
    past = cache_fk.shape[3]
    new_spec = lambda g: pl.BlockSpec((dec_seq, LANES), lambda b, hp: (b, g * N_PAIR + hp))
    cache_spec = pl.BlockSpec((1, 1, LANES, past), lambda b, hp: (layer, b, hp, 0))
    drow_spec = pl.BlockSpec((1, H_FOX, SUBLANES, past), lambda b, hp: (layer, 0, b // SUBLANES, 0))
    cnrow_spec = pl.BlockSpec((1, SUBLANES, dec_seq), lambda b, hp: (b, 0, 0))
    ccol_spec = pl.BlockSpec((dec_seq, LANES), lambda b, hp: (b, 0))
    out_spec = pl.BlockSpec((dec_seq, LANES), lambda b, hp: (b, hp))
    out_shape = [jax.ShapeDtypeStruct((streams * dec_seq, W_ATT), F32)] * 2
    blk = ATT_TILE
    newest_spec = pl.BlockSpec((1, 1, LANES, blk), lambda b, hp: (layer, b, hp, past // blk - 1))
    hbm_spec = pl.BlockSpec(memory_space=pl.ANY)
    return pl.pallas_call(
        functools.partial(_sample_attn_kernel, blk=blk, layer=layer),
        grid=(streams, N_PAIR), name="sample_attention",
        in_specs=[new_spec(0), new_spec(0), new_spec(0), cache_spec, cache_spec, drow_spec,
                  cnrow_spec, ccol_spec, new_spec(1), new_spec(1), new_spec(1), newest_spec,
                  newest_spec, hbm_spec, hbm_spec],
        out_specs=[out_spec, out_spec], out_shape=out_shape,
        scratch_shapes=[pltpu.VMEM((2 * dec_seq, 1), F32), pltpu.VMEM((2 * dec_seq, LANES), F32),
                        pltpu.VMEM((LANES, blk), F32), pltpu.VMEM((LANES, blk), F32),
                        pltpu.SemaphoreType.DMA((2,))],
        compiler_params=_params(("parallel", "parallel"), 48),
    )(q, kb, vb, cache_fk, cache_fv, drow, cnrow, ccol, q, kb, vb, cache_sk, cache_sv,
      cache_sk, cache_sv)


def _merge_kernel(x_ref, of_ref, os_ref, u_ref, vn_ref, ws_ref, bs_ref, gmix_ref, wout_ref,
                  g_ref, b_ref, o_ref, *, alpha):
    tm = x_ref.shape[0]
    r = lax.broadcasted_iota(jnp.int32, (SGU_CHUNK, SGU_CHUNK), 0)
    c = lax.broadcasted_iota(jnp.int32, (SGU_CHUNK, SGU_CHUNK), 1)
    tril = r >= c
    head_mean = jnp.where(r // D_HEAD == c // D_HEAD, 1.0 / D_HEAD, 0.0).astype(BF16)
    lane = lax.broadcasted_iota(jnp.int32, (SGU_CHUNK, LANES), 1)

    gate_pieces = []
    for p in range(W_SGU // LANES):
        w0 = jnp.where(tril, ws_ref[2 * p], 0.0).astype(BF16)
        w1 = jnp.where(tril, ws_ref[2 * p + 1], 0.0).astype(BF16)
        bias = bs_ref[:, p * LANES:(p + 1) * LANES]
        chunks = []
        for ch in range(tm // SGU_CHUNK):
            rows = slice(ch * SGU_CHUNK, (ch + 1) * SGU_CHUNK)
            vp = vn_ref[rows, p * LANES:(p + 1) * LANES].astype(BF16)
            s = jnp.where(lane < D_HEAD, _dot(w0, vp), _dot(w1, vp)) + bias
            chunks.append(u_ref[rows, p * LANES:(p + 1) * LANES] * s)
        gate_pieces.append(jnp.concatenate(chunks, axis=0))

    def attention_pieces(ref):
        if ref.ndim == 2:
            return [ref[:, k * LANES:(k + 1) * LANES] for k in range(N_PAIR)]
        return [jnp.concatenate([ref[qb, k * LANES:(k + 1) * LANES, :].T
                                 for qb in range(ref.shape[0])], axis=0) for k in range(N_PAIR)]

    pieces = attention_pieces(of_ref) + gate_pieces + attention_pieces(os_ref)
    normed = []
    for k, o in enumerate(pieces):
        ms = _dot((o * o).astype(BF16), head_mean)
        gain = gmix_ref[:, k * LANES:(k + 1) * LANES]
        normed.append((o * lax.rsqrt(ms + RMS_EPS) * gain).astype(BF16))
    mix = _dot(jnp.concatenate(normed, axis=1), wout_ref[...])
    o_ref[...] = _layer_norm(alpha * x_ref[...] + mix, g_ref[...], b_ref[...])


def _merge(x, of, os_, u, vn, ws, bs, gmix, wout, g, b, *, alpha, name):
    n, d = x.shape
    tm = ROW_TILE
    row = lambda w: pl.BlockSpec((tm, w), lambda i: (i, 0))
    if of.ndim == 2:
        att = row(W_ATT)
    else:
        tq = of.shape[2]
        att = pl.BlockSpec((tm // tq, W_ATT, tq), lambda i: (i, 0, 0))
    return pl.pallas_call(
        functools.partial(_merge_kernel, alpha=alpha), grid=(n // tm,), name=name,
        in_specs=[row(d), att, att, row(W_SGU), row(W_SGU)]
        + [_resident(a.shape) for a in (ws, bs, gmix, wout, g, b)],
        out_specs=row(d), out_shape=jax.ShapeDtypeStruct((n, d), F32),
        compiler_params=_params(("parallel",), 48),
    )(x, of, os_, u, vn, ws, bs, gmix, wout, g, b)


def _mlp_kernel(x_ref, wup_ref, wdn_ref, g_ref, b_ref, o_ref, *, alpha, ff_chunk):
    x = x_ref[...]
    xb = x.astype(BF16)
    acc = alpha * x
    for c0 in range(0, wup_ref.shape[1], ff_chunk):
        h = jnp.maximum(_dot(xb, wup_ref[:, c0:c0 + ff_chunk]), 0.0)
        acc = acc + _dot((h * h).astype(BF16), wdn_ref[c0:c0 + ff_chunk, :])
    o_ref[...] = _layer_norm(acc, g_ref[...], b_ref[...])


def _mlp(x, wup, wdn, g, b, *, alpha, name):
    n, d = x.shape
    tm = ROW_TILE
    row = pl.BlockSpec((tm, d), lambda i: (i, 0))
    return pl.pallas_call(
        functools.partial(_mlp_kernel, alpha=alpha, ff_chunk=1024), grid=(n // tm,), name=name,
        in_specs=[row] + [_resident(a.shape) for a in (wup, wdn, g, b)],
        out_specs=row, out_shape=jax.ShapeDtypeStruct((n, d), F32),
        compiler_params=_params(("parallel",), 56),
    )(x, wup, wdn, g, b)


def _pack_w_in(w, b_f):
    sizes = [W_ATT, W_ATT, W_ATT, H_FOX, W_SGU, W_SGU, W_ATT, W_ATT, W_ATT]
    offs = [0]
    for s in sizes:
        offs.append(offs[-1] + s)
    q_f, k_f, v_f, f_lg, u_g, v_g, q_s, k_s, v_s = (w[:, offs[i]:offs[i + 1]] for i in range(9))
    pad_cols = lambda a, width: jnp.pad(a, ((0, 0), (0, width - a.shape[1])))
    w_rows = jnp.concatenate([q_f, q_s, k_f, k_s, v_f, v_s, u_g, v_g, pad_cols(f_lg, LANES)],
                             axis=1).astype(BF16)
    w_cols = jnp.concatenate([q_f, q_s, k_f, k_s, v_f, v_s, pad_cols(f_lg, F_ROWS)],
                             axis=1).T.astype(BF16)
    bfc = jnp.pad(b_f, (0, LANES - H_FOX)).reshape(1, LANES)
    bfr = jnp.broadcast_to(jnp.pad(b_f, (0, SUBLANES - H_FOX)).reshape(SUBLANES, 1),
                           (SUBLANES, LANES))
    return w_rows, w_cols, bfc, bfr


def kernel(x_prompt, x_sample, cache_fox_k, cache_fox_v, cache_fox_logf, cache_sb_k, cache_sb_v,
           w_in, b_f, g_v, b_v, w_s, b_s, g_mix, w_out, ln1_g, ln1_b, w_up, w_down, ln2_g, ln2_b):
    depth = w_in.shape[0]
    batch, seq, d_model = x_prompt.shape
    streams, dec_seq, _ = x_sample.shape
    past = cache_fox_k.shape[2]
    n_prompt = batch * seq
    n_sample = streams * dec_seq
    alpha = (2 * depth) ** 0.25
    assert seq % ROW_TILE == 0 and n_sample % ROW_TILE == 0 and ROW_TILE % dec_seq == 0
    assert dec_seq == D_HEAD and 2 * dec_seq == SGU_CHUNK and past % ATT_TILE == 0
    assert streams % SUBLANES == 0

    xp = x_prompt.reshape(n_prompt, d_model)
    xs = x_sample.reshape(n_sample, d_model)
    row2 = lambda a: a.reshape(1, -1)
    cache_t = lambda c: jnp.transpose(c, (0, 1, 3, 4, 2)).reshape(depth, streams, W_ATT, past)
    ck_f, cv_f, ck_s, cv_s = (cache_t(c) for c in (cache_fox_k, cache_fox_v, cache_sb_k, cache_sb_v))
    lf_cache = jnp.transpose(cache_fox_logf.astype(F32), (0, 3, 1, 2))
    drow = _suffix_sum(lf_cache.reshape(depth * H_FOX * streams, past)).reshape(
        depth, H_FOX, streams, past)

    p_rows = s_rows = None
    p_logf, s_logf, s_gate = [], [], []
    for l in range(depth):
        w_rows, w_cols, bfc, bfr = _pack_w_in(w_in[l], b_f[l])
        gv, bv = row2(g_v[l]), row2(b_v[l])
        wout = w_out[l].astype(BF16)
        wup, wdn = w_up[l].astype(BF16), w_down[l].astype(BF16)
        gmix, g1, b1, g2, b2 = (row2(a[l]) for a in (g_mix, ln1_g, ln1_b, ln2_g, ln2_b))

        w_l = w_s[l]
        half = w_l[:, :dec_seq, :dec_seq]
        zeros = jnp.zeros_like(half)
        w_blockdiag = jnp.concatenate([jnp.concatenate([half, zeros], axis=2),
                                       jnp.concatenate([zeros, half], axis=2)], axis=1)
        bias_full = jnp.repeat(b_s[l].T, D_HEAD, axis=1)
        bias_half = jnp.concatenate([bias_full[:dec_seq], bias_full[:dec_seq]], axis=0)

        (qt, kb, *p_rows, vtbf, vtbs, lfrow, crow, ccolp, u, vn, shiftrow,
         head_flags) = _in_proj_prompt(
            xp, w_rows, w_cols, bfr, gv, bv, batch=batch, seq=seq, layer=l, depth=depth,
            cache_rows=p_rows)
        pair_flags = jnp.max(head_flags[:, :H_FOX, 0].reshape(-1, N_PAIR, 2), axis=2).reshape(-1)
        of = _prompt_attention(qt, kb, vtbf, 0, batch=batch, seq=seq, crow=crow, ccolp=ccolp,
                               shiftrow=shiftrow, flags=pair_flags)
        os_ = _prompt_attention(qt, kb, vtbs, 1, batch=batch, seq=seq)
        x1 = _merge(xp, of, os_, u, vn, w_l, bias_full, gmix, wout, g1, b1, alpha=alpha,
                    name="merge_prompt")
        xp = _mlp(x1, wup, wdn, g2, b2, alpha=alpha, name="mlp_prompt")
        p_logf.append(jnp.transpose(lfrow[:H_FOX].reshape(H_FOX, batch, seq), (1, 2, 0)))

        (q, kb, vb, *s_rows, lfcol, ccol, cnrow, u, vn) = _in_proj_sample(
            xs, w_rows, w_cols, bfc, bfr, gv, bv, dec_seq=dec_seq, layer=l, depth=depth,
            cache_rows=s_rows)
        of, os_ = _sample_attention(q, kb, vb, ck_f, cv_f, drow, cnrow, ccol, ck_s, cv_s,
                                    layer=l, streams=streams, dec_seq=dec_seq)
        x1 = _merge(xs, of, os_, u, vn, w_blockdiag, bias_half, gmix, wout, g1, b1, alpha=alpha,
                    name="merge_sample")
        xs = _mlp(x1, wup, wdn, g2, b2, alpha=alpha, name="mlp_sample")
        s_logf.append(lfcol[:, :H_FOX].reshape(streams, dec_seq, H_FOX))
        s_gate.append(vn.reshape(streams, dec_seq, W_SGU))

    p_kf, p_ks, p_vf, p_vs = (jnp.transpose(a.reshape(depth, batch, H_FOX, D_HEAD, seq),
                                            (0, 1, 4, 2, 3)) for a in p_rows)
    s_kf, s_ks, s_vf, s_vs = (jnp.transpose(a, (0, 1, 3, 2, 4)) for a in s_rows)
    return (xp.reshape(batch, seq, d_model), xs.reshape(streams, dec_seq, d_model),
            p_kf, p_vf, jnp.stack(p_logf), p_ks, p_vs,
            s_kf, s_vf, jnp.stack(s_logf), s_ks, s_vs, jnp.stack(s_gate))
```

```python
import functools
import math

import jax
import jax.numpy as jnp
from jax import lax
from jax.experimental import pallas as pl
from jax.experimental.pallas import tpu as pltpu

F32 = jnp.float32
BF16 = jnp.bfloat16

LANES = 128
SUBLANES = 8
VMEM_BYTES_V7X = 64 * 1024 * 1024

D_HEAD = 64
H_FOX = 6
H_SB = 6
H_SGU = 4
W_ATT = H_FOX * D_HEAD
W_SGU = H_SGU * D_HEAD
N_PAIR = W_ATT // LANES
SGU_CHUNK = 128
LN_EPS = 1e-5
RMS_EPS = 1e-6
NEG_INF = -1e30
SB_CLOSED = 110.0
FOX_DEAD = 110.0
LOG2E = math.log2(math.e)
FOX_L_FLOOR = 1e-25
FOX_BOUND_SAFE = 41.0
FOX_WINDOW = 5
SB_WINDOW = 2
SB_SUFFIX_BLOCK = 128

ROW_TILE = 512
ATT_TILE = 256
MLP_FF_CHUNK = 1024

_C_Q = 0
_C_K = 2 * W_ATT
_C_V = 4 * W_ATT
_C_G = 6 * W_ATT
_C_F = _C_G + 2 * W_SGU
W_ROWS = _C_F + LANES
_R_Q = 0
_R_K = 2 * W_ATT
_R_V = 4 * W_ATT
_R_F = 6 * W_ATT
F_ROWS = 2 * SUBLANES
W_COLS = _R_F + F_ROWS


def _params(semantics, vmem_mib, flags=None):
    assert vmem_mib * 1024 * 1024 < VMEM_BYTES_V7X
    return pltpu.CompilerParams(dimension_semantics=semantics,
                                vmem_limit_bytes=vmem_mib * 1024 * 1024, flags=flags)


def _resident(shape):
    nd = len(shape)
    return pl.BlockSpec(shape, lambda *_: (0,) * nd, pipeline_mode=pl.Buffered(1))


def _split3(x):
    hi = x.astype(BF16)
    r1 = x - hi.astype(F32)
    mid = r1.astype(BF16)
    lo = (r1 - mid.astype(F32)).astype(BF16)
    return hi, mid, lo


def _dot(a, b):
    return jnp.dot(a, b, preferred_element_type=F32)


def _dot_nt(a, b):
    return lax.dot_general(a, b, (((1,), (1,)), ((), ())), preferred_element_type=F32)


def _lane_cumsum(x, seg=None):
    n = x.shape[1]
    lane = lax.broadcasted_iota(jnp.int32, x.shape, 1)
    pos = lane if seg is None else lane % seg
    step = 1
    while step < (n if seg is None else seg):
        x = x + jnp.where(pos >= step, pltpu.roll(x, step, axis=1), 0.0)
        step *= 2
    return x


def _ones_dot3(x, ones_mat, left):
    if left:
        return sum(_dot(ones_mat, p) for p in _split3(x))
    return sum(_dot(p, ones_mat) for p in _split3(x))


def _gelu_tanh(x):
    return 0.5 * x * (1.0 + jnp.tanh(0.7978845608028654 * (x + 0.044715 * (x * x * x))))


def _log_sigmoid(x):
    return jnp.minimum(x, 0.0) - jnp.log1p(jnp.exp(-jnp.abs(x)))


def _softplus(z):
    return jnp.maximum(z, 0.0) + jnp.log(1.0 + jnp.exp(-jnp.abs(z)))


def _softplus2(z2):
    return jnp.maximum(z2, 0.0) + jnp.log(1.0 + jnp.exp2(-jnp.abs(z2))) * LOG2E


def _layer_norm(x, g, b):
    mu = jnp.mean(x, axis=-1, keepdims=True)
    xc = x - mu
    var = jnp.mean(xc * xc, axis=-1, keepdims=True)
    return xc * lax.rsqrt(var + LN_EPS) * g + b


def _tri_ones(n, kind, seg=None):
    r = lax.broadcasted_iota(jnp.int32, (n, n), 0)
    c = lax.broadcasted_iota(jnp.int32, (n, n), 1)
    keep = {"lower": c <= r, "upper": r <= c, "strict_lower": r > c, "strict_upper": r < c}[kind]
    if seg is not None:
        keep = jnp.logical_and(keep, r // seg == c // seg)
    return jnp.where(keep, 1.0, 0.0).astype(BF16)


def _pick_lane(x, idx):
    lane = lax.broadcasted_iota(jnp.int32, x.shape, 1)
    return jnp.sum(jnp.where(lane == idx, x, 0.0), axis=1, keepdims=True)


def _pick_row(x, idx):
    sub = lax.broadcasted_iota(jnp.int32, x.shape, 0)
    return jnp.sum(jnp.where(sub == idx, x, 0.0), axis=0, keepdims=True)


def _gates(xb, wr_ref, gv_ref, bv_ref, u_ref, vn_ref):
    u_ref[...] = _gelu_tanh(_dot(xb, wr_ref[:, _C_G:_C_G + W_SGU]))
    vg = _dot(xb, wr_ref[:, _C_G + W_SGU:_C_G + 2 * W_SGU])
    vn_ref[...] = _layer_norm(_gelu_tanh(vg), gv_ref[...], bv_ref[...])


def _log_forget_rows(xb, wc_ref, bfr_ref):
    sub = lax.broadcasted_iota(jnp.int32, (SUBLANES, xb.shape[0]), 0)
    z = _dot_nt(wc_ref[_R_F:_R_F + F_ROWS, :], xb)[:SUBLANES] + bfr_ref[:, 0:1]
    return jnp.where(sub < H_FOX, _log_sigmoid(z), 0.0)


def _in_proj_prompt_kernel(*refs, layer, depth):
    n_in = 6 if layer == 0 else 10
    x_ref, wr_ref, wc_ref, bfr_ref, gv_ref, bv_ref = refs[:6]
    (qt_ref, k_ref, ktf_ref, kts_ref, vtf_ref, vts_ref, vtbf_ref, vtbs_ref,
     lfrow_ref, crow_ref, ccolp_ref, u_ref, vn_ref, shiftrow_ref, flags_ref,
     carry_row_ref, knmax_ref, cend_ref) = refs[n_in:]
    i = pl.program_id(1)
    tm = x_ref.shape[0]
    tk = vtbf_ref.shape[3]

    @pl.when(i == 0)
    def _():
        carry_row_ref[...] = jnp.zeros_like(carry_row_ref)
        knmax_ref[...] = jnp.zeros_like(knmax_ref)

    xb = x_ref[...].astype(BF16)

    def blocks(zt):
        return [zt[:, kb * tk:(kb + 1) * tk].astype(BF16) for kb in range(tm // tk)]

    def store_cache_rows(ref, zt):
        if layer == 0:
            for l in range(depth):
                ref[l, 0] = zt if l == 0 else jnp.zeros_like(zt)
        else:
            ref[0, 0] = zt

    scale = LOG2E / math.sqrt(D_HEAD)
    qn2_rows = []
    for kb, blk in enumerate(blocks(_dot_nt(wc_ref[_R_Q:_R_Q + 2 * W_ATT, :], xb) * scale)):
        qt_ref[0, kb] = blk
        qf = blk[:W_ATT].astype(F32)
        qn2 = jnp.sum((qf * qf).reshape(H_FOX, D_HEAD, tk), axis=1)
        qn2_rows.append(jnp.concatenate([qn2, jnp.zeros((SUBLANES - H_FOX, tk), F32)], axis=0))
    for r0, full_ref, blk_ref in ((_R_V, vtf_ref, vtbf_ref), (_R_V + W_ATT, vts_ref, vtbs_ref)):
        zt = _dot_nt(wc_ref[r0:r0 + W_ATT, :], xb)
        store_cache_rows(full_ref, zt)
        for kb, blk in enumerate(blocks(zt)):
            blk_ref[0, kb] = blk
    k_rows = _dot(xb, wr_ref[:, _C_K:_C_K + 2 * W_ATT])
    k_ref[...] = k_rows.astype(BF16)
    kt_f = k_rows[:, :W_ATT].T
    store_cache_rows(ktf_ref, kt_f)
    store_cache_rows(kts_ref, k_rows[:, W_ATT:].T)

    _gates(xb, wr_ref, gv_ref, bv_ref, u_ref, vn_ref)

    kn2 = jnp.sum((kt_f * kt_f).reshape(H_FOX, D_HEAD, tm), axis=1)
    kn2 = jnp.broadcast_to(jnp.max(kn2, axis=1, keepdims=True), (H_FOX, LANES))
    kn2 = jnp.concatenate([kn2, jnp.zeros((SUBLANES - H_FOX, LANES), F32)], axis=0)
    knmax_ref[...] = jnp.maximum(knmax_ref[...], kn2)

    lf_row = _log_forget_rows(xb, wc_ref, bfr_ref)
    lfrow_ref[...] = lf_row
    c_row = _ones_dot3(lf_row, _tri_ones(tm, "upper"), left=False) + carry_row_ref[:, 0:1]
    carry_row_ref[...] = jnp.broadcast_to(c_row[:, tm - 1:tm], carry_row_ref.shape)
    c_row = c_row * LOG2E
    kn_run = jnp.concatenate([knmax_ref[...]] * (tk // LANES), axis=1)
    for kb in range(tm // tk):
        c_blk = c_row[:, kb * tk:(kb + 1) * tk]
        crow_ref[kb] = c_blk
        bound = jnp.sqrt(qn2_rows[kb] * kn_run) * 1.02
        shiftrow_ref[kb] = c_blk - bound
        blk_id = i * (tm // tk) + kb
        cend_ref[blk_id] = jnp.broadcast_to(c_blk[:, tk - 1:tk], (SUBLANES, LANES))
        left = blk_id - FOX_WINDOW
        decay = c_blk[:, 0:1] - cend_ref[jnp.maximum(left, 0)][:, 0:1]
        live = jnp.logical_and(decay > -FOX_DEAD * LOG2E, left >= 0)
        risky = jnp.max(bound, axis=1, keepdims=True) > FOX_BOUND_SAFE
        flags_ref[kb] = jnp.broadcast_to(
            jnp.where(jnp.logical_or(live, risky), 1, 0).astype(jnp.int32), (SUBLANES, LANES))
    pad = jnp.zeros((LANES - SUBLANES, tm), F32)
    for p in range(N_PAIR):
        rows = c_row if p == 0 else jnp.concatenate([c_row[2 * p:], c_row[:2 * p]], axis=0)
        ccolp_ref[:, p * LANES:(p + 1) * LANES] = jnp.concatenate([rows, pad], axis=0).T


def _in_proj_prompt(x, w_rows, w_cols, bfr, gv, bv, *, batch, seq, layer, depth, cache_rows):
    n, d = x.shape
    tm, tk = ROW_TILE, ATT_TILE
    nt = seq // tm
    row = lambda w: pl.BlockSpec((tm, w), lambda b, i: (b * nt + i, 0))
    if layer == 0:
        t_spec = pl.BlockSpec((depth, 1, W_ATT, tm), lambda b, i: (0, b, 0, i))
        extra_in, extra_specs, aliases = (), [], {}
    else:
        t_spec = pl.BlockSpec((1, 1, W_ATT, tm), lambda b, i: (layer, b, 0, i))
        extra_in = tuple(cache_rows)
        extra_specs = [pl.BlockSpec(memory_space=pl.ANY)] * 4
        aliases = {6 + k: 2 + k for k in range(4)}
    t_shape = jax.ShapeDtypeStruct((depth, batch, W_ATT, seq), F32)
    blk_spec = lambda w: pl.BlockSpec((1, tm // tk, w, tk), lambda b, i: (b, i, 0, 0))
    blk_shape = lambda w: jax.ShapeDtypeStruct((batch, seq // tk, w, tk), BF16)
    out_shape = [blk_shape(2 * W_ATT), jax.ShapeDtypeStruct((n, 2 * W_ATT), BF16),
                 t_shape, t_shape, t_shape, t_shape, blk_shape(W_ATT), blk_shape(W_ATT),
                 jax.ShapeDtypeStruct((SUBLANES, n), F32),
                 jax.ShapeDtypeStruct((n // tk, SUBLANES, tk), F32),
                 jax.ShapeDtypeStruct((n, N_PAIR * LANES), F32),
                 jax.ShapeDtypeStruct((n, W_SGU), F32), jax.ShapeDtypeStruct((n, W_SGU), F32),
                 jax.ShapeDtypeStruct((n // tk, SUBLANES, tk), F32),
                 jax.ShapeDtypeStruct((n // tk, SUBLANES, LANES), jnp.int32)]
    out_specs = [blk_spec(2 * W_ATT), row(2 * W_ATT), t_spec, t_spec, t_spec, t_spec,
                 blk_spec(W_ATT), blk_spec(W_ATT),
                 pl.BlockSpec((SUBLANES, tm), lambda b, i: (0, b * nt + i)),
                 pl.BlockSpec((tm // tk, SUBLANES, tk), lambda b, i: (b * nt + i, 0, 0)),
                 row(N_PAIR * LANES), row(W_SGU), row(W_SGU),
                 pl.BlockSpec((tm // tk, SUBLANES, tk), lambda b, i: (b * nt + i, 0, 0)),
                 pl.BlockSpec((tm // tk, SUBLANES, LANES), lambda b, i: (b * nt + i, 0, 0))]
    return pl.pallas_call(
        functools.partial(_in_proj_prompt_kernel, layer=layer, depth=depth),
        grid=(batch, nt), name="in_proj_prompt",
        in_specs=[row(d)] + [_resident(a.shape) for a in (w_rows, w_cols, bfr, gv, bv)]
        + extra_specs,
        out_specs=out_specs, out_shape=out_shape, input_output_aliases=aliases,
        scratch_shapes=[pltpu.VMEM((SUBLANES, LANES), F32), pltpu.VMEM((SUBLANES, LANES), F32),
                        pltpu.VMEM((seq // tk, SUBLANES, LANES), F32)],
        compiler_params=_params(("parallel", "arbitrary"), 56),
    )(x, w_rows, w_cols, bfr, gv, bv, *extra_in)


def _in_proj_sample_kernel(*refs, dec_seq, layer, depth):
    n_in = 7 if layer == 0 else 11
    x_ref, wr_ref, wc_ref, bfc_ref, bfr_ref, gv_ref, bv_ref = refs[:7]
    (q_ref, kb_ref, vb_ref, kf_ref, ks_ref, vf_ref, vs_ref, lfcol_ref, ccol_ref, crow_ref,
     u_ref, vn_ref) = refs[n_in:]
    tm = x_ref.shape[0]
    xb = x_ref[...].astype(BF16)
    scale = 1.0 / math.sqrt(D_HEAD)
    q_ref[...] = (_dot(xb, wr_ref[:, _C_Q:_C_Q + 2 * W_ATT]) * scale).astype(BF16)

    def store_cache_rows(ref, z):
        for s in range(tm // dec_seq):
            for h in range(W_ATT // D_HEAD):
                blk = z[s * dec_seq:(s + 1) * dec_seq, h * D_HEAD:(h + 1) * D_HEAD]
                ref[0, s, h] = blk
                if layer == 0:
                    for l in range(1, depth):
                        ref[l, s, h] = jnp.zeros_like(blk)

    for c0, refs_fs, half_ref in ((_C_K, (kf_ref, ks_ref), kb_ref), (_C_V, (vf_ref, vs_ref), vb_ref)):
        z = _dot(xb, wr_ref[:, c0:c0 + 2 * W_ATT])
        half_ref[...] = z.astype(BF16)
        store_cache_rows(refs_fs[0], z[:, :W_ATT])
        store_cache_rows(refs_fs[1], z[:, W_ATT:])
    _gates(xb, wr_ref, gv_ref, bv_ref, u_ref, vn_ref)

    lane = lax.broadcasted_iota(jnp.int32, (tm, LANES), 1)
    z = _dot(xb, wr_ref[:, _C_F:_C_F + LANES]) + bfc_ref[...]
    lf_col = jnp.where(lane < H_FOX, _log_sigmoid(z), 0.0)
    lfcol_ref[...] = lf_col
    ccol_ref[...] = _ones_dot3(lf_col, _tri_ones(tm, "lower", dec_seq), left=True)
    lf_row = _log_forget_rows(xb, wc_ref, bfr_ref)
    c_row = _lane_cumsum(lf_row, dec_seq)
    for s in range(tm // dec_seq):
        crow_ref[s] = c_row[:, s * dec_seq:(s + 1) * dec_seq]


def _in_proj_sample(x, w_rows, w_cols, bfc, bfr, gv, bv, *, dec_seq, layer, depth, cache_rows):
    n, d = x.shape
    tm = ROW_TILE
    heads = W_ATT // D_HEAD
    row = lambda w: pl.BlockSpec((tm, w), lambda i: (i, 0))
    sds = lambda w, dt: jax.ShapeDtypeStruct((n, w), dt)
    if layer == 0:
        c_spec = pl.BlockSpec((depth, tm // dec_seq, heads, dec_seq, D_HEAD), lambda i: (0, i, 0, 0, 0))
        extra_in, extra_specs, aliases = (), [], {}
    else:
        c_spec = pl.BlockSpec((1, tm // dec_seq, heads, dec_seq, D_HEAD), lambda i: (layer, i, 0, 0, 0))
        extra_in = tuple(cache_rows)
        extra_specs = [pl.BlockSpec(memory_space=pl.ANY)] * 4
        aliases = {7 + k: 3 + k for k in range(4)}
    c_shape = jax.ShapeDtypeStruct((depth, n // dec_seq, heads, dec_seq, D_HEAD), F32)
    out_shape = [sds(2 * W_ATT, BF16)] * 3 + [c_shape] * 4 + [sds(LANES, F32)] * 2 + [
        jax.ShapeDtypeStruct((n // dec_seq, SUBLANES, dec_seq), F32), sds(W_SGU, F32), sds(W_SGU, F32)]
    out_specs = [row(2 * W_ATT)] * 3 + [c_spec] * 4 + [row(LANES)] * 2 + [
        pl.BlockSpec((tm // dec_seq, SUBLANES, dec_seq), lambda i: (i, 0, 0)), row(W_SGU), row(W_SGU)]
    return pl.pallas_call(
        functools.partial(_in_proj_sample_kernel, dec_seq=dec_seq, layer=layer, depth=depth),
        grid=(n // tm,), name="in_proj_sample",
        in_specs=[row(d)] + [_resident(a.shape) for a in (w_rows, w_cols, bfc, bfr, gv, bv)]
        + extra_specs,
        out_specs=out_specs, out_shape=out_shape, input_output_aliases=aliases,
        compiler_params=_params(("parallel",), 56),
    )(x, w_rows, w_cols, bfc, bfr, gv, bv, *extra_in)


def _split_head_rows(qt):
    sub = lax.broadcasted_iota(jnp.int32, qt.shape, 0)
    low = sub < D_HEAD
    zero = jnp.zeros_like(qt)
    return jnp.where(low, qt, zero), jnp.where(low, zero, qt)


def _fox_update_t(s, ct, pv, state, valid):
    m, l, acc = state
    if valid is not None:
        s = jnp.where(valid, s, NEG_INF)
    m_new = jnp.maximum(m, jnp.max(s, axis=0, keepdims=True) + ct)
    p = jnp.exp2(s + (ct - m_new))
    alpha = jnp.exp2(m - m_new)
    l = alpha * l + jnp.sum(p, axis=0, keepdims=True)
    acc = alpha * acc + pv(p.astype(BF16))
    return m_new, l, acc


def _sb_update_t(z, pv, suffix_ones, state, valid):
    carry, acc = state
    r = _softplus2(z)
    if valid is not None:
        r = jnp.where(valid, r, 0.0)
    r_hi = r.astype(BF16)
    r_lo = (r - r_hi.astype(F32)).astype(BF16)
    after = _dot(suffix_ones, r_hi) + _dot(suffix_ones, r_lo) + carry
    a = jnp.exp2((z - r) - after)
    if valid is not None:
        a = jnp.where(valid, a, 0.0)
    acc = acc + pv(a.astype(BF16))
    carry = carry + jnp.sum(r, axis=0, keepdims=True)
    return carry, acc


def _fox_prompt_kernel(*refs, window_blocks):
    lax.switch(jnp.minimum(pl.program_id(2), window_blocks - 1),
               [functools.partial(_fox_prompt_body, n, *refs) for n in range(1, window_blocks + 1)])


def _fox_prompt_body(nblk, flags_ref, qt_ref, k_ref, vt_ref, crow_ref, ccolp_ref, shiftrow_ref,
                     o_ref):
    tq = qt_ref.shape[3]
    tk = vt_ref.shape[3]
    hp = pl.program_id(1)
    i = pl.program_id(2)
    q2 = jnp.concatenate(_split_head_rows(qt_ref[0, 0]), axis=1)
    c_q = crow_ref[i]
    ct = jnp.concatenate([_pick_row(c_q, 2 * hp), _pick_row(c_q, 2 * hp + 1)], axis=1)

    key = lax.broadcasted_iota(jnp.int32, (tk, 2 * tq), 0)
    qry = lax.broadcasted_iota(jnp.int32, (tk, 2 * tq), 1) % tq
    causal = key <= qry

    def scores(j0, nblk):
        start = pl.multiple_of(j0 * tk, tk)
        kb = k_ref[pl.ds(start, nblk * tk), :]
        cc = ccolp_ref[pl.ds(start, nblk * tk), :]
        ck = jnp.concatenate([jnp.broadcast_to(cc[:, 0:1], (nblk * tk, tq)),
                              jnp.broadcast_to(cc[:, 1:2], (nblk * tk, tq))], axis=1)
        return _dot(kb, q2) - ck

    def pv(p, j0, nblk):
        vts = [vt_ref[0, j0 + w] for w in range(nblk)]
        outs = []
        for h in range(2):
            vth = jnp.concatenate([v[h * D_HEAD:(h + 1) * D_HEAD] for v in vts], axis=1)
            outs.append(_dot(vth, p[:, h * tq:(h + 1) * tq]))
        return jnp.concatenate(outs, axis=1)

    def pair_row(a, b):
        return jnp.concatenate([jnp.broadcast_to(a, (1, tq)), jnp.broadcast_to(b, (1, tq))], axis=1)

    sh = shiftrow_ref[i]
    shift = jnp.concatenate([_pick_row(sh, 2 * hp), _pick_row(sh, 2 * hp + 1)], axis=1)

    def reaches(j):
        return jnp.max(decay(j)) > -FOX_DEAD * LOG2E

    def decay(j):
        c_end = crow_ref[jnp.maximum(j, 0)][:, tk - 1:tk]
        return ct - pair_row(_pick_row(c_end, 2 * hp), _pick_row(c_end, 2 * hp + 1))

    def window(nblk):
        j0 = i - (nblk - 1)
        p = jnp.exp2(scores(j0, nblk) + shift)
        body_rows = (nblk - 1) * tk
        diag = jnp.where(causal, p[body_rows:], 0.0)
        p = diag if nblk == 1 else jnp.concatenate([p[:body_rows], diag], axis=0)
        return (j0 - 1, jnp.sum(p, axis=0, keepdims=True), pv(p.astype(BF16), j0, nblk))

    j, l, acc = window(nblk)

    def cond(c):
        j, go, _, _ = c
        return jnp.logical_and(j >= 0, go)

    def body(c):
        j, _, l, acc = c
        p = jnp.exp2(scores(j, 1) + shift)
        return (j - 1, reaches(j - 1), l + jnp.sum(p, axis=0, keepdims=True),
                acc + pv(p.astype(BF16), j, 1))

    def exact():
        def step(t, state):
            jj = i - 1 - t
            return _fox_update_t(scores(jj, 1), ct, lambda p: pv(p, jj, 1), state, None)

        init = (jnp.full((1, 2 * tq), NEG_INF, F32), jnp.zeros((1, 2 * tq), F32),
                jnp.zeros((D_HEAD, 2 * tq), F32))
        state = _fox_update_t(scores(i, 1), ct, lambda p: pv(p, i, 1), init, causal)
        _, l_x, acc_x = lax.fori_loop(0, i, step, state)
        return l_x, acc_x

    def beyond_window():
        _, _, l_w, acc_w = lax.while_loop(cond, body, (j, reaches(j), l, acc))
        return lax.cond(jnp.min(l_w) < FOX_L_FLOOR, exact, lambda: (l_w, acc_w))

    b = pl.program_id(0)
    uncommon = flags_ref[(b * pl.num_programs(2) + i) * N_PAIR + hp] != 0
    l, acc = lax.cond(uncommon, beyond_window, lambda: (l, acc))
    o = acc / l
    o_ref[0] = jnp.concatenate([o[:, :tq], o[:, tq:]], axis=0)


def _sb_prompt_kernel(*refs, window_blocks):
    lax.switch(jnp.minimum(pl.program_id(2), window_blocks - 1),
               [functools.partial(_sb_prompt_body, n, *refs) for n in range(1, window_blocks + 1)])


def _sb_prompt_body(nblk, qt_ref, k_ref, vt_ref, o_ref):
    tq = qt_ref.shape[3]
    tk = vt_ref.shape[3]
    i = pl.program_id(2)
    q2 = jnp.concatenate(_split_head_rows(qt_ref[0, 0]), axis=1)
    ts = SB_SUFFIX_BLOCK
    sub_ones = _tri_ones(ts, "strict_upper")

    def block(j, state, valid):
        kb = k_ref[pl.ds(pl.multiple_of(j * tk, tk), tk), :]
        vt = vt_ref[0, j]

        def pv(a):
            return jnp.concatenate([_dot(vt[:D_HEAD], a[:, :tq]), _dot(vt[D_HEAD:], a[:, tq:])],
                                   axis=1)

        return _sb_update_t(_dot(kb, q2), pv, _tri_ones(tk, "strict_upper"), state, valid)

    def still_open(carry):
        return jnp.min(carry) < SB_CLOSED * LOG2E

    key = lax.broadcasted_iota(jnp.int32, (tk, 2 * tq), 0)
    qry = lax.broadcasted_iota(jnp.int32, (tk, 2 * tq), 1) % tq
    causal = key < qry

    def window(nblk):
        j0 = i - (nblk - 1)
        z = _dot(k_ref[pl.ds(pl.multiple_of(j0 * tk, tk), nblk * tk), :], q2)
        r = _softplus2(z)
        body_rows = (nblk - 1) * tk
        diag = jnp.where(causal, r[body_rows:], 0.0)
        r = diag if nblk == 1 else jnp.concatenate([r[:body_rows], diag], axis=0)
        carry = jnp.zeros((1, 2 * tq), F32)
        n_sub = nblk * tk // ts
        a_blocks = [None] * n_sub
        for w in reversed(range(n_sub)):
            rows = slice(w * ts, (w + 1) * ts)
            r_w = r[rows]
            after = _dot(sub_ones, r_w.astype(BF16)) + carry
            a_blocks[w] = jnp.exp2((z[rows] - r_w) - after)
            carry = carry + jnp.sum(r_w, axis=0, keepdims=True)
        a = jnp.concatenate(a_blocks, axis=0)
        diag = jnp.where(causal, a[body_rows:], 0.0)
        a = (diag if nblk == 1 else jnp.concatenate([a[:body_rows], diag], axis=0)).astype(BF16)
        vts = [vt_ref[0, j0 + w] for w in range(nblk)]
        outs = []
        for h in range(2):
            vth = jnp.concatenate([v[h * D_HEAD:(h + 1) * D_HEAD] for v in vts], axis=1)
            outs.append(_dot(vth, a[:, h * tq:(h + 1) * tq]))
        return j0 - 1, still_open(carry), (carry, jnp.concatenate(outs, axis=1))

    j, go, state = window(nblk)

    def cond(c):
        j, go, _ = c
        return jnp.logical_and(j >= 0, go)

    def body(c):
        j, _, st = c
        st = block(j, st, None)
        return j - 1, still_open(st[0]), st

    _, _, (_, acc) = lax.while_loop(cond, body, (j, go, state))
    o_ref[0] = jnp.concatenate([acc[:, :tq], acc[:, tq:]], axis=0)


def _prompt_attention(qt, k, vtb, group, *, batch, seq, crow=None, ccolp=None, shiftrow=None,
                      flags=None):
    tq = tk = ATT_TILE
    nq = seq // tq
    col0 = group * N_PAIR
    qt_spec = pl.BlockSpec((1, 1, LANES, tq), lambda b, hp, i, *_: (b, i, col0 + hp, 0))
    k_spec = pl.BlockSpec((seq, LANES), lambda b, hp, i, *_: (b, col0 + hp))
    vt_spec = pl.BlockSpec((1, seq // tk, LANES, tk), lambda b, hp, i, *_: (b, 0, hp, 0))
    out_spec = pl.BlockSpec((1, LANES, tq), lambda b, hp, i, *_: (b * nq + i, hp, 0))
    out_shape = jax.ShapeDtypeStruct((batch * nq, W_ATT, tq), F32)
    params = _params(("parallel", "parallel", "arbitrary"), 40)
    grid = (batch, N_PAIR, nq)
    if group == 1:
        return pl.pallas_call(
            functools.partial(_sb_prompt_kernel, window_blocks=SB_WINDOW), grid=grid,
            name="sb_prompt",
            in_specs=[qt_spec, k_spec, vt_spec], out_specs=out_spec, out_shape=out_shape,
            compiler_params=params)(qt, k, vtb)
    crow_spec = pl.BlockSpec((seq // tk, SUBLANES, tk), lambda b, hp, i, *_: (b, 0, 0))
    ccolp_spec = pl.BlockSpec((seq, LANES), lambda b, hp, i, *_: (b, hp))
    grid_spec = pltpu.PrefetchScalarGridSpec(
        num_scalar_prefetch=1, grid=grid,
        in_specs=[qt_spec, k_spec, vt_spec, crow_spec, ccolp_spec, crow_spec],
        out_specs=out_spec)
    return pl.pallas_call(
        functools.partial(_fox_prompt_kernel, window_blocks=FOX_WINDOW), grid_spec=grid_spec,
        name="fox_prompt", out_shape=out_shape,
        compiler_params=params)(flags, qt, k, vtb, crow, ccolp, shiftrow)


def _suffix_sum_kernel(x_ref, o_ref):
    rows, width = x_ref.shape
    ones = _tri_ones(LANES, "strict_lower")
    carry = jnp.zeros((rows, 1), F32)
    for kb in reversed(range(width // LANES)):
        xb = x_ref[:, kb * LANES:(kb + 1) * LANES]
        o_ref[:, kb * LANES:(kb + 1) * LANES] = _ones_dot3(xb, ones, left=False) + carry
        carry = carry + jnp.sum(xb, axis=1, keepdims=True)


def _suffix_sum(x):
    return pl.pallas_call(
        _suffix_sum_kernel, name="cache_suffix_sum",
        out_shape=jax.ShapeDtypeStruct(x.shape, F32),
        compiler_params=_params(None, 32))(x)


def _split_heads(q2):
    lane = lax.broadcasted_iota(jnp.int32, q2.shape, 1)
    low = lane < D_HEAD
    zero = jnp.zeros_like(q2)
    return jnp.where(low, q2, zero), jnp.where(low, zero, q2)


def _stack_heads(q2):
    return jnp.concatenate(_split_heads(q2), axis=0)


def _unstack_heads(o):
    half = o.shape[0] // 2
    lane = lax.broadcasted_iota(jnp.int32, (half, o.shape[1]), 1)
    return jnp.where(lane < D_HEAD, o[:half], o[half:])


def _sb_update(z, pv, suffix_ones, state, valid):
    carry, acc = state
    r = _softplus(z)
    if valid is not None:
        r = jnp.where(valid, r, 0.0)
    r_hi = r.astype(BF16)
    r_lo = (r - r_hi.astype(F32)).astype(BF16)
    after = _dot(r_hi, suffix_ones) + _dot(r_lo, suffix_ones) + carry
    a = jnp.exp((z - r) - after)
    if valid is not None:
        a = jnp.where(valid, a, 0.0)
    acc = acc + pv(a.astype(BF16))
    carry = carry + jnp.sum(r, axis=1, keepdims=True)
    return carry, acc


def _sample_attn_kernel(qf_ref, kfn_ref, vfn_ref, kfc_ref, vfc_ref, drow_ref, cnrow_ref, ccol_ref,
                        qs_ref, ksn_ref, vsn_ref, ksc_last_ref, vsc_last_ref, ksc_hbm_ref,
                        vsc_hbm_ref, of_ref, os_ref,
                        carry_ref, acc_ref, kbuf_ref, vbuf_ref, sem_ref, *, blk, layer):
    b = pl.program_id(0)
    hp = pl.program_id(1)
    tq = qf_ref.shape[0]
    past = kfc_ref.shape[3]
    row = lax.broadcasted_iota(jnp.int32, (2 * tq, tq), 0) % tq
    col = lax.broadcasted_iota(jnp.int32, (2 * tq, tq), 1)

    def per_head(fn):
        return jnp.concatenate([fn(0), fn(1)], axis=0)

    q2 = _stack_heads(qf_ref[...])
    kct = kfc_ref[0, 0].astype(BF16)
    vct = vfc_ref[0, 0].astype(BF16)
    cc = ccol_ref[...]
    cn = cnrow_ref[0]
    cq = per_head(lambda h: _pick_lane(cc, 2 * hp + h))
    qk_c = _dot(q2, kct)
    qk_n = _dot_nt(q2, kfn_ref[...])
    s_c = per_head(lambda h: qk_c[h * tq:(h + 1) * tq]
                   + _pick_row(drow_ref[0, 2 * hp + h], b % SUBLANES))
    s_n = per_head(lambda h: qk_n[h * tq:(h + 1) * tq] - _pick_row(cn, 2 * hp + h))
    s_n = jnp.where(col <= row, s_n, NEG_INF)
    m = jnp.maximum(jnp.max(s_c, axis=1, keepdims=True), jnp.max(s_n, axis=1, keepdims=True)) + cq
    shift = cq - m
    p_c = jnp.exp(s_c + shift)
    p_n = jnp.exp(s_n + shift)
    l = jnp.sum(p_c, axis=1, keepdims=True) + jnp.sum(p_n, axis=1, keepdims=True)
    o = _dot_nt(p_c.astype(BF16), vct) + _dot(p_n.astype(BF16), vfn_ref[...])
    of_ref[...] = _unstack_heads(o / l)

    q2 = _stack_heads(qs_ref[...])
    vn = vsn_ref[...]
    suffix_ones = _tri_ones(blk, "strict_lower")

    def cache_block(kt_f32, vt_f32, state):
        kt = kt_f32.astype(BF16)
        vt = vt_f32.astype(BF16)
        return _sb_update(_dot(q2, kt), lambda a: _dot_nt(a, vt), suffix_ones, state, None)

    n_blocks = past // blk
    state = (jnp.zeros((2 * tq, 1), F32), jnp.zeros((2 * tq, LANES), F32))
    state = _sb_update(_dot_nt(q2, ksn_ref[...]), lambda a: _dot(a, vn),
                       _tri_ones(tq, "strict_lower"), state, col < row)
    carry_ref[...], acc_ref[...] = cache_block(ksc_last_ref[0, 0], vsc_last_ref[0, 0], state)

    def still_open():
        return jnp.min(carry_ref[...]) < SB_CLOSED

    def fetch(src_ref, jb, buf_ref, sem):
        rows = pl.ds(pl.multiple_of(hp * LANES, LANES), LANES)
        return pltpu.make_async_copy(src_ref.at[layer, b, rows, pl.ds(jb * blk, blk)], buf_ref, sem)

    @pl.when(still_open())
    def _():
        for jb in reversed(range(n_blocks - 1)):
            @pl.when(still_open())
            def _():
                copies = (fetch(ksc_hbm_ref, jb, kbuf_ref, sem_ref.at[0]),
                          fetch(vsc_hbm_ref, jb, vbuf_ref, sem_ref.at[1]))
                for cp in copies:
                    cp.start()
                for cp in copies:
                    cp.wait()
                carry_ref[...], acc_ref[...] = cache_block(
                    kbuf_ref[...], vbuf_ref[...], (carry_ref[...], acc_ref[...]))

    os_ref[...] = _unstack_heads(acc_ref[...])


def _sample_attention(q, kb, vb, cache_fk, cache_fv, drow, cnrow, ccol, cache_sk, cache_sv,
                      *, layer, streams, dec_seq):
    past = cache_fk.shape[3]
    new_spec = lambda g: pl.BlockSpec((dec_seq, LANES), lambda b, hp: (b, g * N_PAIR + hp))
    cache_spec = pl.BlockSpec((1, 1, LANES, past), lambda b, hp: (layer, b, hp, 0))
    drow_spec = pl.BlockSpec((1, H_FOX, SUBLANES, past), lambda b, hp: (layer, 0, b // SUBLANES, 0))
    cnrow_spec = pl.BlockSpec((1, SUBLANES, dec_seq), lambda b, hp: (b, 0, 0))
    ccol_spec = pl.BlockSpec((dec_seq, LANES), lambda b, hp: (b, 0))
    out_spec = pl.BlockSpec((dec_seq, LANES), lambda b, hp: (b, hp))
    out_shape = [jax.ShapeDtypeStruct((streams * dec_seq, W_ATT), F32)] * 2
    blk = ATT_TILE
    newest_spec = pl.BlockSpec((1, 1, LANES, blk), lambda b, hp: (layer, b, hp, past // blk - 1))
    hbm_spec = pl.BlockSpec(memory_space=pl.ANY)
    return pl.pallas_call(
        functools.partial(_sample_attn_kernel, blk=blk, layer=layer),
        grid=(streams, N_PAIR), name="sample_attention",
        in_specs=[new_spec(0), new_spec(0), new_spec(0), cache_spec, cache_spec, drow_spec,
                  cnrow_spec, ccol_spec, new_spec(1), new_spec(1), new_spec(1), newest_spec,
                  newest_spec, hbm_spec, hbm_spec],
        out_specs=[out_spec, out_spec], out_shape=out_shape,
        scratch_shapes=[pltpu.VMEM((2 * dec_seq, 1), F32), pltpu.VMEM((2 * dec_seq, LANES), F32),
                        pltpu.VMEM((LANES, blk), F32), pltpu.VMEM((LANES, blk), F32),
                        pltpu.SemaphoreType.DMA((2,))],
        compiler_params=_params(("parallel", "parallel"), 48),
    )(q, kb, vb, cache_fk, cache_fv, drow, cnrow, ccol, q, kb, vb, cache_sk, cache_sv,
      cache_sk, cache_sv)


def _merge_tile(x_ref, of_ref, os_ref, u_ref, vn_ref, ws_ref, bs_ref, gmix_ref, wout_ref,
                g_ref, b_ref, *, alpha):
    tm = x_ref.shape[0]
    r = lax.broadcasted_iota(jnp.int32, (SGU_CHUNK, SGU_CHUNK), 0)
    c = lax.broadcasted_iota(jnp.int32, (SGU_CHUNK, SGU_CHUNK), 1)
    tril = r >= c
    head_mean = jnp.where(r // D_HEAD == c // D_HEAD, 1.0 / D_HEAD, 0.0).astype(BF16)
    lane = lax.broadcasted_iota(jnp.int32, (SGU_CHUNK, LANES), 1)

    gate_pieces = []
    for p in range(W_SGU // LANES):
        w0 = jnp.where(tril, ws_ref[2 * p], 0.0).astype(BF16)
        w1 = jnp.where(tril, ws_ref[2 * p + 1], 0.0).astype(BF16)
        bias = bs_ref[:, p * LANES:(p + 1) * LANES]
        chunks = []
        for ch in range(tm // SGU_CHUNK):
            rows = slice(ch * SGU_CHUNK, (ch + 1) * SGU_CHUNK)
            vp = vn_ref[rows, p * LANES:(p + 1) * LANES].astype(BF16)
            s = jnp.where(lane < D_HEAD, _dot(w0, vp), _dot(w1, vp)) + bias
            chunks.append(u_ref[rows, p * LANES:(p + 1) * LANES] * s)
        gate_pieces.append(jnp.concatenate(chunks, axis=0))

    def attention_pieces(ref):
        if ref.ndim == 2:
            return [ref[:, k * LANES:(k + 1) * LANES] for k in range(N_PAIR)]
        return [jnp.concatenate([ref[qb, k * LANES:(k + 1) * LANES, :].T
                                 for qb in range(ref.shape[0])], axis=0) for k in range(N_PAIR)]

    pieces = attention_pieces(of_ref) + gate_pieces + attention_pieces(os_ref)
    normed = []
    for k, o in enumerate(pieces):
        ms = _dot((o * o).astype(BF16), head_mean)
        gain = gmix_ref[:, k * LANES:(k + 1) * LANES]
        normed.append((o * lax.rsqrt(ms + RMS_EPS) * gain).astype(BF16))
    mix = _dot(jnp.concatenate(normed, axis=1), wout_ref[...])
    return _layer_norm(alpha * x_ref[...] + mix, g_ref[...], b_ref[...])


def _mlp_tile(x, wup_ref, wdn_ref, g_ref, b_ref, *, alpha, ff_chunk):
    xb = x.astype(BF16)
    acc = alpha * x
    for c0 in range(0, wup_ref.shape[1], ff_chunk):
        h = jnp.maximum(_dot(xb, wup_ref[:, c0:c0 + ff_chunk]), 0.0)
        acc = acc + _dot((h * h).astype(BF16), wdn_ref[c0:c0 + ff_chunk, :])
    return _layer_norm(acc, g_ref[...], b_ref[...])


def _merge_mlp_kernel(x_ref, of_ref, os_ref, u_ref, vn_ref, ws_ref, bs_ref, gmix_ref, wout_ref,
                      g1_ref, b1_ref, wup_ref, wdn_ref, g2_ref, b2_ref, o_ref, *, alpha, ff_chunk):
    x1 = _merge_tile(x_ref, of_ref, os_ref, u_ref, vn_ref, ws_ref, bs_ref, gmix_ref, wout_ref,
                     g1_ref, b1_ref, alpha=alpha)
    o_ref[...] = _mlp_tile(x1, wup_ref, wdn_ref, g2_ref, b2_ref, alpha=alpha, ff_chunk=ff_chunk)


def _merge_mlp(x, of, os_, u, vn, ws, bs, gmix, wout, g1, b1, wup, wdn, g2, b2, *, alpha, name):
    n, d = x.shape
    tm = ROW_TILE
    row = lambda w: pl.BlockSpec((tm, w), lambda i: (i, 0))
    if of.ndim == 2:
        att = row(W_ATT)
    else:
        tq = of.shape[2]
        att = pl.BlockSpec((tm // tq, W_ATT, tq), lambda i: (i, 0, 0))
    weights = (ws, bs, gmix, wout, g1, b1, wup, wdn, g2, b2)
    return pl.pallas_call(
        functools.partial(_merge_mlp_kernel, alpha=alpha, ff_chunk=MLP_FF_CHUNK), grid=(n // tm,),
        name=name,
        in_specs=[row(d), att, att, row(W_SGU), row(W_SGU)]
        + [_resident(a.shape) for a in weights],
        out_specs=row(d), out_shape=jax.ShapeDtypeStruct((n, d), F32),
        compiler_params=_params(("parallel",), 60),
    )(x, of, os_, u, vn, *weights)


def _pack_w_in(w, b_f):
    sizes = [W_ATT, W_ATT, W_ATT, H_FOX, W_SGU, W_SGU, W_ATT, W_ATT, W_ATT]
    offs = [0]
    for s in sizes:
        offs.append(offs[-1] + s)
    q_f, k_f, v_f, f_lg, u_g, v_g, q_s, k_s, v_s = (w[:, offs[i]:offs[i + 1]] for i in range(9))
    pad_cols = lambda a, width: jnp.pad(a, ((0, 0), (0, width - a.shape[1])))
    w_rows = jnp.concatenate([q_f, q_s, k_f, k_s, v_f, v_s, u_g, v_g, pad_cols(f_lg, LANES)],
                             axis=1).astype(BF16)
    w_cols = jnp.concatenate([q_f, q_s, k_f, k_s, v_f, v_s, pad_cols(f_lg, F_ROWS)],
                             axis=1).T.astype(BF16)
    bfc = jnp.pad(b_f, (0, LANES - H_FOX)).reshape(1, LANES)
    bfr = jnp.broadcast_to(jnp.pad(b_f, (0, SUBLANES - H_FOX)).reshape(SUBLANES, 1),
                           (SUBLANES, LANES))
    return w_rows, w_cols, bfc, bfr


def kernel(x_prompt, x_sample, cache_fox_k, cache_fox_v, cache_fox_logf, cache_sb_k, cache_sb_v,
           w_in, b_f, g_v, b_v, w_s, b_s, g_mix, w_out, ln1_g, ln1_b, w_up, w_down, ln2_g, ln2_b):
    depth = w_in.shape[0]
    batch, seq, d_model = x_prompt.shape
    streams, dec_seq, _ = x_sample.shape
    past = cache_fox_k.shape[2]
    n_prompt = batch * seq
    n_sample = streams * dec_seq
    alpha = (2 * depth) ** 0.25
    assert seq % ROW_TILE == 0 and n_sample % ROW_TILE == 0 and ROW_TILE % dec_seq == 0
    assert dec_seq == D_HEAD and 2 * dec_seq == SGU_CHUNK and past % ATT_TILE == 0
    assert streams % SUBLANES == 0

    xp = x_prompt.reshape(n_prompt, d_model)
    xs = x_sample.reshape(n_sample, d_model)
    row2 = lambda a: a.reshape(1, -1)
    cache_t = lambda c: jnp.transpose(c, (0, 1, 3, 4, 2)).reshape(depth, streams, W_ATT, past)
    ck_f, cv_f, ck_s, cv_s = (cache_t(c) for c in (cache_fox_k, cache_fox_v, cache_sb_k, cache_sb_v))
    lf_cache = jnp.transpose(cache_fox_logf.astype(F32), (0, 3, 1, 2))
    drow = _suffix_sum(lf_cache.reshape(depth * H_FOX * streams, past)).reshape(
        depth, H_FOX, streams, past)

    p_rows = s_rows = None
    p_logf, s_logf, s_gate = [], [], []
    for l in range(depth):
        w_rows, w_cols, bfc, bfr = _pack_w_in(w_in[l], b_f[l])
        gv, bv = row2(g_v[l]), row2(b_v[l])
        wout = w_out[l].astype(BF16)
        wup, wdn = w_up[l].astype(BF16), w_down[l].astype(BF16)
        gmix, g1, b1, g2, b2 = (row2(a[l]) for a in (g_mix, ln1_g, ln1_b, ln2_g, ln2_b))

        w_l = w_s[l]
        half = w_l[:, :dec_seq, :dec_seq]
        zeros = jnp.zeros_like(half)
        w_blockdiag = jnp.concatenate([jnp.concatenate([half, zeros], axis=2),
                                       jnp.concatenate([zeros, half], axis=2)], axis=1)
        bias_full = jnp.repeat(b_s[l].T, D_HEAD, axis=1)
        bias_half = jnp.concatenate([bias_full[:dec_seq], bias_full[:dec_seq]], axis=0)

        (qt, kb, *p_rows, vtbf, vtbs, lfrow, crow, ccolp, u, vn, shiftrow,
         head_flags) = _in_proj_prompt(
            xp, w_rows, w_cols, bfr, gv, bv, batch=batch, seq=seq, layer=l, depth=depth,
            cache_rows=p_rows)
        pair_flags = jnp.max(head_flags[:, :H_FOX, 0].reshape(-1, N_PAIR, 2), axis=2).reshape(-1)
        of = _prompt_attention(qt, kb, vtbf, 0, batch=batch, seq=seq, crow=crow, ccolp=ccolp,
                               shiftrow=shiftrow, flags=pair_flags)
        os_ = _prompt_attention(qt, kb, vtbs, 1, batch=batch, seq=seq)
        xp = _merge_mlp(xp, of, os_, u, vn, w_l, bias_full, gmix, wout, g1, b1, wup, wdn, g2, b2,
                        alpha=alpha, name="merge_mlp_prompt")
        p_logf.append(jnp.transpose(lfrow[:H_FOX].reshape(H_FOX, batch, seq), (1, 2, 0)))

        (q, kb, vb, *s_rows, lfcol, ccol, cnrow, u, vn) = _in_proj_sample(
            xs, w_rows, w_cols, bfc, bfr, gv, bv, dec_seq=dec_seq, layer=l, depth=depth,
            cache_rows=s_rows)
        of, os_ = _sample_attention(q, kb, vb, ck_f, cv_f, drow, cnrow, ccol, ck_s, cv_s,
                                    layer=l, streams=streams, dec_seq=dec_seq)
        xs = _merge_mlp(xs, of, os_, u, vn, w_blockdiag, bias_half, gmix, wout, g1, b1, wup, wdn,
                        g2, b2, alpha=alpha, name="merge_mlp_sample")
        s_logf.append(lfcol[:, :H_FOX].reshape(streams, dec_seq, H_FOX))
        s_gate.append(vn.reshape(streams, dec_seq, W_SGU))

    p_kf, p_ks, p_vf, p_vs = (jnp.transpose(a.reshape(depth, batch, H_FOX, D_HEAD, seq),
                                            (0, 1, 4, 2, 3)) for a in p_rows)
    s_kf, s_ks, s_vf, s_vs = (jnp.transpose(a, (0, 1, 3, 2, 4)) for a in s_rows)
    return (xp.reshape(batch, seq, d_model), xs.reshape(streams, dec_seq, d_model),
            p_kf, p_vf, jnp.stack(p_logf), p_ks, p_vs,
            s_kf, s_vf, jnp.stack(s_logf), s_ks, s_vs, jnp.stack(s_gate))
```

```python
import functools
import math

import jax
import jax.numpy as jnp
from jax import lax
from jax.experimental import pallas as pl
from jax.experimental.pallas import tpu as pltpu

F32 = jnp.float32
BF16 = jnp.bfloat16

LANES = 128
SUBLANES = 8
VMEM_BYTES_V7X = 64 * 1024 * 1024

D_HEAD = 64
H_FOX = 6
H_SB = 6
H_SGU = 4
W_ATT = H_FOX * D_HEAD
W_SGU = H_SGU * D_HEAD
N_PAIR = W_ATT // LANES
SGU_CHUNK = 128
LN_EPS = 1e-5
RMS_EPS = 1e-6
NEG_INF = -1e30
SB_CLOSED = 110.0
FOX_DEAD = 110.0
LOG2E = math.log2(math.e)
FOX_L_FLOOR = 1e-25
FOX_BOUND_SAFE = 41.0
FOX_WINDOW = 5
SB_WINDOW = 2
SB_SUFFIX_BLOCK = 128

ROW_TILE = 512
ATT_TILE = 256
MLP_FF_CHUNK = 1024

_C_Q = 0
_C_K = 2 * W_ATT
_C_V = 4 * W_ATT
_C_G = 6 * W_ATT
_C_F = _C_G + 2 * W_SGU
W_ROWS = _C_F + LANES
_R_Q = 0
_R_K = 2 * W_ATT
_R_V = 4 * W_ATT
_R_F = 6 * W_ATT
F_ROWS = 2 * SUBLANES
W_COLS = _R_F + F_ROWS


def _params(semantics, vmem_mib, flags=None):
    assert vmem_mib * 1024 * 1024 < VMEM_BYTES_V7X
    return pltpu.CompilerParams(dimension_semantics=semantics,
                                vmem_limit_bytes=vmem_mib * 1024 * 1024, flags=flags)


def _resident(shape):
    nd = len(shape)
    return pl.BlockSpec(shape, lambda *_: (0,) * nd, pipeline_mode=pl.Buffered(1))


def _split3(x):
    hi = x.astype(BF16)
    r1 = x - hi.astype(F32)
    mid = r1.astype(BF16)
    lo = (r1 - mid.astype(F32)).astype(BF16)
    return hi, mid, lo


def _dot(a, b):
    return jnp.dot(a, b, preferred_element_type=F32)


def _dot_nt(a, b):
    return lax.dot_general(a, b, (((1,), (1,)), ((), ())), preferred_element_type=F32)


def _lane_cumsum(x, seg=None):
    n = x.shape[1]
    lane = lax.broadcasted_iota(jnp.int32, x.shape, 1)
    pos = lane if seg is None else lane % seg
    step = 1
    while step < (n if seg is None else seg):
        x = x + jnp.where(pos >= step, pltpu.roll(x, step, axis=1), 0.0)
        step *= 2
    return x


def _ones_dot3(x, ones_mat, left):
    if left:
        return sum(_dot(ones_mat, p) for p in _split3(x))
    return sum(_dot(p, ones_mat) for p in _split3(x))


def _gelu_tanh(x):
    return 0.5 * x * (1.0 + jnp.tanh(0.7978845608028654 * (x + 0.044715 * (x * x * x))))


def _log_sigmoid(x):
    return jnp.minimum(x, 0.0) - jnp.log1p(jnp.exp(-jnp.abs(x)))


def _softplus(z):
    return jnp.maximum(z, 0.0) + jnp.log(1.0 + jnp.exp(-jnp.abs(z)))


def _softplus2(z2):
    return jnp.maximum(z2, 0.0) + jnp.log(1.0 + jnp.exp2(-jnp.abs(z2))) * LOG2E


def _layer_norm(x, g, b):
    mu = jnp.mean(x, axis=-1, keepdims=True)
    xc = x - mu
    var = jnp.mean(xc * xc, axis=-1, keepdims=True)
    return xc * lax.rsqrt(var + LN_EPS) * g + b


def _tri_ones(n, kind, seg=None):
    r = lax.broadcasted_iota(jnp.int32, (n, n), 0)
    c = lax.broadcasted_iota(jnp.int32, (n, n), 1)
    keep = {"lower": c <= r, "upper": r <= c, "strict_lower": r > c, "strict_upper": r < c}[kind]
    if seg is not None:
        keep = jnp.logical_and(keep, r // seg == c // seg)
    return jnp.where(keep, 1.0, 0.0).astype(BF16)


def _pick_lane(x, idx):
    lane = lax.broadcasted_iota(jnp.int32, x.shape, 1)
    return jnp.sum(jnp.where(lane == idx, x, 0.0), axis=1, keepdims=True)


def _pick_row(x, idx):
    sub = lax.broadcasted_iota(jnp.int32, x.shape, 0)
    return jnp.sum(jnp.where(sub == idx, x, 0.0), axis=0, keepdims=True)


def _gates(xb, wr_ref, gv_ref, bv_ref, u_ref, vn_ref):
    u_ref[...] = _gelu_tanh(_dot(xb, wr_ref[:, _C_G:_C_G + W_SGU]))
    vg = _dot(xb, wr_ref[:, _C_G + W_SGU:_C_G + 2 * W_SGU])
    vn_ref[...] = _layer_norm(_gelu_tanh(vg), gv_ref[...], bv_ref[...])


def _log_forget_rows(xb, wc_ref, bfr_ref):
    sub = lax.broadcasted_iota(jnp.int32, (SUBLANES, xb.shape[0]), 0)
    z = _dot_nt(wc_ref[_R_F:_R_F + F_ROWS, :], xb)[:SUBLANES] + bfr_ref[:, 0:1]
    return jnp.where(sub < H_FOX, _log_sigmoid(z), 0.0)


def _in_proj_prompt_kernel(*refs, layer, depth):
    n_in = 6 if layer == 0 else 10
    x_ref, wr_ref, wc_ref, bfr_ref, gv_ref, bv_ref = refs[:6]
    (qt_ref, k_ref, ktf_ref, kts_ref, vtf_ref, vts_ref, vtbf_ref, vtbs_ref,
     lfrow_ref, crow_ref, ccolp_ref, u_ref, vn_ref, shiftrow_ref, flags_ref,
     carry_row_ref, knmax_ref, cend_ref) = refs[n_in:]
    i = pl.program_id(1)
    tm = x_ref.shape[0]
    tk = vtbf_ref.shape[3]

    @pl.when(i == 0)
    def _():
        carry_row_ref[...] = jnp.zeros_like(carry_row_ref)
        knmax_ref[...] = jnp.zeros_like(knmax_ref)

    xb = x_ref[...].astype(BF16)

    def blocks(zt):
        return [zt[:, kb * tk:(kb + 1) * tk].astype(BF16) for kb in range(tm // tk)]

    def store_cache_rows(ref, zt):
        if layer == 0:
            for l in range(depth):
                ref[l, 0] = zt if l == 0 else jnp.zeros_like(zt)
        else:
            ref[0, 0] = zt

    scale = LOG2E / math.sqrt(D_HEAD)
    qn2_rows = []
    for kb, blk in enumerate(blocks(_dot_nt(wc_ref[_R_Q:_R_Q + 2 * W_ATT, :], xb) * scale)):
        qt_ref[0, kb] = blk
        qf = blk[:W_ATT].astype(F32)
        qn2 = jnp.sum((qf * qf).reshape(H_FOX, D_HEAD, tk), axis=1)
        qn2_rows.append(jnp.concatenate([qn2, jnp.zeros((SUBLANES - H_FOX, tk), F32)], axis=0))
    for r0, full_ref, blk_ref in ((_R_V, vtf_ref, vtbf_ref), (_R_V + W_ATT, vts_ref, vtbs_ref)):
        zt = _dot_nt(wc_ref[r0:r0 + W_ATT, :], xb)
        store_cache_rows(full_ref, zt)
        for kb, blk in enumerate(blocks(zt)):
            blk_ref[0, kb] = blk
    k_rows = _dot(xb, wr_ref[:, _C_K:_C_K + 2 * W_ATT])
    k_ref[...] = k_rows.astype(BF16)
    kt_f = k_rows[:, :W_ATT].T
    store_cache_rows(ktf_ref, kt_f)
    store_cache_rows(kts_ref, k_rows[:, W_ATT:].T)

    _gates(xb, wr_ref, gv_ref, bv_ref, u_ref, vn_ref)

    kn2 = jnp.sum((kt_f * kt_f).reshape(H_FOX, D_HEAD, tm), axis=1)
    kn2 = jnp.broadcast_to(jnp.max(kn2, axis=1, keepdims=True), (H_FOX, LANES))
    kn2 = jnp.concatenate([kn2, jnp.zeros((SUBLANES - H_FOX, LANES), F32)], axis=0)
    knmax_ref[...] = jnp.maximum(knmax_ref[...], kn2)

    lf_row = _log_forget_rows(xb, wc_ref, bfr_ref)
    lfrow_ref[...] = lf_row
    c_row = _ones_dot3(lf_row, _tri_ones(tm, "upper"), left=False) + carry_row_ref[:, 0:1]
    carry_row_ref[...] = jnp.broadcast_to(c_row[:, tm - 1:tm], carry_row_ref.shape)
    c_row = c_row * LOG2E
    kn_run = jnp.concatenate([knmax_ref[...]] * (tk // LANES), axis=1)
    for kb in range(tm // tk):
        c_blk = c_row[:, kb * tk:(kb + 1) * tk]
        crow_ref[kb] = c_blk
        bound = jnp.sqrt(qn2_rows[kb] * kn_run) * 1.02
        shiftrow_ref[kb] = c_blk - bound
        blk_id = i * (tm // tk) + kb
        cend_ref[blk_id] = jnp.broadcast_to(c_blk[:, tk - 1:tk], (SUBLANES, LANES))
        left = blk_id - FOX_WINDOW
        decay = c_blk[:, 0:1] - cend_ref[jnp.maximum(left, 0)][:, 0:1]
        live = jnp.logical_and(decay > -FOX_DEAD * LOG2E, left >= 0)
        risky = jnp.max(bound, axis=1, keepdims=True) > FOX_BOUND_SAFE
        flags_ref[kb] = jnp.broadcast_to(
            jnp.where(jnp.logical_or(live, risky), 1, 0).astype(jnp.int32), (SUBLANES, LANES))
    pad = jnp.zeros((LANES - SUBLANES, tm), F32)
    for p in range(N_PAIR):
        rows = c_row if p == 0 else jnp.concatenate([c_row[2 * p:], c_row[:2 * p]], axis=0)
        ccolp_ref[:, p * LANES:(p + 1) * LANES] = jnp.concatenate([rows, pad], axis=0).T


def _in_proj_prompt(x, w_rows, w_cols, bfr, gv, bv, *, batch, seq, layer, depth, cache_rows):
    n, d = x.shape
    tm, tk = ROW_TILE, ATT_TILE
    nt = seq // tm
    row = lambda w: pl.BlockSpec((tm, w), lambda b, i: (b * nt + i, 0))
    if layer == 0:
        t_spec = pl.BlockSpec((depth, 1, W_ATT, tm), lambda b, i: (0, b, 0, i))
        extra_in, extra_specs, aliases = (), [], {}
    else:
        t_spec = pl.BlockSpec((1, 1, W_ATT, tm), lambda b, i: (layer, b, 0, i))
        extra_in = tuple(cache_rows)
        extra_specs = [pl.BlockSpec(memory_space=pl.ANY)] * 4
        aliases = {6 + k: 2 + k for k in range(4)}
    t_shape = jax.ShapeDtypeStruct((depth, batch, W_ATT, seq), F32)
    blk_spec = lambda w: pl.BlockSpec((1, tm // tk, w, tk), lambda b, i: (b, i, 0, 0))
    blk_shape = lambda w: jax.ShapeDtypeStruct((batch, seq // tk, w, tk), BF16)
    out_shape = [blk_shape(2 * W_ATT), jax.ShapeDtypeStruct((n, 2 * W_ATT), BF16),
                 t_shape, t_shape, t_shape, t_shape, blk_shape(W_ATT), blk_shape(W_ATT),
                 jax.ShapeDtypeStruct((SUBLANES, n), F32),
                 jax.ShapeDtypeStruct((n // tk, SUBLANES, tk), F32),
                 jax.ShapeDtypeStruct((n, N_PAIR * LANES), F32),
                 jax.ShapeDtypeStruct((n, W_SGU), F32), jax.ShapeDtypeStruct((n, W_SGU), F32),
                 jax.ShapeDtypeStruct((n // tk, SUBLANES, tk), F32),
                 jax.ShapeDtypeStruct((n // tk, SUBLANES, LANES), jnp.int32)]
    out_specs = [blk_spec(2 * W_ATT), row(2 * W_ATT), t_spec, t_spec, t_spec, t_spec,
                 blk_spec(W_ATT), blk_spec(W_ATT),
                 pl.BlockSpec((SUBLANES, tm), lambda b, i: (0, b * nt + i)),
                 pl.BlockSpec((tm // tk, SUBLANES, tk), lambda b, i: (b * nt + i, 0, 0)),
                 row(N_PAIR * LANES), row(W_SGU), row(W_SGU),
                 pl.BlockSpec((tm // tk, SUBLANES, tk), lambda b, i: (b * nt + i, 0, 0)),
                 pl.BlockSpec((tm // tk, SUBLANES, LANES), lambda b, i: (b * nt + i, 0, 0))]
    return pl.pallas_call(
        functools.partial(_in_proj_prompt_kernel, layer=layer, depth=depth),
        grid=(batch, nt), name="in_proj_prompt",
        in_specs=[row(d)] + [_resident(a.shape) for a in (w_rows, w_cols, bfr, gv, bv)]
        + extra_specs,
        out_specs=out_specs, out_shape=out_shape, input_output_aliases=aliases,
        scratch_shapes=[pltpu.VMEM((SUBLANES, LANES), F32), pltpu.VMEM((SUBLANES, LANES), F32),
                        pltpu.VMEM((seq // tk, SUBLANES, LANES), F32)],
        compiler_params=_params(("parallel", "arbitrary"), 56),
    )(x, w_rows, w_cols, bfr, gv, bv, *extra_in)


def _in_proj_sample_kernel(*refs, dec_seq, layer, depth):
    n_in = 7 if layer == 0 else 11
    x_ref, wr_ref, wc_ref, bfc_ref, bfr_ref, gv_ref, bv_ref = refs[:7]
    (q_ref, kb_ref, vb_ref, kf_ref, ks_ref, vf_ref, vs_ref, lfcol_ref, ccol_ref, crow_ref,
     u_ref, vn_ref) = refs[n_in:]
    tm = x_ref.shape[0]
    xb = x_ref[...].astype(BF16)
    scale = 1.0 / math.sqrt(D_HEAD)
    q_ref[...] = (_dot(xb, wr_ref[:, _C_Q:_C_Q + 2 * W_ATT]) * scale).astype(BF16)

    def store_cache_rows(ref, z):
        for s in range(tm // dec_seq):
            for h in range(W_ATT // D_HEAD):
                blk = z[s * dec_seq:(s + 1) * dec_seq, h * D_HEAD:(h + 1) * D_HEAD]
                ref[0, s, h] = blk
                if layer == 0:
                    for l in range(1, depth):
                        ref[l, s, h] = jnp.zeros_like(blk)

    for c0, refs_fs, half_ref in ((_C_K, (kf_ref, ks_ref), kb_ref), (_C_V, (vf_ref, vs_ref), vb_ref)):
        z = _dot(xb, wr_ref[:, c0:c0 + 2 * W_ATT])
        half_ref[...] = z.astype(BF16)
        store_cache_rows(refs_fs[0], z[:, :W_ATT])
        store_cache_rows(refs_fs[1], z[:, W_ATT:])
    _gates(xb, wr_ref, gv_ref, bv_ref, u_ref, vn_ref)

    lane = lax.broadcasted_iota(jnp.int32, (tm, LANES), 1)
    z = _dot(xb, wr_ref[:, _C_F:_C_F + LANES]) + bfc_ref[...]
    lf_col = jnp.where(lane < H_FOX, _log_sigmoid(z), 0.0)
    lfcol_ref[...] = lf_col
    ccol_ref[...] = _ones_dot3(lf_col, _tri_ones(tm, "lower", dec_seq), left=True)
    lf_row = _log_forget_rows(xb, wc_ref, bfr_ref)
    c_row = _lane_cumsum(lf_row, dec_seq)
    for s in range(tm // dec_seq):
        crow_ref[s] = c_row[:, s * dec_seq:(s + 1) * dec_seq]


def _in_proj_sample(x, w_rows, w_cols, bfc, bfr, gv, bv, *, dec_seq, layer, depth, cache_rows):
    n, d = x.shape
    tm = ROW_TILE
    heads = W_ATT // D_HEAD
    row = lambda w: pl.BlockSpec((tm, w), lambda i: (i, 0))
    sds = lambda w, dt: jax.ShapeDtypeStruct((n, w), dt)
    if layer == 0:
        c_spec = pl.BlockSpec((depth, tm // dec_seq, heads, dec_seq, D_HEAD), lambda i: (0, i, 0, 0, 0))
        extra_in, extra_specs, aliases = (), [], {}
    else:
        c_spec = pl.BlockSpec((1, tm // dec_seq, heads, dec_seq, D_HEAD), lambda i: (layer, i, 0, 0, 0))
        extra_in = tuple(cache_rows)
        extra_specs = [pl.BlockSpec(memory_space=pl.ANY)] * 4
        aliases = {7 + k: 3 + k for k in range(4)}
    c_shape = jax.ShapeDtypeStruct((depth, n // dec_seq, heads, dec_seq, D_HEAD), F32)
    out_shape = [sds(2 * W_ATT, BF16)] * 3 + [c_shape] * 4 + [sds(LANES, F32)] * 2 + [
        jax.ShapeDtypeStruct((n // dec_seq, SUBLANES, dec_seq), F32), sds(W_SGU, F32), sds(W_SGU, F32)]
    out_specs = [row(2 * W_ATT)] * 3 + [c_spec] * 4 + [row(LANES)] * 2 + [
        pl.BlockSpec((tm // dec_seq, SUBLANES, dec_seq), lambda i: (i, 0, 0)), row(W_SGU), row(W_SGU)]
    return pl.pallas_call(
        functools.partial(_in_proj_sample_kernel, dec_seq=dec_seq, layer=layer, depth=depth),
        grid=(n // tm,), name="in_proj_sample",
        in_specs=[row(d)] + [_resident(a.shape) for a in (w_rows, w_cols, bfc, bfr, gv, bv)]
        + extra_specs,
        out_specs=out_specs, out_shape=out_shape, input_output_aliases=aliases,
        compiler_params=_params(("parallel",), 56),
    )(x, w_rows, w_cols, bfc, bfr, gv, bv, *extra_in)


def _split_head_rows(qt):
    sub = lax.broadcasted_iota(jnp.int32, qt.shape, 0)
    low = sub < D_HEAD
    zero = jnp.zeros_like(qt)
    return jnp.where(low, qt, zero), jnp.where(low, zero, qt)


def _fox_update_t(s, ct, pv, state, valid):
    m, l, acc = state
    if valid is not None:
        s = jnp.where(valid, s, NEG_INF)
    m_new = jnp.maximum(m, jnp.max(s, axis=0, keepdims=True) + ct)
    p = jnp.exp2(s + (ct - m_new))
    alpha = jnp.exp2(m - m_new)
    l = alpha * l + jnp.sum(p, axis=0, keepdims=True)
    acc = alpha * acc + pv(p.astype(BF16))
    return m_new, l, acc


def _sb_update_t(z, pv, suffix_ones, state, valid):
    carry, acc = state
    r = _softplus2(z)
    if valid is not None:
        r = jnp.where(valid, r, 0.0)
    r_hi = r.astype(BF16)
    r_lo = (r - r_hi.astype(F32)).astype(BF16)
    after = _dot(suffix_ones, r_hi) + _dot(suffix_ones, r_lo) + carry
    a = jnp.exp2((z - r) - after)
    if valid is not None:
        a = jnp.where(valid, a, 0.0)
    acc = acc + pv(a.astype(BF16))
    carry = carry + jnp.sum(r, axis=0, keepdims=True)
    return carry, acc


def _fox_prompt_kernel(*refs, window_blocks):
    lax.switch(jnp.minimum(pl.program_id(2), window_blocks - 1),
               [functools.partial(_fox_prompt_body, n, *refs) for n in range(1, window_blocks + 1)])


def _fox_prompt_body(nblk, flags_ref, qt_ref, k_ref, vt_ref, crow_ref, ccolp_ref, shiftrow_ref,
                     o_ref):
    tq = qt_ref.shape[3]
    tk = vt_ref.shape[3]
    hp = pl.program_id(1)
    i = pl.program_id(2)
    q2 = jnp.concatenate(_split_head_rows(qt_ref[0, 0]), axis=1)
    c_q = crow_ref[i]
    ct = jnp.concatenate([_pick_row(c_q, 2 * hp), _pick_row(c_q, 2 * hp + 1)], axis=1)

    key = lax.broadcasted_iota(jnp.int32, (tk, 2 * tq), 0)
    qry = lax.broadcasted_iota(jnp.int32, (tk, 2 * tq), 1) % tq
    causal = key <= qry

    def scores(j0, nblk):
        start = pl.multiple_of(j0 * tk, tk)
        kb = k_ref[pl.ds(start, nblk * tk), :]
        cc = ccolp_ref[pl.ds(start, nblk * tk), :]
        ck = jnp.concatenate([jnp.broadcast_to(cc[:, 0:1], (nblk * tk, tq)),
                              jnp.broadcast_to(cc[:, 1:2], (nblk * tk, tq))], axis=1)
        return _dot(kb, q2) - ck

    def pv(p, j0, nblk):
        vts = [vt_ref[0, j0 + w] for w in range(nblk)]
        outs = []
        for h in range(2):
            vth = jnp.concatenate([v[h * D_HEAD:(h + 1) * D_HEAD] for v in vts], axis=1)
            outs.append(_dot(vth, p[:, h * tq:(h + 1) * tq]))
        return jnp.concatenate(outs, axis=1)

    def pair_row(a, b):
        return jnp.concatenate([jnp.broadcast_to(a, (1, tq)), jnp.broadcast_to(b, (1, tq))], axis=1)

    sh = shiftrow_ref[i]
    shift = jnp.concatenate([_pick_row(sh, 2 * hp), _pick_row(sh, 2 * hp + 1)], axis=1)

    def reaches(j):
        return jnp.max(decay(j)) > -FOX_DEAD * LOG2E

    def decay(j):
        c_end = crow_ref[jnp.maximum(j, 0)][:, tk - 1:tk]
        return ct - pair_row(_pick_row(c_end, 2 * hp), _pick_row(c_end, 2 * hp + 1))

    def window(nblk):
        j0 = i - (nblk - 1)
        p = jnp.exp2(scores(j0, nblk) + shift)
        body_rows = (nblk - 1) * tk
        diag = jnp.where(causal, p[body_rows:], 0.0)
        p = diag if nblk == 1 else jnp.concatenate([p[:body_rows], diag], axis=0)
        return (j0 - 1, jnp.sum(p, axis=0, keepdims=True), pv(p.astype(BF16), j0, nblk))

    j, l, acc = window(nblk)

    def cond(c):
        j, go, _, _ = c
        return jnp.logical_and(j >= 0, go)

    def body(c):
        j, _, l, acc = c
        p = jnp.exp2(scores(j, 1) + shift)
        return (j - 1, reaches(j - 1), l + jnp.sum(p, axis=0, keepdims=True),
                acc + pv(p.astype(BF16), j, 1))

    def exact():
        def step(t, state):
            jj = i - 1 - t
            return _fox_update_t(scores(jj, 1), ct, lambda p: pv(p, jj, 1), state, None)

        init = (jnp.full((1, 2 * tq), NEG_INF, F32), jnp.zeros((1, 2 * tq), F32),
                jnp.zeros((D_HEAD, 2 * tq), F32))
        state = _fox_update_t(scores(i, 1), ct, lambda p: pv(p, i, 1), init, causal)
        _, l_x, acc_x = lax.fori_loop(0, i, step, state)
        return l_x, acc_x

    def beyond_window():
        _, _, l_w, acc_w = lax.while_loop(cond, body, (j, reaches(j), l, acc))
        return lax.cond(jnp.min(l_w) < FOX_L_FLOOR, exact, lambda: (l_w, acc_w))

    b = pl.program_id(0)
    uncommon = flags_ref[(b * pl.num_programs(2) + i) * N_PAIR + hp] != 0
    l, acc = lax.cond(uncommon, beyond_window, lambda: (l, acc))
    o = acc / l
    o_ref[0] = jnp.concatenate([o[:, :tq], o[:, tq:]], axis=0)


def _sb_prompt_kernel(*refs, window_blocks):
    lax.switch(jnp.minimum(pl.program_id(2), window_blocks - 1),
               [functools.partial(_sb_prompt_body, n, *refs) for n in range(1, window_blocks + 1)])


def _sb_prompt_body(nblk, qt_ref, k_ref, vt_ref, o_ref):
    tq = qt_ref.shape[3]
    tk = vt_ref.shape[3]
    i = pl.program_id(2)
    q2 = jnp.concatenate(_split_head_rows(qt_ref[0, 0]), axis=1)
    ts = SB_SUFFIX_BLOCK
    sub_ones = _tri_ones(ts, "strict_upper")

    def block(j, state, valid):
        kb = k_ref[pl.ds(pl.multiple_of(j * tk, tk), tk), :]
        vt = vt_ref[0, j]

        def pv(a):
            return jnp.concatenate([_dot(vt[:D_HEAD], a[:, :tq]), _dot(vt[D_HEAD:], a[:, tq:])],
                                   axis=1)

        return _sb_update_t(_dot(kb, q2), pv, _tri_ones(tk, "strict_upper"), state, valid)

    def still_open(carry):
        return jnp.min(carry) < SB_CLOSED * LOG2E

    key = lax.broadcasted_iota(jnp.int32, (tk, 2 * tq), 0)
    qry = lax.broadcasted_iota(jnp.int32, (tk, 2 * tq), 1) % tq
    causal = key < qry

    def window(nblk):
        j0 = i - (nblk - 1)
        z = _dot(k_ref[pl.ds(pl.multiple_of(j0 * tk, tk), nblk * tk), :], q2)
        r = _softplus2(z)
        body_rows = (nblk - 1) * tk
        diag = jnp.where(causal, r[body_rows:], 0.0)
        r = diag if nblk == 1 else jnp.concatenate([r[:body_rows], diag], axis=0)
        carry = jnp.zeros((1, 2 * tq), F32)
        n_sub = nblk * tk // ts
        a_blocks = [None] * n_sub
        for w in reversed(range(n_sub)):
            rows = slice(w * ts, (w + 1) * ts)
            r_w = r[rows]
            after = _dot(sub_ones, r_w.astype(BF16)) + carry
            a_blocks[w] = jnp.exp2((z[rows] - r_w) - after)
            carry = carry + jnp.sum(r_w, axis=0, keepdims=True)
        a = jnp.concatenate(a_blocks, axis=0)
        diag = jnp.where(causal, a[body_rows:], 0.0)
        a = (diag if nblk == 1 else jnp.concatenate([a[:body_rows], diag], axis=0)).astype(BF16)
        vts = [vt_ref[0, j0 + w] for w in range(nblk)]
        outs = []
        for h in range(2):
            vth = jnp.concatenate([v[h * D_HEAD:(h + 1) * D_HEAD] for v in vts], axis=1)
            outs.append(_dot(vth, a[:, h * tq:(h + 1) * tq]))
        return j0 - 1, still_open(carry), (carry, jnp.concatenate(outs, axis=1))

    j, go, state = window(nblk)

    def cond(c):
        j, go, _ = c
        return jnp.logical_and(j >= 0, go)

    def body(c):
        j, _, st = c
        st = block(j, st, None)
        return j - 1, still_open(st[0]), st

    _, _, (_, acc) = lax.while_loop(cond, body, (j, go, state))
    o_ref[0] = jnp.concatenate([acc[:, :tq], acc[:, tq:]], axis=0)


def _prompt_attention(qt, k, vtb, group, *, batch, seq, crow=None, ccolp=None, shiftrow=None,
                      flags=None):
    tq = tk = ATT_TILE
    nq = seq // tq
    col0 = group * N_PAIR
    qt_spec = pl.BlockSpec((1, 1, LANES, tq), lambda b, hp, i, *_: (b, i, col0 + hp, 0))
    k_spec = pl.BlockSpec((seq, LANES), lambda b, hp, i, *_: (b, col0 + hp))
    vt_spec = pl.BlockSpec((1, seq // tk, LANES, tk), lambda b, hp, i, *_: (b, 0, hp, 0))
    out_spec = pl.BlockSpec((1, LANES, tq), lambda b, hp, i, *_: (b * nq + i, hp, 0))
    out_shape = jax.ShapeDtypeStruct((batch * nq, W_ATT, tq), F32)
    params = _params(("parallel", "parallel", "arbitrary"), 40)
    grid = (batch, N_PAIR, nq)
    if group == 1:
        return pl.pallas_call(
            functools.partial(_sb_prompt_kernel, window_blocks=SB_WINDOW), grid=grid,
            name="sb_prompt",
            in_specs=[qt_spec, k_spec, vt_spec], out_specs=out_spec, out_shape=out_shape,
            compiler_params=params)(qt, k, vtb)
    crow_spec = pl.BlockSpec((seq // tk, SUBLANES, tk), lambda b, hp, i, *_: (b, 0, 0))
    ccolp_spec = pl.BlockSpec((seq, LANES), lambda b, hp, i, *_: (b, hp))
    grid_spec = pltpu.PrefetchScalarGridSpec(
        num_scalar_prefetch=1, grid=grid,
        in_specs=[qt_spec, k_spec, vt_spec, crow_spec, ccolp_spec, crow_spec],
        out_specs=out_spec)
    return pl.pallas_call(
        functools.partial(_fox_prompt_kernel, window_blocks=FOX_WINDOW), grid_spec=grid_spec,
        name="fox_prompt", out_shape=out_shape,
        compiler_params=params)(flags, qt, k, vtb, crow, ccolp, shiftrow)


def _suffix_sum_kernel(x_ref, o_ref):
    rows, width = x_ref.shape
    ones = _tri_ones(LANES, "strict_lower")
    carry = jnp.zeros((rows, 1), F32)
    for kb in reversed(range(width // LANES)):
        xb = x_ref[:, kb * LANES:(kb + 1) * LANES]
        o_ref[:, kb * LANES:(kb + 1) * LANES] = _ones_dot3(xb, ones, left=False) + carry
        carry = carry + jnp.sum(xb, axis=1, keepdims=True)


def _suffix_sum(x):
    return pl.pallas_call(
        _suffix_sum_kernel, name="cache_suffix_sum",
        out_shape=jax.ShapeDtypeStruct(x.shape, F32),
        compiler_params=_params(None, 32))(x)


def _split_heads(q2):
    lane = lax.broadcasted_iota(jnp.int32, q2.shape, 1)
    low = lane < D_HEAD
    zero = jnp.zeros_like(q2)
    return jnp.where(low, q2, zero), jnp.where(low, zero, q2)


def _stack_heads(q2):
    return jnp.concatenate(_split_heads(q2), axis=0)


def _unstack_heads(o):
    half = o.shape[0] // 2
    lane = lax.broadcasted_iota(jnp.int32, (half, o.shape[1]), 1)
    return jnp.where(lane < D_HEAD, o[:half], o[half:])


def _sb_update(z, pv, suffix_ones, state, valid):
    carry, acc = state
    r = _softplus(z)
    if valid is not None:
        r = jnp.where(valid, r, 0.0)
    r_hi = r.astype(BF16)
    r_lo = (r - r_hi.astype(F32)).astype(BF16)
    after = _dot(r_hi, suffix_ones) + _dot(r_lo, suffix_ones) + carry
    a = jnp.exp((z - r) - after)
    if valid is not None:
        a = jnp.where(valid, a, 0.0)
    acc = acc + pv(a.astype(BF16))
    carry = carry + jnp.sum(r, axis=1, keepdims=True)
    return carry, acc


def _sample_attn_kernel(qf_ref, kfn_ref, vfn_ref, kfc_ref, vfc_ref, drow_ref, cnrow_ref, ccol_ref,
                        qs_ref, ksn_ref, vsn_ref, ksc_last_ref, vsc_last_ref, ksc_hbm_ref,
                        vsc_hbm_ref, of_ref, os_ref,
                        carry_ref, acc_ref, kbuf_ref, vbuf_ref, sem_ref, *, blk, layer):
    b = pl.program_id(0)
    hp = pl.program_id(1)
    tq = qf_ref.shape[0]
    past = kfc_ref.shape[3]
    row = lax.broadcasted_iota(jnp.int32, (2 * tq, tq), 0) % tq
    col = lax.broadcasted_iota(jnp.int32, (2 * tq, tq), 1)

    def per_head(fn):
        return jnp.concatenate([fn(0), fn(1)], axis=0)

    q2 = _stack_heads(qf_ref[...])
    kct = kfc_ref[0, 0].astype(BF16)
    vct = vfc_ref[0, 0].astype(BF16)
    cc = ccol_ref[...]
    cn = cnrow_ref[0]
    cq = per_head(lambda h: _pick_lane(cc, 2 * hp + h))
    qk_c = _dot(q2, kct)
    qk_n = _dot_nt(q2, kfn_ref[...])
    s_c = per_head(lambda h: qk_c[h * tq:(h + 1) * tq]
                   + _pick_row(drow_ref[0, 2 * hp + h], b % SUBLANES))
    s_n = per_head(lambda h: qk_n[h * tq:(h + 1) * tq] - _pick_row(cn, 2 * hp + h))
    s_n = jnp.where(col <= row, s_n, NEG_INF)
    m = jnp.maximum(jnp.max(s_c, axis=1, keepdims=True), jnp.max(s_n, axis=1, keepdims=True)) + cq
    shift = cq - m
    p_c = jnp.exp(s_c + shift)
    p_n = jnp.exp(s_n + shift)
    l = jnp.sum(p_c, axis=1, keepdims=True) + jnp.sum(p_n, axis=1, keepdims=True)
    o = _dot_nt(p_c.astype(BF16), vct) + _dot(p_n.astype(BF16), vfn_ref[...])
    of_ref[...] = _unstack_heads(o / l)

    q2 = _stack_heads(qs_ref[...])
    vn = vsn_ref[...]
    suffix_ones = _tri_ones(blk, "strict_lower")

    def cache_block(kt_f32, vt_f32, state):
        kt = kt_f32.astype(BF16)
        vt = vt_f32.astype(BF16)
        return _sb_update(_dot(q2, kt), lambda a: _dot_nt(a, vt), suffix_ones, state, None)

    n_blocks = past // blk
    state = (jnp.zeros((2 * tq, 1), F32), jnp.zeros((2 * tq, LANES), F32))
    state = _sb_update(_dot_nt(q2, ksn_ref[...]), lambda a: _dot(a, vn),
                       _tri_ones(tq, "strict_lower"), state, col < row)
    carry_ref[...], acc_ref[...] = cache_block(ksc_last_ref[0, 0], vsc_last_ref[0, 0], state)

    def still_open():
        return jnp.min(carry_ref[...]) < SB_CLOSED

    def fetch(src_ref, jb, buf_ref, sem):
        rows = pl.ds(pl.multiple_of(hp * LANES, LANES), LANES)
        return pltpu.make_async_copy(src_ref.at[layer, b, rows, pl.ds(jb * blk, blk)], buf_ref, sem)

    @pl.when(still_open())
    def _():
        for jb in reversed(range(n_blocks - 1)):
            @pl.when(still_open())
            def _():
                copies = (fetch(ksc_hbm_ref, jb, kbuf_ref, sem_ref.at[0]),
                          fetch(vsc_hbm_ref, jb, vbuf_ref, sem_ref.at[1]))
                for cp in copies:
                    cp.start()
                for cp in copies:
                    cp.wait()
                carry_ref[...], acc_ref[...] = cache_block(
                    kbuf_ref[...], vbuf_ref[...], (carry_ref[...], acc_ref[...]))

    os_ref[...] = _unstack_heads(acc_ref[...])


def _sample_attention(q, kb, vb, cache_fk, cache_fv, drow, cnrow, ccol, cache_sk, cache_sv,
                      *, layer, streams, dec_seq):
    past = cache_fk.shape[3]
    new_spec = lambda g: pl.BlockSpec((dec_seq, LANES), lambda b, hp: (b, g * N_PAIR + hp))
    cache_spec = pl.BlockSpec((1, 1, LANES, past), lambda b, hp: (layer, b, hp, 0))
    drow_spec = pl.BlockSpec((1, H_FOX, SUBLANES, past), lambda b, hp: (layer, 0, b // SUBLANES, 0))
    cnrow_spec = pl.BlockSpec((1, SUBLANES, dec_seq), lambda b, hp: (b, 0, 0))
    ccol_spec = pl.BlockSpec((dec_seq, LANES), lambda b, hp: (b, 0))
    out_spec = pl.BlockSpec((dec_seq, LANES), lambda b, hp: (b, hp))
    out_shape = [jax.ShapeDtypeStruct((streams * dec_seq, W_ATT), F32)] * 2
    blk = ATT_TILE
    newest_spec = pl.BlockSpec((1, 1, LANES, blk), lambda b, hp: (layer, b, hp, past // blk - 1))
    hbm_spec = pl.BlockSpec(memory_space=pl.ANY)
    return pl.pallas_call(
        functools.partial(_sample_attn_kernel, blk=blk, layer=layer),
        grid=(streams, N_PAIR), name="sample_attention",
        in_specs=[new_spec(0), new_spec(0), new_spec(0), cache_spec, cache_spec, drow_spec,
                  cnrow_spec, ccol_spec, new_spec(1), new_spec(1), new_spec(1), newest_spec,
                  newest_spec, hbm_spec, hbm_spec],
        out_specs=[out_spec, out_spec], out_shape=out_shape,
        scratch_shapes=[pltpu.VMEM((2 * dec_seq, 1), F32), pltpu.VMEM((2 * dec_seq, LANES), F32),
                        pltpu.VMEM((LANES, blk), F32), pltpu.VMEM((LANES, blk), F32),
                        pltpu.SemaphoreType.DMA((2,))],
        compiler_params=_params(("parallel", "parallel"), 48),
    )(q, kb, vb, cache_fk, cache_fv, drow, cnrow, ccol, q, kb, vb, cache_sk, cache_sv,
      cache_sk, cache_sv)


def _merge_tile(x_ref, of_ref, os_ref, u_ref, vn_ref, ws_ref, bs_ref, gmix_ref, wout_ref,
                g_ref, b_ref, *, alpha):
    tm = x_ref.shape[0]
    r = lax.broadcasted_iota(jnp.int32, (SGU_CHUNK, SGU_CHUNK), 0)
    c = lax.broadcasted_iota(jnp.int32, (SGU_CHUNK, SGU_CHUNK), 1)
    tril = r >= c
    head_mean = jnp.where(r // D_HEAD == c // D_HEAD, 1.0 / D_HEAD, 0.0).astype(BF16)
    lane = lax.broadcasted_iota(jnp.int32, (SGU_CHUNK, LANES), 1)

    gate_pieces = []
    for p in range(W_SGU // LANES):
        w0 = jnp.where(tril, ws_ref[2 * p], 0.0).astype(BF16)
        w1 = jnp.where(tril, ws_ref[2 * p + 1], 0.0).astype(BF16)
        bias = bs_ref[:, p * LANES:(p + 1) * LANES]
        chunks = []
        for ch in range(tm // SGU_CHUNK):
            rows = slice(ch * SGU_CHUNK, (ch + 1) * SGU_CHUNK)
            vp = vn_ref[rows, p * LANES:(p + 1) * LANES].astype(BF16)
            s = jnp.where(lane < D_HEAD, _dot(w0, vp), _dot(w1, vp)) + bias
            chunks.append(u_ref[rows, p * LANES:(p + 1) * LANES] * s)
        gate_pieces.append(jnp.concatenate(chunks, axis=0))

    def attention_pieces(ref):
        if ref.ndim == 2:
            return [ref[:, k * LANES:(k + 1) * LANES] for k in range(N_PAIR)]
        return [jnp.concatenate([ref[qb, k * LANES:(k + 1) * LANES, :].T
                                 for qb in range(ref.shape[0])], axis=0) for k in range(N_PAIR)]

    pieces = attention_pieces(of_ref) + gate_pieces + attention_pieces(os_ref)
    normed = []
    for k, o in enumerate(pieces):
        ms = _dot((o * o).astype(BF16), head_mean)
        gain = gmix_ref[:, k * LANES:(k + 1) * LANES]
        normed.append((o * lax.rsqrt(ms + RMS_EPS) * gain).astype(BF16))
    mix = _dot(jnp.concatenate(normed, axis=1), wout_ref[0])
    return _layer_norm(alpha * x_ref[...] + mix, g_ref[...], b_ref[...])


def _mlp_tile(x, wup_ref, wdn_ref, g_ref, b_ref, *, alpha, ff_chunk):
    xb = x.astype(BF16)
    acc = alpha * x
    for c0 in range(0, wup_ref.shape[2], ff_chunk):
        h = jnp.maximum(_dot(xb, wup_ref[0, :, c0:c0 + ff_chunk]), 0.0)
        acc = acc + _dot((h * h).astype(BF16), wdn_ref[0, c0:c0 + ff_chunk, :])
    return _layer_norm(acc, g_ref[...], b_ref[...])


def _merge_mlp_kernel(x_ref, of_ref, os_ref, u_ref, vn_ref, ws_ref, bs_ref, gmix_ref, wout_ref,
                      g1_ref, b1_ref, wup_ref, wdn_ref, g2_ref, b2_ref, o_ref, *, alpha, ff_chunk):
    x1 = _merge_tile(x_ref, of_ref, os_ref, u_ref, vn_ref, ws_ref, bs_ref, gmix_ref, wout_ref,
                     g1_ref, b1_ref, alpha=alpha)
    o_ref[...] = _mlp_tile(x1, wup_ref, wdn_ref, g2_ref, b2_ref, alpha=alpha, ff_chunk=ff_chunk)


def _merge_mlp(x, of, os_, u, vn, ws, bs, gmix, wout, g1, b1, wup, wdn, g2, b2, *, alpha, name,
               layer):
    n, d = x.shape
    tm = ROW_TILE
    row = lambda w: pl.BlockSpec((tm, w), lambda i: (i, 0))
    if of.ndim == 2:
        att = row(W_ATT)
    else:
        tq = of.shape[2]
        att = pl.BlockSpec((tm // tq, W_ATT, tq), lambda i: (i, 0, 0))
    weights = (ws, bs, gmix, wout, g1, b1, wup, wdn, g2, b2)

    def weight_spec(a):
        if a is wout or a is wup or a is wdn:
            return pl.BlockSpec((1,) + a.shape[1:], lambda i: (layer, 0, 0),
                                pipeline_mode=pl.Buffered(1))
        return _resident(a.shape)

    return pl.pallas_call(
        functools.partial(_merge_mlp_kernel, alpha=alpha, ff_chunk=MLP_FF_CHUNK), grid=(n // tm,),
        name=name,
        in_specs=[row(d), att, att, row(W_SGU), row(W_SGU)] + [weight_spec(a) for a in weights],
        out_specs=row(d), out_shape=jax.ShapeDtypeStruct((n, d), F32),
        compiler_params=_params(("parallel",), 60),
    )(x, of, os_, u, vn, *weights)


def _pack_w_in(w, b_f):
    sizes = [W_ATT, W_ATT, W_ATT, H_FOX, W_SGU, W_SGU, W_ATT, W_ATT, W_ATT]
    offs = [0]
    for s in sizes:
        offs.append(offs[-1] + s)
    q_f, k_f, v_f, f_lg, u_g, v_g, q_s, k_s, v_s = (w[:, offs[i]:offs[i + 1]] for i in range(9))
    pad_cols = lambda a, width: jnp.pad(a, ((0, 0), (0, width - a.shape[1])))
    w_rows = jnp.concatenate([q_f, q_s, k_f, k_s, v_f, v_s, u_g, v_g, pad_cols(f_lg, LANES)],
                             axis=1).astype(BF16)
    w_cols = jnp.concatenate([q_f, q_s, k_f, k_s, v_f, v_s, pad_cols(f_lg, F_ROWS)],
                             axis=1).T.astype(BF16)
    bfc = jnp.pad(b_f, (0, LANES - H_FOX)).reshape(1, LANES)
    bfr = jnp.broadcast_to(jnp.pad(b_f, (0, SUBLANES - H_FOX)).reshape(SUBLANES, 1),
                           (SUBLANES, LANES))
    return w_rows, w_cols, bfc, bfr


def kernel(x_prompt, x_sample, cache_fox_k, cache_fox_v, cache_fox_logf, cache_sb_k, cache_sb_v,
           w_in, b_f, g_v, b_v, w_s, b_s, g_mix, w_out, ln1_g, ln1_b, w_up, w_down, ln2_g, ln2_b):
    depth = w_in.shape[0]
    batch, seq, d_model = x_prompt.shape
    streams, dec_seq, _ = x_sample.shape
    past = cache_fox_k.shape[2]
    n_prompt = batch * seq
    n_sample = streams * dec_seq
    alpha = (2 * depth) ** 0.25
    assert seq % ROW_TILE == 0 and n_sample % ROW_TILE == 0 and ROW_TILE % dec_seq == 0
    assert dec_seq == D_HEAD and 2 * dec_seq == SGU_CHUNK and past % ATT_TILE == 0
    assert streams % SUBLANES == 0

    xp = x_prompt.reshape(n_prompt, d_model)
    xs = x_sample.reshape(n_sample, d_model)
    row2 = lambda a: a.reshape(1, -1)
    cache_t = lambda c: jnp.transpose(c, (0, 1, 3, 4, 2)).reshape(depth, streams, W_ATT, past)
    ck_f, cv_f, ck_s, cv_s = (cache_t(c) for c in (cache_fox_k, cache_fox_v, cache_sb_k, cache_sb_v))
    lf_cache = jnp.transpose(cache_fox_logf.astype(F32), (0, 3, 1, 2))
    drow = _suffix_sum(lf_cache.reshape(depth * H_FOX * streams, past)).reshape(
        depth, H_FOX, streams, past)

    wout, wup, wdn = w_out.astype(BF16), w_up.astype(BF16), w_down.astype(BF16)
    p_rows = s_rows = None
    p_logf, s_logf, s_gate = [], [], []
    for l in range(depth):
        w_rows, w_cols, bfc, bfr = _pack_w_in(w_in[l], b_f[l])
        gv, bv = row2(g_v[l]), row2(b_v[l])
        gmix, g1, b1, g2, b2 = (row2(a[l]) for a in (g_mix, ln1_g, ln1_b, ln2_g, ln2_b))

        w_l = w_s[l]
        half = w_l[:, :dec_seq, :dec_seq]
        zeros = jnp.zeros_like(half)
        w_blockdiag = jnp.concatenate([jnp.concatenate([half, zeros], axis=2),
                                       jnp.concatenate([zeros, half], axis=2)], axis=1)
        bias_full = jnp.repeat(b_s[l].T, D_HEAD, axis=1)
        bias_half = jnp.concatenate([bias_full[:dec_seq], bias_full[:dec_seq]], axis=0)

        (qt, kb, *p_rows, vtbf, vtbs, lfrow, crow, ccolp, u, vn, shiftrow,
         head_flags) = _in_proj_prompt(
            xp, w_rows, w_cols, bfr, gv, bv, batch=batch, seq=seq, layer=l, depth=depth,
            cache_rows=p_rows)
        pair_flags = jnp.max(head_flags[:, :H_FOX, 0].reshape(-1, N_PAIR, 2), axis=2).reshape(-1)
        of = _prompt_attention(qt, kb, vtbf, 0, batch=batch, seq=seq, crow=crow, ccolp=ccolp,
                               shiftrow=shiftrow, flags=pair_flags)
        os_ = _prompt_attention(qt, kb, vtbs, 1, batch=batch, seq=seq)
        xp = _merge_mlp(xp, of, os_, u, vn, w_l, bias_full, gmix, wout, g1, b1, wup, wdn, g2, b2,
                        alpha=alpha, name="merge_mlp_prompt", layer=l)
        p_logf.append(jnp.transpose(lfrow[:H_FOX].reshape(H_FOX, batch, seq), (1, 2, 0)))

        (q, kb, vb, *s_rows, lfcol, ccol, cnrow, u, vn) = _in_proj_sample(
            xs, w_rows, w_cols, bfc, bfr, gv, bv, dec_seq=dec_seq, layer=l, depth=depth,
            cache_rows=s_rows)
        of, os_ = _sample_attention(q, kb, vb, ck_f, cv_f, drow, cnrow, ccol, ck_s, cv_s,
                                    layer=l, streams=streams, dec_seq=dec_seq)
        xs = _merge_mlp(xs, of, os_, u, vn, w_blockdiag, bias_half, gmix, wout, g1, b1, wup, wdn,
                        g2, b2, alpha=alpha, name="merge_mlp_sample", layer=l)
        s_logf.append(lfcol[:, :H_FOX].reshape(streams, dec_seq, H_FOX))
        s_gate.append(vn.reshape(streams, dec_seq, W_SGU))

    p_kf, p_ks, p_vf, p_vs = (jnp.transpose(a.reshape(depth, batch, H_FOX, D_HEAD, seq),
                                            (0, 1, 4, 2, 3)) for a in p_rows)
    s_kf, s_ks, s_vf, s_vs = (jnp.transpose(a, (0, 1, 3, 2, 4)) for a in s_rows)
    return (xp.reshape(batch, seq, d_model), xs.reshape(streams, dec_seq, d_model),
            p_kf, p_vf, jnp.stack(p_logf), p_ks, p_vs,
            s_kf, s_vf, jnp.stack(s_logf), s_ks, s_vs, jnp.stack(s_gate))
```

```python
import functools
import math

import jax
import jax.numpy as jnp
from jax import lax
from jax.experimental import pallas as pl
from jax.experimental.pallas import tpu as pltpu

F32 = jnp.float32
BF16 = jnp.bfloat16

LANES = 128
SUBLANES = 8
VMEM_BYTES_V7X = 64 * 1024 * 1024

D_HEAD = 64
H_FOX = 6
H_SB = 6
H_SGU = 4
W_ATT = H_FOX * D_HEAD
W_SGU = H_SGU * D_HEAD
N_PAIR = W_ATT // LANES
SGU_CHUNK = 128
LN_EPS = 1e-5
RMS_EPS = 1e-6
NEG_INF = -1e30
SB_CLOSED = 110.0
FOX_DEAD = 110.0
LOG2E = math.log2(math.e)
FOX_L_FLOOR = 1e-25
FOX_BOUND_SAFE = 41.0
FOX_WINDOW = 5
SB_WINDOW = 2
SB_SUFFIX_BLOCK = 128

ROW_TILE = 512
ATT_TILE = 256
MLP_FF_CHUNK = 1024

_C_Q = 0
_C_K = 2 * W_ATT
_C_V = 4 * W_ATT
_C_G = 6 * W_ATT
_C_F = _C_G + 2 * W_SGU
W_ROWS = _C_F + LANES
_R_Q = 0
_R_K = 2 * W_ATT
_R_V = 4 * W_ATT
_R_F = 6 * W_ATT
F_ROWS = 2 * SUBLANES
W_COLS = _R_F + F_ROWS


def _params(semantics, vmem_mib, flags=None):
    assert vmem_mib * 1024 * 1024 < VMEM_BYTES_V7X
    return pltpu.CompilerParams(dimension_semantics=semantics,
                                vmem_limit_bytes=vmem_mib * 1024 * 1024, flags=flags)


def _resident(shape):
    nd = len(shape)
    return pl.BlockSpec(shape, lambda *_: (0,) * nd, pipeline_mode=pl.Buffered(1))


def _split3(x):
    hi = x.astype(BF16)
    r1 = x - hi.astype(F32)
    mid = r1.astype(BF16)
    lo = (r1 - mid.astype(F32)).astype(BF16)
    return hi, mid, lo


def _dot(a, b):
    return jnp.dot(a, b, preferred_element_type=F32)


def _dot_nt(a, b):
    return lax.dot_general(a, b, (((1,), (1,)), ((), ())), preferred_element_type=F32)


def _lane_cumsum(x, seg=None):
    n = x.shape[1]
    lane = lax.broadcasted_iota(jnp.int32, x.shape, 1)
    pos = lane if seg is None else lane % seg
    step = 1
    while step < (n if seg is None else seg):
        x = x + jnp.where(pos >= step, pltpu.roll(x, step, axis=1), 0.0)
        step *= 2
    return x


def _ones_dot3(x, ones_mat, left):
    if left:
        return sum(_dot(ones_mat, p) for p in _split3(x))
    return sum(_dot(p, ones_mat) for p in _split3(x))


def _gelu_tanh(x):
    return 0.5 * x * (1.0 + jnp.tanh(0.7978845608028654 * (x + 0.044715 * (x * x * x))))


def _log_sigmoid(x):
    return jnp.minimum(x, 0.0) - jnp.log1p(jnp.exp(-jnp.abs(x)))


def _softplus(z):
    return jnp.maximum(z, 0.0) + jnp.log(1.0 + jnp.exp(-jnp.abs(z)))


def _softplus2(z2):
    return jnp.maximum(z2, 0.0) + jnp.log(1.0 + jnp.exp2(-jnp.abs(z2))) * LOG2E


def _layer_norm(x, g, b):
    mu = jnp.mean(x, axis=-1, keepdims=True)
    xc = x - mu
    var = jnp.mean(xc * xc, axis=-1, keepdims=True)
    return xc * lax.rsqrt(var + LN_EPS) * g + b


def _tri_ones(n, kind, seg=None):
    r = lax.broadcasted_iota(jnp.int32, (n, n), 0)
    c = lax.broadcasted_iota(jnp.int32, (n, n), 1)
    keep = {"lower": c <= r, "upper": r <= c, "strict_lower": r > c, "strict_upper": r < c}[kind]
    if seg is not None:
        keep = jnp.logical_and(keep, r // seg == c // seg)
    return jnp.where(keep, 1.0, 0.0).astype(BF16)


def _pick_lane(x, idx):
    lane = lax.broadcasted_iota(jnp.int32, x.shape, 1)
    return jnp.sum(jnp.where(lane == idx, x, 0.0), axis=1, keepdims=True)


def _pick_row(x, idx):
    sub = lax.broadcasted_iota(jnp.int32, x.shape, 0)
    return jnp.sum(jnp.where(sub == idx, x, 0.0), axis=0, keepdims=True)


def _gates(xb, wr_ref, gv_ref, bv_ref, u_ref, vn_ref):
    u_ref[...] = _gelu_tanh(_dot(xb, wr_ref[:, _C_G:_C_G + W_SGU]))
    vg = _dot(xb, wr_ref[:, _C_G + W_SGU:_C_G + 2 * W_SGU])
    vn_ref[...] = _layer_norm(_gelu_tanh(vg), gv_ref[...], bv_ref[...])


def _log_forget_rows(xb, wc_ref, bfr_ref):
    sub = lax.broadcasted_iota(jnp.int32, (SUBLANES, xb.shape[0]), 0)
    z = _dot_nt(wc_ref[_R_F:_R_F + F_ROWS, :], xb)[:SUBLANES] + bfr_ref[:, 0:1]
    return jnp.where(sub < H_FOX, _log_sigmoid(z), 0.0)


def _in_proj_prompt_kernel(*refs, layer, depth):
    n_in = 6 if layer == 0 else 10
    x_ref, wr_ref, wc_ref, bfr_ref, gv_ref, bv_ref = refs[:6]
    (qt_ref, k_ref, ktf_ref, kts_ref, vtf_ref, vts_ref, vtbf_ref, vtbs_ref,
     lfrow_ref, crow_ref, ccolp_ref, u_ref, vn_ref, shiftrow_ref, flags_ref,
     carry_row_ref, knmax_ref, cend_ref) = refs[n_in:]
    i = pl.program_id(1)
    tm = x_ref.shape[0]
    tk = vtbf_ref.shape[3]

    @pl.when(i == 0)
    def _():
        carry_row_ref[...] = jnp.zeros_like(carry_row_ref)
        knmax_ref[...] = jnp.zeros_like(knmax_ref)

    xb = x_ref[...].astype(BF16)

    def blocks(zt):
        return [zt[:, kb * tk:(kb + 1) * tk].astype(BF16) for kb in range(tm // tk)]

    def store_cache_rows(ref, zt):
        if layer == 0:
            for l in range(depth):
                ref[l, 0] = zt if l == 0 else jnp.zeros_like(zt)
        else:
            ref[0, 0] = zt

    scale = LOG2E / math.sqrt(D_HEAD)
    qn2_rows = []
    for kb, blk in enumerate(blocks(_dot_nt(wc_ref[_R_Q:_R_Q + 2 * W_ATT, :], xb) * scale)):
        qt_ref[0, kb] = blk
        qf = blk[:W_ATT].astype(F32)
        qn2 = jnp.sum((qf * qf).reshape(H_FOX, D_HEAD, tk), axis=1)
        qn2_rows.append(jnp.concatenate([qn2, jnp.zeros((SUBLANES - H_FOX, tk), F32)], axis=0))
    for r0, full_ref, blk_ref in ((_R_V, vtf_ref, vtbf_ref), (_R_V + W_ATT, vts_ref, vtbs_ref)):
        zt = _dot_nt(wc_ref[r0:r0 + W_ATT, :], xb)
        store_cache_rows(full_ref, zt)
        for kb, blk in enumerate(blocks(zt)):
            blk_ref[0, kb] = blk
    k_rows = _dot(xb, wr_ref[:, _C_K:_C_K + 2 * W_ATT])
    k_ref[...] = k_rows.astype(BF16)
    kt_f = k_rows[:, :W_ATT].T
    store_cache_rows(ktf_ref, kt_f)
    store_cache_rows(kts_ref, k_rows[:, W_ATT:].T)

    _gates(xb, wr_ref, gv_ref, bv_ref, u_ref, vn_ref)

    kn2 = jnp.sum((kt_f * kt_f).reshape(H_FOX, D_HEAD, tm), axis=1)
    kn2 = jnp.broadcast_to(jnp.max(kn2, axis=1, keepdims=True), (H_FOX, LANES))
    kn2 = jnp.concatenate([kn2, jnp.zeros((SUBLANES - H_FOX, LANES), F32)], axis=0)
    knmax_ref[...] = jnp.maximum(knmax_ref[...], kn2)

    lf_row = _log_forget_rows(xb, wc_ref, bfr_ref)
    lfrow_ref[...] = lf_row
    c_row = _ones_dot3(lf_row, _tri_ones(tm, "upper"), left=False) + carry_row_ref[:, 0:1]
    carry_row_ref[...] = jnp.broadcast_to(c_row[:, tm - 1:tm], carry_row_ref.shape)
    c_row = c_row * LOG2E
    kn_run = jnp.concatenate([knmax_ref[...]] * (tk // LANES), axis=1)
    for kb in range(tm // tk):
        c_blk = c_row[:, kb * tk:(kb + 1) * tk]
        crow_ref[kb] = c_blk
        bound = jnp.sqrt(qn2_rows[kb] * kn_run) * 1.02
        shiftrow_ref[kb] = c_blk - bound
        blk_id = i * (tm // tk) + kb
        cend_ref[blk_id] = jnp.broadcast_to(c_blk[:, tk - 1:tk], (SUBLANES, LANES))
        left = blk_id - FOX_WINDOW
        decay = c_blk[:, 0:1] - cend_ref[jnp.maximum(left, 0)][:, 0:1]
        live = jnp.logical_and(decay > -FOX_DEAD * LOG2E, left >= 0)
        risky = jnp.max(bound, axis=1, keepdims=True) > FOX_BOUND_SAFE
        flags_ref[kb] = jnp.broadcast_to(
            jnp.where(jnp.logical_or(live, risky), 1, 0).astype(jnp.int32), (SUBLANES, LANES))
    pad = jnp.zeros((LANES - SUBLANES, tm), F32)
    for p in range(N_PAIR):
        rows = c_row if p == 0 else jnp.concatenate([c_row[2 * p:], c_row[:2 * p]], axis=0)
        ccolp_ref[:, p * LANES:(p + 1) * LANES] = jnp.concatenate([rows, pad], axis=0).T


def _in_proj_prompt(x, w_rows, w_cols, bfr, gv, bv, *, batch, seq, layer, depth, cache_rows):
    n, d = x.shape
    tm, tk = ROW_TILE, ATT_TILE
    nt = seq // tm
    row = lambda w: pl.BlockSpec((tm, w), lambda b, i: (b * nt + i, 0))
    if layer == 0:
        t_spec = pl.BlockSpec((depth, 1, W_ATT, tm), lambda b, i: (0, b, 0, i))
        extra_in, extra_specs, aliases = (), [], {}
    else:
        t_spec = pl.BlockSpec((1, 1, W_ATT, tm), lambda b, i: (layer, b, 0, i))
        extra_in = tuple(cache_rows)
        extra_specs = [pl.BlockSpec(memory_space=pl.ANY)] * 4
        aliases = {6 + k: 2 + k for k in range(4)}
    t_shape = jax.ShapeDtypeStruct((depth, batch, W_ATT, seq), F32)
    blk_spec = lambda w: pl.BlockSpec((1, tm // tk, w, tk), lambda b, i: (b, i, 0, 0))
    blk_shape = lambda w: jax.ShapeDtypeStruct((batch, seq // tk, w, tk), BF16)
    out_shape = [blk_shape(2 * W_ATT), jax.ShapeDtypeStruct((n, 2 * W_ATT), BF16),
                 t_shape, t_shape, t_shape, t_shape, blk_shape(W_ATT), blk_shape(W_ATT),
                 jax.ShapeDtypeStruct((SUBLANES, n), F32),
                 jax.ShapeDtypeStruct((n // tk, SUBLANES, tk), F32),
                 jax.ShapeDtypeStruct((n, N_PAIR * LANES), F32),
                 jax.ShapeDtypeStruct((n, W_SGU), F32), jax.ShapeDtypeStruct((n, W_SGU), F32),
                 jax.ShapeDtypeStruct((n // tk, SUBLANES, tk), F32),
                 jax.ShapeDtypeStruct((n // tk, SUBLANES, LANES), jnp.int32)]
    out_specs = [blk_spec(2 * W_ATT), row(2 * W_ATT), t_spec, t_spec, t_spec, t_spec,
                 blk_spec(W_ATT), blk_spec(W_ATT),
                 pl.BlockSpec((SUBLANES, tm), lambda b, i: (0, b * nt + i)),
                 pl.BlockSpec((tm // tk, SUBLANES, tk), lambda b, i: (b * nt + i, 0, 0)),
                 row(N_PAIR * LANES), row(W_SGU), row(W_SGU),
                 pl.BlockSpec((tm // tk, SUBLANES, tk), lambda b, i: (b * nt + i, 0, 0)),
                 pl.BlockSpec((tm // tk, SUBLANES, LANES), lambda b, i: (b * nt + i, 0, 0))]
    return pl.pallas_call(
        functools.partial(_in_proj_prompt_kernel, layer=layer, depth=depth),
        grid=(batch, nt), name="in_proj_prompt",
        in_specs=[row(d)] + [_resident(a.shape) for a in (w_rows, w_cols, bfr, gv, bv)]
        + extra_specs,
        out_specs=out_specs, out_shape=out_shape, input_output_aliases=aliases,
        scratch_shapes=[pltpu.VMEM((SUBLANES, LANES), F32), pltpu.VMEM((SUBLANES, LANES), F32),
                        pltpu.VMEM((seq // tk, SUBLANES, LANES), F32)],
        compiler_params=_params(("parallel", "arbitrary"), 56),
    )(x, w_rows, w_cols, bfr, gv, bv, *extra_in)


def _in_proj_sample_kernel(*refs, dec_seq, layer, depth):
    n_in = 7 if layer == 0 else 11
    x_ref, wr_ref, wc_ref, bfc_ref, bfr_ref, gv_ref, bv_ref = refs[:7]
    (q_ref, kb_ref, vb_ref, kf_ref, ks_ref, vf_ref, vs_ref, lfcol_ref, ccol_ref, crow_ref,
     u_ref, vn_ref) = refs[n_in:]
    tm = x_ref.shape[0]
    xb = x_ref[...].astype(BF16)
    scale = 1.0 / math.sqrt(D_HEAD)
    q_ref[...] = (_dot(xb, wr_ref[:, _C_Q:_C_Q + 2 * W_ATT]) * scale).astype(BF16)

    def store_cache_rows(ref, z):
        for s in range(tm // dec_seq):
            for h in range(W_ATT // D_HEAD):
                blk = z[s * dec_seq:(s + 1) * dec_seq, h * D_HEAD:(h + 1) * D_HEAD]
                ref[0, s, h] = blk
                if layer == 0:
                    for l in range(1, depth):
                        ref[l, s, h] = jnp.zeros_like(blk)

    for c0, refs_fs, half_ref in ((_C_K, (kf_ref, ks_ref), kb_ref), (_C_V, (vf_ref, vs_ref), vb_ref)):
        z = _dot(xb, wr_ref[:, c0:c0 + 2 * W_ATT])
        half_ref[...] = z.astype(BF16)
        store_cache_rows(refs_fs[0], z[:, :W_ATT])
        store_cache_rows(refs_fs[1], z[:, W_ATT:])
    _gates(xb, wr_ref, gv_ref, bv_ref, u_ref, vn_ref)

    lane = lax.broadcasted_iota(jnp.int32, (tm, LANES), 1)
    z = _dot(xb, wr_ref[:, _C_F:_C_F + LANES]) + bfc_ref[...]
    lf_col = jnp.where(lane < H_FOX, _log_sigmoid(z), 0.0)
    lfcol_ref[...] = lf_col
    ccol_ref[...] = _ones_dot3(lf_col, _tri_ones(tm, "lower", dec_seq), left=True)
    lf_row = _log_forget_rows(xb, wc_ref, bfr_ref)
    c_row = _lane_cumsum(lf_row, dec_seq)
    for s in range(tm // dec_seq):
        crow_ref[s] = c_row[:, s * dec_seq:(s + 1) * dec_seq]


def _in_proj_sample(x, w_rows, w_cols, bfc, bfr, gv, bv, *, dec_seq, layer, depth, cache_rows):
    n, d = x.shape
    tm = ROW_TILE
    heads = W_ATT // D_HEAD
    row = lambda w: pl.BlockSpec((tm, w), lambda i: (i, 0))
    sds = lambda w, dt: jax.ShapeDtypeStruct((n, w), dt)
    if layer == 0:
        c_spec = pl.BlockSpec((depth, tm // dec_seq, heads, dec_seq, D_HEAD), lambda i: (0, i, 0, 0, 0))
        extra_in, extra_specs, aliases = (), [], {}
    else:
        c_spec = pl.BlockSpec((1, tm // dec_seq, heads, dec_seq, D_HEAD), lambda i: (layer, i, 0, 0, 0))
        extra_in = tuple(cache_rows)
        extra_specs = [pl.BlockSpec(memory_space=pl.ANY)] * 4
        aliases = {7 + k: 3 + k for k in range(4)}
    c_shape = jax.ShapeDtypeStruct((depth, n // dec_seq, heads, dec_seq, D_HEAD), F32)
    out_shape = [sds(2 * W_ATT, BF16)] * 3 + [c_shape] * 4 + [sds(LANES, F32)] * 2 + [
        jax.ShapeDtypeStruct((n // dec_seq, SUBLANES, dec_seq), F32), sds(W_SGU, F32), sds(W_SGU, F32)]
    out_specs = [row(2 * W_ATT)] * 3 + [c_spec] * 4 + [row(LANES)] * 2 + [
        pl.BlockSpec((tm // dec_seq, SUBLANES, dec_seq), lambda i: (i, 0, 0)), row(W_SGU), row(W_SGU)]
    return pl.pallas_call(
        functools.partial(_in_proj_sample_kernel, dec_seq=dec_seq, layer=layer, depth=depth),
        grid=(n // tm,), name="in_proj_sample",
        in_specs=[row(d)] + [_resident(a.shape) for a in (w_rows, w_cols, bfc, bfr, gv, bv)]
        + extra_specs,
        out_specs=out_specs, out_shape=out_shape, input_output_aliases=aliases,
        compiler_params=_params(("parallel",), 56),
    )(x, w_rows, w_cols, bfc, bfr, gv, bv, *extra_in)


def _split_head_rows(qt):
    sub = lax.broadcasted_iota(jnp.int32, qt.shape, 0)
    low = sub < D_HEAD
    zero = jnp.zeros_like(qt)
    return jnp.where(low, qt, zero), jnp.where(low, zero, qt)


def _fox_update_t(s, ct, pv, state, valid):
    m, l, acc = state
    if valid is not None:
        s = jnp.where(valid, s, NEG_INF)
    m_new = jnp.maximum(m, jnp.max(s, axis=0, keepdims=True) + ct)
    p = jnp.exp2(s + (ct - m_new))
    alpha = jnp.exp2(m - m_new)
    l = alpha * l + jnp.sum(p, axis=0, keepdims=True)
    acc = alpha * acc + pv(p.astype(BF16))
    return m_new, l, acc


def _sb_update_t(z, pv, suffix_ones, state, valid):
    carry, acc = state
    r = _softplus2(z)
    if valid is not None:
        r = jnp.where(valid, r, 0.0)
    r_hi = r.astype(BF16)
    r_lo = (r - r_hi.astype(F32)).astype(BF16)
    after = _dot(suffix_ones, r_hi) + _dot(suffix_ones, r_lo) + carry
    a = jnp.exp2((z - r) - after)
    if valid is not None:
        a = jnp.where(valid, a, 0.0)
    acc = acc + pv(a.astype(BF16))
    carry = carry + jnp.sum(r, axis=0, keepdims=True)
    return carry, acc


def _fox_prompt_kernel(*refs, window_blocks):
    lax.switch(jnp.minimum(pl.program_id(2), window_blocks - 1),
               [functools.partial(_fox_prompt_body, n, *refs) for n in range(1, window_blocks + 1)])


def _fox_prompt_body(nblk, flags_ref, qt_ref, k_ref, vt_ref, crow_ref, ccolp_ref, shiftrow_ref,
                     o_ref):
    tq = qt_ref.shape[3]
    tk = vt_ref.shape[3]
    hp = pl.program_id(1)
    i = pl.program_id(2)
    q2 = jnp.concatenate(_split_head_rows(qt_ref[0, 0]), axis=1)
    c_q = crow_ref[i]
    ct = jnp.concatenate([_pick_row(c_q, 2 * hp), _pick_row(c_q, 2 * hp + 1)], axis=1)

    key = lax.broadcasted_iota(jnp.int32, (tk, 2 * tq), 0)
    qry = lax.broadcasted_iota(jnp.int32, (tk, 2 * tq), 1) % tq
    causal = key <= qry

    def scores(j0, nblk):
        start = pl.multiple_of(j0 * tk, tk)
        kb = k_ref[pl.ds(start, nblk * tk), :]
        cc = ccolp_ref[pl.ds(start, nblk * tk), :]
        ck = jnp.concatenate([jnp.broadcast_to(cc[:, 0:1], (nblk * tk, tq)),
                              jnp.broadcast_to(cc[:, 1:2], (nblk * tk, tq))], axis=1)
        return _dot(kb, q2) - ck

    def pv(p, j0, nblk):
        vts = [vt_ref[0, j0 + w] for w in range(nblk)]
        outs = []
        for h in range(2):
            vth = jnp.concatenate([v[h * D_HEAD:(h + 1) * D_HEAD] for v in vts], axis=1)
            outs.append(_dot(vth, p[:, h * tq:(h + 1) * tq]))
        return jnp.concatenate(outs, axis=1)

    def pair_row(a, b):
        return jnp.concatenate([jnp.broadcast_to(a, (1, tq)), jnp.broadcast_to(b, (1, tq))], axis=1)

    sh = shiftrow_ref[i]
    shift = jnp.concatenate([_pick_row(sh, 2 * hp), _pick_row(sh, 2 * hp + 1)], axis=1)

    def reaches(j):
        return jnp.max(decay(j)) > -FOX_DEAD * LOG2E

    def decay(j):
        c_end = crow_ref[jnp.maximum(j, 0)][:, tk - 1:tk]
        return ct - pair_row(_pick_row(c_end, 2 * hp), _pick_row(c_end, 2 * hp + 1))

    def window(nblk):
        j0 = i - (nblk - 1)
        p = jnp.exp2(scores(j0, nblk) + shift)
        body_rows = (nblk - 1) * tk
        diag = jnp.where(causal, p[body_rows:], 0.0)
        p = diag if nblk == 1 else jnp.concatenate([p[:body_rows], diag], axis=0)
        return (j0 - 1, jnp.sum(p, axis=0, keepdims=True), pv(p.astype(BF16), j0, nblk))

    j, l, acc = window(nblk)

    def cond(c):
        j, go, _, _ = c
        return jnp.logical_and(j >= 0, go)

    def body(c):
        j, _, l, acc = c
        p = jnp.exp2(scores(j, 1) + shift)
        return (j - 1, reaches(j - 1), l + jnp.sum(p, axis=0, keepdims=True),
                acc + pv(p.astype(BF16), j, 1))

    def exact():
        def step(t, state):
            jj = i - 1 - t
            return _fox_update_t(scores(jj, 1), ct, lambda p: pv(p, jj, 1), state, None)

        init = (jnp.full((1, 2 * tq), NEG_INF, F32), jnp.zeros((1, 2 * tq), F32),
                jnp.zeros((D_HEAD, 2 * tq), F32))
        state = _fox_update_t(scores(i, 1), ct, lambda p: pv(p, i, 1), init, causal)
        _, l_x, acc_x = lax.fori_loop(0, i, step, state)
        return l_x, acc_x

    def beyond_window():
        _, _, l_w, acc_w = lax.while_loop(cond, body, (j, reaches(j), l, acc))
        return lax.cond(jnp.min(l_w) < FOX_L_FLOOR, exact, lambda: (l_w, acc_w))

    b = pl.program_id(0)
    uncommon = flags_ref[(b * pl.num_programs(2) + i) * N_PAIR + hp] != 0
    l, acc = lax.cond(uncommon, beyond_window, lambda: (l, acc))
    o = acc / l
    o_ref[0] = jnp.concatenate([o[:, :tq], o[:, tq:]], axis=0)


def _sb_prompt_kernel(*refs, window_blocks):
    lax.switch(jnp.minimum(pl.program_id(2), window_blocks - 1),
               [functools.partial(_sb_prompt_body, n, *refs) for n in range(1, window_blocks + 1)])


def _sb_prompt_body(nblk, qt_ref, k_ref, vt_ref, o_ref):
    tq = qt_ref.shape[3]
    tk = vt_ref.shape[3]
    i = pl.program_id(2)
    q2 = jnp.concatenate(_split_head_rows(qt_ref[0, 0]), axis=1)
    ts = SB_SUFFIX_BLOCK
    sub_ones = _tri_ones(ts, "strict_upper")

    def block(j, state, valid):
        kb = k_ref[pl.ds(pl.multiple_of(j * tk, tk), tk), :]
        vt = vt_ref[0, j]

        def pv(a):
            return jnp.concatenate([_dot(vt[:D_HEAD], a[:, :tq]), _dot(vt[D_HEAD:], a[:, tq:])],
                                   axis=1)

        return _sb_update_t(_dot(kb, q2), pv, _tri_ones(tk, "strict_upper"), state, valid)

    def still_open(carry):
        return jnp.min(carry) < SB_CLOSED * LOG2E

    key = lax.broadcasted_iota(jnp.int32, (tk, 2 * tq), 0)
    qry = lax.broadcasted_iota(jnp.int32, (tk, 2 * tq), 1) % tq
    causal = key < qry

    def window(nblk):
        j0 = i - (nblk - 1)
        z = _dot(k_ref[pl.ds(pl.multiple_of(j0 * tk, tk), nblk * tk), :], q2)
        r = _softplus2(z)
        body_rows = (nblk - 1) * tk
        diag = jnp.where(causal, r[body_rows:], 0.0)
        r = diag if nblk == 1 else jnp.concatenate([r[:body_rows], diag], axis=0)
        carry = jnp.zeros((1, 2 * tq), F32)
        n_sub = nblk * tk // ts
        a_blocks = [None] * n_sub
        for w in reversed(range(n_sub)):
            rows = slice(w * ts, (w + 1) * ts)
            r_w = r[rows]
            after = _dot(sub_ones, r_w.astype(BF16)) + carry
            a_blocks[w] = jnp.exp2((z[rows] - r_w) - after)
            carry = carry + jnp.sum(r_w, axis=0, keepdims=True)
        a = jnp.concatenate(a_blocks, axis=0)
        diag = jnp.where(causal, a[body_rows:], 0.0)
        a = (diag if nblk == 1 else jnp.concatenate([a[:body_rows], diag], axis=0)).astype(BF16)
        vts = [vt_ref[0, j0 + w] for w in range(nblk)]
        outs = []
        for h in range(2):
            vth = jnp.concatenate([v[h * D_HEAD:(h + 1) * D_HEAD] for v in vts], axis=1)
            outs.append(_dot(vth, a[:, h * tq:(h + 1) * tq]))
        return j0 - 1, still_open(carry), (carry, jnp.concatenate(outs, axis=1))

    j, go, state = window(nblk)

    def cond(c):
        j, go, _ = c
        return jnp.logical_and(j >= 0, go)

    def body(c):
        j, _, st = c
        st = block(j, st, None)
        return j - 1, still_open(st[0]), st

    _, _, (_, acc) = lax.while_loop(cond, body, (j, go, state))
    o_ref[0] = jnp.concatenate([acc[:, :tq], acc[:, tq:]], axis=0)


def _prompt_attention(qt, k, vtb, group, *, batch, seq, crow=None, ccolp=None, shiftrow=None,
                      flags=None):
    tq = tk = ATT_TILE
    nq = seq // tq
    col0 = group * N_PAIR
    qt_spec = pl.BlockSpec((1, 1, LANES, tq), lambda b, hp, i, *_: (b, i, col0 + hp, 0))
    k_spec = pl.BlockSpec((seq, LANES), lambda b, hp, i, *_: (b, col0 + hp))
    vt_spec = pl.BlockSpec((1, seq // tk, LANES, tk), lambda b, hp, i, *_: (b, 0, hp, 0))
    out_spec = pl.BlockSpec((1, LANES, tq), lambda b, hp, i, *_: (b * nq + i, hp, 0))
    out_shape = jax.ShapeDtypeStruct((batch * nq, W_ATT, tq), F32)
    params = _params(("parallel", "parallel", "arbitrary"), 40)
    grid = (batch, N_PAIR, nq)
    if group == 1:
        return pl.pallas_call(
            functools.partial(_sb_prompt_kernel, window_blocks=SB_WINDOW), grid=grid,
            name="sb_prompt",
            in_specs=[qt_spec, k_spec, vt_spec], out_specs=out_spec, out_shape=out_shape,
            compiler_params=params)(qt, k, vtb)
    crow_spec = pl.BlockSpec((seq // tk, SUBLANES, tk), lambda b, hp, i, *_: (b, 0, 0))
    ccolp_spec = pl.BlockSpec((seq, LANES), lambda b, hp, i, *_: (b, hp))
    grid_spec = pltpu.PrefetchScalarGridSpec(
        num_scalar_prefetch=1, grid=grid,
        in_specs=[qt_spec, k_spec, vt_spec, crow_spec, ccolp_spec, crow_spec],
        out_specs=out_spec)
    return pl.pallas_call(
        functools.partial(_fox_prompt_kernel, window_blocks=FOX_WINDOW), grid_spec=grid_spec,
        name="fox_prompt", out_shape=out_shape,
        compiler_params=params)(flags, qt, k, vtb, crow, ccolp, shiftrow)


def _suffix_sum_kernel(x_ref, o_ref):
    rows, width = x_ref.shape
    ones = _tri_ones(LANES, "strict_lower")
    carry = jnp.zeros((rows, 1), F32)
    for kb in reversed(range(width // LANES)):
        xb = x_ref[:, kb * LANES:(kb + 1) * LANES]
        o_ref[:, kb * LANES:(kb + 1) * LANES] = _ones_dot3(xb, ones, left=False) + carry
        carry = carry + jnp.sum(xb, axis=1, keepdims=True)


def _suffix_sum(x):
    return pl.pallas_call(
        _suffix_sum_kernel, name="cache_suffix_sum",
        out_shape=jax.ShapeDtypeStruct(x.shape, F32),
        compiler_params=_params(None, 32))(x)


def _split_heads(q2):
    lane = lax.broadcasted_iota(jnp.int32, q2.shape, 1)
    low = lane < D_HEAD
    zero = jnp.zeros_like(q2)
    return jnp.where(low, q2, zero), jnp.where(low, zero, q2)


def _stack_heads(q2):
    return jnp.concatenate(_split_heads(q2), axis=0)


def _unstack_heads(o):
    half = o.shape[0] // 2
    lane = lax.broadcasted_iota(jnp.int32, (half, o.shape[1]), 1)
    return jnp.where(lane < D_HEAD, o[:half], o[half:])


def _sb_update(z, pv, suffix_ones, state, valid):
    carry, acc = state
    r = _softplus(z)
    if valid is not None:
        r = jnp.where(valid, r, 0.0)
    after = _dot(r.astype(BF16), suffix_ones) + carry
    a = jnp.exp((z - r) - after)
    if valid is not None:
        a = jnp.where(valid, a, 0.0)
    acc = acc + pv(a.astype(BF16))
    carry = carry + jnp.sum(r, axis=1, keepdims=True)
    return carry, acc


def _sample_attn_kernel(qf_ref, kfn_ref, vfn_ref, kfc_ref, vfc_ref, drow_ref, cnrow_ref, ccol_ref,
                        qs_ref, ksn_ref, vsn_ref, ksc_last_ref, vsc_last_ref, ksc_hbm_ref,
                        vsc_hbm_ref, of_ref, os_ref,
                        carry_ref, acc_ref, kbuf_ref, vbuf_ref, sem_ref, *, blk, layer):
    b = pl.program_id(0)
    hp = pl.program_id(1)
    tq = qf_ref.shape[0]
    past = kfc_ref.shape[3]
    row = lax.broadcasted_iota(jnp.int32, (2 * tq, tq), 0) % tq
    col = lax.broadcasted_iota(jnp.int32, (2 * tq, tq), 1)

    def per_head(fn):
        return jnp.concatenate([fn(0), fn(1)], axis=0)

    q2 = _stack_heads(qf_ref[...])
    kct = kfc_ref[0, 0].astype(BF16)
    vct = vfc_ref[0, 0].astype(BF16)
    cc = ccol_ref[...]
    cn = cnrow_ref[0]
    cq = per_head(lambda h: _pick_lane(cc, 2 * hp + h))
    qk_c = _dot(q2, kct)
    qk_n = _dot_nt(q2, kfn_ref[...])
    s_c = per_head(lambda h: qk_c[h * tq:(h + 1) * tq]
                   + _pick_row(drow_ref[0, 2 * hp + h], b % SUBLANES))
    s_n = per_head(lambda h: qk_n[h * tq:(h + 1) * tq] - _pick_row(cn, 2 * hp + h))
    s_n = jnp.where(col <= row, s_n, NEG_INF)
    m = jnp.maximum(jnp.max(s_c, axis=1, keepdims=True), jnp.max(s_n, axis=1, keepdims=True)) + cq
    shift = cq - m
    p_c = jnp.exp(s_c + shift)
    p_n = jnp.exp(s_n + shift)
    l = jnp.sum(p_c, axis=1, keepdims=True) + jnp.sum(p_n, axis=1, keepdims=True)
    o = _dot_nt(p_c.astype(BF16), vct) + _dot(p_n.astype(BF16), vfn_ref[...])
    of_ref[...] = _unstack_heads(o / l)

    q2 = _stack_heads(qs_ref[...])
    vn = vsn_ref[...]
    suffix_ones = _tri_ones(blk, "strict_lower")

    def cache_block(kt_f32, vt_f32, state):
        kt = kt_f32.astype(BF16)
        vt = vt_f32.astype(BF16)
        return _sb_update(_dot(q2, kt), lambda a: _dot_nt(a, vt), suffix_ones, state, None)

    n_blocks = past // blk
    state = (jnp.zeros((2 * tq, 1), F32), jnp.zeros((2 * tq, LANES), F32))
    state = _sb_update(_dot_nt(q2, ksn_ref[...]), lambda a: _dot(a, vn),
                       _tri_ones(tq, "strict_lower"), state, col < row)
    carry_ref[...], acc_ref[...] = cache_block(ksc_last_ref[0, 0], vsc_last_ref[0, 0], state)

    def still_open():
        return jnp.min(carry_ref[...]) < SB_CLOSED

    def fetch(src_ref, jb, buf_ref, sem):
        rows = pl.ds(pl.multiple_of(hp * LANES, LANES), LANES)
        return pltpu.make_async_copy(src_ref.at[layer, b, rows, pl.ds(jb * blk, blk)], buf_ref, sem)

    @pl.when(still_open())
    def _():
        for jb in reversed(range(n_blocks - 1)):
            @pl.when(still_open())
            def _():
                copies = (fetch(ksc_hbm_ref, jb, kbuf_ref, sem_ref.at[0]),
                          fetch(vsc_hbm_ref, jb, vbuf_ref, sem_ref.at[1]))
                for cp in copies:
                    cp.start()
                for cp in copies:
                    cp.wait()
                carry_ref[...], acc_ref[...] = cache_block(
                    kbuf_ref[...], vbuf_ref[...], (carry_ref[...], acc_ref[...]))

    os_ref[...] = _unstack_heads(acc_ref[...])


def _sample_attention(q, kb, vb, cache_fk, cache_fv, drow, cnrow, ccol, cache_sk, cache_sv,
                      *, layer, streams, dec_seq):
    past = cache_fk.shape[3]
    new_spec = lambda g: pl.BlockSpec((dec_seq, LANES), lambda b, hp: (b, g * N_PAIR + hp))
    cache_spec = pl.BlockSpec((1, 1, LANES, past), lambda b, hp: (layer, b, hp, 0))
    drow_spec = pl.BlockSpec((1, H_FOX, SUBLANES, past), lambda b, hp: (layer, 0, b // SUBLANES, 0))
    cnrow_spec = pl.BlockSpec((1, SUBLANES, dec_seq), lambda b, hp: (b, 0, 0))
    ccol_spec = pl.BlockSpec((dec_seq, LANES), lambda b, hp: (b, 0))
    out_spec = pl.BlockSpec((dec_seq, LANES), lambda b, hp: (b, hp))
    out_shape = [jax.ShapeDtypeStruct((streams * dec_seq, W_ATT), F32)] * 2
    blk = ATT_TILE
    newest_spec = pl.BlockSpec((1, 1, LANES, blk), lambda b, hp: (layer, b, hp, past // blk - 1))
    hbm_spec = pl.BlockSpec(memory_space=pl.ANY)
    return pl.pallas_call(
        functools.partial(_sample_attn_kernel, blk=blk, layer=layer),
        grid=(streams, N_PAIR), name="sample_attention",
        in_specs=[new_spec(0), new_spec(0), new_spec(0), cache_spec, cache_spec, drow_spec,
                  cnrow_spec, ccol_spec, new_spec(1), new_spec(1), new_spec(1), newest_spec,
                  newest_spec, hbm_spec, hbm_spec],
        out_specs=[out_spec, out_spec], out_shape=out_shape,
        scratch_shapes=[pltpu.VMEM((2 * dec_seq, 1), F32), pltpu.VMEM((2 * dec_seq, LANES), F32),
                        pltpu.VMEM((LANES, blk), F32), pltpu.VMEM((LANES, blk), F32),
                        pltpu.SemaphoreType.DMA((2,))],
        compiler_params=_params(("parallel", "parallel"), 48),
    )(q, kb, vb, cache_fk, cache_fv, drow, cnrow, ccol, q, kb, vb, cache_sk, cache_sv,
      cache_sk, cache_sv)


def _merge_tile(x_ref, of_ref, os_ref, u_ref, vn_ref, ws_ref, bs_ref, gmix_ref, wout_ref,
                g_ref, b_ref, *, alpha):
    tm = x_ref.shape[0]
    r = lax.broadcasted_iota(jnp.int32, (SGU_CHUNK, SGU_CHUNK), 0)
    c = lax.broadcasted_iota(jnp.int32, (SGU_CHUNK, SGU_CHUNK), 1)
    tril = r >= c
    lane = lax.broadcasted_iota(jnp.int32, (SGU_CHUNK, LANES), 1)

    gate_pieces = []
    for p in range(W_SGU // LANES):
        w0 = jnp.where(tril, ws_ref[2 * p], 0.0).astype(BF16)
        w1 = jnp.where(tril, ws_ref[2 * p + 1], 0.0).astype(BF16)
        bias = bs_ref[:, p * LANES:(p + 1) * LANES]
        chunks = []
        for ch in range(tm // SGU_CHUNK):
            rows = slice(ch * SGU_CHUNK, (ch + 1) * SGU_CHUNK)
            vp = vn_ref[rows, p * LANES:(p + 1) * LANES].astype(BF16)
            s = jnp.where(lane < D_HEAD, _dot(w0, vp), _dot(w1, vp)) + bias
            chunks.append(u_ref[rows, p * LANES:(p + 1) * LANES] * s)
        gate_pieces.append(jnp.concatenate(chunks, axis=0))

    def attention_pieces(ref):
        if ref.ndim == 2:
            return [ref[:, k * LANES:(k + 1) * LANES] for k in range(N_PAIR)]
        return [jnp.concatenate([ref[qb, k * LANES:(k + 1) * LANES, :].T
                                 for qb in range(ref.shape[0])], axis=0) for k in range(N_PAIR)]

    pieces = attention_pieces(of_ref) + gate_pieces + attention_pieces(os_ref)
    wide = 2 * LANES
    r2 = lax.broadcasted_iota(jnp.int32, (wide, wide), 0)
    c2 = lax.broadcasted_iota(jnp.int32, (wide, wide), 1)
    head_mean = jnp.where(r2 // D_HEAD == c2 // D_HEAD, 1.0 / D_HEAD, 0.0).astype(BF16)
    normed = []
    for k in range(0, len(pieces), 2):
        o = jnp.concatenate(pieces[k:k + 2], axis=1)
        ms = _dot((o * o).astype(BF16), head_mean)
        gain = gmix_ref[:, k * LANES:(k + 2) * LANES]
        normed.append((o * lax.rsqrt(ms + RMS_EPS) * gain).astype(BF16))
    mix = _dot(jnp.concatenate(normed, axis=1), wout_ref[0])
    return _layer_norm(alpha * x_ref[...] + mix, g_ref[...], b_ref[...])


def _mlp_tile(x, wup_ref, wdn_ref, g_ref, b_ref, *, alpha, ff_chunk):
    xb = x.astype(BF16)
    acc = alpha * x
    for c0 in range(0, wup_ref.shape[2], ff_chunk):
        h = jnp.maximum(_dot(xb, wup_ref[0, :, c0:c0 + ff_chunk]), 0.0)
        acc = acc + _dot((h * h).astype(BF16), wdn_ref[0, c0:c0 + ff_chunk, :])
    return _layer_norm(acc, g_ref[...], b_ref[...])


def _merge_mlp_kernel(x_ref, of_ref, os_ref, u_ref, vn_ref, ws_ref, bs_ref, gmix_ref, wout_ref,
                      g1_ref, b1_ref, wup_ref, wdn_ref, g2_ref, b2_ref, o_ref, *, alpha, ff_chunk):
    x1 = _merge_tile(x_ref, of_ref, os_ref, u_ref, vn_ref, ws_ref, bs_ref, gmix_ref, wout_ref,
                     g1_ref, b1_ref, alpha=alpha)
    o_ref[...] = _mlp_tile(x1, wup_ref, wdn_ref, g2_ref, b2_ref, alpha=alpha, ff_chunk=ff_chunk)


def _merge_mlp(x, of, os_, u, vn, ws, bs, gmix, wout, g1, b1, wup, wdn, g2, b2, *, alpha, name,
               layer):
    n, d = x.shape
    tm = ROW_TILE
    row = lambda w: pl.BlockSpec((tm, w), lambda i: (i, 0))
    if of.ndim == 2:
        att = row(W_ATT)
    else:
        tq = of.shape[2]
        att = pl.BlockSpec((tm // tq, W_ATT, tq), lambda i: (i, 0, 0))
    weights = (ws, bs, gmix, wout, g1, b1, wup, wdn, g2, b2)

    def weight_spec(a):
        if a is wout or a is wup or a is wdn:
            return pl.BlockSpec((1,) + a.shape[1:], lambda i: (layer, 0, 0),
                                pipeline_mode=pl.Buffered(1))
        return _resident(a.shape)

    return pl.pallas_call(
        functools.partial(_merge_mlp_kernel, alpha=alpha, ff_chunk=MLP_FF_CHUNK), grid=(n // tm,),
        name=name,
        in_specs=[row(d), att, att, row(W_SGU), row(W_SGU)] + [weight_spec(a) for a in weights],
        out_specs=row(d), out_shape=jax.ShapeDtypeStruct((n, d), F32),
        compiler_params=_params(("parallel",), 60),
    )(x, of, os_, u, vn, *weights)


def _pack_w_in(w, b_f):
    sizes = [W_ATT, W_ATT, W_ATT, H_FOX, W_SGU, W_SGU, W_ATT, W_ATT, W_ATT]
    offs = [0]
    for s in sizes:
        offs.append(offs[-1] + s)
    q_f, k_f, v_f, f_lg, u_g, v_g, q_s, k_s, v_s = (w[:, offs[i]:offs[i + 1]] for i in range(9))
    pad_cols = lambda a, width: jnp.pad(a, ((0, 0), (0, width - a.shape[1])))
    w_rows = jnp.concatenate([q_f, q_s, k_f, k_s, v_f, v_s, u_g, v_g, pad_cols(f_lg, LANES)],
                             axis=1).astype(BF16)
    w_cols = jnp.concatenate([q_f, q_s, k_f, k_s, v_f, v_s, pad_cols(f_lg, F_ROWS)],
                             axis=1).T.astype(BF16)
    bfc = jnp.pad(b_f, (0, LANES - H_FOX)).reshape(1, LANES)
    bfr = jnp.broadcast_to(jnp.pad(b_f, (0, SUBLANES - H_FOX)).reshape(SUBLANES, 1),
                           (SUBLANES, LANES))
    return w_rows, w_cols, bfc, bfr


def kernel(x_prompt, x_sample, cache_fox_k, cache_fox_v, cache_fox_logf, cache_sb_k, cache_sb_v,
           w_in, b_f, g_v, b_v, w_s, b_s, g_mix, w_out, ln1_g, ln1_b, w_up, w_down, ln2_g, ln2_b):
    depth = w_in.shape[0]
    batch, seq, d_model = x_prompt.shape
    streams, dec_seq, _ = x_sample.shape
    past = cache_fox_k.shape[2]
    n_prompt = batch * seq
    n_sample = streams * dec_seq
    alpha = (2 * depth) ** 0.25
    assert seq % ROW_TILE == 0 and n_sample % ROW_TILE == 0 and ROW_TILE % dec_seq == 0
    assert dec_seq == D_HEAD and 2 * dec_seq == SGU_CHUNK and past % ATT_TILE == 0
    assert streams % SUBLANES == 0

    xp = x_prompt.reshape(n_prompt, d_model)
    xs = x_sample.reshape(n_sample, d_model)
    row2 = lambda a: a.reshape(1, -1)
    cache_t = lambda c: jnp.transpose(c, (0, 1, 3, 4, 2)).reshape(depth, streams, W_ATT, past)
    ck_f, cv_f, ck_s, cv_s = (cache_t(c) for c in (cache_fox_k, cache_fox_v, cache_sb_k, cache_sb_v))
    lf_cache = jnp.transpose(cache_fox_logf.astype(F32), (0, 3, 1, 2))
    drow = _suffix_sum(lf_cache.reshape(depth * H_FOX * streams, past)).reshape(
        depth, H_FOX, streams, past)

    wout, wup, wdn = w_out.astype(BF16), w_up.astype(BF16), w_down.astype(BF16)
    p_rows = s_rows = None
    p_logf, s_logf, s_gate = [], [], []
    for l in range(depth):
        w_rows, w_cols, bfc, bfr = _pack_w_in(w_in[l], b_f[l])
        gv, bv = row2(g_v[l]), row2(b_v[l])
        gmix, g1, b1, g2, b2 = (row2(a[l]) for a in (g_mix, ln1_g, ln1_b, ln2_g, ln2_b))

        w_l = w_s[l]
        half = w_l[:, :dec_seq, :dec_seq]
        zeros = jnp.zeros_like(half)
        w_blockdiag = jnp.concatenate([jnp.concatenate([half, zeros], axis=2),
                                       jnp.concatenate([zeros, half], axis=2)], axis=1)
        bias_full = jnp.repeat(b_s[l].T, D_HEAD, axis=1)
        bias_half = jnp.concatenate([bias_full[:dec_seq], bias_full[:dec_seq]], axis=0)

        (qt, kb, *p_rows, vtbf, vtbs, lfrow, crow, ccolp, u, vn, shiftrow,
         head_flags) = _in_proj_prompt(
            xp, w_rows, w_cols, bfr, gv, bv, batch=batch, seq=seq, layer=l, depth=depth,
            cache_rows=p_rows)
        pair_flags = jnp.max(head_flags[:, :H_FOX, 0].reshape(-1, N_PAIR, 2), axis=2).reshape(-1)
        of = _prompt_attention(qt, kb, vtbf, 0, batch=batch, seq=seq, crow=crow, ccolp=ccolp,
                               shiftrow=shiftrow, flags=pair_flags)
        os_ = _prompt_attention(qt, kb, vtbs, 1, batch=batch, seq=seq)
        xp = _merge_mlp(xp, of, os_, u, vn, w_l, bias_full, gmix, wout, g1, b1, wup, wdn, g2, b2,
                        alpha=alpha, name="merge_mlp_prompt", layer=l)
        p_logf.append(jnp.transpose(lfrow[:H_FOX].reshape(H_FOX, batch, seq), (1, 2, 0)))

        (q, kb, vb, *s_rows, lfcol, ccol, cnrow, u, vn) = _in_proj_sample(
            xs, w_rows, w_cols, bfc, bfr, gv, bv, dec_seq=dec_seq, layer=l, depth=depth,
            cache_rows=s_rows)
        of, os_ = _sample_attention(q, kb, vb, ck_f, cv_f, drow, cnrow, ccol, ck_s, cv_s,
                                    layer=l, streams=streams, dec_seq=dec_seq)
        xs = _merge_mlp(xs, of, os_, u, vn, w_blockdiag, bias_half, gmix, wout, g1, b1, wup, wdn,
                        g2, b2, alpha=alpha, name="merge_mlp_sample", layer=l)
        s_logf.append(lfcol[:, :H_FOX].reshape(streams, dec_seq, H_FOX))
        s_gate.append(vn.reshape(streams, dec_seq, W_SGU))

    p_kf, p_ks, p_vf, p_vs = (jnp.transpose(a.reshape(depth, batch, H_FOX, D_HEAD, seq),
                                            (0, 1, 4, 2, 3)) for a in p_rows)
    s_kf, s_ks, s_vf, s_vs = (jnp.transpose(a, (0, 1, 3, 2, 4)) for a in s_rows)
    return (xp.reshape(batch, seq, d_model), xs.reshape(streams, dec_seq, d_model),
            p_kf, p_vf, jnp.stack(p_logf), p_ks, p_vs,
            s_kf, s_vf, jnp.stack(s_logf), s_ks, s_vs, jnp.stack(s_gate))
```

```python
import functools
import math

import jax
import jax.numpy as jnp
from jax import lax
from jax.experimental import pallas as pl
from jax.experimental.pallas import tpu as pltpu

F32 = jnp.float32
BF16 = jnp.bfloat16

LANES = 128
SUBLANES = 8
VMEM_BYTES_V7X = 64 * 1024 * 1024

D_HEAD = 64
H_FOX = 6
H_SGU = 4
W_ATT = H_FOX * D_HEAD
W_SGU = H_SGU * D_HEAD
N_PAIR = W_ATT // LANES
SGU_CHUNK = 128
LN_EPS = 1e-5
RMS_EPS = 1e-6
NEG_INF = -1e30
SB_CLOSED = 110.0
FOX_DEAD = 110.0
LOG2E = math.log2(math.e)
FOX_L_FLOOR = 1e-25
FOX_BOUND_SAFE = 41.0
FOX_WINDOW = 5
SB_WINDOW = 2
SB_SUFFIX_BLOCK = 128

ROW_TILE = 512
ATT_TILE = 256
MLP_FF_CHUNK = 1024

_C_Q = 0
_C_K = 2 * W_ATT
_C_V = 4 * W_ATT
_C_G = 6 * W_ATT
_C_F = _C_G + 2 * W_SGU
W_ROWS = _C_F + LANES
_R_Q = 0
_R_K = 2 * W_ATT
_R_V = 4 * W_ATT
_R_F = 6 * W_ATT
F_ROWS = 2 * SUBLANES
W_COLS = _R_F + F_ROWS


def _params(semantics, vmem_mib, flags=None):
    assert vmem_mib * 1024 * 1024 < VMEM_BYTES_V7X
    return pltpu.CompilerParams(dimension_semantics=semantics,
                                vmem_limit_bytes=vmem_mib * 1024 * 1024, flags=flags)


def _resident(shape):
    nd = len(shape)
    return pl.BlockSpec(shape, lambda *_: (0,) * nd, pipeline_mode=pl.Buffered(1))


def _split3(x):
    hi = x.astype(BF16)
    r1 = x - hi.astype(F32)
    mid = r1.astype(BF16)
    lo = (r1 - mid.astype(F32)).astype(BF16)
    return hi, mid, lo


def _dot(a, b):
    return jnp.dot(a, b, preferred_element_type=F32)


def _dot_nt(a, b):
    return lax.dot_general(a, b, (((1,), (1,)), ((), ())), preferred_element_type=F32)


def _lane_cumsum(x, seg=None):
    n = x.shape[1]
    lane = lax.broadcasted_iota(jnp.int32, x.shape, 1)
    pos = lane if seg is None else lane % seg
    step = 1
    while step < (n if seg is None else seg):
        x = x + jnp.where(pos >= step, pltpu.roll(x, step, axis=1), 0.0)
        step *= 2
    return x


def _ones_dot3(x, ones_mat, left):
    if left:
        return sum(_dot(ones_mat, p) for p in _split3(x))
    return sum(_dot(p, ones_mat) for p in _split3(x))


def _gelu_tanh(x):
    return 0.5 * x * (1.0 + jnp.tanh(0.7978845608028654 * (x + 0.044715 * (x * x * x))))


def _log_sigmoid(x):
    return jnp.minimum(x, 0.0) - jnp.log1p(jnp.exp(-jnp.abs(x)))


def _softplus(z):
    return jnp.maximum(z, 0.0) + jnp.log(1.0 + jnp.exp(-jnp.abs(z)))


def _softplus2(z2):
    return jnp.maximum(z2, 0.0) + jnp.log(1.0 + jnp.exp2(-jnp.abs(z2))) * LOG2E


def _layer_norm(x, g, b):
    mu = jnp.mean(x, axis=-1, keepdims=True)
    xc = x - mu
    var = jnp.mean(xc * xc, axis=-1, keepdims=True)
    return xc * lax.rsqrt(var + LN_EPS) * g + b


def _tri_ones(n, kind, seg=None):
    r = lax.broadcasted_iota(jnp.int32, (n, n), 0)
    c = lax.broadcasted_iota(jnp.int32, (n, n), 1)
    keep = {"lower": c <= r, "upper": r <= c, "strict_lower": r > c, "strict_upper": r < c}[kind]
    if seg is not None:
        keep = jnp.logical_and(keep, r // seg == c // seg)
    return jnp.where(keep, 1.0, 0.0).astype(BF16)


def _pick_lane(x, idx):
    lane = lax.broadcasted_iota(jnp.int32, x.shape, 1)
    return jnp.sum(jnp.where(lane == idx, x, 0.0), axis=1, keepdims=True)


def _pick_row(x, idx):
    sub = lax.broadcasted_iota(jnp.int32, x.shape, 0)
    return jnp.sum(jnp.where(sub == idx, x, 0.0), axis=0, keepdims=True)


def _gates(xb, wr_ref, gv_ref, bv_ref, u_ref, vn_ref):
    u_ref[...] = _gelu_tanh(_dot(xb, wr_ref[:, _C_G:_C_G + W_SGU]))
    vg = _dot(xb, wr_ref[:, _C_G + W_SGU:_C_G + 2 * W_SGU])
    vn_ref[...] = _layer_norm(_gelu_tanh(vg), gv_ref[...], bv_ref[...])


def _log_forget_rows(xb, wc_ref, bfr_ref):
    sub = lax.broadcasted_iota(jnp.int32, (SUBLANES, xb.shape[0]), 0)
    z = _dot_nt(wc_ref[_R_F:_R_F + F_ROWS, :], xb)[:SUBLANES] + bfr_ref[:, 0:1]
    return jnp.where(sub < H_FOX, _log_sigmoid(z), 0.0)


def _in_proj_prompt_kernel(*refs, layer, depth):
    n_in = 6 if layer == 0 else 10
    x_ref, wr_ref, wc_ref, bfr_ref, gv_ref, bv_ref = refs[:6]
    (qt_ref, k_ref, ktf_ref, kts_ref, vtf_ref, vts_ref, vtbf_ref, vtbs_ref,
     lfrow_ref, crow_ref, ccolp_ref, u_ref, vn_ref, shiftrow_ref, flags_ref,
     carry_row_ref, knmax_ref, cend_ref) = refs[n_in:]
    i = pl.program_id(1)
    tm = x_ref.shape[0]
    tk = vtbf_ref.shape[3]

    @pl.when(i == 0)
    def _():
        carry_row_ref[...] = jnp.zeros_like(carry_row_ref)
        knmax_ref[...] = jnp.zeros_like(knmax_ref)

    xb = x_ref[...].astype(BF16)

    def blocks(zt):
        return [zt[:, kb * tk:(kb + 1) * tk].astype(BF16) for kb in range(tm // tk)]

    def store_cache_rows(ref, zt):
        if layer == 0:
            for l in range(depth):
                ref[l, 0] = zt if l == 0 else jnp.zeros_like(zt)
        else:
            ref[0, 0] = zt

    scale = LOG2E / math.sqrt(D_HEAD)
    qn2_rows = []
    for kb, blk in enumerate(blocks(_dot_nt(wc_ref[_R_Q:_R_Q + 2 * W_ATT, :], xb) * scale)):
        qt_ref[0, kb] = blk
        qf = blk[:W_ATT].astype(F32)
        qn2 = jnp.sum((qf * qf).reshape(H_FOX, D_HEAD, tk), axis=1)
        qn2_rows.append(jnp.concatenate([qn2, jnp.zeros((SUBLANES - H_FOX, tk), F32)], axis=0))
    for r0, full_ref, blk_ref in ((_R_V, vtf_ref, vtbf_ref), (_R_V + W_ATT, vts_ref, vtbs_ref)):
        zt = _dot_nt(wc_ref[r0:r0 + W_ATT, :], xb)
        store_cache_rows(full_ref, zt)
        for kb, blk in enumerate(blocks(zt)):
            blk_ref[0, kb] = blk
    k_rows = _dot(xb, wr_ref[:, _C_K:_C_K + 2 * W_ATT])
    k_ref[...] = k_rows.astype(BF16)
    kt_f = k_rows[:, :W_ATT].T
    store_cache_rows(ktf_ref, kt_f)
    store_cache_rows(kts_ref, k_rows[:, W_ATT:].T)

    _gates(xb, wr_ref, gv_ref, bv_ref, u_ref, vn_ref)

    kn2 = jnp.sum((kt_f * kt_f).reshape(H_FOX, D_HEAD, tm), axis=1)
    kn2 = jnp.broadcast_to(jnp.max(kn2, axis=1, keepdims=True), (H_FOX, LANES))
    kn2 = jnp.concatenate([kn2, jnp.zeros((SUBLANES - H_FOX, LANES), F32)], axis=0)
    knmax_ref[...] = jnp.maximum(knmax_ref[...], kn2)

    lf_row = _log_forget_rows(xb, wc_ref, bfr_ref)
    lfrow_ref[...] = lf_row
    c_row = _ones_dot3(lf_row, _tri_ones(tm, "upper"), left=False) + carry_row_ref[:, 0:1]
    carry_row_ref[...] = jnp.broadcast_to(c_row[:, tm - 1:tm], carry_row_ref.shape)
    c_row = c_row * LOG2E
    kn_run = jnp.concatenate([knmax_ref[...]] * (tk // LANES), axis=1)
    for kb in range(tm // tk):
        c_blk = c_row[:, kb * tk:(kb + 1) * tk]
        crow_ref[kb] = c_blk
        bound = jnp.sqrt(qn2_rows[kb] * kn_run) * 1.02
        shiftrow_ref[kb] = c_blk - bound
        blk_id = i * (tm // tk) + kb
        cend_ref[blk_id] = jnp.broadcast_to(c_blk[:, tk - 1:tk], (SUBLANES, LANES))
        left = blk_id - FOX_WINDOW
        decay = c_blk[:, 0:1] - cend_ref[jnp.maximum(left, 0)][:, 0:1]
        live = jnp.logical_and(decay > -FOX_DEAD * LOG2E, left >= 0)
        risky = jnp.max(bound, axis=1, keepdims=True) > FOX_BOUND_SAFE
        flags_ref[kb] = jnp.broadcast_to(
            jnp.where(jnp.logical_or(live, risky), 1, 0).astype(jnp.int32), (SUBLANES, LANES))
    pad = jnp.zeros((LANES - SUBLANES, tm), F32)
    for p in range(N_PAIR):
        rows = c_row if p == 0 else jnp.concatenate([c_row[2 * p:], c_row[:2 * p]], axis=0)
        ccolp_ref[:, p * LANES:(p + 1) * LANES] = jnp.concatenate([rows, pad], axis=0).T


def _in_proj_prompt(x, w_rows, w_cols, bfr, gv, bv, *, batch, seq, layer, depth, cache_rows):
    n, d = x.shape
    tm, tk = ROW_TILE, ATT_TILE
    nt = seq // tm
    row = lambda w: pl.BlockSpec((tm, w), lambda b, i: (b * nt + i, 0))
    if layer == 0:
        t_spec = pl.BlockSpec((depth, 1, W_ATT, tm), lambda b, i: (0, b, 0, i))
        extra_in, extra_specs, aliases = (), [], {}
    else:
        t_spec = pl.BlockSpec((1, 1, W_ATT, tm), lambda b, i: (layer, b, 0, i))
        extra_in = tuple(cache_rows)
        extra_specs = [pl.BlockSpec(memory_space=pl.ANY)] * 4
        aliases = {6 + k: 2 + k for k in range(4)}
    t_shape = jax.ShapeDtypeStruct((depth, batch, W_ATT, seq), F32)
    blk_spec = lambda w: pl.BlockSpec((1, tm // tk, w, tk), lambda b, i: (b, i, 0, 0))
    blk_shape = lambda w: jax.ShapeDtypeStruct((batch, seq // tk, w, tk), BF16)
    out_shape = [blk_shape(2 * W_ATT), jax.ShapeDtypeStruct((n, 2 * W_ATT), BF16),
                 t_shape, t_shape, t_shape, t_shape, blk_shape(W_ATT), blk_shape(W_ATT),
                 jax.ShapeDtypeStruct((SUBLANES, n), F32),
                 jax.ShapeDtypeStruct((n // tk, SUBLANES, tk), F32),
                 jax.ShapeDtypeStruct((n, N_PAIR * LANES), F32),
                 jax.ShapeDtypeStruct((n, W_SGU), F32), jax.ShapeDtypeStruct((n, W_SGU), F32),
                 jax.ShapeDtypeStruct((n // tk, SUBLANES, tk), F32),
                 jax.ShapeDtypeStruct((n // tk, SUBLANES, LANES), jnp.int32)]
    out_specs = [blk_spec(2 * W_ATT), row(2 * W_ATT), t_spec, t_spec, t_spec, t_spec,
                 blk_spec(W_ATT), blk_spec(W_ATT),
                 pl.BlockSpec((SUBLANES, tm), lambda b, i: (0, b * nt + i)),
                 pl.BlockSpec((tm // tk, SUBLANES, tk), lambda b, i: (b * nt + i, 0, 0)),
                 row(N_PAIR * LANES), row(W_SGU), row(W_SGU),
                 pl.BlockSpec((tm // tk, SUBLANES, tk), lambda b, i: (b * nt + i, 0, 0)),
                 pl.BlockSpec((tm // tk, SUBLANES, LANES), lambda b, i: (b * nt + i, 0, 0))]
    return pl.pallas_call(
        functools.partial(_in_proj_prompt_kernel, layer=layer, depth=depth),
        grid=(batch, nt), name="in_proj_prompt",
        in_specs=[row(d)] + [_resident(a.shape) for a in (w_rows, w_cols, bfr, gv, bv)]
        + extra_specs,
        out_specs=out_specs, out_shape=out_shape, input_output_aliases=aliases,
        scratch_shapes=[pltpu.VMEM((SUBLANES, LANES), F32), pltpu.VMEM((SUBLANES, LANES), F32),
                        pltpu.VMEM((seq // tk, SUBLANES, LANES), F32)],
        compiler_params=_params(("parallel", "arbitrary"), 56),
    )(x, w_rows, w_cols, bfr, gv, bv, *extra_in)


def _in_proj_sample_kernel(*refs, dec_seq, layer, depth):
    n_in = 7 if layer == 0 else 11
    x_ref, wr_ref, wc_ref, bfc_ref, bfr_ref, gv_ref, bv_ref = refs[:7]
    (q_ref, kb_ref, vb_ref, kf_ref, ks_ref, vf_ref, vs_ref, lfcol_ref, ccol_ref, crow_ref,
     u_ref, vn_ref) = refs[n_in:]
    tm = x_ref.shape[0]
    xb = x_ref[...].astype(BF16)
    scale = 1.0 / math.sqrt(D_HEAD)
    q_ref[...] = (_dot(xb, wr_ref[:, _C_Q:_C_Q + 2 * W_ATT]) * scale).astype(BF16)

    def store_cache_rows(ref, z):
        for s in range(tm // dec_seq):
            for h in range(W_ATT // D_HEAD):
                blk = z[s * dec_seq:(s + 1) * dec_seq, h * D_HEAD:(h + 1) * D_HEAD]
                ref[0, s, h] = blk
                if layer == 0:
                    for l in range(1, depth):
                        ref[l, s, h] = jnp.zeros_like(blk)

    for c0, refs_fs, half_ref in ((_C_K, (kf_ref, ks_ref), kb_ref), (_C_V, (vf_ref, vs_ref), vb_ref)):
        z = _dot(xb, wr_ref[:, c0:c0 + 2 * W_ATT])
        half_ref[...] = z.astype(BF16)
        store_cache_rows(refs_fs[0], z[:, :W_ATT])
        store_cache_rows(refs_fs[1], z[:, W_ATT:])
    _gates(xb, wr_ref, gv_ref, bv_ref, u_ref, vn_ref)

    lane = lax.broadcasted_iota(jnp.int32, (tm, LANES), 1)
    z = _dot(xb, wr_ref[:, _C_F:_C_F + LANES]) + bfc_ref[...]
    lf_col = jnp.where(lane < H_FOX, _log_sigmoid(z), 0.0)
    lfcol_ref[...] = lf_col
    ccol_ref[...] = _ones_dot3(lf_col, _tri_ones(tm, "lower", dec_seq), left=True)
    lf_row = _log_forget_rows(xb, wc_ref, bfr_ref)
    c_row = _lane_cumsum(lf_row, dec_seq)
    for s in range(tm // dec_seq):
        crow_ref[s] = c_row[:, s * dec_seq:(s + 1) * dec_seq]


def _in_proj_sample(x, w_rows, w_cols, bfc, bfr, gv, bv, *, dec_seq, layer, depth, cache_rows):
    n, d = x.shape
    tm = ROW_TILE
    heads = W_ATT // D_HEAD
    row = lambda w: pl.BlockSpec((tm, w), lambda i: (i, 0))
    sds = lambda w, dt: jax.ShapeDtypeStruct((n, w), dt)
    if layer == 0:
        c_spec = pl.BlockSpec((depth, tm // dec_seq, heads, dec_seq, D_HEAD), lambda i: (0, i, 0, 0, 0))
        extra_in, extra_specs, aliases = (), [], {}
    else:
        c_spec = pl.BlockSpec((1, tm // dec_seq, heads, dec_seq, D_HEAD), lambda i: (layer, i, 0, 0, 0))
        extra_in = tuple(cache_rows)
        extra_specs = [pl.BlockSpec(memory_space=pl.ANY)] * 4
        aliases = {7 + k: 3 + k for k in range(4)}
    c_shape = jax.ShapeDtypeStruct((depth, n // dec_seq, heads, dec_seq, D_HEAD), F32)
    out_shape = [sds(2 * W_ATT, BF16)] * 3 + [c_shape] * 4 + [sds(LANES, F32)] * 2 + [
        jax.ShapeDtypeStruct((n // dec_seq, SUBLANES, dec_seq), F32), sds(W_SGU, F32), sds(W_SGU, F32)]
    out_specs = [row(2 * W_ATT)] * 3 + [c_spec] * 4 + [row(LANES)] * 2 + [
        pl.BlockSpec((tm // dec_seq, SUBLANES, dec_seq), lambda i: (i, 0, 0)), row(W_SGU), row(W_SGU)]
    return pl.pallas_call(
        functools.partial(_in_proj_sample_kernel, dec_seq=dec_seq, layer=layer, depth=depth),
        grid=(n // tm,), name="in_proj_sample",
        in_specs=[row(d)] + [_resident(a.shape) for a in (w_rows, w_cols, bfc, bfr, gv, bv)]
        + extra_specs,
        out_specs=out_specs, out_shape=out_shape, input_output_aliases=aliases,
        compiler_params=_params(("parallel",), 56),
    )(x, w_rows, w_cols, bfc, bfr, gv, bv, *extra_in)


def _split_head_rows(qt):
    sub = lax.broadcasted_iota(jnp.int32, qt.shape, 0)
    low = sub < D_HEAD
    zero = jnp.zeros_like(qt)
    return jnp.where(low, qt, zero), jnp.where(low, zero, qt)


def _fox_update_t(s, ct, pv, state, valid):
    m, l, acc = state
    if valid is not None:
        s = jnp.where(valid, s, NEG_INF)
    m_new = jnp.maximum(m, jnp.max(s, axis=0, keepdims=True) + ct)
    p = jnp.exp2(s + (ct - m_new))
    alpha = jnp.exp2(m - m_new)
    l = alpha * l + jnp.sum(p, axis=0, keepdims=True)
    acc = alpha * acc + pv(p.astype(BF16))
    return m_new, l, acc


def _sb_update_t(z, pv, suffix_ones, state, valid):
    carry, acc = state
    r = _softplus2(z)
    if valid is not None:
        r = jnp.where(valid, r, 0.0)
    r_hi = r.astype(BF16)
    r_lo = (r - r_hi.astype(F32)).astype(BF16)
    after = _dot(suffix_ones, r_hi) + _dot(suffix_ones, r_lo) + carry
    a = jnp.exp2((z - r) - after)
    if valid is not None:
        a = jnp.where(valid, a, 0.0)
    acc = acc + pv(a.astype(BF16))
    carry = carry + jnp.sum(r, axis=0, keepdims=True)
    return carry, acc


def _fox_prompt_kernel(*refs, window_blocks):
    lax.switch(jnp.minimum(pl.program_id(2), window_blocks - 1),
               [functools.partial(_fox_prompt_body, n, *refs) for n in range(1, window_blocks + 1)])


def _fox_prompt_body(nblk, flags_ref, qt_ref, k_ref, vt_ref, crow_ref, ccolp_ref, shiftrow_ref,
                     o_ref):
    tq = qt_ref.shape[3]
    tk = vt_ref.shape[3]
    hp = pl.program_id(1)
    i = pl.program_id(2)
    q2 = jnp.concatenate(_split_head_rows(qt_ref[0, 0]), axis=1)
    c_q = crow_ref[i]
    ct = jnp.concatenate([_pick_row(c_q, 2 * hp), _pick_row(c_q, 2 * hp + 1)], axis=1)

    key = lax.broadcasted_iota(jnp.int32, (tk, 2 * tq), 0)
    qry = lax.broadcasted_iota(jnp.int32, (tk, 2 * tq), 1) % tq
    causal = key <= qry

    def scores(j0, nblk):
        start = pl.multiple_of(j0 * tk, tk)
        kb = k_ref[pl.ds(start, nblk * tk), :]
        cc = ccolp_ref[pl.ds(start, nblk * tk), :]
        ck = jnp.concatenate([jnp.broadcast_to(cc[:, 0:1], (nblk * tk, tq)),
                              jnp.broadcast_to(cc[:, 1:2], (nblk * tk, tq))], axis=1)
        return _dot(kb, q2) - ck

    def pv(p, j0, nblk):
        vts = [vt_ref[0, j0 + w] for w in range(nblk)]
        outs = []
        for h in range(2):
            vth = jnp.concatenate([v[h * D_HEAD:(h + 1) * D_HEAD] for v in vts], axis=1)
            outs.append(_dot(vth, p[:, h * tq:(h + 1) * tq]))
        return jnp.concatenate(outs, axis=1)

    def pair_row(a, b):
        return jnp.concatenate([jnp.broadcast_to(a, (1, tq)), jnp.broadcast_to(b, (1, tq))], axis=1)

    sh = shiftrow_ref[i]
    shift = jnp.concatenate([_pick_row(sh, 2 * hp), _pick_row(sh, 2 * hp + 1)], axis=1)

    def reaches(j):
        return jnp.max(decay(j)) > -FOX_DEAD * LOG2E

    def decay(j):
        c_end = crow_ref[jnp.maximum(j, 0)][:, tk - 1:tk]
        return ct - pair_row(_pick_row(c_end, 2 * hp), _pick_row(c_end, 2 * hp + 1))

    def window(nblk):
        j0 = i - (nblk - 1)
        p = jnp.exp2(scores(j0, nblk) + shift)
        body_rows = (nblk - 1) * tk
        diag = jnp.where(causal, p[body_rows:], 0.0)
        p = diag if nblk == 1 else jnp.concatenate([p[:body_rows], diag], axis=0)
        return (j0 - 1, jnp.sum(p, axis=0, keepdims=True), pv(p.astype(BF16), j0, nblk))

    j, l, acc = window(nblk)

    def cond(c):
        j, go, _, _ = c
        return jnp.logical_and(j >= 0, go)

    def body(c):
        j, _, l, acc = c
        p = jnp.exp2(scores(j, 1) + shift)
        return (j - 1, reaches(j - 1), l + jnp.sum(p, axis=0, keepdims=True),
                acc + pv(p.astype(BF16), j, 1))

    def exact():
        def step(t, state):
            jj = i - 1 - t
            return _fox_update_t(scores(jj, 1), ct, lambda p: pv(p, jj, 1), state, None)

        init = (jnp.full((1, 2 * tq), NEG_INF, F32), jnp.zeros((1, 2 * tq), F32),
                jnp.zeros((D_HEAD, 2 * tq), F32))
        state = _fox_update_t(scores(i, 1), ct, lambda p: pv(p, i, 1), init, causal)
        _, l_x, acc_x = lax.fori_loop(0, i, step, state)
        return l_x, acc_x

    def beyond_window():
        _, _, l_w, acc_w = lax.while_loop(cond, body, (j, reaches(j), l, acc))
        return lax.cond(jnp.min(l_w) < FOX_L_FLOOR, exact, lambda: (l_w, acc_w))

    b = pl.program_id(0)
    uncommon = flags_ref[(b * pl.num_programs(2) + i) * N_PAIR + hp] != 0
    l, acc = lax.cond(uncommon, beyond_window, lambda: (l, acc))
    o = acc / l
    o_ref[0] = jnp.concatenate([o[:, :tq], o[:, tq:]], axis=0)


def _sb_prompt_kernel(*refs, window_blocks):
    lax.switch(jnp.minimum(pl.program_id(2), window_blocks - 1),
               [functools.partial(_sb_prompt_body, n, *refs) for n in range(1, window_blocks + 1)])


def _sb_prompt_body(nblk, qt_ref, k_ref, vt_ref, o_ref):
    tq = qt_ref.shape[3]
    tk = vt_ref.shape[3]
    i = pl.program_id(2)
    q2 = jnp.concatenate(_split_head_rows(qt_ref[0, 0]), axis=1)
    ts = SB_SUFFIX_BLOCK
    sub_ones = _tri_ones(ts, "strict_upper")

    def block(j, state, valid):
        kb = k_ref[pl.ds(pl.multiple_of(j * tk, tk), tk), :]
        vt = vt_ref[0, j]

        def pv(a):
            return jnp.concatenate([_dot(vt[:D_HEAD], a[:, :tq]), _dot(vt[D_HEAD:], a[:, tq:])],
                                   axis=1)

        return _sb_update_t(_dot(kb, q2), pv, _tri_ones(tk, "strict_upper"), state, valid)

    def still_open(carry):
        return jnp.min(carry) < SB_CLOSED * LOG2E

    key = lax.broadcasted_iota(jnp.int32, (tk, 2 * tq), 0)
    qry = lax.broadcasted_iota(jnp.int32, (tk, 2 * tq), 1) % tq
    causal = key < qry

    def window(nblk):
        j0 = i - (nblk - 1)
        z = _dot(k_ref[pl.ds(pl.multiple_of(j0 * tk, tk), nblk * tk), :], q2)
        r = _softplus2(z)
        body_rows = (nblk - 1) * tk
        diag = jnp.where(causal, r[body_rows:], 0.0)
        r = diag if nblk == 1 else jnp.concatenate([r[:body_rows], diag], axis=0)
        carry = jnp.zeros((1, 2 * tq), F32)
        n_sub = nblk * tk // ts
        a_blocks = [None] * n_sub
        for w in reversed(range(n_sub)):
            rows = slice(w * ts, (w + 1) * ts)
            r_w = r[rows]
            after = _dot(sub_ones, r_w.astype(BF16)) + carry
            a_blocks[w] = jnp.exp2((z[rows] - r_w) - after)
            carry = carry + jnp.sum(r_w, axis=0, keepdims=True)
        a = jnp.concatenate(a_blocks, axis=0)
        diag = jnp.where(causal, a[body_rows:], 0.0)
        a = (diag if nblk == 1 else jnp.concatenate([a[:body_rows], diag], axis=0)).astype(BF16)
        vts = [vt_ref[0, j0 + w] for w in range(nblk)]
        outs = []
        for h in range(2):
            vth = jnp.concatenate([v[h * D_HEAD:(h + 1) * D_HEAD] for v in vts], axis=1)
            outs.append(_dot(vth, a[:, h * tq:(h + 1) * tq]))
        return j0 - 1, still_open(carry), (carry, jnp.concatenate(outs, axis=1))

    j, go, state = window(nblk)

    def cond(c):
        j, go, _ = c
        return jnp.logical_and(j >= 0, go)

    def body(c):
        j, _, st = c
        st = block(j, st, None)
        return j - 1, still_open(st[0]), st

    _, _, (_, acc) = lax.while_loop(cond, body, (j, go, state))
    o_ref[0] = jnp.concatenate([acc[:, :tq], acc[:, tq:]], axis=0)


def _prompt_attention(qt, k, vtb, group, *, batch, seq, crow=None, ccolp=None, shiftrow=None,
                      flags=None):
    tq = tk = ATT_TILE
    nq = seq // tq
    col0 = group * N_PAIR
    qt_spec = pl.BlockSpec((1, 1, LANES, tq), lambda b, hp, i, *_: (b, i, col0 + hp, 0))
    k_spec = pl.BlockSpec((seq, LANES), lambda b, hp, i, *_: (b, col0 + hp))
    vt_spec = pl.BlockSpec((1, seq // tk, LANES, tk), lambda b, hp, i, *_: (b, 0, hp, 0))
    out_spec = pl.BlockSpec((1, LANES, tq), lambda b, hp, i, *_: (b * nq + i, hp, 0))
    out_shape = jax.ShapeDtypeStruct((batch * nq, W_ATT, tq), F32)
    params = _params(("parallel", "parallel", "arbitrary"), 40)
    grid = (batch, N_PAIR, nq)
    if group == 1:
        return pl.pallas_call(
            functools.partial(_sb_prompt_kernel, window_blocks=SB_WINDOW), grid=grid,
            name="sb_prompt",
            in_specs=[qt_spec, k_spec, vt_spec], out_specs=out_spec, out_shape=out_shape,
            compiler_params=params)(qt, k, vtb)
    crow_spec = pl.BlockSpec((seq // tk, SUBLANES, tk), lambda b, hp, i, *_: (b, 0, 0))
    ccolp_spec = pl.BlockSpec((seq, LANES), lambda b, hp, i, *_: (b, hp))
    grid_spec = pltpu.PrefetchScalarGridSpec(
        num_scalar_prefetch=1, grid=grid,
        in_specs=[qt_spec, k_spec, vt_spec, crow_spec, ccolp_spec, crow_spec],
        out_specs=out_spec)
    return pl.pallas_call(
        functools.partial(_fox_prompt_kernel, window_blocks=FOX_WINDOW), grid_spec=grid_spec,
        name="fox_prompt", out_shape=out_shape,
        compiler_params=params)(flags, qt, k, vtb, crow, ccolp, shiftrow)


def _suffix_sum_kernel(x_ref, o_ref):
    rows, width = x_ref.shape
    ones = _tri_ones(LANES, "strict_lower")
    carry = jnp.zeros((rows, 1), F32)
    for kb in reversed(range(width // LANES)):
        xb = x_ref[:, kb * LANES:(kb + 1) * LANES]
        o_ref[:, kb * LANES:(kb + 1) * LANES] = _ones_dot3(xb, ones, left=False) + carry
        carry = carry + jnp.sum(xb, axis=1, keepdims=True)


def _suffix_sum(x):
    return pl.pallas_call(
        _suffix_sum_kernel, name="cache_suffix_sum",
        out_shape=jax.ShapeDtypeStruct(x.shape, F32),
        compiler_params=_params(None, 32))(x)


def _split_heads(q2):
    lane = lax.broadcasted_iota(jnp.int32, q2.shape, 1)
    low = lane < D_HEAD
    zero = jnp.zeros_like(q2)
    return jnp.where(low, q2, zero), jnp.where(low, zero, q2)


def _stack_heads(q2):
    return jnp.concatenate(_split_heads(q2), axis=0)


def _unstack_heads(o):
    half = o.shape[0] // 2
    lane = lax.broadcasted_iota(jnp.int32, (half, o.shape[1]), 1)
    return jnp.where(lane < D_HEAD, o[:half], o[half:])


def _sb_update(z, pv, suffix_ones, state, valid):
    carry, acc = state
    r = _softplus(z)
    if valid is not None:
        r = jnp.where(valid, r, 0.0)
    after = _dot(r.astype(BF16), suffix_ones) + carry
    a = jnp.exp((z - r) - after)
    if valid is not None:
        a = jnp.where(valid, a, 0.0)
    acc = acc + pv(a.astype(BF16))
    carry = carry + jnp.sum(r, axis=1, keepdims=True)
    return carry, acc


def _sample_attn_kernel(qf_ref, kfn_ref, vfn_ref, kfc_ref, vfc_ref, drow_ref, cnrow_ref, ccol_ref,
                        qs_ref, ksn_ref, vsn_ref, ksc_last_ref, vsc_last_ref, ksc_hbm_ref,
                        vsc_hbm_ref, of_ref, os_ref,
                        carry_ref, acc_ref, kbuf_ref, vbuf_ref, sem_ref, *, blk, layer):
    b = pl.program_id(0)
    hp = pl.program_id(1)
    tq = qf_ref.shape[0]
    past = kfc_ref.shape[3]
    row = lax.broadcasted_iota(jnp.int32, (2 * tq, tq), 0) % tq
    col = lax.broadcasted_iota(jnp.int32, (2 * tq, tq), 1)

    def per_head(fn):
        return jnp.concatenate([fn(0), fn(1)], axis=0)

    q2 = _stack_heads(qf_ref[...])
    kct = kfc_ref[0, 0].astype(BF16)
    vct = vfc_ref[0, 0].astype(BF16)
    cc = ccol_ref[...]
    cn = cnrow_ref[0]
    cq = per_head(lambda h: _pick_lane(cc, 2 * hp + h))
    qk_c = _dot(q2, kct)
    qk_n = _dot_nt(q2, kfn_ref[...])
    s_c = per_head(lambda h: qk_c[h * tq:(h + 1) * tq]
                   + _pick_row(drow_ref[0, 2 * hp + h], b % SUBLANES))
    s_n = per_head(lambda h: qk_n[h * tq:(h + 1) * tq] - _pick_row(cn, 2 * hp + h))
    s_n = jnp.where(col <= row, s_n, NEG_INF)
    m = jnp.maximum(jnp.max(s_c, axis=1, keepdims=True), jnp.max(s_n, axis=1, keepdims=True)) + cq
    shift = cq - m
    p_c = jnp.exp(s_c + shift)
    p_n = jnp.exp(s_n + shift)
    l = jnp.sum(p_c, axis=1, keepdims=True) + jnp.sum(p_n, axis=1, keepdims=True)
    o = _dot_nt(p_c.astype(BF16), vct) + _dot(p_n.astype(BF16), vfn_ref[...])
    of_ref[...] = _unstack_heads(o / l)

    q2 = _stack_heads(qs_ref[...])
    vn = vsn_ref[...]
    suffix_ones = _tri_ones(blk, "strict_lower")

    def cache_block(kt_f32, vt_f32, state):
        kt = kt_f32.astype(BF16)
        vt = vt_f32.astype(BF16)
        return _sb_update(_dot(q2, kt), lambda a: _dot_nt(a, vt), suffix_ones, state, None)

    n_blocks = past // blk
    state = (jnp.zeros((2 * tq, 1), F32), jnp.zeros((2 * tq, LANES), F32))
    state = _sb_update(_dot_nt(q2, ksn_ref[...]), lambda a: _dot(a, vn),
                       _tri_ones(tq, "strict_lower"), state, col < row)
    carry_ref[...], acc_ref[...] = cache_block(ksc_last_ref[0, 0], vsc_last_ref[0, 0], state)

    def still_open():
        return jnp.min(carry_ref[...]) < SB_CLOSED

    def fetch(src_ref, jb, buf_ref, sem):
        rows = pl.ds(pl.multiple_of(hp * LANES, LANES), LANES)
        return pltpu.make_async_copy(src_ref.at[layer, b, rows, pl.ds(jb * blk, blk)], buf_ref, sem)

    @pl.when(still_open())
    def _():
        for jb in reversed(range(n_blocks - 1)):
            @pl.when(still_open())
            def _():
                copies = (fetch(ksc_hbm_ref, jb, kbuf_ref, sem_ref.at[0]),
                          fetch(vsc_hbm_ref, jb, vbuf_ref, sem_ref.at[1]))
                for cp in copies:
                    cp.start()
                for cp in copies:
                    cp.wait()
                carry_ref[...], acc_ref[...] = cache_block(
                    kbuf_ref[...], vbuf_ref[...], (carry_ref[...], acc_ref[...]))

    os_ref[...] = _unstack_heads(acc_ref[...])


def _sample_attention(q, kb, vb, cache_fk, cache_fv, drow, cnrow, ccol, cache_sk, cache_sv,
                      *, layer, streams, dec_seq):
    past = cache_fk.shape[3]
    new_spec = lambda g: pl.BlockSpec((dec_seq, LANES), lambda b, hp: (b, g * N_PAIR + hp))
    cache_spec = pl.BlockSpec((1, 1, LANES, past), lambda b, hp: (layer, b, hp, 0))
    drow_spec = pl.BlockSpec((1, H_FOX, SUBLANES, past), lambda b, hp: (layer, 0, b // SUBLANES, 0))
    cnrow_spec = pl.BlockSpec((1, SUBLANES, dec_seq), lambda b, hp: (b, 0, 0))
    ccol_spec = pl.BlockSpec((dec_seq, LANES), lambda b, hp: (b, 0))
    out_spec = pl.BlockSpec((dec_seq, LANES), lambda b, hp: (b, hp))
    out_shape = [jax.ShapeDtypeStruct((streams * dec_seq, W_ATT), F32)] * 2
    blk = ATT_TILE
    newest_spec = pl.BlockSpec((1, 1, LANES, blk), lambda b, hp: (layer, b, hp, past // blk - 1))
    hbm_spec = pl.BlockSpec(memory_space=pl.ANY)
    return pl.pallas_call(
        functools.partial(_sample_attn_kernel, blk=blk, layer=layer),
        grid=(streams, N_PAIR), name="sample_attention",
        in_specs=[new_spec(0), new_spec(0), new_spec(0), cache_spec, cache_spec, drow_spec,
                  cnrow_spec, ccol_spec, new_spec(1), new_spec(1), new_spec(1), newest_spec,
                  newest_spec, hbm_spec, hbm_spec],
        out_specs=[out_spec, out_spec], out_shape=out_shape,
        scratch_shapes=[pltpu.VMEM((2 * dec_seq, 1), F32), pltpu.VMEM((2 * dec_seq, LANES), F32),
                        pltpu.VMEM((LANES, blk), F32), pltpu.VMEM((LANES, blk), F32),
                        pltpu.SemaphoreType.DMA((2,))],
        compiler_params=_params(("parallel", "parallel"), 48),
    )(q, kb, vb, cache_fk, cache_fv, drow, cnrow, ccol, q, kb, vb, cache_sk, cache_sv,
      cache_sk, cache_sv)


def _merge_tile(x_ref, of_ref, os_ref, u_ref, vn_ref, ws_ref, bs_ref, gmix_ref, wout_ref,
                g_ref, b_ref, *, alpha):
    tm = x_ref.shape[0]
    r = lax.broadcasted_iota(jnp.int32, (SGU_CHUNK, SGU_CHUNK), 0)
    c = lax.broadcasted_iota(jnp.int32, (SGU_CHUNK, SGU_CHUNK), 1)
    tril = r >= c
    lane = lax.broadcasted_iota(jnp.int32, (SGU_CHUNK, LANES), 1)

    gate_pieces = []
    for p in range(W_SGU // LANES):
        w0 = jnp.where(tril, ws_ref[2 * p], 0.0).astype(BF16)
        w1 = jnp.where(tril, ws_ref[2 * p + 1], 0.0).astype(BF16)
        bias = bs_ref[:, p * LANES:(p + 1) * LANES]
        chunks = []
        for ch in range(tm // SGU_CHUNK):
            rows = slice(ch * SGU_CHUNK, (ch + 1) * SGU_CHUNK)
            vp = vn_ref[rows, p * LANES:(p + 1) * LANES].astype(BF16)
            s = jnp.where(lane < D_HEAD, _dot(w0, vp), _dot(w1, vp)) + bias
            chunks.append(u_ref[rows, p * LANES:(p + 1) * LANES] * s)
        gate_pieces.append(jnp.concatenate(chunks, axis=0))

    def attention_pieces(ref):
        if ref.ndim == 2:
            return [ref[:, k * LANES:(k + 1) * LANES] for k in range(N_PAIR)]
        return [jnp.concatenate([ref[qb, k * LANES:(k + 1) * LANES, :].T
                                 for qb in range(ref.shape[0])], axis=0) for k in range(N_PAIR)]

    pieces = attention_pieces(of_ref) + gate_pieces + attention_pieces(os_ref)
    wide = 2 * LANES
    r2 = lax.broadcasted_iota(jnp.int32, (wide, wide), 0)
    c2 = lax.broadcasted_iota(jnp.int32, (wide, wide), 1)
    head_mean = jnp.where(r2 // D_HEAD == c2 // D_HEAD, 1.0 / D_HEAD, 0.0).astype(BF16)
    normed = []
    for k in range(0, len(pieces), 2):
        o = jnp.concatenate(pieces[k:k + 2], axis=1)
        ms = _dot((o * o).astype(BF16), head_mean)
        gain = gmix_ref[:, k * LANES:(k + 2) * LANES]
        normed.append((o * lax.rsqrt(ms + RMS_EPS) * gain).astype(BF16))
    mix = _dot(jnp.concatenate(normed, axis=1), wout_ref[0])
    return _layer_norm(alpha * x_ref[...] + mix, g_ref[...], b_ref[...])


def _mlp_tile(x, wup_ref, wdn_ref, g_ref, b_ref, *, alpha, ff_chunk):
    xb = x.astype(BF16)
    acc = alpha * x
    for c0 in range(0, wup_ref.shape[2], ff_chunk):
        h = jnp.maximum(_dot(xb, wup_ref[0, :, c0:c0 + ff_chunk]), 0.0)
        acc = acc + _dot((h * h).astype(BF16), wdn_ref[0, c0:c0 + ff_chunk, :])
    return _layer_norm(acc, g_ref[...], b_ref[...])


def _merge_mlp_kernel(x_ref, of_ref, os_ref, u_ref, vn_ref, ws_ref, bs_ref, gmix_ref, wout_ref,
                      g1_ref, b1_ref, wup_ref, wdn_ref, g2_ref, b2_ref, o_ref, *, alpha, ff_chunk):
    x1 = _merge_tile(x_ref, of_ref, os_ref, u_ref, vn_ref, ws_ref, bs_ref, gmix_ref, wout_ref,
                     g1_ref, b1_ref, alpha=alpha)
    o_ref[...] = _mlp_tile(x1, wup_ref, wdn_ref, g2_ref, b2_ref, alpha=alpha, ff_chunk=ff_chunk)


def _merge_mlp(x, of, os_, u, vn, ws, bs, gmix, wout, g1, b1, wup, wdn, g2, b2, *, alpha, name,
               layer):
    n, d = x.shape
    tm = ROW_TILE
    row = lambda w: pl.BlockSpec((tm, w), lambda i: (i, 0))
    if of.ndim == 2:
        att = row(W_ATT)
    else:
        tq = of.shape[2]
        att = pl.BlockSpec((tm // tq, W_ATT, tq), lambda i: (i, 0, 0))
    weights = (ws, bs, gmix, wout, g1, b1, wup, wdn, g2, b2)

    def weight_spec(a):
        if a is wout or a is wup or a is wdn:
            return pl.BlockSpec((1,) + a.shape[1:], lambda i: (layer, 0, 0),
                                pipeline_mode=pl.Buffered(1))
        return _resident(a.shape)

    return pl.pallas_call(
        functools.partial(_merge_mlp_kernel, alpha=alpha, ff_chunk=MLP_FF_CHUNK), grid=(n // tm,),
        name=name,
        in_specs=[row(d), att, att, row(W_SGU), row(W_SGU)] + [weight_spec(a) for a in weights],
        out_specs=row(d), out_shape=jax.ShapeDtypeStruct((n, d), F32),
        compiler_params=_params(("parallel",), 60),
    )(x, of, os_, u, vn, *weights)


def _pack_w_in(w, b_f):
    sizes = [W_ATT, W_ATT, W_ATT, H_FOX, W_SGU, W_SGU, W_ATT, W_ATT, W_ATT]
    offs = [0]
    for s in sizes:
        offs.append(offs[-1] + s)
    q_f, k_f, v_f, f_lg, u_g, v_g, q_s, k_s, v_s = (w[:, offs[i]:offs[i + 1]] for i in range(9))
    pad_cols = lambda a, width: jnp.pad(a, ((0, 0), (0, width - a.shape[1])))
    w_rows = jnp.concatenate([q_f, q_s, k_f, k_s, v_f, v_s, u_g, v_g, pad_cols(f_lg, LANES)],
                             axis=1).astype(BF16)
    w_cols = jnp.concatenate([q_f, q_s, k_f, k_s, v_f, v_s, pad_cols(f_lg, F_ROWS)],
                             axis=1).T.astype(BF16)
    bfc = jnp.pad(b_f, (0, LANES - H_FOX)).reshape(1, LANES)
    bfr = jnp.broadcast_to(jnp.pad(b_f, (0, SUBLANES - H_FOX)).reshape(SUBLANES, 1),
                           (SUBLANES, LANES))
    return w_rows, w_cols, bfc, bfr


def kernel(x_prompt, x_sample, cache_fox_k, cache_fox_v, cache_fox_logf, cache_sb_k, cache_sb_v,
           w_in, b_f, g_v, b_v, w_s, b_s, g_mix, w_out, ln1_g, ln1_b, w_up, w_down, ln2_g, ln2_b):
    depth = w_in.shape[0]
    batch, seq, d_model = x_prompt.shape
    streams, dec_seq, _ = x_sample.shape
    past = cache_fox_k.shape[2]
    n_prompt = batch * seq
    n_sample = streams * dec_seq
    alpha = (2 * depth) ** 0.25
    assert seq % ROW_TILE == 0 and n_sample % ROW_TILE == 0 and ROW_TILE % dec_seq == 0
    assert dec_seq == D_HEAD and 2 * dec_seq == SGU_CHUNK and past % ATT_TILE == 0
    assert streams % SUBLANES == 0

    xp = x_prompt.reshape(n_prompt, d_model)
    xs = x_sample.reshape(n_sample, d_model)
    row2 = lambda a: a.reshape(1, -1)
    cache_t = lambda c: jnp.transpose(c, (0, 1, 3, 4, 2)).reshape(depth, streams, W_ATT, past)
    ck_f, cv_f, ck_s, cv_s = (cache_t(c) for c in (cache_fox_k, cache_fox_v, cache_sb_k, cache_sb_v))
    lf_cache = jnp.transpose(cache_fox_logf.astype(F32), (0, 3, 1, 2))
    drow = _suffix_sum(lf_cache.reshape(depth * H_FOX * streams, past)).reshape(
        depth, H_FOX, streams, past)

    wout, wup, wdn = w_out.astype(BF16), w_up.astype(BF16), w_down.astype(BF16)
    p_rows = s_rows = None
    p_logf, s_logf, s_gate = [], [], []
    for l in range(depth):
        w_rows, w_cols, bfc, bfr = _pack_w_in(w_in[l], b_f[l])
        gv, bv = row2(g_v[l]), row2(b_v[l])
        gmix, g1, b1, g2, b2 = (row2(a[l]) for a in (g_mix, ln1_g, ln1_b, ln2_g, ln2_b))

        w_l = w_s[l]
        half = w_l[:, :dec_seq, :dec_seq]
        zeros = jnp.zeros_like(half)
        w_blockdiag = jnp.concatenate([jnp.concatenate([half, zeros], axis=2),
                                       jnp.concatenate([zeros, half], axis=2)], axis=1)
        bias_full = jnp.repeat(b_s[l].T, D_HEAD, axis=1)
        bias_half = jnp.concatenate([bias_full[:dec_seq], bias_full[:dec_seq]], axis=0)

        (qt, kb, *p_rows, vtbf, vtbs, lfrow, crow, ccolp, u, vn, shiftrow,
         head_flags) = _in_proj_prompt(
            xp, w_rows, w_cols, bfr, gv, bv, batch=batch, seq=seq, layer=l, depth=depth,
            cache_rows=p_rows)
        pair_flags = jnp.max(head_flags[:, :H_FOX, 0].reshape(-1, N_PAIR, 2), axis=2).reshape(-1)
        of = _prompt_attention(qt, kb, vtbf, 0, batch=batch, seq=seq, crow=crow, ccolp=ccolp,
                               shiftrow=shiftrow, flags=pair_flags)
        os_ = _prompt_attention(qt, kb, vtbs, 1, batch=batch, seq=seq)
        xp = _merge_mlp(xp, of, os_, u, vn, w_l, bias_full, gmix, wout, g1, b1, wup, wdn, g2, b2,
                        alpha=alpha, name="merge_mlp_prompt", layer=l)
        p_logf.append(jnp.transpose(lfrow[:H_FOX].reshape(H_FOX, batch, seq), (1, 2, 0)))

        (q, kb, vb, *s_rows, lfcol, ccol, cnrow, u, vn) = _in_proj_sample(
            xs, w_rows, w_cols, bfc, bfr, gv, bv, dec_seq=dec_seq, layer=l, depth=depth,
            cache_rows=s_rows)
        of, os_ = _sample_attention(q, kb, vb, ck_f, cv_f, drow, cnrow, ccol, ck_s, cv_s,
                                    layer=l, streams=streams, dec_seq=dec_seq)
        xs = _merge_mlp(xs, of, os_, u, vn, w_blockdiag, bias_half, gmix, wout, g1, b1, wup, wdn,
                        g2, b2, alpha=alpha, name="merge_mlp_sample", layer=l)
        s_logf.append(lfcol[:, :H_FOX].reshape(streams, dec_seq, H_FOX))
        s_gate.append(vn.reshape(streams, dec_seq, W_SGU))

    p_kf, p_ks, p_vf, p_vs = (jnp.transpose(a.reshape(depth, batch, H_FOX, D_HEAD, seq),
                                            (0, 1, 4, 2, 3)) for a in p_rows)
    s_kf, s_ks, s_vf, s_vs = (jnp.transpose(a, (0, 1, 3, 2, 4)) for a in s_rows)
    return (xp.reshape(batch, seq, d_model), xs.reshape(streams, dec_seq, d_model),
            p_kf, p_vf, jnp.stack(p_logf), p_ks, p_vs,
            s_kf, s_vf, jnp.stack(s_logf), s_ks, s_vs, jnp.stack(s_gate))
```

```python
import functools
import math

import jax
import jax.numpy as jnp
from jax import lax
from jax.experimental import pallas as pl
from jax.experimental.pallas import tpu as pltpu

F32 = jnp.float32
BF16 = jnp.bfloat16

LANES = 128
SUBLANES = 8
VMEM_BYTES_V7X = 64 * 1024 * 1024

D_HEAD = 64
H_FOX = 6
H_SGU = 4
W_ATT = H_FOX * D_HEAD
W_SGU = H_SGU * D_HEAD
N_PAIR = W_ATT // LANES
SGU_CHUNK = 128
LN_EPS = 1e-5
RMS_EPS = 1e-6
NEG_INF = -1e30
SB_CLOSED = 110.0
FOX_DEAD = 110.0
LOG2E = math.log2(math.e)
FOX_L_FLOOR = 1e-25
FOX_BOUND_SAFE = 41.0
FOX_WINDOW = 5
SB_WINDOW = 2
SB_SUFFIX_BLOCK = 128

ROW_TILE = 512
ATT_TILE = 256
MLP_FF_CHUNK = 1024

_C_Q = 0
_C_K = 2 * W_ATT
_C_V = 4 * W_ATT
_C_G = 6 * W_ATT
_C_F = _C_G + 2 * W_SGU
W_ROWS = _C_F + LANES
_R_Q = 0
_R_K = 2 * W_ATT
_R_V = 4 * W_ATT
_R_F = 6 * W_ATT
F_ROWS = 2 * SUBLANES
W_COLS = _R_F + F_ROWS


def _params(semantics, vmem_mib, flags=None):
    assert vmem_mib * 1024 * 1024 < VMEM_BYTES_V7X
    return pltpu.CompilerParams(dimension_semantics=semantics,
                                vmem_limit_bytes=vmem_mib * 1024 * 1024, flags=flags)


def _resident(shape):
    nd = len(shape)
    return pl.BlockSpec(shape, lambda *_: (0,) * nd, pipeline_mode=pl.Buffered(1))


def _split3(x):
    hi = x.astype(BF16)
    r1 = x - hi.astype(F32)
    mid = r1.astype(BF16)
    lo = (r1 - mid.astype(F32)).astype(BF16)
    return hi, mid, lo


def _dot(a, b):
    return jnp.dot(a, b, preferred_element_type=F32)


def _dot_nt(a, b):
    return lax.dot_general(a, b, (((1,), (1,)), ((), ())), preferred_element_type=F32)


def _lane_cumsum(x, seg=None):
    n = x.shape[1]
    lane = lax.broadcasted_iota(jnp.int32, x.shape, 1)
    pos = lane if seg is None else lane % seg
    step = 1
    while step < (n if seg is None else seg):
        x = x + jnp.where(pos >= step, pltpu.roll(x, step, axis=1), 0.0)
        step *= 2
    return x


def _ones_dot3(x, ones_mat, left):
    if left:
        return sum(_dot(ones_mat, p) for p in _split3(x))
    return sum(_dot(p, ones_mat) for p in _split3(x))


def _gelu_tanh(x):
    return 0.5 * x * (1.0 + jnp.tanh(0.7978845608028654 * (x + 0.044715 * (x * x * x))))


def _log_sigmoid(x):
    return jnp.minimum(x, 0.0) - jnp.log1p(jnp.exp(-jnp.abs(x)))


def _softplus(z):
    return jnp.maximum(z, 0.0) + jnp.log(1.0 + jnp.exp(-jnp.abs(z)))


def _softplus2(z2):
    return jnp.maximum(z2, 0.0) + jnp.log(1.0 + jnp.exp2(-jnp.abs(z2))) * LOG2E


def _layer_norm(x, g, b):
    mu = jnp.mean(x, axis=-1, keepdims=True)
    xc = x - mu
    var = jnp.mean(xc * xc, axis=-1, keepdims=True)
    return xc * lax.rsqrt(var + LN_EPS) * g + b


def _tri_ones(n, kind, seg=None):
    r = lax.broadcasted_iota(jnp.int32, (n, n), 0)
    c = lax.broadcasted_iota(jnp.int32, (n, n), 1)
    keep = {"lower": c <= r, "upper": r <= c, "strict_lower": r > c, "strict_upper": r < c}[kind]
    if seg is not None:
        keep = jnp.logical_and(keep, r // seg == c // seg)
    return jnp.where(keep, 1.0, 0.0).astype(BF16)


def _pick_lane(x, idx):
    lane = lax.broadcasted_iota(jnp.int32, x.shape, 1)
    return jnp.sum(jnp.where(lane == idx, x, 0.0), axis=1, keepdims=True)


def _pick_row(x, idx):
    sub = lax.broadcasted_iota(jnp.int32, x.shape, 0)
    return jnp.sum(jnp.where(sub == idx, x, 0.0), axis=0, keepdims=True)


def _gates(xb, wr_ref, gv_ref, bv_ref, u_ref, vn_ref):
    u_ref[...] = _gelu_tanh(_dot(xb, wr_ref[:, _C_G:_C_G + W_SGU]))
    vg = _dot(xb, wr_ref[:, _C_G + W_SGU:_C_G + 2 * W_SGU])
    vn_ref[...] = _layer_norm(_gelu_tanh(vg), gv_ref[...], bv_ref[...])


def _log_forget_rows(xb, wc_ref, bfr_ref):
    sub = lax.broadcasted_iota(jnp.int32, (SUBLANES, xb.shape[0]), 0)
    z = _dot_nt(wc_ref[_R_F:_R_F + F_ROWS, :], xb)[:SUBLANES] + bfr_ref[:, 0:1]
    return jnp.where(sub < H_FOX, _log_sigmoid(z), 0.0)


def _in_proj_prompt_kernel(*refs, layer, depth):
    n_in = 6 if layer == 0 else 10
    x_ref, wr_ref, wc_ref, bfr_ref, gv_ref, bv_ref = refs[:6]
    (qt_ref, k_ref, ktf_ref, kts_ref, vtf_ref, vts_ref, vtbf_ref, vtbs_ref,
     lfrow_ref, crow_ref, ccolp_ref, u_ref, vn_ref, shiftrow_ref, flags_ref,
     carry_row_ref, knmax_ref, cend_ref) = refs[n_in:]
    i = pl.program_id(1)
    tm = x_ref.shape[0]
    tk = vtbf_ref.shape[3]

    @pl.when(i == 0)
    def _():
        carry_row_ref[...] = jnp.zeros_like(carry_row_ref)
        knmax_ref[...] = jnp.zeros_like(knmax_ref)

    xb = x_ref[...].astype(BF16)

    def blocks(zt):
        return [zt[:, kb * tk:(kb + 1) * tk].astype(BF16) for kb in range(tm // tk)]

    def store_cache_rows(ref, zt):
        if layer == 0:
            for l in range(depth):
                ref[l, 0] = zt if l == 0 else jnp.zeros_like(zt)
        else:
            ref[0, 0] = zt

    scale = LOG2E / math.sqrt(D_HEAD)
    qn2_rows = []
    for kb, blk in enumerate(blocks(_dot_nt(wc_ref[_R_Q:_R_Q + 2 * W_ATT, :], xb) * scale)):
        qt_ref[0, kb] = blk
        qf = blk[:W_ATT].astype(F32)
        qn2 = jnp.sum((qf * qf).reshape(H_FOX, D_HEAD, tk), axis=1)
        qn2_rows.append(jnp.concatenate([qn2, jnp.zeros((SUBLANES - H_FOX, tk), F32)], axis=0))
    for r0, full_ref, blk_ref in ((_R_V, vtf_ref, vtbf_ref), (_R_V + W_ATT, vts_ref, vtbs_ref)):
        zt = _dot_nt(wc_ref[r0:r0 + W_ATT, :], xb)
        store_cache_rows(full_ref, zt)
        for kb, blk in enumerate(blocks(zt)):
            blk_ref[0, kb] = blk
    k_rows = _dot(xb, wr_ref[:, _C_K:_C_K + 2 * W_ATT])
    k_ref[...] = k_rows.astype(BF16)
    kt_f = k_rows[:, :W_ATT].T
    store_cache_rows(ktf_ref, kt_f)
    store_cache_rows(kts_ref, k_rows[:, W_ATT:].T)

    _gates(xb, wr_ref, gv_ref, bv_ref, u_ref, vn_ref)

    kn2 = jnp.sum((kt_f * kt_f).reshape(H_FOX, D_HEAD, tm), axis=1)
    kn2 = jnp.broadcast_to(jnp.max(kn2, axis=1, keepdims=True), (H_FOX, LANES))
    kn2 = jnp.concatenate([kn2, jnp.zeros((SUBLANES - H_FOX, LANES), F32)], axis=0)
    knmax_ref[...] = jnp.maximum(knmax_ref[...], kn2)

    lf_row = _log_forget_rows(xb, wc_ref, bfr_ref)
    lfrow_ref[...] = lf_row
    c_row = _ones_dot3(lf_row, _tri_ones(tm, "upper"), left=False) + carry_row_ref[:, 0:1]
    carry_row_ref[...] = jnp.broadcast_to(c_row[:, tm - 1:tm], carry_row_ref.shape)
    c_row = c_row * LOG2E
    kn_run = jnp.concatenate([knmax_ref[...]] * (tk // LANES), axis=1)
    for kb in range(tm // tk):
        c_blk = c_row[:, kb * tk:(kb + 1) * tk]
        crow_ref[kb] = c_blk
        bound = jnp.sqrt(qn2_rows[kb] * kn_run) * 1.02
        shiftrow_ref[kb] = c_blk - bound
        blk_id = i * (tm // tk) + kb
        cend_ref[blk_id] = jnp.broadcast_to(c_blk[:, tk - 1:tk], (SUBLANES, LANES))
        left = blk_id - FOX_WINDOW
        decay = c_blk[:, 0:1] - cend_ref[jnp.maximum(left, 0)][:, 0:1]
        live = jnp.logical_and(decay > -FOX_DEAD * LOG2E, left >= 0)
        risky = jnp.max(bound, axis=1, keepdims=True) > FOX_BOUND_SAFE
        flags_ref[kb] = jnp.broadcast_to(
            jnp.where(jnp.logical_or(live, risky), 1, 0).astype(jnp.int32), (SUBLANES, LANES))
    pad = jnp.zeros((LANES - SUBLANES, tm), F32)
    for p in range(N_PAIR):
        rows = c_row if p == 0 else jnp.concatenate([c_row[2 * p:], c_row[:2 * p]], axis=0)
        ccolp_ref[:, p * LANES:(p + 1) * LANES] = jnp.concatenate([rows, pad], axis=0).T


def _in_proj_prompt(x, w_rows, w_cols, bfr, gv, bv, *, batch, seq, layer, depth, cache_rows):
    n, d = x.shape
    tm, tk = ROW_TILE, ATT_TILE
    nt = seq // tm
    row = lambda w: pl.BlockSpec((tm, w), lambda b, i: (b * nt + i, 0))
    if layer == 0:
        t_spec = pl.BlockSpec((depth, 1, W_ATT, tm), lambda b, i: (0, b, 0, i))
        extra_in, extra_specs, aliases = (), [], {}
    else:
        t_spec = pl.BlockSpec((1, 1, W_ATT, tm), lambda b, i: (layer, b, 0, i))
        extra_in = tuple(cache_rows)
        extra_specs = [pl.BlockSpec(memory_space=pl.ANY)] * 4
        aliases = {6 + k: 2 + k for k in range(4)}
    t_shape = jax.ShapeDtypeStruct((depth, batch, W_ATT, seq), F32)
    blk_spec = lambda w: pl.BlockSpec((1, tm // tk, w, tk), lambda b, i: (b, i, 0, 0))
    blk_shape = lambda w: jax.ShapeDtypeStruct((batch, seq // tk, w, tk), BF16)
    out_shape = [blk_shape(2 * W_ATT), jax.ShapeDtypeStruct((n, 2 * W_ATT), BF16),
                 t_shape, t_shape, t_shape, t_shape, blk_shape(W_ATT), blk_shape(W_ATT),
                 jax.ShapeDtypeStruct((SUBLANES, n), F32),
                 jax.ShapeDtypeStruct((n // tk, SUBLANES, tk), F32),
                 jax.ShapeDtypeStruct((n, N_PAIR * LANES), F32),
                 jax.ShapeDtypeStruct((n, W_SGU), F32), jax.ShapeDtypeStruct((n, W_SGU), F32),
                 jax.ShapeDtypeStruct((n // tk, SUBLANES, tk), F32),
                 jax.ShapeDtypeStruct((n // tk, SUBLANES, LANES), jnp.int32)]
    out_specs = [blk_spec(2 * W_ATT), row(2 * W_ATT), t_spec, t_spec, t_spec, t_spec,
                 blk_spec(W_ATT), blk_spec(W_ATT),
                 pl.BlockSpec((SUBLANES, tm), lambda b, i: (0, b * nt + i)),
                 pl.BlockSpec((tm // tk, SUBLANES, tk), lambda b, i: (b * nt + i, 0, 0)),
                 row(N_PAIR * LANES), row(W_SGU), row(W_SGU),
                 pl.BlockSpec((tm // tk, SUBLANES, tk), lambda b, i: (b * nt + i, 0, 0)),
                 pl.BlockSpec((tm // tk, SUBLANES, LANES), lambda b, i: (b * nt + i, 0, 0))]
    return pl.pallas_call(
        functools.partial(_in_proj_prompt_kernel, layer=layer, depth=depth),
        grid=(batch, nt), name="in_proj_prompt",
        in_specs=[row(d)] + [_resident(a.shape) for a in (w_rows, w_cols, bfr, gv, bv)]
        + extra_specs,
        out_specs=out_specs, out_shape=out_shape, input_output_aliases=aliases,
        scratch_shapes=[pltpu.VMEM((SUBLANES, LANES), F32), pltpu.VMEM((SUBLANES, LANES), F32),
                        pltpu.VMEM((seq // tk, SUBLANES, LANES), F32)],
        compiler_params=_params(("parallel", "arbitrary"), 56),
    )(x, w_rows, w_cols, bfr, gv, bv, *extra_in)


def _in_proj_sample_kernel(*refs, dec_seq, layer, depth):
    n_in = 7 if layer == 0 else 11
    x_ref, wr_ref, wc_ref, bfc_ref, bfr_ref, gv_ref, bv_ref = refs[:7]
    (q_ref, kb_ref, vb_ref, kf_ref, ks_ref, vf_ref, vs_ref, lfcol_ref, ccol_ref, crow_ref,
     u_ref, vn_ref) = refs[n_in:]
    tm = x_ref.shape[0]
    xb = x_ref[...].astype(BF16)
    scale = 1.0 / math.sqrt(D_HEAD)
    q_ref[...] = (_dot(xb, wr_ref[:, _C_Q:_C_Q + 2 * W_ATT]) * scale).astype(BF16)

    def store_cache_rows(ref, z):
        for s in range(tm // dec_seq):
            for h in range(W_ATT // D_HEAD):
                blk = z[s * dec_seq:(s + 1) * dec_seq, h * D_HEAD:(h + 1) * D_HEAD]
                ref[0, s, h] = blk
                if layer == 0:
                    for l in range(1, depth):
                        ref[l, s, h] = jnp.zeros_like(blk)

    for c0, refs_fs, half_ref in ((_C_K, (kf_ref, ks_ref), kb_ref), (_C_V, (vf_ref, vs_ref), vb_ref)):
        z = _dot(xb, wr_ref[:, c0:c0 + 2 * W_ATT])
        half_ref[...] = z.astype(BF16)
        store_cache_rows(refs_fs[0], z[:, :W_ATT])
        store_cache_rows(refs_fs[1], z[:, W_ATT:])
    _gates(xb, wr_ref, gv_ref, bv_ref, u_ref, vn_ref)

    lane = lax.broadcasted_iota(jnp.int32, (tm, LANES), 1)
    z = _dot(xb, wr_ref[:, _C_F:_C_F + LANES]) + bfc_ref[...]
    lf_col = jnp.where(lane < H_FOX, _log_sigmoid(z), 0.0)
    lfcol_ref[...] = lf_col
    ccol_ref[...] = _ones_dot3(lf_col, _tri_ones(tm, "lower", dec_seq), left=True)
    lf_row = _log_forget_rows(xb, wc_ref, bfr_ref)
    c_row = _lane_cumsum(lf_row, dec_seq)
    for s in range(tm // dec_seq):
        crow_ref[s] = c_row[:, s * dec_seq:(s + 1) * dec_seq]


def _in_proj_sample(x, w_rows, w_cols, bfc, bfr, gv, bv, *, dec_seq, layer, depth, cache_rows):
    n, d = x.shape
    tm = ROW_TILE
    heads = W_ATT // D_HEAD
    row = lambda w: pl.BlockSpec((tm, w), lambda i: (i, 0))
    sds = lambda w, dt: jax.ShapeDtypeStruct((n, w), dt)
    if layer == 0:
        c_spec = pl.BlockSpec((depth, tm // dec_seq, heads, dec_seq, D_HEAD), lambda i: (0, i, 0, 0, 0))
        extra_in, extra_specs, aliases = (), [], {}
    else:
        c_spec = pl.BlockSpec((1, tm // dec_seq, heads, dec_seq, D_HEAD), lambda i: (layer, i, 0, 0, 0))
        extra_in = tuple(cache_rows)
        extra_specs = [pl.BlockSpec(memory_space=pl.ANY)] * 4
        aliases = {7 + k: 3 + k for k in range(4)}
    c_shape = jax.ShapeDtypeStruct((depth, n // dec_seq, heads, dec_seq, D_HEAD), F32)
    out_shape = [sds(2 * W_ATT, BF16)] * 3 + [c_shape] * 4 + [sds(LANES, F32)] * 2 + [
        jax.ShapeDtypeStruct((n // dec_seq, SUBLANES, dec_seq), F32), sds(W_SGU, F32), sds(W_SGU, F32)]
    out_specs = [row(2 * W_ATT)] * 3 + [c_spec] * 4 + [row(LANES)] * 2 + [
        pl.BlockSpec((tm // dec_seq, SUBLANES, dec_seq), lambda i: (i, 0, 0)), row(W_SGU), row(W_SGU)]
    return pl.pallas_call(
        functools.partial(_in_proj_sample_kernel, dec_seq=dec_seq, layer=layer, depth=depth),
        grid=(n // tm,), name="in_proj_sample",
        in_specs=[row(d)] + [_resident(a.shape) for a in (w_rows, w_cols, bfc, bfr, gv, bv)]
        + extra_specs,
        out_specs=out_specs, out_shape=out_shape, input_output_aliases=aliases,
        compiler_params=_params(("parallel",), 56),
    )(x, w_rows, w_cols, bfc, bfr, gv, bv, *extra_in)


def _split_head_rows(qt):
    sub = lax.broadcasted_iota(jnp.int32, qt.shape, 0)
    low = sub < D_HEAD
    zero = jnp.zeros_like(qt)
    return jnp.where(low, qt, zero), jnp.where(low, zero, qt)


def _fox_update_t(s, ct, pv, state, valid):
    m, l, acc = state
    if valid is not None:
        s = jnp.where(valid, s, NEG_INF)
    m_new = jnp.maximum(m, jnp.max(s, axis=0, keepdims=True) + ct)
    p = jnp.exp2(s + (ct - m_new))
    alpha = jnp.exp2(m - m_new)
    l = alpha * l + jnp.sum(p, axis=0, keepdims=True)
    acc = alpha * acc + pv(p.astype(BF16))
    return m_new, l, acc


def _sb_update_t(z, pv, suffix_ones, state, valid):
    carry, acc = state
    r = _softplus2(z)
    if valid is not None:
        r = jnp.where(valid, r, 0.0)
    r_hi = r.astype(BF16)
    r_lo = (r - r_hi.astype(F32)).astype(BF16)
    after = _dot(suffix_ones, r_hi) + _dot(suffix_ones, r_lo) + carry
    a = jnp.exp2((z - r) - after)
    if valid is not None:
        a = jnp.where(valid, a, 0.0)
    acc = acc + pv(a.astype(BF16))
    carry = carry + jnp.sum(r, axis=0, keepdims=True)
    return carry, acc


def _fox_prompt_kernel(*refs, window_blocks):
    lax.switch(jnp.minimum(pl.program_id(2), window_blocks - 1),
               [functools.partial(_fox_prompt_body, n, *refs) for n in range(1, window_blocks + 1)])


def _fox_prompt_body(nblk, flags_ref, qt_ref, k_ref, vt_ref, crow_ref, ccolp_ref, shiftrow_ref,
                     o_ref):
    tq = qt_ref.shape[3]
    tk = vt_ref.shape[3]
    hp = pl.program_id(1)
    i = pl.program_id(2)
    q2 = jnp.concatenate(_split_head_rows(qt_ref[0, 0]), axis=1)
    c_q = crow_ref[i]
    ct = jnp.concatenate([_pick_row(c_q, 2 * hp), _pick_row(c_q, 2 * hp + 1)], axis=1)

    key = lax.broadcasted_iota(jnp.int32, (tk, 2 * tq), 0)
    qry = lax.broadcasted_iota(jnp.int32, (tk, 2 * tq), 1) % tq
    causal = key <= qry

    def scores(j0, nblk):
        start = pl.multiple_of(j0 * tk, tk)
        kb = k_ref[pl.ds(start, nblk * tk), :]
        cc = ccolp_ref[pl.ds(start, nblk * tk), :]
        ck = jnp.concatenate([jnp.broadcast_to(cc[:, 0:1], (nblk * tk, tq)),
                              jnp.broadcast_to(cc[:, 1:2], (nblk * tk, tq))], axis=1)
        return _dot(kb, q2) - ck

    def pv(p, j0, nblk):
        vts = [vt_ref[0, j0 + w] for w in range(nblk)]
        outs = []
        for h in range(2):
            vth = jnp.concatenate([v[h * D_HEAD:(h + 1) * D_HEAD] for v in vts], axis=1)
            outs.append(_dot(vth, p[:, h * tq:(h + 1) * tq]))
        return jnp.concatenate(outs, axis=1)

    def pair_row(a, b):
        return jnp.concatenate([jnp.broadcast_to(a, (1, tq)), jnp.broadcast_to(b, (1, tq))], axis=1)

    sh = shiftrow_ref[i]
    shift = jnp.concatenate([_pick_row(sh, 2 * hp), _pick_row(sh, 2 * hp + 1)], axis=1)

    def reaches(j):
        return jnp.max(decay(j)) > -FOX_DEAD * LOG2E

    def decay(j):
        c_end = crow_ref[jnp.maximum(j, 0)][:, tk - 1:tk]
        return ct - pair_row(_pick_row(c_end, 2 * hp), _pick_row(c_end, 2 * hp + 1))

    def window(nblk):
        j0 = i - (nblk - 1)
        p = jnp.exp2(scores(j0, nblk) + shift)
        body_rows = (nblk - 1) * tk
        diag = jnp.where(causal, p[body_rows:], 0.0)
        p = diag if nblk == 1 else jnp.concatenate([p[:body_rows], diag], axis=0)
        return (j0 - 1, jnp.sum(p, axis=0, keepdims=True), pv(p.astype(BF16), j0, nblk))

    j, l, acc = window(nblk)

    def cond(c):
        j, go, _, _ = c
        return jnp.logical_and(j >= 0, go)

    def body(c):
        j, _, l, acc = c
        p = jnp.exp2(scores(j, 1) + shift)
        return (j - 1, reaches(j - 1), l + jnp.sum(p, axis=0, keepdims=True),
                acc + pv(p.astype(BF16), j, 1))

    def exact():
        def step(t, state):
            jj = i - 1 - t
            return _fox_update_t(scores(jj, 1), ct, lambda p: pv(p, jj, 1), state, None)

        init = (jnp.full((1, 2 * tq), NEG_INF, F32), jnp.zeros((1, 2 * tq), F32),
                jnp.zeros((D_HEAD, 2 * tq), F32))
        state = _fox_update_t(scores(i, 1), ct, lambda p: pv(p, i, 1), init, causal)
        _, l_x, acc_x = lax.fori_loop(0, i, step, state)
        return l_x, acc_x

    def beyond_window():
        _, _, l_w, acc_w = lax.while_loop(cond, body, (j, reaches(j), l, acc))
        return lax.cond(jnp.min(l_w) < FOX_L_FLOOR, exact, lambda: (l_w, acc_w))

    b = pl.program_id(0)
    uncommon = flags_ref[(b * pl.num_programs(2) + i) * N_PAIR + hp] != 0
    l, acc = lax.cond(uncommon, beyond_window, lambda: (l, acc))
    o = acc / l
    o_ref[0] = jnp.concatenate([o[:, :tq], o[:, tq:]], axis=0)


def _sb_prompt_kernel(*refs, window_blocks):
    lax.switch(jnp.minimum(pl.program_id(2), window_blocks - 1),
               [functools.partial(_sb_prompt_body, n, *refs) for n in range(1, window_blocks + 1)])


def _sb_prompt_body(nblk, qt_ref, k_ref, vt_ref, o_ref):
    tq = qt_ref.shape[3]
    tk = vt_ref.shape[3]
    i = pl.program_id(2)
    q2 = jnp.concatenate(_split_head_rows(qt_ref[0, 0]), axis=1)
    ts = SB_SUFFIX_BLOCK
    sub_ones = _tri_ones(ts, "strict_upper")

    def block(j, state, valid):
        kb = k_ref[pl.ds(pl.multiple_of(j * tk, tk), tk), :]
        vt = vt_ref[0, j]

        def pv(a):
            return jnp.concatenate([_dot(vt[:D_HEAD], a[:, :tq]), _dot(vt[D_HEAD:], a[:, tq:])],
                                   axis=1)

        return _sb_update_t(_dot(kb, q2), pv, _tri_ones(tk, "strict_upper"), state, valid)

    def still_open(carry):
        return jnp.min(carry) < SB_CLOSED * LOG2E

    key = lax.broadcasted_iota(jnp.int32, (tk, 2 * tq), 0)
    qry = lax.broadcasted_iota(jnp.int32, (tk, 2 * tq), 1) % tq
    causal = key < qry

    def window(nblk):
        j0 = i - (nblk - 1)
        z = _dot(k_ref[pl.ds(pl.multiple_of(j0 * tk, tk), nblk * tk), :], q2)
        r = _softplus2(z)
        body_rows = (nblk - 1) * tk
        diag = jnp.where(causal, r[body_rows:], 0.0)
        r = diag if nblk == 1 else jnp.concatenate([r[:body_rows], diag], axis=0)
        carry = jnp.zeros((1, 2 * tq), F32)
        n_sub = nblk * tk // ts
        a_blocks = [None] * n_sub
        for w in reversed(range(n_sub)):
            rows = slice(w * ts, (w + 1) * ts)
            r_w = r[rows]
            after = _dot(sub_ones, r_w.astype(BF16)) + carry
            a_blocks[w] = jnp.exp2((z[rows] - r_w) - after)
            carry = carry + jnp.sum(r_w, axis=0, keepdims=True)
        a = jnp.concatenate(a_blocks, axis=0)
        diag = jnp.where(causal, a[body_rows:], 0.0)
        a = (diag if nblk == 1 else jnp.concatenate([a[:body_rows], diag], axis=0)).astype(BF16)
        vts = [vt_ref[0, j0 + w] for w in range(nblk)]
        outs = []
        for h in range(2):
            vth = jnp.concatenate([v[h * D_HEAD:(h + 1) * D_HEAD] for v in vts], axis=1)
            outs.append(_dot(vth, a[:, h * tq:(h + 1) * tq]))
        return j0 - 1, still_open(carry), (carry, jnp.concatenate(outs, axis=1))

    j, go, state = window(nblk)

    def cond(c):
        j, go, _ = c
        return jnp.logical_and(j >= 0, go)

    def body(c):
        j, _, st = c
        st = block(j, st, None)
        return j - 1, still_open(st[0]), st

    _, _, (_, acc) = lax.while_loop(cond, body, (j, go, state))
    o_ref[0] = jnp.concatenate([acc[:, :tq], acc[:, tq:]], axis=0)


def _prompt_attention(qt, k, vtb, group, *, batch, seq, crow=None, ccolp=None, shiftrow=None,
                      flags=None):
    tq = tk = ATT_TILE
    nq = seq // tq
    col0 = group * N_PAIR
    qt_spec = pl.BlockSpec((1, 1, LANES, tq), lambda b, hp, i, *_: (b, i, col0 + hp, 0))
    k_spec = pl.BlockSpec((seq, LANES), lambda b, hp, i, *_: (b, col0 + hp))
    vt_spec = pl.BlockSpec((1, seq // tk, LANES, tk), lambda b, hp, i, *_: (b, 0, hp, 0))
    out_spec = pl.BlockSpec((1, LANES, tq), lambda b, hp, i, *_: (b * nq + i, hp, 0))
    out_shape = jax.ShapeDtypeStruct((batch * nq, W_ATT, tq), F32)
    params = _params(("parallel", "parallel", "arbitrary"), 40)
    grid = (batch, N_PAIR, nq)
    if group == 1:
        return pl.pallas_call(
            functools.partial(_sb_prompt_kernel, window_blocks=SB_WINDOW), grid=grid,
            name="sb_prompt",
            in_specs=[qt_spec, k_spec, vt_spec], out_specs=out_spec, out_shape=out_shape,
            compiler_params=params)(qt, k, vtb)
    crow_spec = pl.BlockSpec((seq // tk, SUBLANES, tk), lambda b, hp, i, *_: (b, 0, 0))
    ccolp_spec = pl.BlockSpec((seq, LANES), lambda b, hp, i, *_: (b, hp))
    grid_spec = pltpu.PrefetchScalarGridSpec(
        num_scalar_prefetch=1, grid=grid,
        in_specs=[qt_spec, k_spec, vt_spec, crow_spec, ccolp_spec, crow_spec],
        out_specs=out_spec)
    return pl.pallas_call(
        functools.partial(_fox_prompt_kernel, window_blocks=FOX_WINDOW), grid_spec=grid_spec,
        name="fox_prompt", out_shape=out_shape,
        compiler_params=params)(flags, qt, k, vtb, crow, ccolp, shiftrow)


def _suffix_sum_kernel(x_ref, o_ref):
    rows, width = x_ref.shape
    ones = _tri_ones(LANES, "strict_lower")
    carry = jnp.zeros((rows, 1), F32)
    for kb in reversed(range(width // LANES)):
        xb = x_ref[:, kb * LANES:(kb + 1) * LANES]
        o_ref[:, kb * LANES:(kb + 1) * LANES] = _ones_dot3(xb, ones, left=False) + carry
        carry = carry + jnp.sum(xb, axis=1, keepdims=True)


def _suffix_sum(x):
    return pl.pallas_call(
        _suffix_sum_kernel, name="cache_suffix_sum",
        out_shape=jax.ShapeDtypeStruct(x.shape, F32),
        compiler_params=_params(None, 32))(x)


def _split_heads(q2):
    lane = lax.broadcasted_iota(jnp.int32, q2.shape, 1)
    low = lane < D_HEAD
    zero = jnp.zeros_like(q2)
    return jnp.where(low, q2, zero), jnp.where(low, zero, q2)


def _stack_heads(q2):
    return jnp.concatenate(_split_heads(q2), axis=0)


def _unstack_heads(o):
    half = o.shape[0] // 2
    lane = lax.broadcasted_iota(jnp.int32, (half, o.shape[1]), 1)
    return jnp.where(lane < D_HEAD, o[:half], o[half:])


def _sb_update(z, pv, suffix_ones, state, valid):
    carry, acc = state
    r = _softplus(z)
    if valid is not None:
        r = jnp.where(valid, r, 0.0)
    after = _dot(r.astype(BF16), suffix_ones) + carry
    a = jnp.exp((z - r) - after)
    if valid is not None:
        a = jnp.where(valid, a, 0.0)
    acc = acc + pv(a.astype(BF16))
    carry = carry + jnp.sum(r, axis=1, keepdims=True)
    return carry, acc


def _sample_attn_kernel(qf_ref, kfn_ref, vfn_ref, kfc_ref, vfc_ref, drow_ref, cnrow_ref, ccol_ref,
                        qs_ref, ksn_ref, vsn_ref, ksc_last_ref, vsc_last_ref, ksc_hbm_ref,
                        vsc_hbm_ref, of_ref, os_ref, *scratch, blk, layer):
    for hp in range(N_PAIR):
        cols = pl.ds(hp * LANES, LANES)
        lane_view = lambda ref: ref.at[:, cols]
        row_view = lambda ref: ref.at[:, :, cols, :]
        _sample_pair(hp, lane_view(qf_ref), lane_view(kfn_ref), lane_view(vfn_ref),
                     row_view(kfc_ref), row_view(vfc_ref), drow_ref, cnrow_ref, ccol_ref,
                     lane_view(qs_ref), lane_view(ksn_ref), lane_view(vsn_ref),
                     row_view(ksc_last_ref), row_view(vsc_last_ref), ksc_hbm_ref, vsc_hbm_ref,
                     lane_view(of_ref), lane_view(os_ref), *scratch, blk=blk, layer=layer)


def _sample_pair(hp, qf_ref, kfn_ref, vfn_ref, kfc_ref, vfc_ref, drow_ref, cnrow_ref, ccol_ref,
                        qs_ref, ksn_ref, vsn_ref, ksc_last_ref, vsc_last_ref, ksc_hbm_ref,
                        vsc_hbm_ref, of_ref, os_ref,
                        carry_ref, acc_ref, kbuf_ref, vbuf_ref, sem_ref, *, blk, layer):
    b = pl.program_id(0)
    tq = qf_ref.shape[0]
    past = kfc_ref.shape[3]
    row = lax.broadcasted_iota(jnp.int32, (2 * tq, tq), 0) % tq
    col = lax.broadcasted_iota(jnp.int32, (2 * tq, tq), 1)

    def per_head(fn):
        return jnp.concatenate([fn(0), fn(1)], axis=0)

    q2 = _stack_heads(qf_ref[...])
    kct = kfc_ref[0, 0].astype(BF16)
    vct = vfc_ref[0, 0].astype(BF16)
    cc = ccol_ref[...]
    cn = cnrow_ref[0]
    cq = per_head(lambda h: _pick_lane(cc, 2 * hp + h))
    qk_c = _dot(q2, kct)
    qk_n = _dot_nt(q2, kfn_ref[...])
    s_c = per_head(lambda h: qk_c[h * tq:(h + 1) * tq]
                   + _pick_row(drow_ref[0, 2 * hp + h], b % SUBLANES))
    s_n = per_head(lambda h: qk_n[h * tq:(h + 1) * tq] - _pick_row(cn, 2 * hp + h))
    s_n = jnp.where(col <= row, s_n, NEG_INF)
    m = jnp.maximum(jnp.max(s_c, axis=1, keepdims=True), jnp.max(s_n, axis=1, keepdims=True)) + cq
    shift = cq - m
    p_c = jnp.exp(s_c + shift)
    p_n = jnp.exp(s_n + shift)
    l = jnp.sum(p_c, axis=1, keepdims=True) + jnp.sum(p_n, axis=1, keepdims=True)
    o = _dot_nt(p_c.astype(BF16), vct) + _dot(p_n.astype(BF16), vfn_ref[...])
    of_ref[...] = _unstack_heads(o / l)

    q2 = _stack_heads(qs_ref[...])
    vn = vsn_ref[...]
    suffix_ones = _tri_ones(blk, "strict_lower")

    def cache_block(kt_f32, vt_f32, state):
        kt = kt_f32.astype(BF16)
        vt = vt_f32.astype(BF16)
        return _sb_update(_dot(q2, kt), lambda a: _dot_nt(a, vt), suffix_ones, state, None)

    n_blocks = past // blk
    state = (jnp.zeros((2 * tq, 1), F32), jnp.zeros((2 * tq, LANES), F32))
    state = _sb_update(_dot_nt(q2, ksn_ref[...]), lambda a: _dot(a, vn),
                       _tri_ones(tq, "strict_lower"), state, col < row)
    carry_ref[...], acc_ref[...] = cache_block(ksc_last_ref[0, 0], vsc_last_ref[0, 0], state)

    def still_open():
        return jnp.min(carry_ref[...]) < SB_CLOSED

    def fetch(src_ref, jb, buf_ref, sem):
        rows = pl.ds(hp * LANES, LANES)
        return pltpu.make_async_copy(src_ref.at[layer, b, rows, pl.ds(jb * blk, blk)], buf_ref, sem)

    @pl.when(still_open())
    def _():
        for jb in reversed(range(n_blocks - 1)):
            @pl.when(still_open())
            def _():
                copies = (fetch(ksc_hbm_ref, jb, kbuf_ref, sem_ref.at[0]),
                          fetch(vsc_hbm_ref, jb, vbuf_ref, sem_ref.at[1]))
                for cp in copies:
                    cp.start()
                for cp in copies:
                    cp.wait()
                carry_ref[...], acc_ref[...] = cache_block(
                    kbuf_ref[...], vbuf_ref[...], (carry_ref[...], acc_ref[...]))

    os_ref[...] = _unstack_heads(acc_ref[...])


def _sample_attention(q, kb, vb, cache_fk, cache_fv, drow, cnrow, ccol, cache_sk, cache_sv,
                      *, layer, streams, dec_seq):
    past = cache_fk.shape[3]
    new_spec = lambda g: pl.BlockSpec((dec_seq, W_ATT), lambda b: (b, g))
    cache_spec = pl.BlockSpec((1, 1, W_ATT, past), lambda b: (layer, b, 0, 0))
    drow_spec = pl.BlockSpec((1, H_FOX, SUBLANES, past), lambda b: (layer, 0, b // SUBLANES, 0))
    cnrow_spec = pl.BlockSpec((1, SUBLANES, dec_seq), lambda b: (b, 0, 0))
    ccol_spec = pl.BlockSpec((dec_seq, LANES), lambda b: (b, 0))
    out_spec = pl.BlockSpec((dec_seq, W_ATT), lambda b: (b, 0))
    out_shape = [jax.ShapeDtypeStruct((streams * dec_seq, W_ATT), F32)] * 2
    blk = ATT_TILE
    newest_spec = pl.BlockSpec((1, 1, W_ATT, blk), lambda b: (layer, b, 0, past // blk - 1))
    hbm_spec = pl.BlockSpec(memory_space=pl.ANY)
    return pl.pallas_call(
        functools.partial(_sample_attn_kernel, blk=blk, layer=layer),
        grid=(streams,), name="sample_attention",
        in_specs=[new_spec(0), new_spec(0), new_spec(0), cache_spec, cache_spec, drow_spec,
                  cnrow_spec, ccol_spec, new_spec(1), new_spec(1), new_spec(1), newest_spec,
                  newest_spec, hbm_spec, hbm_spec],
        out_specs=[out_spec, out_spec], out_shape=out_shape,
        scratch_shapes=[pltpu.VMEM((2 * dec_seq, 1), F32), pltpu.VMEM((2 * dec_seq, LANES), F32),
                        pltpu.VMEM((LANES, blk), F32), pltpu.VMEM((LANES, blk), F32),
                        pltpu.SemaphoreType.DMA((2,))],
        compiler_params=_params(("parallel",), 48),
    )(q, kb, vb, cache_fk, cache_fv, drow, cnrow, ccol, q, kb, vb, cache_sk, cache_sv,
      cache_sk, cache_sv)


def _merge_tile(x_ref, of_ref, os_ref, u_ref, vn_ref, ws_ref, bs_ref, gmix_ref, wout_ref,
                g_ref, b_ref, *, alpha):
    tm = x_ref.shape[0]
    r = lax.broadcasted_iota(jnp.int32, (SGU_CHUNK, SGU_CHUNK), 0)
    c = lax.broadcasted_iota(jnp.int32, (SGU_CHUNK, SGU_CHUNK), 1)
    tril = r >= c
    lane = lax.broadcasted_iota(jnp.int32, (SGU_CHUNK, LANES), 1)

    gate_pieces = []
    for p in range(W_SGU // LANES):
        w0 = jnp.where(tril, ws_ref[2 * p], 0.0).astype(BF16)
        w1 = jnp.where(tril, ws_ref[2 * p + 1], 0.0).astype(BF16)
        bias = bs_ref[:, p * LANES:(p + 1) * LANES]
        chunks = []
        for ch in range(tm // SGU_CHUNK):
            rows = slice(ch * SGU_CHUNK, (ch + 1) * SGU_CHUNK)
            vp = vn_ref[rows, p * LANES:(p + 1) * LANES].astype(BF16)
            s = jnp.where(lane < D_HEAD, _dot(w0, vp), _dot(w1, vp)) + bias
            chunks.append(u_ref[rows, p * LANES:(p + 1) * LANES] * s)
        gate_pieces.append(jnp.concatenate(chunks, axis=0))

    def attention_pieces(ref):
        if ref.ndim == 2:
            return [ref[:, k * LANES:(k + 1) * LANES] for k in range(N_PAIR)]
        return [jnp.concatenate([ref[qb, k * LANES:(k + 1) * LANES, :].T
                                 for qb in range(ref.shape[0])], axis=0) for k in range(N_PAIR)]

    pieces = attention_pieces(of_ref) + gate_pieces + attention_pieces(os_ref)
    wide = 2 * LANES
    r2 = lax.broadcasted_iota(jnp.int32, (wide, wide), 0)
    c2 = lax.broadcasted_iota(jnp.int32, (wide, wide), 1)
    head_mean = jnp.where(r2 // D_HEAD == c2 // D_HEAD, 1.0 / D_HEAD, 0.0).astype(BF16)
    normed = []
    for k in range(0, len(pieces), 2):
        o = jnp.concatenate(pieces[k:k + 2], axis=1)
        ms = _dot((o * o).astype(BF16), head_mean)
        gain = gmix_ref[:, k * LANES:(k + 2) * LANES]
        normed.append((o * lax.rsqrt(ms + RMS_EPS) * gain).astype(BF16))
    mix = _dot(jnp.concatenate(normed, axis=1), wout_ref[0])
    return _layer_norm(alpha * x_ref[...] + mix, g_ref[...], b_ref[...])


def _mlp_tile(x, wup_ref, wdn_ref, g_ref, b_ref, *, alpha, ff_chunk):
    xb = x.astype(BF16)
    acc = alpha * x
    for c0 in range(0, wup_ref.shape[2], ff_chunk):
        h = jnp.maximum(_dot(xb, wup_ref[0, :, c0:c0 + ff_chunk]), 0.0)
        acc = acc + _dot((h * h).astype(BF16), wdn_ref[0, c0:c0 + ff_chunk, :])
    return _layer_norm(acc, g_ref[...], b_ref[...])


def _merge_mlp_kernel(x_ref, of_ref, os_ref, u_ref, vn_ref, ws_ref, bs_ref, gmix_ref, wout_ref,
                      g1_ref, b1_ref, wup_ref, wdn_ref, g2_ref, b2_ref, o_ref, *, alpha, ff_chunk):
    x1 = _merge_tile(x_ref, of_ref, os_ref, u_ref, vn_ref, ws_ref, bs_ref, gmix_ref, wout_ref,
                     g1_ref, b1_ref, alpha=alpha)
    o_ref[...] = _mlp_tile(x1, wup_ref, wdn_ref, g2_ref, b2_ref, alpha=alpha, ff_chunk=ff_chunk)


def _merge_mlp(x, of, os_, u, vn, ws, bs, gmix, wout, g1, b1, wup, wdn, g2, b2, *, alpha, name,
               layer):
    n, d = x.shape
    tm = ROW_TILE
    row = lambda w: pl.BlockSpec((tm, w), lambda i: (i, 0))
    if of.ndim == 2:
        att = row(W_ATT)
    else:
        tq = of.shape[2]
        att = pl.BlockSpec((tm // tq, W_ATT, tq), lambda i: (i, 0, 0))
    weights = (ws, bs, gmix, wout, g1, b1, wup, wdn, g2, b2)

    def weight_spec(a):
        if a is wout or a is wup or a is wdn:
            return pl.BlockSpec((1,) + a.shape[1:], lambda i: (layer, 0, 0),
                                pipeline_mode=pl.Buffered(1))
        return _resident(a.shape)

    return pl.pallas_call(
        functools.partial(_merge_mlp_kernel, alpha=alpha, ff_chunk=MLP_FF_CHUNK), grid=(n // tm,),
        name=name,
        in_specs=[row(d), att, att, row(W_SGU), row(W_SGU)] + [weight_spec(a) for a in weights],
        out_specs=row(d), out_shape=jax.ShapeDtypeStruct((n, d), F32),
        compiler_params=_params(("parallel",), 60),
    )(x, of, os_, u, vn, *weights)


def _pack_w_in(w, b_f):
    sizes = [W_ATT, W_ATT, W_ATT, H_FOX, W_SGU, W_SGU, W_ATT, W_ATT, W_ATT]
    offs = [0]
    for s in sizes:
        offs.append(offs[-1] + s)
    q_f, k_f, v_f, f_lg, u_g, v_g, q_s, k_s, v_s = (w[:, offs[i]:offs[i + 1]] for i in range(9))
    pad_cols = lambda a, width: jnp.pad(a, ((0, 0), (0, width - a.shape[1])))
    w_rows = jnp.concatenate([q_f, q_s, k_f, k_s, v_f, v_s, u_g, v_g, pad_cols(f_lg, LANES)],
                             axis=1).astype(BF16)
    w_cols = jnp.concatenate([q_f, q_s, k_f, k_s, v_f, v_s, pad_cols(f_lg, F_ROWS)],
                             axis=1).T.astype(BF16)
    bfc = jnp.pad(b_f, (0, LANES - H_FOX)).reshape(1, LANES)
    bfr = jnp.broadcast_to(jnp.pad(b_f, (0, SUBLANES - H_FOX)).reshape(SUBLANES, 1),
                           (SUBLANES, LANES))
    return w_rows, w_cols, bfc, bfr


def kernel(x_prompt, x_sample, cache_fox_k, cache_fox_v, cache_fox_logf, cache_sb_k, cache_sb_v,
           w_in, b_f, g_v, b_v, w_s, b_s, g_mix, w_out, ln1_g, ln1_b, w_up, w_down, ln2_g, ln2_b):
    depth = w_in.shape[0]
    batch, seq, d_model = x_prompt.shape
    streams, dec_seq, _ = x_sample.shape
    past = cache_fox_k.shape[2]
    n_prompt = batch * seq
    n_sample = streams * dec_seq
    alpha = (2 * depth) ** 0.25
    assert seq % ROW_TILE == 0 and n_sample % ROW_TILE == 0 and ROW_TILE % dec_seq == 0
    assert dec_seq == D_HEAD and 2 * dec_seq == SGU_CHUNK and past % ATT_TILE == 0
    assert streams % SUBLANES == 0

    xp = x_prompt.reshape(n_prompt, d_model)
    xs = x_sample.reshape(n_sample, d_model)
    row2 = lambda a: a.reshape(1, -1)
    cache_t = lambda c: jnp.transpose(c, (0, 1, 3, 4, 2)).reshape(depth, streams, W_ATT, past)
    ck_f, cv_f, ck_s, cv_s = (cache_t(c) for c in (cache_fox_k, cache_fox_v, cache_sb_k, cache_sb_v))
    lf_cache = jnp.transpose(cache_fox_logf.astype(F32), (0, 3, 1, 2))
    drow = _suffix_sum(lf_cache.reshape(depth * H_FOX * streams, past)).reshape(
        depth, H_FOX, streams, past)

    wout, wup, wdn = w_out.astype(BF16), w_up.astype(BF16), w_down.astype(BF16)
    p_rows = s_rows = None
    p_logf, s_logf, s_gate = [], [], []
    for l in range(depth):
        w_rows, w_cols, bfc, bfr = _pack_w_in(w_in[l], b_f[l])
        gv, bv = row2(g_v[l]), row2(b_v[l])
        gmix, g1, b1, g2, b2 = (row2(a[l]) for a in (g_mix, ln1_g, ln1_b, ln2_g, ln2_b))

        w_l = w_s[l]
        half = w_l[:, :dec_seq, :dec_seq]
        zeros = jnp.zeros_like(half)
        w_blockdiag = jnp.concatenate([jnp.concatenate([half, zeros], axis=2),
                                       jnp.concatenate([zeros, half], axis=2)], axis=1)
        bias_full = jnp.repeat(b_s[l].T, D_HEAD, axis=1)
        bias_half = jnp.concatenate([bias_full[:dec_seq], bias_full[:dec_seq]], axis=0)

        (qt, kb, *p_rows, vtbf, vtbs, lfrow, crow, ccolp, u, vn, shiftrow,
         head_flags) = _in_proj_prompt(
            xp, w_rows, w_cols, bfr, gv, bv, batch=batch, seq=seq, layer=l, depth=depth,
            cache_rows=p_rows)
        pair_flags = jnp.max(head_flags[:, :H_FOX, 0].reshape(-1, N_PAIR, 2), axis=2).reshape(-1)
        of = _prompt_attention(qt, kb, vtbf, 0, batch=batch, seq=seq, crow=crow, ccolp=ccolp,
                               shiftrow=shiftrow, flags=pair_flags)
        os_ = _prompt_attention(qt, kb, vtbs, 1, batch=batch, seq=seq)
        xp = _merge_mlp(xp, of, os_, u, vn, w_l, bias_full, gmix, wout, g1, b1, wup, wdn, g2, b2,
                        alpha=alpha, name="merge_mlp_prompt", layer=l)
        p_logf.append(jnp.transpose(lfrow[:H_FOX].reshape(H_FOX, batch, seq), (1, 2, 0)))

        (q, kb, vb, *s_rows, lfcol, ccol, cnrow, u, vn) = _in_proj_sample(
            xs, w_rows, w_cols, bfc, bfr, gv, bv, dec_seq=dec_seq, layer=l, depth=depth,
            cache_rows=s_rows)
        of, os_ = _sample_attention(q, kb, vb, ck_f, cv_f, drow, cnrow, ccol, ck_s, cv_s,
                                    layer=l, streams=streams, dec_seq=dec_seq)
        xs = _merge_mlp(xs, of, os_, u, vn, w_blockdiag, bias_half, gmix, wout, g1, b1, wup, wdn,
                        g2, b2, alpha=alpha, name="merge_mlp_sample", layer=l)
        s_logf.append(lfcol[:, :H_FOX].reshape(streams, dec_seq, H_FOX))
        s_gate.append(vn.reshape(streams, dec_seq, W_SGU))

    p_kf, p_ks, p_vf, p_vs = (jnp.transpose(a.reshape(depth, batch, H_FOX, D_HEAD, seq),
                                            (0, 1, 4, 2, 3)) for a in p_rows)
    s_kf, s_ks, s_vf, s_vs = (jnp.transpose(a, (0, 1, 3, 2, 4)) for a in s_rows)
    return (xp.reshape(batch, seq, d_model), xs.reshape(streams, dec_seq, d_model),
            p_kf, p_vf, jnp.stack(p_logf), p_ks, p_vs,
            s_kf, s_vf, jnp.stack(s_logf), s_ks, s_vs, jnp.stack(s_gate))
```
